```python
import math
import jax
import jax.numpy as jnp
from jax import lax
import numpy as np

D_MODEL = 1024
BATCH = 32
SEQ = 256
DEPTH = 2
DEC_BATCH = 4
DEC_SEQ = 1024
PAST_LEN = 256

GRID_W = 64
POS_BASE = 10000.0
N_MIXERS = 2
N_LAYERS_A = (DEPTH + 1) // 2
N_LAYERS_B = DEPTH // 2
N_DIR = 2
CHUNK = 64
HA = 8
DKA = 128
DVA = 128
CONV_K = 3
HB = 8
DKB = 128
DVB = 128
N_EXPERTS = 16
N_GROUPS = 4
EXP_PER_GROUP = N_EXPERTS // N_GROUPS
TOP_K = 2
D_FF = 512
EPS = 1e-6

PA = 2 * HA * DKA + 2 * HA * DVA + 2 * N_DIR * HA
CONV_CH = 2 * HA * DKA + HA * DVA
PB = 2 * HB * DKB + 2 * HB * DVB + 2 * N_DIR * HB

kernel_name = "bidir_gdn_mlstm_prefix_diffusion_step"


def _rms_norm(x, g):
    xf = x.astype(jnp.float32)
    y = xf * lax.rsqrt(jnp.mean(xf * xf, axis=-1, keepdims=True) + EPS)
    return (y * g.astype(jnp.float32)).astype(x.dtype)


def _l2_normalize(x):
    return x * lax.rsqrt(jnp.sum(x * x, axis=-1, keepdims=True) + EPS)


def _grid_pos_embed(n_tokens):
    rows = n_tokens // GRID_W
    r = jnp.repeat(jnp.arange(rows, dtype=jnp.float32), GRID_W)
    col = jnp.tile(jnp.arange(GRID_W, dtype=jnp.float32), rows)
    quarter = D_MODEL // 4
    freq = jnp.exp(jnp.arange(quarter, dtype=jnp.float32) * (-math.log(POS_BASE) / quarter))

    def axis_embed(pos):
        a = pos[:, None] * freq[None, :]
        return jnp.concatenate([jnp.sin(a), jnp.cos(a)], axis=-1)

    return jnp.concatenate([axis_embed(r), axis_embed(col)], axis=-1)


def _centred_depthwise_conv(x, w):
    pad = CONV_K // 2
    return lax.conv_general_dilated(x, w[:, None, :].astype(x.dtype), window_strides=(1,),
                                    padding=[(pad, pad)], dimension_numbers=('NWC', 'WIO', 'NWC'),
                                    feature_group_count=x.shape[-1])


def _gated_delta_chunked(q, k, v, g, beta, s0):
    B, H, T, DK = q.shape
    DV = v.shape[-1]
    nc = T // CHUNK
    ch = lambda a: jnp.moveaxis(a.reshape(B, H, nc, CHUNK, *a.shape[3:]), 2, 0)
    q, k, v, g, beta = map(ch, (q, k, v, g, beta))
    idx = jnp.arange(CHUNK)
    causal = idx[:, None] >= idx[None, :]
    strict = idx[:, None] > idx[None, :]
    G = jnp.cumsum(g, axis=-1)
    decay = jnp.exp(jnp.where(causal, G[..., :, None] - G[..., None, :], -jnp.inf))
    kb = k * beta[..., None]
    lower = jnp.where(strict, jnp.einsum('nbhcd,nbhsd->nbhcs', kb, k) * decay, 0.0)
    unit_lower = lower + jnp.eye(CHUNK, dtype=lower.dtype)
    rhs = jnp.concatenate([v * beta[..., None], kb * jnp.exp(G)[..., None]], axis=-1)
    sol = lax.linalg.triangular_solve(unit_lower, rhs, left_side=True, lower=True, unit_diagonal=True)
    u, w = sol[..., :DV], sol[..., DV:]
    qk = jnp.where(causal, jnp.einsum('nbhcd,nbhsd->nbhcs', q, k) * decay, 0.0)
    q_dec = q * jnp.exp(G)[..., None]
    k_end = k * jnp.exp(G[..., -1:] - G)[..., None]
    g_end = jnp.exp(G[..., -1])

    def step(S, inp):
        qdc, kec, uc, wc, qkc, gec = inp
        v_new = uc - jnp.einsum('bhcd,bhde->bhce', wc, S)
        o = jnp.einsum('bhcd,bhde->bhce', qdc, S) + jnp.einsum('bhcs,bhse->bhce', qkc, v_new)
        S = gec[..., None, None] * S + jnp.einsum('bhcd,bhce->bhde', kec, v_new)
        return S, o

    S, o = lax.scan(step, s0, (q_dec, k_end, u, w, qk, g_end))
    return jnp.moveaxis(o, 0, 2).reshape(B, H, T, DV), S


def _mlstm_chunked(q, k, v, i_pre, logf, c0, n0, m0):
    B, H, T, DK = q.shape
    DV = v.shape[-1]
    nc = T // CHUNK
    ch = lambda a: jnp.moveaxis(a.reshape(B, H, nc, CHUNK, *a.shape[3:]), 2, 0)
    q, k, v, i_pre, logf = map(ch, (q, k, v, i_pre, logf))
    idx = jnp.arange(CHUNK)
    causal = idx[:, None] >= idx[None, :]
    b = jnp.cumsum(logf, axis=-1)
    log_d = jnp.where(causal, b[..., :, None] - b[..., None, :] + i_pre[..., None, :], -jnp.inf)
    log_d_max = jnp.max(log_d, axis=-1)
    qk = jnp.einsum('nbhcd,nbhsd->nbhcs', q, k)
    log_w_end = b[..., -1:] - b + i_pre

    def step(carry, inp):
        C, nvec, m = carry
        qc, kc, vc, bc, ldc, ldmax, qkc, lwe = inp
        log_inter = bc + m[..., None]
        mt = jnp.maximum(log_inter, ldmax)
        s_inter = jnp.exp(log_inter - mt)
        P = jnp.exp(ldc - mt[..., None]) * qkc
        num = s_inter[..., None] * jnp.einsum('bhcd,bhde->bhce', qc, C) + jnp.einsum('bhcs,bhse->bhce', P, vc)
        den = s_inter * jnp.einsum('bhcd,bhd->bhc', qc, nvec) + jnp.sum(P, axis=-1)
        hc = num / jnp.maximum(jnp.abs(den), jnp.exp(-mt))[..., None]
        log_last = bc[..., -1] + m
        m_new = jnp.maximum(log_last, jnp.max(lwe, axis=-1))
        dec = jnp.exp(log_last - m_new)
        kw = kc * jnp.exp(lwe - m_new[..., None])[..., None]
        C = dec[..., None, None] * C + jnp.einsum('bhcd,bhce->bhde', kw, vc)
        nvec = dec[..., None] * nvec + jnp.sum(kw, axis=-2)
        return (C, nvec, m_new), hc

    (C, nvec, m), hs = lax.scan(step, (c0, n0, m0), (q, k, v, b, log_d, log_d_max, qk, log_w_end))
    return jnp.moveaxis(hs, 0, 2).reshape(B, H, T, DV), C, nvec, m


def _flip_t(a):
    return jnp.flip(a, axis=2)


def _gdn_mixer(h, s0, w_in, conv_w, a_log, dt_bias, norm_g, w_out):
    B, T, _ = h.shape
    f32 = jnp.float32
    nqk, nv = HA * DKA, HA * DVA
    proj = jnp.einsum('btd,dp->btp', h, w_in)
    qkv = jax.nn.silu(_centred_depthwise_conv(proj[..., :2 * nqk + nv], conv_w)).astype(f32)
    z = proj[..., 2 * nqk + nv:2 * nqk + 2 * nv].astype(f32).reshape(B, T, HA, DVA)
    ab = proj[..., 2 * nqk + 2 * nv:].astype(f32).reshape(B, T, 2, N_DIR, HA)
    heads = lambda a, d: a.reshape(B, T, -1, d).transpose(0, 2, 1, 3)
    q = _l2_normalize(heads(qkv[..., :nqk], DKA)) * (DKA ** -0.5)
    k = _l2_normalize(heads(qkv[..., nqk:2 * nqk], DKA))
    v = heads(qkv[..., 2 * nqk:], DVA)
    g = -jnp.exp(a_log.astype(f32)) * jax.nn.softplus(ab[:, :, 0] + dt_bias.astype(f32))
    beta = jax.nn.sigmoid(ab[:, :, 1])
    g = g.transpose(0, 2, 3, 1)
    beta = beta.transpose(0, 2, 3, 1)
    s0 = s0.astype(f32)
    o_f, s_f = _gated_delta_chunked(q, k, v, g[:, 0], beta[:, 0], s0[:, 0])
    o_b, s_b = _gated_delta_chunked(_flip_t(q), _flip_t(k), _flip_t(v), _flip_t(g[:, 1]), _flip_t(beta[:, 1]), s0[:, 1])
    o = (o_f + _flip_t(o_b)).transpose(0, 2, 1, 3)
    o = _rms_norm(o, norm_g) * jax.nn.silu(z)
    out = jnp.einsum('bte,ed->btd', o.reshape(B, T, nv).astype(h.dtype), w_out)
    return out, jnp.stack([s_f, s_b], axis=1)


def _mlstm_mixer(h, c0, n0, m0, w_in, gate_b, norm_g, w_out):
    B, T, _ = h.shape
    f32 = jnp.float32
    nqk, nv = HB * DKB, HB * DVB
    proj = jnp.einsum('btd,dp->btp', h, w_in).astype(f32)
    heads = lambda a, d: a.reshape(B, T, -1, d).transpose(0, 2, 1, 3)
    q = heads(proj[..., :nqk], DKB) * (DKB ** -0.5)
    k = heads(proj[..., nqk:2 * nqk], DKB)
    v = heads(proj[..., 2 * nqk:2 * nqk + nv], DVB)
    o_gate = jax.nn.sigmoid(proj[..., 2 * nqk + nv:2 * nqk + 2 * nv].reshape(B, T, HB, DVB))
    gates = proj[..., 2 * nqk + 2 * nv:].reshape(B, T, 2, N_DIR, HB) + gate_b.astype(f32)
    i_pre = gates[:, :, 0].transpose(0, 2, 3, 1)
    logf = jax.nn.log_sigmoid(gates[:, :, 1]).transpose(0, 2, 3, 1)
    c0, n0, m0 = c0.astype(f32), n0.astype(f32), m0.astype(f32)
    h_f, C_f, n_f, m_f = _mlstm_chunked(q, k, v, i_pre[:, 0], logf[:, 0], c0[:, 0], n0[:, 0], m0[:, 0])
    h_b, C_b, n_b, m_b = _mlstm_chunked(_flip_t(q), _flip_t(k), _flip_t(v), _flip_t(i_pre[:, 1]),
                                        _flip_t(logf[:, 1]), c0[:, 1], n0[:, 1], m0[:, 1])
    hs = (h_f + _flip_t(h_b)).transpose(0, 2, 1, 3)
    y = _rms_norm(hs, norm_g) * o_gate
    out = jnp.einsum('bte,ed->btd', y.reshape(B, T, nv).astype(h.dtype), w_out)
    return out, jnp.stack([C_f, C_b], axis=1), jnp.stack([n_f, n_b], axis=1), jnp.stack([m_f, m_b], axis=1)


def _moe(h, router_w, router_bias, w_gate, w_up, w_down):
    B, T, D = h.shape
    f32 = jnp.float32
    x = h.reshape(B * T, D)
    probs = jax.nn.softmax(jnp.einsum('nd,de->ne', x, router_w).astype(f32), axis=-1)
    sel = (probs + router_bias.astype(f32)).reshape(-1, N_GROUPS, EXP_PER_GROUP)
    group_score = jnp.sum(lax.top_k(sel, TOP_K)[0], axis=-1)
    grp = jnp.argmax(group_score, axis=-1)
    sel_in = jnp.take_along_axis(sel, grp[:, None, None], axis=1)[:, 0]
    _, local = lax.top_k(sel_in, TOP_K)
    expert = grp[:, None] * EXP_PER_GROUP + local
    w = jnp.take_along_axis(probs, expert, axis=-1)
    w = w / jnp.sum(w, axis=-1, keepdims=True)
    gates = jnp.sum(jax.nn.one_hot(expert, N_EXPERTS, dtype=f32) * w[..., None], axis=1).astype(h.dtype)
    hid = jax.nn.silu(jnp.einsum('nd,edf->nef', x, w_gate)) * jnp.einsum('nd,edf->nef', x, w_up)
    y = jnp.einsum('nef,efd->nd', hid * gates[..., None], w_down)
    return y.reshape(B, T, D)


def _trunk(x, cond, init_gdn, init_mC, init_mn, init_mm, p):
    cs = jax.nn.silu(cond.astype(jnp.float32)).astype(x.dtype)
    gdn_s, m_C, m_n, m_m = [], [], [], []
    for layer in range(DEPTH):
        mod = (cs @ p['ada_w'][layer] + p['ada_b'][layer])[:, None, :]
        sh1, sc1, g1, sh2, sc2, g2 = jnp.split(mod, 6, axis=-1)
        hn = _rms_norm(x, p['norm1_g'][layer]) * (1 + sc1) + sh1
        j = layer // N_MIXERS
        if layer % N_MIXERS == 0:
            out, s = _gdn_mixer(hn, init_gdn[:, j], p['gdn_w_in'][j], p['gdn_conv_w'][j], p['gdn_a_log'][j],
                                p['gdn_dt_bias'][j], p['gdn_norm_g'][j], p['gdn_w_out'][j])
            gdn_s.append(s)
        else:
            out, C, n, m = _mlstm_mixer(hn, init_mC[:, j], init_mn[:, j], init_mm[:, j], p['mlstm_w_in'][j],
                                        p['mlstm_gate_b'][j], p['mlstm_norm_g'][j], p['mlstm_w_out'][j])
            m_C.append(C)
            m_n.append(n)
            m_m.append(m)
        x = x + g1 * out.astype(x.dtype)
        hn = _rms_norm(x, p['norm2_g'][layer]) * (1 + sc2) + sh2
        x = x + g2 * _moe(hn, p['router_w'], p['router_bias'], p['exp_w_gate'][layer],
                          p['exp_w_up'][layer], p['exp_w_down'][layer])
    y = _rms_norm(x, p['final_norm_g'])
    return y, jnp.stack(gdn_s, axis=1), jnp.stack(m_C, axis=1), jnp.stack(m_n, axis=1), jnp.stack(m_m, axis=1)


def setup_inputs(seed: int = 0) -> dict:
    key = jax.random.key(seed)
    ks = jax.random.split(key, 32)
    f32 = jnp.float32
    nrm = lambda k, shape, scale: jax.random.normal(k, shape, f32) * scale
    dt = jnp.exp(jax.random.uniform(ks[13], (N_LAYERS_A, N_DIR, HA), f32, math.log(1e-3), math.log(1e-1)))
    return {
        'x_prompt': nrm(ks[0], (BATCH, SEQ, D_MODEL), 1.0),
        'x_sample': nrm(ks[1], (DEC_BATCH, DEC_SEQ, D_MODEL), 1.0),
        'state_gdn_S': nrm(ks[2], (DEC_BATCH, N_LAYERS_A, N_DIR, HA, DKA, DVA), DKA ** -0.5),
        'state_mlstm_C': nrm(ks[3], (DEC_BATCH, N_LAYERS_B, N_DIR, HB, DKB, DVB), 1.0),
        'state_mlstm_n': nrm(ks[4], (DEC_BATCH, N_LAYERS_B, N_DIR, HB, DKB), 1.0),
        'state_mlstm_m': nrm(ks[5], (DEC_BATCH, N_LAYERS_B, N_DIR, HB), 0.5),
        'c': nrm(ks[6], (DEC_BATCH, D_MODEL), 1.0),
        'c_ctx': nrm(ks[7], (D_MODEL,), 1.0),
        'ada_w': nrm(ks[8], (DEPTH, D_MODEL, 6 * D_MODEL), 0.5 * D_MODEL ** -0.5),
        'ada_b': nrm(ks[9], (DEPTH, 6 * D_MODEL), 0.02),
        'norm1_g': 1.0 + nrm(ks[10], (DEPTH, D_MODEL), 0.02),
        'norm2_g': 1.0 + nrm(ks[11], (DEPTH, D_MODEL), 0.02),
        'gdn_w_in': nrm(ks[12], (N_LAYERS_A, D_MODEL, PA), D_MODEL ** -0.5),
        'gdn_conv_w': nrm(ks[14], (N_LAYERS_A, CONV_K, CONV_CH), CONV_K ** -0.5),
        'gdn_a_log': jnp.log(jax.random.uniform(ks[15], (N_LAYERS_A, N_DIR, HA), f32, 1.0, 16.0)),
        'gdn_dt_bias': dt + jnp.log(-jnp.expm1(-dt)),
        'gdn_norm_g': 1.0 + nrm(ks[16], (N_LAYERS_A, DVA), 0.02),
        'gdn_w_out': nrm(ks[17], (N_LAYERS_A, HA * DVA, D_MODEL), (HA * DVA) ** -0.5),
        'mlstm_w_in': nrm(ks[18], (N_LAYERS_B, D_MODEL, PB), D_MODEL ** -0.5),
        'mlstm_gate_b': jnp.stack([nrm(ks[19], (N_LAYERS_B, N_DIR, HB), 0.1),
                                   3.0 + nrm(ks[20], (N_LAYERS_B, N_DIR, HB), 0.5)], axis=1),
        'mlstm_norm_g': 1.0 + nrm(ks[21], (N_LAYERS_B, DVB), 0.02),
        'mlstm_w_out': nrm(ks[22], (N_LAYERS_B, HB * DVB, D_MODEL), (HB * DVB) ** -0.5),
        'router_w': nrm(ks[23], (D_MODEL, N_EXPERTS), D_MODEL ** -0.5),
        'router_bias': nrm(ks[24], (N_EXPERTS,), 0.01),
        'exp_w_gate': nrm(ks[25], (DEPTH, N_EXPERTS, D_MODEL, D_FF), D_MODEL ** -0.5),
        'exp_w_up': nrm(ks[26], (DEPTH, N_EXPERTS, D_MODEL, D_FF), D_MODEL ** -0.5),
        'exp_w_down': nrm(ks[27], (DEPTH, N_EXPERTS, D_FF, D_MODEL), D_FF ** -0.5),
        'final_norm_g': 1.0 + nrm(ks[28], (D_MODEL,), 0.02),
    }


def reference(x_prompt, x_sample, state_gdn_S, state_mlstm_C, state_mlstm_n, state_mlstm_m, c, c_ctx,
              ada_w, ada_b, norm1_g, norm2_g, gdn_w_in, gdn_conv_w, gdn_a_log, gdn_dt_bias, gdn_norm_g,
              gdn_w_out, mlstm_w_in, mlstm_gate_b, mlstm_norm_g, mlstm_w_out, router_w, router_bias,
              exp_w_gate, exp_w_up, exp_w_down, final_norm_g):
    p = dict(ada_w=ada_w, ada_b=ada_b, norm1_g=norm1_g, norm2_g=norm2_g, gdn_w_in=gdn_w_in,
             gdn_conv_w=gdn_conv_w, gdn_a_log=gdn_a_log, gdn_dt_bias=gdn_dt_bias, gdn_norm_g=gdn_norm_g,
             gdn_w_out=gdn_w_out, mlstm_w_in=mlstm_w_in, mlstm_gate_b=mlstm_gate_b, mlstm_norm_g=mlstm_norm_g,
             mlstm_w_out=mlstm_w_out, router_w=router_w, router_bias=router_bias, exp_w_gate=exp_w_gate,
             exp_w_up=exp_w_up, exp_w_down=exp_w_down, final_norm_g=final_norm_g)
    f32 = jnp.float32
    bp = x_prompt.shape[0]
    z_S = jnp.zeros((bp, N_LAYERS_A, N_DIR, HA, DKA, DVA), f32)
    z_C = jnp.zeros((bp, N_LAYERS_B, N_DIR, HB, DKB, DVB), f32)
    z_n = jnp.zeros((bp, N_LAYERS_B, N_DIR, HB, DKB), f32)
    z_m = jnp.zeros((bp, N_LAYERS_B, N_DIR, HB), f32)
    y_prompt, new_gdn_S, new_mlstm_C, new_mlstm_n, new_mlstm_m = _trunk(
        x_prompt, c_ctx[None, :], z_S, z_C, z_n, z_m, p)
    pos = _grid_pos_embed(x_sample.shape[1]).astype(x_sample.dtype)
    y_sample = _trunk(x_sample + pos[None], c, state_gdn_S, state_mlstm_C, state_mlstm_n, state_mlstm_m, p)[0]
    return (y_prompt, y_sample, new_gdn_S, new_mlstm_C, new_mlstm_n, new_mlstm_m)
```

```python
import functools
import math

import jax
import jax.numpy as jnp
from jax import lax
from jax.experimental import pallas as pl
from jax.experimental.pallas import tpu as pltpu

F32 = jnp.float32
BF16 = jnp.bfloat16
HIGHEST = lax.Precision.HIGHEST

D_MODEL = 1024
N_HEADS = 8
D_HEAD = 128
CHUNK = 64
N_EXPERTS = 16
N_GROUPS = 4
EXP_PER_GROUP = 4
D_FF = 512
EPS = 1e-6
GRID_W = 64
POS_BASE = 10000.0
N_MOD_ROWS = 8
VMEM_LIMIT = 56 * 1024 * 1024

TM_PROJ = 256
TM_MOE = 512


def _silu(x):
    return x * jax.nn.sigmoid(x)


def _softplus(x):
    return jnp.maximum(x, 0.0) + jnp.log(1.0 + jnp.exp(-jnp.abs(x)))


def _dot(a, b, precision=None):
    return jnp.dot(a, b, preferred_element_type=F32, precision=precision)


def _dot_nt(a, b, precision=None):
    return lax.dot_general(a, b, (((1,), (1,)), ((), ())), preferred_element_type=F32, precision=precision)


def _dot_tn(a, b, precision=None):
    return lax.dot_general(a, b, (((0,), (0,)), ((), ())), preferred_element_type=F32, precision=precision)


def _rms(x):
    return x * lax.rsqrt(jnp.mean(x * x, axis=-1, keepdims=True) + EPS)


def _ada_kernel(c_ref, w_ref, b_ref, o_ref):
    cs = _silu(c_ref[...]).astype(BF16)
    o_ref[0] = _dot(cs, w_ref[0].astype(BF16)) + b_ref[0]


def _ada_mods(conds, ada_w, ada_b):
    depth, d, n6 = ada_w.shape
    tn = 1536
    return pl.pallas_call(
        _ada_kernel,
        grid=(depth, n6 // tn),
        in_specs=[
            pl.BlockSpec((N_MOD_ROWS, d), lambda l, j: (0, 0)),
            pl.BlockSpec((1, d, tn), lambda l, j: (l, 0, j)),
            pl.BlockSpec((1, 1, tn), lambda l, j: (l, 0, j)),
        ],
        out_specs=pl.BlockSpec((1, N_MOD_ROWS, tn), lambda l, j: (l, 0, j)),
        out_shape=jax.ShapeDtypeStruct((depth, N_MOD_ROWS, n6), F32),
        compiler_params=pltpu.CompilerParams(dimension_semantics=("arbitrary", "arbitrary"),
                                             vmem_limit_bytes=VMEM_LIMIT),
        name="ada_mods",
    )(conds, ada_w, ada_b.reshape(depth, 1, n6))


def _inproj_kernel(x_ref, ng_ref, sc_ref, sh_ref, w_ref, wg_ref, wgt_ref, gb_r_ref, gm_r_ref, gb_c_ref,
                   gm_c_ref, main_ref, g_ref, gt_ref, *, kind):
    x = x_ref[...]
    hn = _rms(x) * ng_ref[...]
    hn = hn * (1.0 + sc_ref[0]) + sh_ref[0]
    hb = hn.astype(BF16)
    main_ref[...] = _dot(hb, w_ref[...])

    def activate(pre, bias, mul, first):
        z = pre + bias
        if kind == "gdn":
            return jnp.where(first, mul * _softplus(z), jax.nn.sigmoid(z))
        return jnp.where(first, z, -_softplus(-z))

    pre = _dot(hb, wg_ref[...])
    col = lax.broadcasted_iota(jnp.int32, pre.shape, 1)
    g_ref[...] = activate(pre, gb_r_ref[...], gm_r_ref[...], (col % 4) < 2)
    pre_t = _dot_nt(wgt_ref[...], hb)
    row = lax.broadcasted_iota(jnp.int32, pre_t.shape, 0)
    gt_ref[...] = activate(pre_t, gb_c_ref[...], gm_c_ref[...], (row % 4) < 2)


def _mod_row_of_tile(tm, n_ctx, t_dec):
    def f(i):
        r = i * tm
        return jnp.where(r < n_ctx, 0, 1 + (r - n_ctx) // t_dec)
    return f


def _inproj(x, mods, mod_base, norm_g, w_main, w_gate, gate_bias, gate_mul, kind, n_ctx, t_dec):
    n, d = x.shape
    tm = TM_PROJ
    pm = w_main.shape[1]
    ng = w_gate.shape[1]
    mrow = _mod_row_of_tile(tm, n_ctx, t_dec)
    sc_map = lambda i: (mod_base + mrow(i) * 6 + 1, 0, 0)
    sh_map = lambda i: (mod_base + mrow(i) * 6 + 0, 0, 0)
    const2 = lambda i: (0, 0)
    return pl.pallas_call(
        functools.partial(_inproj_kernel, kind=kind),
        grid=(n // tm,),
        in_specs=[
            pl.BlockSpec((tm, d), lambda i: (i, 0)),
            pl.BlockSpec((1, d), const2),
            pl.BlockSpec((1, 1, d), sc_map),
            pl.BlockSpec((1, 1, d), sh_map),
            pl.BlockSpec((d, pm), const2),
            pl.BlockSpec((d, ng), const2),
            pl.BlockSpec((ng, d), const2),
            pl.BlockSpec((1, ng), const2),
            pl.BlockSpec((1, ng), const2),
            pl.BlockSpec((ng, 1), const2),
            pl.BlockSpec((ng, 1), const2),
        ],
        out_specs=[
            pl.BlockSpec((tm, pm), lambda i: (i, 0)),
            pl.BlockSpec((tm, ng), lambda i: (i, 0)),
            pl.BlockSpec((ng, tm), lambda i: (0, i)),
        ],
        out_shape=[
            jax.ShapeDtypeStruct((n, pm), F32),
            jax.ShapeDtypeStruct((n, ng), F32),
            jax.ShapeDtypeStruct((ng, n), F32),
        ],
        compiler_params=pltpu.CompilerParams(dimension_semantics=("arbitrary",), vmem_limit_bytes=VMEM_LIMIT),
        name="inproj_" + kind,
    )(x, norm_g.reshape(1, d), mods, mods, w_main.astype(BF16), w_gate.astype(BF16), w_gate.T.astype(BF16),
      gate_bias.reshape(1, ng), gate_mul.reshape(1, ng), gate_bias.reshape(ng, 1), gate_mul.reshape(ng, 1))


def _chunk_masks():
    r = lax.broadcasted_iota(jnp.int32, (CHUNK, CHUNK), 0)
    c = lax.broadcasted_iota(jnp.int32, (CHUNK, CHUNK), 1)
    return r >= c, r > c, r <= c, r < c


def _cumsum_both(col, row, lower_f, upper_f, d):
    left = lower_f if d == 0 else upper_f
    right = upper_f if d == 0 else lower_f
    col_b = _dot(left, jnp.broadcast_to(col, (CHUNK, D_HEAD)), HIGHEST)
    row_b = _dot(jnp.broadcast_to(row, (CHUNK, CHUNK)), right, HIGHEST)
    return col_b, row_b


def _unit_tri_solve(a, rhs):
    x = rhs - _dot(a, rhs, HIGHEST)
    p = a
    for _ in range(5):
        p = _dot(p, p, HIGHEST)
        x = x + _dot(p, x, HIGHEST)
    return x


def _gdn_kernel(*refs, t_len, has_init, emit_state):
    it = iter(refs)
    q_ref, k_ref, v_ref, z_ref, gc_ref, gr_ref, cwq_ref, cwk_ref, cwv_ref, ng_ref = (next(it) for _ in range(10))
    s0_ref = next(it) if has_init else None
    o_ref = next(it)
    s_ref = next(it) if emit_state else None
    qs, ks, vs, oacc = (next(it) for _ in range(4))
    nc = t_len // CHUNK

    rows = lax.broadcasted_iota(jnp.int32, (t_len, 1), 0)

    def conv_silu(x, cw):
        xm = jnp.where(rows == 0, 0.0, pltpu.roll(x, 1, axis=0))
        xp = jnp.where(rows == t_len - 1, 0.0, pltpu.roll(x, t_len - 1, axis=0))
        return _silu(xm * cw[0:1] + x * cw[1:2] + xp * cw[2:3])

    def l2n(x):
        return x * lax.rsqrt(jnp.sum(x * x, axis=-1, keepdims=True) + EPS)

    qs[...] = l2n(conv_silu(q_ref[...], cwq_ref[...])) * (D_HEAD ** -0.5)
    ks[...] = l2n(conv_silu(k_ref[...], cwk_ref[...]))
    vs[...] = conv_silu(v_ref[...], cwv_ref[...])
    oacc[...] = jnp.zeros_like(oacc)

    lo_i, lo_s, up_i, up_s = _chunk_masks()
    lower_f = lo_i.astype(F32)
    upper_f = up_i.astype(F32)

    def chunk_step(c, d, s):
        sl = pl.ds(pl.multiple_of(c * CHUNK, CHUNK), CHUNK)
        q, k, v = qs[sl, :], ks[sl, :], vs[sl, :]
        gc4 = gc_ref[0, sl, :]
        gr4 = gr_ref[0, c]
        beta = gc4[:, 2 + d:3 + d]
        incl, strict = (lo_i, lo_s) if d == 0 else (up_i, up_s)
        g_col, g_row = _cumsum_both(gc4[:, d:d + 1], gr4[d:d + 1, :], lower_f, upper_f, d)
        decay = jnp.exp(jnp.where(incl, g_col[:, :CHUNK] - g_row, -jnp.inf))
        kb = k * beta
        k16 = k.astype(BF16)
        a = jnp.where(strict, _dot_nt(kb.astype(BF16), k16) * decay, 0.0)
        e_g = jnp.exp(g_col)
        sol = _unit_tri_solve(a, jnp.concatenate([v * beta, kb * e_g], axis=1))
        u, w = sol[:, :D_HEAD], sol[:, D_HEAD:]
        qk = jnp.where(incl, _dot_nt(q.astype(BF16), k16) * decay, 0.0)
        g_last = g_col[CHUNK - 1:CHUNK, :] if d == 0 else g_col[0:1, :]
        k_end = k * jnp.exp(g_last - g_col)
        s16 = s.astype(BF16)
        v_new = u - _dot(w.astype(BF16), s16)
        v16 = v_new.astype(BF16)
        o = _dot((q * e_g).astype(BF16), s16) + _dot(qk.astype(BF16), v16)
        oacc[sl, :] += o
        return jnp.exp(g_last) * s + _dot_tn(k_end.astype(BF16), v16)

    def body(i, carry):
        s_f, s_b = carry
        return chunk_step(i, 0, s_f), chunk_step(nc - 1 - i, 1, s_b)

    if has_init:
        init = (s0_ref[0, 0, 0], s0_ref[0, 1, 0])
    else:
        init = (jnp.zeros((D_HEAD, D_HEAD), F32), jnp.zeros((D_HEAD, D_HEAD), F32))
    s_f, s_b = lax.fori_loop(0, nc, body, init)
    if emit_state:
        s_ref[0, 0, 0] = s_f
        s_ref[0, 1, 0] = s_b
    o_ref[...] = (_rms(oacc[...]) * ng_ref[...] * _silu(z_ref[...])).astype(o_ref.dtype)


def _head_specs(t_len, row0, n_proj):
    rb = row0 // t_len
    nc = t_len // CHUNK
    specs = [pl.BlockSpec((t_len, D_HEAD), functools.partial(lambda b, h, j: (rb + b, j * N_HEADS + h), j=j))
             for j in range(n_proj)]
    specs.append(pl.BlockSpec((1, t_len, 4), lambda b, h: (h, rb + b, 0)))
    specs.append(pl.BlockSpec((1, nc, 4, CHUNK), lambda b, h: (h, rb + b, 0, 0)))
    return specs


def _gdn_scan(main, gcol, grow, conv_w, norm_g, s0, o_prev, *, batch, t_len, row0, emit_state):
    n = main.shape[0]
    has_init = s0 is not None
    in_specs = _head_specs(t_len, row0, 4)
    in_specs += [pl.BlockSpec((3, D_HEAD), functools.partial(lambda b, h, j: (0, j * N_HEADS + h), j=j))
                 for j in range(3)]
    in_specs.append(pl.BlockSpec((1, D_HEAD), lambda b, h: (0, 0)))
    args = [main, main, main, main, gcol, grow, conv_w, conv_w, conv_w, norm_g.reshape(1, D_HEAD)]
    state_spec = pl.BlockSpec((1, 2, 1, D_HEAD, D_HEAD), lambda b, h: (b, 0, h, 0, 0))
    if has_init:
        in_specs.append(state_spec)
        args.append(s0)
    rb = row0 // t_len
    out_specs = [pl.BlockSpec((t_len, D_HEAD), lambda b, h: (rb + b, h))]
    out_shape = [jax.ShapeDtypeStruct((n, N_HEADS * D_HEAD), BF16)]
    if emit_state:
        out_specs.append(state_spec)
        out_shape.append(jax.ShapeDtypeStruct((batch, 2, N_HEADS, D_HEAD, D_HEAD), F32))
    aliases = {}
    if o_prev is not None:
        in_specs.append(pl.BlockSpec(memory_space=pl.ANY))
        args.append(o_prev)
        aliases = {len(args) - 1: 0}

    def kern(*refs):
        refs = list(refs)
        if o_prev is not None:
            del refs[len(args) - 1]
        return _gdn_kernel(*refs, t_len=t_len, has_init=has_init, emit_state=emit_state)

    return pl.pallas_call(
        kern,
        grid=(batch, N_HEADS),
        in_specs=in_specs,
        out_specs=out_specs,
        out_shape=out_shape,
        scratch_shapes=[pltpu.VMEM((t_len, D_HEAD), F32) for _ in range(4)],
        input_output_aliases=aliases,
        compiler_params=pltpu.CompilerParams(dimension_semantics=("arbitrary", "arbitrary"),
                                             vmem_limit_bytes=VMEM_LIMIT),
        name="gdn_scan_t%d" % t_len,
    )(*args)


def _mlstm_kernel(*refs, t_len, has_init, emit_state):
    it = iter(refs)
    q_ref, k_ref, v_ref, og_ref, gc_ref, gr_ref, ng_ref = (next(it) for _ in range(7))
    if has_init:
        c0_ref, n0_ref, m0_ref = (next(it) for _ in range(3))
    o_ref = next(it)
    if emit_state:
        c_ref, n_ref, m_ref = (next(it) for _ in range(3))
    hacc = next(it)
    nc = t_len // CHUNK

    hacc[...] = jnp.zeros_like(hacc)
    lo_i, _, up_i, _ = _chunk_masks()
    lower_f = lo_i.astype(F32)
    upper_f = up_i.astype(F32)

    def chunk_step(c, d, carry):
        cmat, nvec, m = carry
        sl = pl.ds(pl.multiple_of(c * CHUNK, CHUNK), CHUNK)
        q = q_ref[sl, :] * (D_HEAD ** -0.5)
        k, v = k_ref[sl, :], v_ref[sl, :]
        gc4 = gc_ref[0, sl, :]
        gr4 = gr_ref[0, c]
        incl = lo_i if d == 0 else up_i
        b_col, b_row = _cumsum_both(gc4[:, 2 + d:3 + d], gr4[2 + d:3 + d, :], lower_f, upper_f, d)
        log_d = jnp.where(incl, b_col[:, :CHUNK] - b_row + gr4[d:d + 1, :], -jnp.inf)
        ld_max = jnp.max(log_d, axis=-1, keepdims=True)
        q16, k16, v16 = q.astype(BF16), k.astype(BF16), v.astype(BF16)
        qk = _dot_nt(q16, k16)
        b_last = b_col[CHUNK - 1:CHUNK, :] if d == 0 else b_col[0:1, :]
        lwe = b_last - b_col + gc4[:, d:d + 1]
        log_inter = b_col + m
        mt = jnp.maximum(log_inter, ld_max)
        s_inter = jnp.exp(log_inter - mt)
        p = jnp.exp(log_d - mt[:, :CHUNK]) * qk
        num = s_inter * _dot(q16, cmat.astype(BF16)) + _dot(p.astype(BF16), v16)
        den = s_inter[:, :1] * jnp.sum(q * nvec, axis=-1, keepdims=True) + jnp.sum(p, axis=-1, keepdims=True)
        hacc[sl, :] += num / jnp.maximum(jnp.abs(den), jnp.exp(-mt[:, :1]))
        log_last = b_last + m
        m_new = jnp.maximum(log_last, jnp.max(lwe, axis=0, keepdims=True))
        dec = jnp.exp(log_last - m_new)
        kw = k * jnp.exp(lwe - m_new)
        cmat = dec * cmat + _dot_tn(kw.astype(BF16), v16)
        nvec = dec * nvec + jnp.sum(kw, axis=0, keepdims=True)
        return cmat, nvec, m_new

    def body(i, carry):
        return chunk_step(i, 0, carry[0]), chunk_step(nc - 1 - i, 1, carry[1])

    if has_init:
        init = tuple((c0_ref[0, d, 0], n0_ref[0, 0, d:d + 1, :], m0_ref[0, 0, d:d + 1, :]) for d in range(2))
    else:
        z = lambda *s: jnp.zeros(s, F32)
        init = tuple((z(D_HEAD, D_HEAD), z(1, D_HEAD), z(1, D_HEAD)) for _ in range(2))
    fin = lax.fori_loop(0, nc, body, init)
    if emit_state:
        for d in range(2):
            c_ref[0, d, 0] = fin[d][0]
            n_ref[0, 0, d:d + 1, :] = fin[d][1]
            m_ref[0, 0, d:d + 1, :] = fin[d][2]
    o_ref[...] = (_rms(hacc[...]) * ng_ref[...] * jax.nn.sigmoid(og_ref[...])).astype(o_ref.dtype)


def _mlstm_scan(main, gcol, grow, norm_g, init, o_prev, *, batch, t_len, row0, emit_state):
    n = main.shape[0]
    has_init = init is not None
    in_specs = _head_specs(t_len, row0, 4)
    in_specs.append(pl.BlockSpec((1, D_HEAD), lambda b, h: (0, 0)))
    args = [main, main, main, main, gcol, grow, norm_g.reshape(1, D_HEAD)]
    c_spec = pl.BlockSpec((1, 2, 1, D_HEAD, D_HEAD), lambda b, h: (b, 0, h, 0, 0))
    v_spec = pl.BlockSpec((1, 1, 2, D_HEAD), lambda b, h: (b, h, 0, 0))
    if has_init:
        in_specs += [c_spec, v_spec, v_spec]
        args += list(init)
    rb = row0 // t_len
    out_specs = [pl.BlockSpec((t_len, D_HEAD), lambda b, h: (rb + b, h))]
    out_shape = [jax.ShapeDtypeStruct((n, N_HEADS * D_HEAD), BF16)]
    if emit_state:
        out_specs += [c_spec, v_spec, v_spec]
        out_shape += [jax.ShapeDtypeStruct((batch, 2, N_HEADS, D_HEAD, D_HEAD), F32),
                      jax.ShapeDtypeStruct((batch, N_HEADS, 2, D_HEAD), F32),
                      jax.ShapeDtypeStruct((batch, N_HEADS, 2, D_HEAD), F32)]
    aliases = {}
    if o_prev is not None:
        in_specs.append(pl.BlockSpec(memory_space=pl.ANY))
        args.append(o_prev)
        aliases = {len(args) - 1: 0}

    def kern(*refs):
        refs = list(refs)
        if o_prev is not None:
            del refs[len(args) - 1]
        return _mlstm_kernel(*refs, t_len=t_len, has_init=has_init, emit_state=emit_state)

    return pl.pallas_call(
        kern,
        grid=(batch, N_HEADS),
        in_specs=in_specs,
        out_specs=out_specs,
        out_shape=out_shape,
        scratch_shapes=[pltpu.VMEM((t_len, D_HEAD), F32)],
        input_output_aliases=aliases,
        compiler_params=pltpu.CompilerParams(dimension_semantics=("arbitrary", "arbitrary"),
                                             vmem_limit_bytes=VMEM_LIMIT),
        name="mlstm_scan_t%d" % t_len,
    )(*args)


def _top2_of4(v):
    best, i1 = v[0], jnp.zeros(v[0].shape, jnp.int32)
    for j in range(1, 4):
        take = v[j] > best
        i1 = jnp.where(take, j, i1)
        best = jnp.where(take, v[j], best)
    best2, i2 = None, None
    for j in range(4):
        vj = jnp.where(i1 == j, -jnp.inf, v[j])
        if best2 is None:
            best2, i2 = vj, jnp.zeros(v[0].shape, jnp.int32)
        else:
            take = vj > best2
            i2 = jnp.where(take, j, i2)
            best2 = jnp.where(take, vj, best2)
    return i1, i2


def _pick(rows, idx):
    out = rows[0]
    for j in range(1, len(rows)):
        out = jnp.where(idx == j, rows[j], out)
    return out


def _outproj_kernel(x_ref, o_ref, w_ref, g1_ref, ng_ref, sc_ref, sh_ref, rw_ref, rb_ref, xo_ref, hn_ref, gt_ref):
    xn = x_ref[...] + g1_ref[0] * _dot(o_ref[...], w_ref[...])
    xo_ref[...] = xn
    hn = _rms(xn) * ng_ref[...]
    hn = hn * (1.0 + sc_ref[0]) + sh_ref[0]
    hn_ref[...] = hn.astype(BF16)

    logits = _dot_nt(rw_ref[...], hn, HIGHEST)
    ex = jnp.exp(logits - jnp.max(logits, axis=0, keepdims=True))
    probs = ex / jnp.sum(ex, axis=0, keepdims=True)
    sel = probs + rb_ref[...]
    sel_rows = [sel[e:e + 1, :] for e in range(N_EXPERTS)]
    prob_rows = [probs[e:e + 1, :] for e in range(N_EXPERTS)]
    scores = []
    for g in range(N_GROUPS):
        r = sel_rows[4 * g:4 * g + 4]
        a, b = jnp.maximum(r[0], r[1]), jnp.minimum(r[0], r[1])
        c, d = jnp.maximum(r[2], r[3]), jnp.minimum(r[2], r[3])
        scores.append(jnp.maximum(a, c) + jnp.maximum(jnp.minimum(a, c), jnp.maximum(b, d)))
    best, grp = scores[0], jnp.zeros(scores[0].shape, jnp.int32)
    for g in range(1, N_GROUPS):
        take = scores[g] > best
        grp = jnp.where(take, g, grp)
        best = jnp.where(take, scores[g], best)
    sel_in = [_pick([sel_rows[4 * g + j] for g in range(N_GROUPS)], grp) for j in range(4)]
    prob_in = [_pick([prob_rows[4 * g + j] for g in range(N_GROUPS)], grp) for j in range(4)]
    i1, i2 = _top2_of4(sel_in)
    w1, w2 = _pick(prob_in, i1), _pick(prob_in, i2)
    tot = w1 + w2
    eidx = lax.broadcasted_iota(jnp.int32, logits.shape, 0)
    gt_ref[...] = (jnp.where(eidx == grp * 4 + i1, w1 / tot, 0.0)
                   + jnp.where(eidx == grp * 4 + i2, w2 / tot, 0.0))


def _outproj_route(x, o, w_out, mods, mod_base, norm_g, router_w, router_bias, n_ctx, t_dec):
    n, d = x.shape
    tm = TM_PROJ
    mrow = _mod_row_of_tile(tm, n_ctx, t_dec)
    mod_map = lambda k: (lambda i: (mod_base + mrow(i) * 6 + k, 0, 0))
    const2 = lambda i: (0, 0)
    return pl.pallas_call(
        _outproj_kernel,
        grid=(n // tm,),
        in_specs=[
            pl.BlockSpec((tm, d), lambda i: (i, 0)),
            pl.BlockSpec((tm, d), lambda i: (i, 0)),
            pl.BlockSpec((d, d), const2),
            pl.BlockSpec((1, 1, d), mod_map(2)),
            pl.BlockSpec((1, d), const2),
            pl.BlockSpec((1, 1, d), mod_map(4)),
            pl.BlockSpec((1, 1, d), mod_map(3)),
            pl.BlockSpec((N_EXPERTS, d), const2),
            pl.BlockSpec((N_EXPERTS, 1), const2),
        ],
        out_specs=[
            pl.BlockSpec((tm, d), lambda i: (i, 0)),
            pl.BlockSpec((tm, d), lambda i: (i, 0)),
            pl.BlockSpec((N_EXPERTS, tm), lambda i: (0, i)),
        ],
        out_shape=[
            jax.ShapeDtypeStruct((n, d), F32),
            jax.ShapeDtypeStruct((n, d), BF16),
            jax.ShapeDtypeStruct((N_EXPERTS, n), F32),
        ],
        compiler_params=pltpu.CompilerParams(dimension_semantics=("arbitrary",), vmem_limit_bytes=VMEM_LIMIT),
        name="outproj_route",
    )(x, o, w_out.astype(BF16), mods, norm_g.reshape(1, d), mods, mods, router_w.T, router_bias.reshape(N_EXPERTS, 1))


def _moe_kernel(hn_ref, gate_ref, wg_ref, wu_ref, wd_ref, x_ref, g2_ref, fg_ref, o_ref, acc_ref, *, final_norm):
    e = pl.program_id(1)

    @pl.when(e == 0)
    def _():
        acc_ref[...] = jnp.zeros_like(acc_ref)

    h = hn_ref[...]
    hid = _silu(_dot(h, wg_ref[0])) * _dot(h, wu_ref[0])
    acc_ref[...] += gate_ref[0] * _dot(hid.astype(BF16), wd_ref[0])

    @pl.when(e == N_EXPERTS - 1)
    def _():
        xn = x_ref[...] + g2_ref[0] * acc_ref[...]
        if final_norm:
            xn = _rms(xn) * fg_ref[...]
        o_ref[...] = xn


def _moe(x, hn, gates_t, w_gate, w_up, w_down, mods, mod_base, final_g, final_norm, n_ctx, t_dec):
    n, d = x.shape
    tm = TM_MOE
    mrow = _mod_row_of_tile(tm, n_ctx, t_dec)
    return pl.pallas_call(
        functools.partial(_moe_kernel, final_norm=final_norm),
        grid=(n // tm, N_EXPERTS),
        in_specs=[
            pl.BlockSpec((tm, d), lambda i, e: (i, 0)),
            pl.BlockSpec((1, tm, 1), lambda i, e: (e, i, 0)),
            pl.BlockSpec((1, d, D_FF), lambda i, e: (e, 0, 0)),
            pl.BlockSpec((1, d, D_FF), lambda i, e: (e, 0, 0)),
            pl.BlockSpec((1, D_FF, d), lambda i, e: (e, 0, 0)),
            pl.BlockSpec((tm, d), lambda i, e: (i, 0)),
            pl.BlockSpec((1, 1, d), lambda i, e: (mod_base + mrow(i) * 6 + 5, 0, 0)),
            pl.BlockSpec((1, d), lambda i, e: (0, 0)),
        ],
        out_specs=pl.BlockSpec((tm, d), lambda i, e: (i, 0)),
        out_shape=jax.ShapeDtypeStruct((n, d), F32),
        scratch_shapes=[pltpu.VMEM((tm, d), F32)],
        compiler_params=pltpu.CompilerParams(dimension_semantics=("arbitrary", "arbitrary"),
                                             vmem_limit_bytes=VMEM_LIMIT),
        name="moe_dense",
    )(hn, gates_t.reshape(N_EXPERTS, n, 1), w_gate.astype(BF16), w_up.astype(BF16), w_down.astype(BF16),
      x, mods, final_g.reshape(1, d))


def _grid_pos_embed(n_tokens):
    rows = n_tokens // GRID_W
    r = jnp.repeat(jnp.arange(rows, dtype=F32), GRID_W)
    col = jnp.tile(jnp.arange(GRID_W, dtype=F32), rows)
    quarter = D_MODEL // 4
    freq = jnp.exp(jnp.arange(quarter, dtype=F32) * (-math.log(POS_BASE) / quarter))

    def axis_embed(pos):
        a = pos[:, None] * freq[None, :]
        return jnp.concatenate([jnp.sin(a), jnp.cos(a)], axis=-1)

    return jnp.concatenate([axis_embed(r), axis_embed(col)], axis=-1)


def _split_in_weights(w_in):
    pm = 4 * N_HEADS * D_HEAD
    wg = w_in[:, pm:].reshape(-1, 2, 2, N_HEADS)
    return w_in[:, :pm], wg.transpose(0, 3, 1, 2).reshape(-1, 4 * N_HEADS)


def _gate_layouts(gates, gates_t):
    n = gates.shape[0]
    gcol = gates.reshape(n, N_HEADS, 4).transpose(1, 0, 2)
    grow = gates_t.reshape(N_HEADS, 4, n // CHUNK, CHUNK).transpose(0, 2, 1, 3)
    return gcol, grow


def _head_params(first, second):
    return jnp.stack([first, second], axis=0).transpose(2, 0, 1).reshape(-1).astype(F32)


def kernel(x_prompt, x_sample, state_gdn_S, state_mlstm_C, state_mlstm_n, state_mlstm_m, c, c_ctx, ada_w, ada_b,
           norm1_g, norm2_g, gdn_w_in, gdn_conv_w, gdn_a_log, gdn_dt_bias, gdn_norm_g, gdn_w_out, mlstm_w_in,
           mlstm_gate_b, mlstm_norm_g, mlstm_w_out, router_w, router_bias, exp_w_gate, exp_w_up, exp_w_down,
           final_norm_g):
    bp, tp, d = x_prompt.shape
    bs, ts, _ = x_sample.shape
    n_ctx = bp * tp
    depth = ada_w.shape[0]
    assert n_ctx % ts == 0 and ts % TM_MOE == 0 and tp % TM_PROJ == 0 and bs + 1 <= N_MOD_ROWS

    pos = _grid_pos_embed(ts).astype(F32)
    x = jnp.concatenate([x_prompt.reshape(n_ctx, d), (x_sample + pos[None]).reshape(bs * ts, d)], axis=0)

    conds = jnp.concatenate([c_ctx[None, :], c, jnp.zeros((N_MOD_ROWS - 1 - bs, d), F32)], axis=0)
    mods = _ada_mods(conds, ada_w, ada_b).reshape(depth * N_MOD_ROWS * 6, 1, d)

    zeros_dh = jnp.zeros_like(gdn_a_log[0])
    outs = {}
    for layer in range(depth):
        j = layer // 2
        mod_base = layer * N_MOD_ROWS * 6
        if layer % 2 == 0:
            w_main, w_gate = _split_in_weights(gdn_w_in[j])
            bias = _head_params(gdn_dt_bias[j], zeros_dh)
            mul = _head_params(-jnp.exp(gdn_a_log[j].astype(F32)), zeros_dh)
            main, gates, gates_t = _inproj(x, mods, mod_base, norm1_g[layer], w_main, w_gate, bias, mul, "gdn",
                                           n_ctx, ts)
            gcol, grow = _gate_layouts(gates, gates_t)
            o, s_new = _gdn_scan(main, gcol, grow, gdn_conv_w[j], gdn_norm_g[j], None, None,
                                 batch=bp, t_len=tp, row0=0, emit_state=True)
            (o,) = _gdn_scan(main, gcol, grow, gdn_conv_w[j], gdn_norm_g[j], state_gdn_S[:, j], o,
                             batch=bs, t_len=ts, row0=n_ctx, emit_state=False)
            outs.setdefault("gdn", []).append(s_new)
            w_out = gdn_w_out[j]
        else:
            w_main, w_gate = _split_in_weights(mlstm_w_in[j])
            bias = _head_params(mlstm_gate_b[j, 0], mlstm_gate_b[j, 1])
            main, gates, gates_t = _inproj(x, mods, mod_base, norm1_g[layer], w_main, w_gate, bias,
                                           jnp.zeros_like(bias), "mlstm", n_ctx, ts)
            gcol, grow = _gate_layouts(gates, gates_t)
            o, c_new, n_new, m_new = _mlstm_scan(main, gcol, grow, mlstm_norm_g[j], None, None,
                                                 batch=bp, t_len=tp, row0=0, emit_state=True)
            init = (state_mlstm_C[:, j].astype(F32),
                    state_mlstm_n[:, j].astype(F32).transpose(0, 2, 1, 3),
                    jnp.broadcast_to(state_mlstm_m[:, j].astype(F32).transpose(0, 2, 1)[..., None],
                                     (bs, N_HEADS, 2, D_HEAD)))
            (o,) = _mlstm_scan(main, gcol, grow, mlstm_norm_g[j], init, o,
                               batch=bs, t_len=ts, row0=n_ctx, emit_state=False)
            outs.setdefault("mC", []).append(c_new)
            outs.setdefault("mn", []).append(n_new.transpose(0, 2, 1, 3))
            outs.setdefault("mm", []).append(m_new[..., 0].transpose(0, 2, 1))
            w_out = mlstm_w_out[j]
        x, hn, gates_t = _outproj_route(x, o, w_out, mods, mod_base, norm2_g[layer], router_w, router_bias, n_ctx, ts)
        x = _moe(x, hn, gates_t, exp_w_gate[layer], exp_w_up[layer], exp_w_down[layer], mods, mod_base,
                 final_norm_g, layer == depth - 1, n_ctx, ts)

    y_prompt = x[:n_ctx].reshape(bp, tp, d)
    y_sample = x[n_ctx:].reshape(bs, ts, d)
    return (y_prompt, y_sample, jnp.stack(outs["gdn"], axis=1), jnp.stack(outs["mC"], axis=1),
            jnp.stack(outs["mn"], axis=1), jnp.stack(outs["mm"], axis=1))
```

```python
import functools
import math

import jax
import jax.numpy as jnp
from jax import lax
from jax.experimental import pallas as pl
from jax.experimental.pallas import tpu as pltpu

F32 = jnp.float32
BF16 = jnp.bfloat16
HIGHEST = lax.Precision.HIGHEST

D_MODEL = 1024
N_HEADS = 8
D_HEAD = 128
CHUNK = 64
N_EXPERTS = 16
N_GROUPS = 4
EXP_PER_GROUP = 4
D_FF = 512
EPS = 1e-6
GRID_W = 64
POS_BASE = 10000.0
N_MOD_ROWS = 8
VMEM_LIMIT = 56 * 1024 * 1024

TM_PROJ = 256
TM_MOE = 512
SCAN_CHAINS = 32


def _silu(x):
    return x * jax.nn.sigmoid(x)


def _softplus(x):
    return jnp.maximum(x, 0.0) + jnp.log(1.0 + jnp.exp(-jnp.abs(x)))


def _dot(a, b, precision=None):
    return jnp.dot(a, b, preferred_element_type=F32, precision=precision)


def _dot_nt(a, b, precision=None):
    return lax.dot_general(a, b, (((1,), (1,)), ((), ())), preferred_element_type=F32, precision=precision)


def _dot_tn(a, b, precision=None):
    return lax.dot_general(a, b, (((0,), (0,)), ((), ())), preferred_element_type=F32, precision=precision)


def _rms(x):
    return x * lax.rsqrt(jnp.mean(x * x, axis=-1, keepdims=True) + EPS)


def _ada_kernel(c_ref, w_ref, b_ref, o_ref):
    cs = _silu(c_ref[...]).astype(BF16)
    o_ref[0] = _dot(cs, w_ref[0].astype(BF16)) + b_ref[0]


def _ada_mods(conds, ada_w, ada_b):
    depth, d, n6 = ada_w.shape
    tn = 1536
    return pl.pallas_call(
        _ada_kernel,
        grid=(depth, n6 // tn),
        in_specs=[
            pl.BlockSpec((N_MOD_ROWS, d), lambda l, j: (0, 0)),
            pl.BlockSpec((1, d, tn), lambda l, j: (l, 0, j)),
            pl.BlockSpec((1, 1, tn), lambda l, j: (l, 0, j)),
        ],
        out_specs=pl.BlockSpec((1, N_MOD_ROWS, tn), lambda l, j: (l, 0, j)),
        out_shape=jax.ShapeDtypeStruct((depth, N_MOD_ROWS, n6), F32),
        compiler_params=pltpu.CompilerParams(dimension_semantics=("arbitrary", "arbitrary"),
                                             vmem_limit_bytes=VMEM_LIMIT),
        name="ada_mods",
    )(conds, ada_w, ada_b.reshape(depth, 1, n6))


def _split3(x):
    hi = x.astype(BF16)
    r1 = x - hi.astype(F32)
    mid = r1.astype(BF16)
    lo = (r1 - mid.astype(F32)).astype(BF16)
    return hi, mid, lo


def _inproj_kernel(x_ref, ng_ref, sc_ref, sh_ref, w_ref, wg_ref, wgt_ref, gb_r_ref, gm_r_ref, gb_c_ref,
                   gm_c_ref, main_ref, g_ref, gt_ref, *, kind):
    x = x_ref[...]
    hn = _rms(x) * ng_ref[...]
    hn = hn * (1.0 + sc_ref[0]) + sh_ref[0]
    hb = hn.astype(BF16)
    main_ref[...] = _dot(hb, w_ref[...])

    def activate(pre, bias, mul, first):
        z = pre + bias
        if kind == "gdn":
            return jnp.where(first, mul * _softplus(z), jax.nn.sigmoid(z))
        return jnp.where(first, z, -_softplus(-z))

    tm = x.shape[0]
    r = lax.broadcasted_iota(jnp.int32, (tm, tm), 0)
    c = lax.broadcasted_iota(jnp.int32, (tm, tm), 1)
    same = (r // CHUNK) == (c // CHUNK)
    lower = jnp.where(same, jnp.where(r >= c, 1.0, 0.0), 0.0).astype(BF16)
    upper = jnp.where(same, jnp.where(r <= c, 1.0, 0.0), 0.0).astype(BF16)
    scanned_kind = 0 if kind == "gdn" else 1

    pre = _dot(hb, wg_ref[...])
    col = lax.broadcasted_iota(jnp.int32, pre.shape, 1)
    act = activate(pre, gb_r_ref[...], gm_r_ref[...], (col % 4) < 2)
    parts = _split3(act)
    fwd = sum(_dot(lower, p) for p in parts)
    bwd = sum(_dot(upper, p) for p in parts)
    g_ref[...] = jnp.where((col % 4) // 2 == scanned_kind, jnp.where(col % 2 == 0, fwd, bwd), act)

    pre_t = _dot_nt(wgt_ref[...], hb)
    row = lax.broadcasted_iota(jnp.int32, pre_t.shape, 0)
    act_t = activate(pre_t, gb_c_ref[...], gm_c_ref[...], (row % 4) < 2)
    parts = _split3(act_t)
    fwd = sum(_dot(p, upper) for p in parts)
    bwd = sum(_dot(p, lower) for p in parts)
    gt_ref[...] = jnp.where((row % 4) // 2 == scanned_kind, jnp.where(row % 2 == 0, fwd, bwd), act_t)


def _mod_row_of_tile(tm, n_ctx, t_dec):
    def f(i):
        r = i * tm
        return jnp.where(r < n_ctx, 0, 1 + (r - n_ctx) // t_dec)
    return f


def _inproj(x, mods, mod_base, norm_g, w_main, w_gate, gate_bias, gate_mul, kind, n_ctx, t_dec):
    n, d = x.shape
    tm = TM_PROJ
    pm = w_main.shape[1]
    ng = w_gate.shape[1]
    mrow = _mod_row_of_tile(tm, n_ctx, t_dec)
    sc_map = lambda i: (mod_base + mrow(i) * 6 + 1, 0, 0)
    sh_map = lambda i: (mod_base + mrow(i) * 6 + 0, 0, 0)
    const2 = lambda i: (0, 0)
    return pl.pallas_call(
        functools.partial(_inproj_kernel, kind=kind),
        grid=(n // tm,),
        in_specs=[
            pl.BlockSpec((tm, d), lambda i: (i, 0)),
            pl.BlockSpec((1, d), const2),
            pl.BlockSpec((1, 1, d), sc_map),
            pl.BlockSpec((1, 1, d), sh_map),
            pl.BlockSpec((d, pm), const2),
            pl.BlockSpec((d, ng), const2),
            pl.BlockSpec((ng, d), const2),
            pl.BlockSpec((1, ng), const2),
            pl.BlockSpec((1, ng), const2),
            pl.BlockSpec((ng, 1), const2),
            pl.BlockSpec((ng, 1), const2),
        ],
        out_specs=[
            pl.BlockSpec((tm, pm), lambda i: (i, 0)),
            pl.BlockSpec((tm, ng), lambda i: (i, 0)),
            pl.BlockSpec((ng, tm), lambda i: (0, i)),
        ],
        out_shape=[
            jax.ShapeDtypeStruct((n, pm), F32),
            jax.ShapeDtypeStruct((n, ng), F32),
            jax.ShapeDtypeStruct((ng, n), F32),
        ],
        compiler_params=pltpu.CompilerParams(dimension_semantics=("arbitrary",), vmem_limit_bytes=VMEM_LIMIT),
        name="inproj_" + kind,
    )(x, norm_g.reshape(1, d), mods, mods, w_main.astype(BF16), w_gate.astype(BF16), w_gate.T.astype(BF16),
      gate_bias.reshape(1, ng), gate_mul.reshape(1, ng), gate_bias.reshape(ng, 1), gate_mul.reshape(ng, 1))


def _chunk_masks():
    r = lax.broadcasted_iota(jnp.int32, (CHUNK, CHUNK), 0)
    c = lax.broadcasted_iota(jnp.int32, (CHUNK, CHUNK), 1)
    return r >= c, r > c, r <= c, r < c


def _chunk_rows(c):
    if isinstance(c, int):
        return pl.ds(c * CHUNK, CHUNK)
    return pl.ds(pl.multiple_of(c * CHUNK, CHUNK), CHUNK)


def _head_cols(h):
    return slice(h * D_HEAD, (h + 1) * D_HEAD)


def _last_row(x, d):
    return x[CHUNK - 1:CHUNK, :] if d == 0 else x[0:1, :]


def _lanes(col):
    return jnp.broadcast_to(col, (CHUNK, D_HEAD))


def _unit_tri_solve(a_list, rhs_list):
    mm = lambda x, y: _dot(x.astype(BF16), y.astype(BF16))
    r = lax.broadcasted_iota(jnp.int32, (CHUNK, CHUNK), 0)
    c = lax.broadcasted_iota(jnp.int32, (CHUNK, CHUNK), 1)
    same = (r // 16) == (c // 16)
    eye = jnp.where(r == c, 1.0, 0.0)
    d = [jnp.where(same, a, 0.0) for a in a_list]
    t = [eye - di for di in d]
    for _ in range(3):
        d = [mm(di, di) for di in d]
        t = [ti + mm(ti, di) for ti, di in zip(t, d)]
    b = [mm(ti, jnp.where(same, 0.0, a)) for ti, a in zip(t, a_list)]
    x = [mm(ti, ri) for ti, ri in zip(t, rhs_list)]
    b2 = [mm(bi, bi) for bi in b]
    x = [xi - mm(bi, xi) for xi, bi in zip(x, b)]
    return [xi + mm(bi, xi) for xi, bi in zip(x, b2)]


def _for_chunk_groups(nc, group, fn):
    if nc == group:
        fn(list(range(nc)))
        return

    def body(g, carry):
        fn([g * group + j for j in range(group)])
        return carry
    lax.fori_loop(0, nc // group, body, 0)


def _scan_specs(t_len, row0, hb, n_proj):
    rb = row0 // t_len
    nc = t_len // CHUNK
    ngrp = N_HEADS // hb
    specs = [pl.BlockSpec((t_len, hb * D_HEAD), functools.partial(lambda b, g, j: (rb + b, j * ngrp + g), j=j))
             for j in range(n_proj)]
    specs.append(pl.BlockSpec((hb, t_len, 4), lambda b, g: (g, rb + b, 0)))
    specs.append(pl.BlockSpec((hb, nc, 4, CHUNK), lambda b, g: (g, rb + b, 0, 0)))
    return specs


def _scan_call(kern, name, args, in_specs, out_specs, out_shape, scratch, o_prev, batch, hb):
    aliases = {}
    n_in = len(args)
    if o_prev is not None:
        in_specs = in_specs + [pl.BlockSpec(memory_space=pl.ANY)]
        args = args + [o_prev]
        aliases = {n_in: 0}

    def body(*refs):
        refs = list(refs)
        if o_prev is not None:
            del refs[n_in]
        return kern(*refs)

    return pl.pallas_call(
        body,
        grid=(batch, N_HEADS // hb),
        in_specs=in_specs,
        out_specs=out_specs,
        out_shape=out_shape,
        scratch_shapes=scratch,
        input_output_aliases=aliases,
        compiler_params=pltpu.CompilerParams(dimension_semantics=("arbitrary", "arbitrary"),
                                             vmem_limit_bytes=VMEM_LIMIT),
        name=name,
    )(*args)


def _gdn_kernel(*refs, t_len, hb, has_init, emit_state):
    it = iter(refs)
    q_ref, k_ref, v_ref, z_ref, gc_ref, gr_ref, cwq_ref, cwk_ref, cwv_ref, ng_ref = (next(it) for _ in range(10))
    s0_ref = next(it) if has_init else None
    o_ref = next(it)
    s_ref = next(it) if emit_state else None
    qs, ks, vs, oacc, qp_s, op_s, km_s, nm_s, ge_s = (next(it) for _ in range(9))
    nc = t_len // CHUNK

    rows = lax.broadcasted_iota(jnp.int32, (t_len, 1), 0)

    def conv_silu(x, cw):
        xm = jnp.where(rows == 0, 0.0, pltpu.roll(x, 1, axis=0))
        xp = jnp.where(rows == t_len - 1, 0.0, pltpu.roll(x, t_len - 1, axis=0))
        return _silu(xm * cw[0:1] + x * cw[1:2] + xp * cw[2:3])

    def l2n(x):
        return x * lax.rsqrt(jnp.sum(x * x, axis=-1, keepdims=True) + EPS)

    for h in range(hb):
        hc = _head_cols(h)
        qs[:, hc] = l2n(conv_silu(q_ref[:, hc], cwq_ref[:, hc])) * (D_HEAD ** -0.5)
        ks[:, hc] = l2n(conv_silu(k_ref[:, hc], cwk_ref[:, hc]))
        vs[:, hc] = conv_silu(v_ref[:, hc], cwv_ref[:, hc])
    oacc[...] = jnp.zeros_like(oacc)

    lo_i, lo_s, up_i, up_s = _chunk_masks()

    def slot(h, d, c):
        return (h * 2 + d) * nc + c

    def intra(chunks):
        items = [(h, c, d) for h in range(hb) for c in chunks for d in range(2)]
        a_list, rhs_list, keep = [], [], []
        for h, c, d in items:
            sl, hc = _chunk_rows(c), _head_cols(h)
            q, k, v = qs[sl, hc], ks[sl, hc], vs[sl, hc]
            gc4 = gc_ref[h, sl, :]
            gr4 = gr_ref[h, c]
            g_col, beta = _lanes(gc4[:, d:d + 1]), _lanes(gc4[:, 2 + d:3 + d])
            incl, strict = (lo_i, lo_s) if d == 0 else (up_i, up_s)
            decay = jnp.exp(jnp.where(incl, g_col[:, :CHUNK] - gr4[d:d + 1, :], -jnp.inf))
            kb = k * beta
            k16 = k.astype(BF16)
            a_list.append(jnp.where(strict, _dot_nt(kb.astype(BF16), k16) * decay, 0.0))
            e_g = jnp.exp(g_col)
            rhs_list.append(jnp.concatenate([v * beta, kb * e_g], axis=1))
            qk16 = jnp.where(incl, _dot_nt(q.astype(BF16), k16) * decay, 0.0).astype(BF16)
            g_last = _last_row(g_col, d)
            ge_s[slot(h, d, c)] = jnp.broadcast_to(jnp.exp(g_last), (8, D_HEAD))
            keep.append((qk16, q * e_g, (k * jnp.exp(g_last - g_col)).astype(BF16)))
        sols = [s.astype(BF16) for s in _unit_tri_solve(a_list, rhs_list)]
        qw = [_dot(kp[0], s) for kp, s in zip(keep, sols)]
        kw = [_dot_tn(kp[2], s) for kp, s in zip(keep, sols)]
        for (h, c, d), kp, qwi, kwi in zip(items, keep, qw, kw):
            i = h * 2 + d
            sl = _chunk_rows(c)
            op_s[i, sl, :] = qwi[:, :D_HEAD]
            qp_s[i, sl, :] = (kp[1] - qwi[:, D_HEAD:]).astype(BF16)
            nm_s[slot(h, d, c)] = kwi[:, :D_HEAD]
            km_s[slot(h, d, c)] = (-kwi[:, D_HEAD:]).astype(BF16)

    _for_chunk_groups(nc, min(nc, max(1, SCAN_CHAINS // (2 * hb))), intra)

    chains = [(h, d) for h in range(hb) for d in range(2)]

    def inter_body(i, carry):
        cs = (i, nc - 1 - i)
        s16 = [s.astype(BF16) for s in carry]
        outs = [_dot(qp_s[h * 2 + d, _chunk_rows(cs[d]), :], s) for (h, d), s in zip(chains, s16)]
        upds = [_dot(km_s[slot(h, d, cs[d])], s) for (h, d), s in zip(chains, s16)]
        for (h, d), o in zip(chains, outs):
            oacc[_chunk_rows(cs[d]), _head_cols(h)] += o + op_s[h * 2 + d, _chunk_rows(cs[d]), :]
        return tuple(ge_s[slot(h, d, cs[d])][0:1, :] * s + u + nm_s[slot(h, d, cs[d])]
                     for (h, d), s, u in zip(chains, carry, upds))

    if has_init:
        init = tuple(s0_ref[0, d, h] for h, d in chains)
    else:
        init = tuple(jnp.zeros((D_HEAD, D_HEAD), F32) for _ in chains)
    fin = lax.fori_loop(0, nc, inter_body, init)
    if emit_state:
        for (h, d), s in zip(chains, fin):
            s_ref[0, d, h] = s
    for h in range(hb):
        hc = _head_cols(h)
        o_ref[:, hc] = (_rms(oacc[:, hc]) * ng_ref[...] * _silu(z_ref[:, hc])).astype(o_ref.dtype)


def _gdn_scan(main, gcol, grow, conv_w, norm_g, s0, o_prev, *, batch, t_len, row0, hb, emit_state):
    n = main.shape[0]
    nc = t_len // CHUNK
    ngrp = N_HEADS // hb
    has_init = s0 is not None
    in_specs = _scan_specs(t_len, row0, hb, 4)
    in_specs += [pl.BlockSpec((3, hb * D_HEAD), functools.partial(lambda b, g, j: (0, j * ngrp + g), j=j))
                 for j in range(3)]
    in_specs.append(pl.BlockSpec((1, D_HEAD), lambda b, g: (0, 0)))
    args = [main, main, main, main, gcol, grow, conv_w, conv_w, conv_w, norm_g.reshape(1, D_HEAD)]
    state_spec = pl.BlockSpec((1, 2, hb, D_HEAD, D_HEAD), lambda b, g: (b, 0, g, 0, 0))
    if has_init:
        in_specs.append(state_spec)
        args.append(s0)
    rb = row0 // t_len
    out_specs = [pl.BlockSpec((t_len, hb * D_HEAD), lambda b, g: (rb + b, g))]
    out_shape = [jax.ShapeDtypeStruct((n, N_HEADS * D_HEAD), BF16)]
    if emit_state:
        out_specs.append(state_spec)
        out_shape.append(jax.ShapeDtypeStruct((batch, 2, N_HEADS, D_HEAD, D_HEAD), F32))
    scratch = ([pltpu.VMEM((t_len, hb * D_HEAD), F32) for _ in range(4)]
               + [pltpu.VMEM((2 * hb, t_len, D_HEAD), BF16),
                  pltpu.VMEM((2 * hb, t_len, D_HEAD), F32),
                  pltpu.VMEM((2 * hb * nc, D_HEAD, D_HEAD), BF16),
                  pltpu.VMEM((2 * hb * nc, D_HEAD, D_HEAD), F32),
                  pltpu.VMEM((2 * hb * nc, 8, D_HEAD), F32)])
    kern = functools.partial(_gdn_kernel, t_len=t_len, hb=hb, has_init=has_init, emit_state=emit_state)
    return _scan_call(kern, "gdn_scan_t%d" % t_len, args, in_specs, out_specs, out_shape, scratch, o_prev, batch, hb)


def _mlstm_kernel(*refs, t_len, hb, has_init, emit_state):
    it = iter(refs)
    q_ref, k_ref, v_ref, og_ref, gc_ref, gr_ref, ng_ref = (next(it) for _ in range(7))
    if has_init:
        c0_ref, n0_ref, m0_ref = (next(it) for _ in range(3))
    o_ref = next(it)
    if emit_state:
        c_ref, n_ref, m_ref = (next(it) for _ in range(3))
    hacc, q16_s, v1_s, qk_s, ld_s, bb_s, lm_s, lw_s, sc_s = (next(it) for _ in range(9))
    nc = t_len // CHUNK

    hacc[...] = jnp.zeros_like(hacc)
    ones = jnp.ones((t_len, D_HEAD), BF16)
    for h in range(hb):
        hc = _head_cols(h)
        q16_s[:, hc] = (q_ref[:, hc] * (D_HEAD ** -0.5)).astype(BF16)
        v1_s[:, h * 2 * D_HEAD:(h * 2 + 1) * D_HEAD] = v_ref[:, hc].astype(BF16)
        v1_s[:, (h * 2 + 1) * D_HEAD:(h * 2 + 2) * D_HEAD] = ones
    lo_i, _, up_i, _ = _chunk_masks()

    def slot(h, d, c):
        return (h * 2 + d) * nc + c

    def intra(chunks):
        for h in range(hb):
            for c in chunks:
                sl, hc = _chunk_rows(c), _head_cols(h)
                qk_s[h, sl, :] = _dot_nt(q16_s[sl, hc], k_ref[sl, hc].astype(BF16))
                gc4 = gc_ref[h, sl, :]
                gr4 = gr_ref[h, c]
                for d in range(2):
                    i = h * 2 + d
                    b_col, i_col = _lanes(gc4[:, 2 + d:3 + d]), _lanes(gc4[:, d:d + 1])
                    b_last = _last_row(b_col, d)
                    lwe = b_last - b_col + i_col
                    log_d = jnp.where(lo_i if d == 0 else up_i,
                                      b_col[:, :CHUNK] - gr4[2 + d:3 + d, :] + gr4[d:d + 1, :], -jnp.inf)
                    ld_s[i, sl, :] = log_d
                    lm_s[i, sl, :] = _lanes(jnp.max(log_d, axis=-1, keepdims=True))
                    bb_s[i, sl, :] = b_col
                    lw_s[i, sl, :] = lwe
                    sc_s[slot(h, d, c), 0:1, :] = b_last
                    sc_s[slot(h, d, c), 1:2, :] = jnp.max(lwe, axis=0, keepdims=True)

    _for_chunk_groups(nc, min(nc, 4), intra)

    chains = [(h, d) for h in range(hb) for d in range(2)]

    def body(step, carry):
        cs = (step, nc - 1 - step)
        state, m = zip(*carry)
        sls = [_chunk_rows(cs[d]) for h, d in chains]
        idx = [h * 2 + d for h, d in chains]
        v1 = [v1_s[sl, h * 2 * D_HEAD:(h * 2 + 2) * D_HEAD] for (h, d), sl in zip(chains, sls)]
        log_last = [sc_s[slot(h, d, cs[d]), 0:1, :] + mi for (h, d), mi in zip(chains, m)]
        m_new = [jnp.maximum(ll, sc_s[slot(h, d, cs[d]), 1:2, :]) for (h, d), ll in zip(chains, log_last)]
        dec = [jnp.exp(ll - mn) for ll, mn in zip(log_last, m_new)]
        kw = [(k_ref[sl, _head_cols(h)] * jnp.exp(lw_s[i, sl, :] - mn)).astype(BF16)
              for (h, d), sl, i, mn in zip(chains, sls, idx, m_new)]
        upd = [_dot_tn(a, b) for a, b in zip(kw, v1)]
        qc = [_dot(q16_s[sl, _head_cols(h)], s.astype(BF16)) for (h, d), sl, s in zip(chains, sls, state)]
        log_inter = [bb_s[i, sl, :] + mi for i, sl, mi in zip(idx, sls, m)]
        mt = [jnp.maximum(li, lm_s[i, sl, :]) for li, i, sl in zip(log_inter, idx, sls)]
        s_inter = [jnp.exp(li - t) for li, t in zip(log_inter, mt)]
        p = [(jnp.exp(ld_s[i, sl, :] - t[:, :CHUNK]) * qk_s[h, sl, :]).astype(BF16)
             for (h, d), i, sl, t in zip(chains, idx, sls, mt)]
        pv = [_dot(a, b) for a, b in zip(p, v1)]
        for (h, d), sl, si, qci, pvi, t in zip(chains, sls, s_inter, qc, pv, mt):
            num = si * qci[:, :D_HEAD] + pvi[:, :D_HEAD]
            den = si * qci[:, D_HEAD:] + pvi[:, D_HEAD:]
            hacc[sl, _head_cols(h)] += num / jnp.maximum(jnp.abs(den), jnp.exp(-t))
        return tuple((jnp.concatenate([dc, dc], axis=1) * s + u, mn)
                     for dc, s, u, mn in zip(dec, state, upd, m_new))

    if has_init:
        init = tuple((jnp.concatenate([c0_ref[0, d, h],
                                       jnp.transpose(jnp.broadcast_to(n0_ref[0, h, d:d + 1, :], (D_HEAD, D_HEAD)))],
                                      axis=1),
                      m0_ref[0, h, d:d + 1, :]) for h, d in chains)
    else:
        init = tuple((jnp.zeros((D_HEAD, 2 * D_HEAD), F32), jnp.zeros((1, D_HEAD), F32)) for _ in chains)
    fin = lax.fori_loop(0, nc, body, init)
    if emit_state:
        for (h, d), (s, m) in zip(chains, fin):
            c_ref[0, d, h] = s[:, :D_HEAD]
            n_ref[0, h, d:d + 1, :] = jnp.transpose(s[:, D_HEAD:])[0:1, :]
            m_ref[0, h, d:d + 1, :] = m
    for h in range(hb):
        hc = _head_cols(h)
        o_ref[:, hc] = (_rms(hacc[:, hc]) * ng_ref[...] * jax.nn.sigmoid(og_ref[:, hc])).astype(o_ref.dtype)


def _mlstm_scan(main, gcol, grow, norm_g, init, o_prev, *, batch, t_len, row0, hb, emit_state):
    n = main.shape[0]
    nc = t_len // CHUNK
    has_init = init is not None
    in_specs = _scan_specs(t_len, row0, hb, 4)
    in_specs.append(pl.BlockSpec((1, D_HEAD), lambda b, g: (0, 0)))
    args = [main, main, main, main, gcol, grow, norm_g.reshape(1, D_HEAD)]
    c_spec = pl.BlockSpec((1, 2, hb, D_HEAD, D_HEAD), lambda b, g: (b, 0, g, 0, 0))
    v_spec = pl.BlockSpec((1, hb, 2, D_HEAD), lambda b, g: (b, g, 0, 0))
    if has_init:
        in_specs += [c_spec, v_spec, v_spec]
        args += list(init)
    rb = row0 // t_len
    out_specs = [pl.BlockSpec((t_len, hb * D_HEAD), lambda b, g: (rb + b, g))]
    out_shape = [jax.ShapeDtypeStruct((n, N_HEADS * D_HEAD), BF16)]
    if emit_state:
        out_specs += [c_spec, v_spec, v_spec]
        out_shape += [jax.ShapeDtypeStruct((batch, 2, N_HEADS, D_HEAD, D_HEAD), F32),
                      jax.ShapeDtypeStruct((batch, N_HEADS, 2, D_HEAD), F32),
                      jax.ShapeDtypeStruct((batch, N_HEADS, 2, D_HEAD), F32)]
    scratch = [pltpu.VMEM((t_len, hb * D_HEAD), F32),
               pltpu.VMEM((t_len, hb * D_HEAD), BF16),
               pltpu.VMEM((t_len, hb * 2 * D_HEAD), BF16),
               pltpu.VMEM((hb, t_len, CHUNK), F32),
               pltpu.VMEM((2 * hb, t_len, CHUNK), F32),
               pltpu.VMEM((2 * hb, t_len, D_HEAD), F32),
               pltpu.VMEM((2 * hb, t_len, D_HEAD), F32),
               pltpu.VMEM((2 * hb, t_len, D_HEAD), F32),
               pltpu.VMEM((2 * hb * nc, 8, D_HEAD), F32)]
    kern = functools.partial(_mlstm_kernel, t_len=t_len, hb=hb, has_init=has_init, emit_state=emit_state)
    return _scan_call(kern, "mlstm_scan_t%d" % t_len, args, in_specs, out_specs, out_shape, scratch, o_prev, batch,
                      hb)


def _top2_of4(v):
    best, i1 = v[0], jnp.zeros(v[0].shape, jnp.int32)
    for j in range(1, 4):
        take = v[j] > best
        i1 = jnp.where(take, j, i1)
        best = jnp.where(take, v[j], best)
    best2, i2 = None, None
    for j in range(4):
        vj = jnp.where(i1 == j, -jnp.inf, v[j])
        if best2 is None:
            best2, i2 = vj, jnp.zeros(v[0].shape, jnp.int32)
        else:
            take = vj > best2
            i2 = jnp.where(take, j, i2)
            best2 = jnp.where(take, vj, best2)
    return i1, i2


def _pick(rows, idx):
    out = rows[0]
    for j in range(1, len(rows)):
        out = jnp.where(idx == j, rows[j], out)
    return out


def _outproj_kernel(x_ref, o_ref, w_ref, g1_ref, ng_ref, sc_ref, sh_ref, rw_ref, rb_ref, xo_ref, hn_ref, gt_ref):
    xn = x_ref[...] + g1_ref[0] * _dot(o_ref[...], w_ref[...])
    xo_ref[...] = xn
    hn = _rms(xn) * ng_ref[...]
    hn = hn * (1.0 + sc_ref[0]) + sh_ref[0]
    hn_ref[...] = hn.astype(BF16)

    logits = _dot_nt(rw_ref[...], hn, HIGHEST)
    ex = jnp.exp(logits - jnp.max(logits, axis=0, keepdims=True))
    probs = ex / jnp.sum(ex, axis=0, keepdims=True)
    sel = probs + rb_ref[...]
    sel_rows = [sel[e:e + 1, :] for e in range(N_EXPERTS)]
    prob_rows = [probs[e:e + 1, :] for e in range(N_EXPERTS)]
    scores = []
    for g in range(N_GROUPS):
        r = sel_rows[4 * g:4 * g + 4]
        a, b = jnp.maximum(r[0], r[1]), jnp.minimum(r[0], r[1])
        c, d = jnp.maximum(r[2], r[3]), jnp.minimum(r[2], r[3])
        scores.append(jnp.maximum(a, c) + jnp.maximum(jnp.minimum(a, c), jnp.maximum(b, d)))
    best, grp = scores[0], jnp.zeros(scores[0].shape, jnp.int32)
    for g in range(1, N_GROUPS):
        take = scores[g] > best
        grp = jnp.where(take, g, grp)
        best = jnp.where(take, scores[g], best)
    sel_in = [_pick([sel_rows[4 * g + j] for g in range(N_GROUPS)], grp) for j in range(4)]
    prob_in = [_pick([prob_rows[4 * g + j] for g in range(N_GROUPS)], grp) for j in range(4)]
    i1, i2 = _top2_of4(sel_in)
    w1, w2 = _pick(prob_in, i1), _pick(prob_in, i2)
    tot = w1 + w2
    eidx = lax.broadcasted_iota(jnp.int32, logits.shape, 0)
    gt_ref[...] = (jnp.where(eidx == grp * 4 + i1, w1 / tot, 0.0)
                   + jnp.where(eidx == grp * 4 + i2, w2 / tot, 0.0))


def _outproj_route(x, o, w_out, mods, mod_base, norm_g, router_w, router_bias, n_ctx, t_dec):
    n, d = x.shape
    tm = TM_PROJ
    mrow = _mod_row_of_tile(tm, n_ctx, t_dec)
    mod_map = lambda k: (lambda i: (mod_base + mrow(i) * 6 + k, 0, 0))
    const2 = lambda i: (0, 0)
    return pl.pallas_call(
        _outproj_kernel,
        grid=(n // tm,),
        in_specs=[
            pl.BlockSpec((tm, d), lambda i: (i, 0)),
            pl.BlockSpec((tm, d), lambda i: (i, 0)),
            pl.BlockSpec((d, d), const2),
            pl.BlockSpec((1, 1, d), mod_map(2)),
            pl.BlockSpec((1, d), const2),
            pl.BlockSpec((1, 1, d), mod_map(4)),
            pl.BlockSpec((1, 1, d), mod_map(3)),
            pl.BlockSpec((N_EXPERTS, d), const2),
            pl.BlockSpec((N_EXPERTS, 1), const2),
        ],
        out_specs=[
            pl.BlockSpec((tm, d), lambda i: (i, 0)),
            pl.BlockSpec((tm, d), lambda i: (i, 0)),
            pl.BlockSpec((N_EXPERTS, tm), lambda i: (0, i)),
        ],
        out_shape=[
            jax.ShapeDtypeStruct((n, d), F32),
            jax.ShapeDtypeStruct((n, d), BF16),
            jax.ShapeDtypeStruct((N_EXPERTS, n), F32),
        ],
        compiler_params=pltpu.CompilerParams(dimension_semantics=("arbitrary",), vmem_limit_bytes=VMEM_LIMIT),
        name="outproj_route",
    )(x, o, w_out.astype(BF16), mods, norm_g.reshape(1, d), mods, mods, router_w.T, router_bias.reshape(N_EXPERTS, 1))


def _moe_kernel(hn_ref, gate_ref, wg_ref, wu_ref, wd_ref, x_ref, g2_ref, fg_ref, o_ref, acc_ref, *, final_norm):
    e = pl.program_id(1)

    @pl.when(e == 0)
    def _():
        acc_ref[...] = jnp.zeros_like(acc_ref)

    h = hn_ref[...]
    hid = _silu(_dot(h, wg_ref[0])) * _dot(h, wu_ref[0])
    acc_ref[...] += gate_ref[0] * _dot(hid.astype(BF16), wd_ref[0])

    @pl.when(e == N_EXPERTS - 1)
    def _():
        xn = x_ref[...] + g2_ref[0] * acc_ref[...]
        if final_norm:
            xn = _rms(xn) * fg_ref[...]
        o_ref[...] = xn


def _moe(x, hn, gates_t, w_gate, w_up, w_down, mods, mod_base, final_g, final_norm, n_ctx, t_dec):
    n, d = x.shape
    tm = TM_MOE
    mrow = _mod_row_of_tile(tm, n_ctx, t_dec)
    return pl.pallas_call(
        functools.partial(_moe_kernel, final_norm=final_norm),
        grid=(n // tm, N_EXPERTS),
        in_specs=[
            pl.BlockSpec((tm, d), lambda i, e: (i, 0)),
            pl.BlockSpec((1, tm, 1), lambda i, e: (e, i, 0)),
            pl.BlockSpec((1, d, D_FF), lambda i, e: (e, 0, 0)),
            pl.BlockSpec((1, d, D_FF), lambda i, e: (e, 0, 0)),
            pl.BlockSpec((1, D_FF, d), lambda i, e: (e, 0, 0)),
            pl.BlockSpec((tm, d), lambda i, e: (i, 0)),
            pl.BlockSpec((1, 1, d), lambda i, e: (mod_base + mrow(i) * 6 + 5, 0, 0)),
            pl.BlockSpec((1, d), lambda i, e: (0, 0)),
        ],
        out_specs=pl.BlockSpec((tm, d), lambda i, e: (i, 0)),
        out_shape=jax.ShapeDtypeStruct((n, d), F32),
        scratch_shapes=[pltpu.VMEM((tm, d), F32)],
        compiler_params=pltpu.CompilerParams(dimension_semantics=("arbitrary", "arbitrary"),
                                             vmem_limit_bytes=VMEM_LIMIT),
        name="moe_dense",
    )(hn, gates_t.reshape(N_EXPERTS, n, 1), w_gate.astype(BF16), w_up.astype(BF16), w_down.astype(BF16),
      x, mods, final_g.reshape(1, d))


def _grid_pos_embed(n_tokens):
    rows = n_tokens // GRID_W
    r = jnp.repeat(jnp.arange(rows, dtype=F32), GRID_W)
    col = jnp.tile(jnp.arange(GRID_W, dtype=F32), rows)
    quarter = D_MODEL // 4
    freq = jnp.exp(jnp.arange(quarter, dtype=F32) * (-math.log(POS_BASE) / quarter))

    def axis_embed(pos):
        a = pos[:, None] * freq[None, :]
        return jnp.concatenate([jnp.sin(a), jnp.cos(a)], axis=-1)

    return jnp.concatenate([axis_embed(r), axis_embed(col)], axis=-1)


def _split_in_weights(w_in):
    pm = 4 * N_HEADS * D_HEAD
    wg = w_in[:, pm:].reshape(-1, 2, 2, N_HEADS)
    return w_in[:, :pm], wg.transpose(0, 3, 1, 2).reshape(-1, 4 * N_HEADS)


def _gate_layouts(gates, gates_t):
    n = gates.shape[0]
    gcol = gates.reshape(n, N_HEADS, 4).transpose(1, 0, 2)
    grow = gates_t.reshape(N_HEADS, 4, n // CHUNK, CHUNK).transpose(0, 2, 1, 3)
    return gcol, grow


def _head_params(first, second):
    return jnp.stack([first, second], axis=0).transpose(2, 0, 1).reshape(-1).astype(F32)


def _heads_per_step(t_len):
    return 4 if t_len <= 256 else 2


def kernel(x_prompt, x_sample, state_gdn_S, state_mlstm_C, state_mlstm_n, state_mlstm_m, c, c_ctx, ada_w, ada_b,
           norm1_g, norm2_g, gdn_w_in, gdn_conv_w, gdn_a_log, gdn_dt_bias, gdn_norm_g, gdn_w_out, mlstm_w_in,
           mlstm_gate_b, mlstm_norm_g, mlstm_w_out, router_w, router_bias, exp_w_gate, exp_w_up, exp_w_down,
           final_norm_g):
    bp, tp, d = x_prompt.shape
    bs, ts, _ = x_sample.shape
    n_ctx = bp * tp
    depth = ada_w.shape[0]
    assert n_ctx % ts == 0 and ts % TM_MOE == 0 and tp % TM_PROJ == 0 and bs + 1 <= N_MOD_ROWS

    pos = _grid_pos_embed(ts).astype(F32)
    x = jnp.concatenate([x_prompt.reshape(n_ctx, d), (x_sample + pos[None]).reshape(bs * ts, d)], axis=0)

    conds = jnp.concatenate([c_ctx[None, :], c, jnp.zeros((N_MOD_ROWS - 1 - bs, d), F32)], axis=0)
    mods = _ada_mods(conds, ada_w, ada_b).reshape(depth * N_MOD_ROWS * 6, 1, d)

    zeros_dh = jnp.zeros_like(gdn_a_log[0])
    ctx = dict(batch=bp, t_len=tp, row0=0, hb=_heads_per_step(tp), emit_state=True)
    dec = dict(batch=bs, t_len=ts, row0=n_ctx, hb=_heads_per_step(ts), emit_state=False)
    outs = {}
    for layer in range(depth):
        j = layer // 2
        mod_base = layer * N_MOD_ROWS * 6
        if layer % 2 == 0:
            w_main, w_gate = _split_in_weights(gdn_w_in[j])
            bias = _head_params(gdn_dt_bias[j], zeros_dh)
            mul = _head_params(-jnp.exp(gdn_a_log[j].astype(F32)), zeros_dh)
            main, gates, gates_t = _inproj(x, mods, mod_base, norm1_g[layer], w_main, w_gate, bias, mul, "gdn",
                                           n_ctx, ts)
            gcol, grow = _gate_layouts(gates, gates_t)
            o, s_new = _gdn_scan(main, gcol, grow, gdn_conv_w[j], gdn_norm_g[j], None, None, **ctx)
            (o,) = _gdn_scan(main, gcol, grow, gdn_conv_w[j], gdn_norm_g[j], state_gdn_S[:, j].astype(F32), o, **dec)
            outs.setdefault("gdn", []).append(s_new)
            w_out = gdn_w_out[j]
        else:
            w_main, w_gate = _split_in_weights(mlstm_w_in[j])
            bias = _head_params(mlstm_gate_b[j, 0], mlstm_gate_b[j, 1])
            main, gates, gates_t = _inproj(x, mods, mod_base, norm1_g[layer], w_main, w_gate, bias,
                                           jnp.zeros_like(bias), "mlstm", n_ctx, ts)
            gcol, grow = _gate_layouts(gates, gates_t)
            o, c_new, n_new, m_new = _mlstm_scan(main, gcol, grow, mlstm_norm_g[j], None, None, **ctx)
            init = (state_mlstm_C[:, j].astype(F32),
                    state_mlstm_n[:, j].astype(F32).transpose(0, 2, 1, 3),
                    jnp.broadcast_to(state_mlstm_m[:, j].astype(F32).transpose(0, 2, 1)[..., None],
                                     (bs, N_HEADS, 2, D_HEAD)))
            (o,) = _mlstm_scan(main, gcol, grow, mlstm_norm_g[j], init, o, **dec)
            outs.setdefault("mC", []).append(c_new)
            outs.setdefault("mn", []).append(n_new.transpose(0, 2, 1, 3))
            outs.setdefault("mm", []).append(m_new[..., 0].transpose(0, 2, 1))
            w_out = mlstm_w_out[j]
        x, hn, gates_t = _outproj_route(x, o, w_out, mods, mod_base, norm2_g[layer], router_w, router_bias, n_ctx, ts)
        x = _moe(x, hn, gates_t, exp_w_gate[layer], exp_w_up[layer], exp_w_down[layer], mods, mod_base,
                 final_norm_g, layer == depth - 1, n_ctx, ts)

    y_prompt = x[:n_ctx].reshape(bp, tp, d)
    y_sample = x[n_ctx:].reshape(bs, ts, d)
    return (y_prompt, y_sample, jnp.stack(outs["gdn"], axis=1), jnp.stack(outs["mC"], axis=1),
            jnp.stack(outs["mn"], axis=1), jnp.stack(outs["mm"], axis=1))
```

```python
import functools
import math

import jax
import jax.numpy as jnp
from jax import lax
from jax.experimental import pallas as pl
from jax.experimental.pallas import tpu as pltpu

F32 = jnp.float32
BF16 = jnp.bfloat16
HIGHEST = lax.Precision.HIGHEST

D_MODEL = 1024
N_HEADS = 8
D_HEAD = 128
CHUNK = 64
N_EXPERTS = 16
N_GROUPS = 4
EXP_PER_GROUP = 4
D_FF = 512
EPS = 1e-6
GRID_W = 64
POS_BASE = 10000.0
N_MOD_ROWS = 8
VMEM_LIMIT = 56 * 1024 * 1024

TM_PROJ = 256
TM_MOE = 256
SCAN_CHAINS = 32


def _silu(x):
    return x * jax.nn.sigmoid(x)


def _softplus(x):
    return jnp.maximum(x, 0.0) + jnp.log(1.0 + jnp.exp(-jnp.abs(x)))


def _dot(a, b, precision=None):
    return jnp.dot(a, b, preferred_element_type=F32, precision=precision)


def _dot_nt(a, b, precision=None):
    return lax.dot_general(a, b, (((1,), (1,)), ((), ())), preferred_element_type=F32, precision=precision)


def _dot_tn(a, b, precision=None):
    return lax.dot_general(a, b, (((0,), (0,)), ((), ())), preferred_element_type=F32, precision=precision)


def _rms(x):
    return x * lax.rsqrt(jnp.mean(x * x, axis=-1, keepdims=True) + EPS)


def _ada_kernel(c_ref, w_ref, b_ref, o_ref):
    cs = _silu(c_ref[...]).astype(BF16)
    o_ref[0] = _dot(cs, w_ref[0].astype(BF16)) + b_ref[0]


def _ada_mods(conds, ada_w, ada_b):
    depth, d, n6 = ada_w.shape
    tn = 1536
    return pl.pallas_call(
        _ada_kernel,
        grid=(depth, n6 // tn),
        in_specs=[
            pl.BlockSpec((N_MOD_ROWS, d), lambda l, j: (0, 0)),
            pl.BlockSpec((1, d, tn), lambda l, j: (l, 0, j)),
            pl.BlockSpec((1, 1, tn), lambda l, j: (l, 0, j)),
        ],
        out_specs=pl.BlockSpec((1, N_MOD_ROWS, tn), lambda l, j: (l, 0, j)),
        out_shape=jax.ShapeDtypeStruct((depth, N_MOD_ROWS, n6), F32),
        compiler_params=pltpu.CompilerParams(dimension_semantics=("arbitrary", "arbitrary"),
                                             vmem_limit_bytes=VMEM_LIMIT),
        name="ada_mods",
    )(conds, ada_w, ada_b.reshape(depth, 1, n6))


def _split3(x):
    hi = x.astype(BF16)
    r1 = x - hi.astype(F32)
    mid = r1.astype(BF16)
    lo = (r1 - mid.astype(F32)).astype(BF16)
    return hi, mid, lo


def _inproj_kernel(x_ref, ng_ref, sc_ref, sh_ref, w_ref, wgt_ref, gb_ref, gm_ref, main_ref, gr_ref, *, kind):
    x = x_ref[...]
    hn = _rms(x) * ng_ref[...]
    hn = hn * (1.0 + sc_ref[0]) + sh_ref[0]
    hb = hn.astype(BF16)
    main_ref[...] = _dot(hb, w_ref[...])

    pre = _dot_nt(wgt_ref[...], hb) + gb_ref[...]
    row = lax.broadcasted_iota(jnp.int32, pre.shape, 0)
    first = (row % 4) < 2
    if kind == "gdn":
        act = jnp.where(first, gm_ref[...] * _softplus(pre), jax.nn.sigmoid(pre))
        scanned_kind = 0
    else:
        act = jnp.where(first, pre, -_softplus(-pre))
        scanned_kind = 1

    tm = x.shape[0]
    r = lax.broadcasted_iota(jnp.int32, (tm, tm), 0)
    c = lax.broadcasted_iota(jnp.int32, (tm, tm), 1)
    same = (r // CHUNK) == (c // CHUNK)
    before = jnp.where(same, jnp.where(r <= c, 1.0, 0.0), 0.0).astype(BF16)
    after = jnp.where(same, jnp.where(r >= c, 1.0, 0.0), 0.0).astype(BF16)
    parts = _split3(act)
    fwd = sum(_dot(p, before) for p in parts)
    bwd = sum(_dot(p, after) for p in parts)
    gt = jnp.where((row % 4) // 2 == scanned_kind, jnp.where(row % 2 == 0, fwd, bwd), act)
    for h in range(N_HEADS):
        for ch in range(tm // CHUNK):
            gr_ref[h, ch] = gt[4 * h:4 * h + 4, ch * CHUNK:(ch + 1) * CHUNK]


def _mod_row_of_tile(tm, n_ctx, t_dec):
    def f(i):
        r = i * tm
        return jnp.where(r < n_ctx, 0, 1 + (r - n_ctx) // t_dec)
    return f


def _inproj(x, mods, mod_base, norm_g, w_main, w_gate, gate_bias, gate_mul, kind, n_ctx, t_dec):
    n, d = x.shape
    tm = TM_PROJ
    pm = w_main.shape[1]
    ng = w_gate.shape[1]
    mrow = _mod_row_of_tile(tm, n_ctx, t_dec)
    sc_map = lambda i: (mod_base + mrow(i) * 6 + 1, 0, 0)
    sh_map = lambda i: (mod_base + mrow(i) * 6 + 0, 0, 0)
    const2 = lambda i: (0, 0)
    return pl.pallas_call(
        functools.partial(_inproj_kernel, kind=kind),
        grid=(n // tm,),
        in_specs=[
            pl.BlockSpec((tm, d), lambda i: (i, 0)),
            pl.BlockSpec((1, d), const2),
            pl.BlockSpec((1, 1, d), sc_map),
            pl.BlockSpec((1, 1, d), sh_map),
            pl.BlockSpec((d, pm), const2),
            pl.BlockSpec((ng, d), const2),
            pl.BlockSpec((ng, 1), const2),
            pl.BlockSpec((ng, 1), const2),
        ],
        out_specs=[
            pl.BlockSpec((tm, pm), lambda i: (i, 0)),
            pl.BlockSpec((N_HEADS, tm // CHUNK, 4, CHUNK), lambda i: (0, i, 0, 0)),
        ],
        out_shape=[
            jax.ShapeDtypeStruct((n, pm), F32),
            jax.ShapeDtypeStruct((N_HEADS, n // CHUNK, 4, CHUNK), F32),
        ],
        compiler_params=pltpu.CompilerParams(dimension_semantics=("arbitrary",), vmem_limit_bytes=VMEM_LIMIT),
        name="inproj_" + kind,
    )(x, norm_g.reshape(1, d), mods, mods, w_main.astype(BF16), w_gate.T.astype(BF16),
      gate_bias.reshape(ng, 1), gate_mul.reshape(ng, 1))


def _chunk_masks():
    r = lax.broadcasted_iota(jnp.int32, (CHUNK, CHUNK), 0)
    c = lax.broadcasted_iota(jnp.int32, (CHUNK, CHUNK), 1)
    return r >= c, r > c, r <= c, r < c


def _chunk_rows(c):
    if isinstance(c, int):
        return pl.ds(c * CHUNK, CHUNK)
    return pl.ds(pl.multiple_of(c * CHUNK, CHUNK), CHUNK)


def _head_cols(h):
    return slice(h * D_HEAD, (h + 1) * D_HEAD)


def _last_row(x, d):
    return x[CHUNK - 1:CHUNK, :] if d == 0 else x[0:1, :]


def _gate_columns(gr4):
    a = jnp.concatenate([gr4, jnp.zeros_like(gr4)], axis=0)
    hi = a.astype(BF16).astype(F32)
    mid = (a - hi).astype(BF16).astype(F32)
    lo = a - hi - mid
    parts = jnp.concatenate([hi, mid, lo, jnp.zeros_like(a)], axis=0).astype(BF16)
    j = lax.broadcasted_iota(jnp.int32, (32, 4 * D_HEAD), 0) % 8
    lane = lax.broadcasted_iota(jnp.int32, (32, 4 * D_HEAD), 1)
    return _dot_tn(parts, jnp.where(lane // D_HEAD == j, 1.0, 0.0).astype(BF16))


def _gate_column(cols, j):
    return cols[:, j * D_HEAD:(j + 1) * D_HEAD]


def _unit_tri_solve(a_list, rhs_list):
    mm = lambda x, y: _dot(x.astype(BF16), y.astype(BF16))
    r = lax.broadcasted_iota(jnp.int32, (CHUNK, CHUNK), 0)
    c = lax.broadcasted_iota(jnp.int32, (CHUNK, CHUNK), 1)
    same = (r // 16) == (c // 16)
    eye = jnp.where(r == c, 1.0, 0.0)
    d = [jnp.where(same, a, 0.0) for a in a_list]
    t = [eye - di for di in d]
    for _ in range(3):
        d = [mm(di, di) for di in d]
        t = [ti + mm(ti, di) for ti, di in zip(t, d)]
    b = [mm(ti, jnp.where(same, 0.0, a)) for ti, a in zip(t, a_list)]
    x = [mm(ti, ri) for ti, ri in zip(t, rhs_list)]
    b2 = [mm(bi, bi) for bi in b]
    x = [xi - mm(bi, xi) for xi, bi in zip(x, b)]
    return [xi + mm(bi, xi) for xi, bi in zip(x, b2)]


def _for_chunk_groups(nc, group, fn):
    if nc == group:
        fn(list(range(nc)))
        return

    def body(g, carry):
        fn([g * group + j for j in range(group)])
        return carry
    lax.fori_loop(0, nc // group, body, 0)


def _scan_specs(t_len, row0, hb, n_proj):
    rb = row0 // t_len
    nc = t_len // CHUNK
    ngrp = N_HEADS // hb
    specs = [pl.BlockSpec((t_len, hb * D_HEAD), functools.partial(lambda b, g, j: (rb + b, j * ngrp + g), j=j))
             for j in range(n_proj)]
    specs.append(pl.BlockSpec((hb, nc, 4, CHUNK), lambda b, g: (g, rb + b, 0, 0)))
    return specs


def _scan_call(kern, name, args, in_specs, out_specs, out_shape, scratch, batch, hb):
    return pl.pallas_call(
        kern,
        grid=(batch, N_HEADS // hb),
        in_specs=in_specs,
        out_specs=out_specs,
        out_shape=out_shape,
        scratch_shapes=scratch,
        compiler_params=pltpu.CompilerParams(dimension_semantics=("arbitrary", "arbitrary"),
                                             vmem_limit_bytes=VMEM_LIMIT),
        name=name,
    )(*args)


def _gdn_kernel(*refs, t_len, hb, has_init, emit_state):
    it = iter(refs)
    q_ref, k_ref, v_ref, z_ref, gr_ref, cwq_ref, cwk_ref, cwv_ref, ng_ref = (next(it) for _ in range(9))
    s0_ref = next(it) if has_init else None
    o_ref = next(it)
    s_ref = next(it) if emit_state else None
    qs, ks, vs, oacc, qp_s, op_s, km_s, nm_s, ge_s = (next(it) for _ in range(9))
    nc = t_len // CHUNK

    rows = lax.broadcasted_iota(jnp.int32, (t_len, 1), 0)

    def conv_silu(x, cw):
        xm = jnp.where(rows == 0, 0.0, pltpu.roll(x, 1, axis=0))
        xp = jnp.where(rows == t_len - 1, 0.0, pltpu.roll(x, t_len - 1, axis=0))
        return _silu(xm * cw[0:1] + x * cw[1:2] + xp * cw[2:3])

    def l2n(x):
        return x * lax.rsqrt(jnp.sum(x * x, axis=-1, keepdims=True) + EPS)

    for h in range(hb):
        hc = _head_cols(h)
        qs[:, hc] = l2n(conv_silu(q_ref[:, hc], cwq_ref[:, hc])) * (D_HEAD ** -0.5)
        ks[:, hc] = l2n(conv_silu(k_ref[:, hc], cwk_ref[:, hc]))
        vs[:, hc] = conv_silu(v_ref[:, hc], cwv_ref[:, hc])
    oacc[...] = jnp.zeros_like(oacc)

    lo_i, lo_s, up_i, up_s = _chunk_masks()

    def slot(h, d, c):
        return (h * 2 + d) * nc + c

    def intra(chunks):
        items = [(h, c, d) for h in range(hb) for c in chunks for d in range(2)]
        cols = [[_gate_columns(gr_ref[h, c]) for c in chunks] for h in range(hb)]
        a_list, rhs_list, keep = [], [], []
        for n_item, (h, c, d) in enumerate(items):
            sl, hc = _chunk_rows(c), _head_cols(h)
            q, k, v = qs[sl, hc], ks[sl, hc], vs[sl, hc]
            gr4 = gr_ref[h, c]
            col4 = cols[h][(n_item // 2) % len(chunks)]
            g_col, beta = _gate_column(col4, d), _gate_column(col4, 2 + d)
            incl, strict = (lo_i, lo_s) if d == 0 else (up_i, up_s)
            decay = jnp.exp(jnp.where(incl, g_col[:, :CHUNK] - gr4[d:d + 1, :], -jnp.inf))
            kb = k * beta
            k16 = k.astype(BF16)
            a_list.append(jnp.where(strict, _dot_nt(kb.astype(BF16), k16) * decay, 0.0))
            e_g = jnp.exp(g_col)
            rhs_list.append(jnp.concatenate([v * beta, kb * e_g], axis=1))
            qk16 = jnp.where(incl, _dot_nt(q.astype(BF16), k16) * decay, 0.0).astype(BF16)
            g_last = _last_row(g_col, d)
            ge_s[slot(h, d, c)] = jnp.broadcast_to(jnp.exp(g_last), (8, D_HEAD))
            keep.append((qk16, q * e_g, (k * jnp.exp(g_last - g_col)).astype(BF16)))
        sols = [s.astype(BF16) for s in _unit_tri_solve(a_list, rhs_list)]
        qw = [_dot(kp[0], s) for kp, s in zip(keep, sols)]
        kw = [_dot_tn(kp[2], s) for kp, s in zip(keep, sols)]
        for (h, c, d), kp, qwi, kwi in zip(items, keep, qw, kw):
            i = h * 2 + d
            sl = _chunk_rows(c)
            op_s[i, sl, :] = qwi[:, :D_HEAD]
            qp_s[i, sl, :] = (kp[1] - qwi[:, D_HEAD:]).astype(BF16)
            nm_s[slot(h, d, c)] = kwi[:, :D_HEAD]
            km_s[slot(h, d, c)] = (-kwi[:, D_HEAD:]).astype(BF16)

    _for_chunk_groups(nc, min(nc, max(1, SCAN_CHAINS // (2 * hb))), intra)

    chains = [(h, d) for h in range(hb) for d in range(2)]

    def inter_body(i, carry):
        cs = (i, nc - 1 - i)
        s16 = [s.astype(BF16) for s in carry]
        outs = [_dot(qp_s[h * 2 + d, _chunk_rows(cs[d]), :], s) for (h, d), s in zip(chains, s16)]
        upds = [_dot(km_s[slot(h, d, cs[d])], s) for (h, d), s in zip(chains, s16)]
        for (h, d), o in zip(chains, outs):
            oacc[_chunk_rows(cs[d]), _head_cols(h)] += o + op_s[h * 2 + d, _chunk_rows(cs[d]), :]
        return tuple(ge_s[slot(h, d, cs[d])][0:1, :] * s + u + nm_s[slot(h, d, cs[d])]
                     for (h, d), s, u in zip(chains, carry, upds))

    if has_init:
        init = tuple(s0_ref[0, d, h] for h, d in chains)
    else:
        init = tuple(jnp.zeros((D_HEAD, D_HEAD), F32) for _ in chains)
    fin = lax.fori_loop(0, nc, inter_body, init)
    if emit_state:
        for (h, d), s in zip(chains, fin):
            s_ref[0, d, h] = s
    for h in range(hb):
        hc = _head_cols(h)
        o_ref[:, hc] = (_rms(oacc[:, hc]) * ng_ref[...] * _silu(z_ref[:, hc])).astype(o_ref.dtype)


def _gdn_scan(main, grow, conv_w, norm_g, s0, *, batch, t_len, row0, hb, emit_state):
    nc = t_len // CHUNK
    ngrp = N_HEADS // hb
    has_init = s0 is not None
    in_specs = _scan_specs(t_len, row0, hb, 4)
    in_specs += [pl.BlockSpec((3, hb * D_HEAD), functools.partial(lambda b, g, j: (0, j * ngrp + g), j=j))
                 for j in range(3)]
    in_specs.append(pl.BlockSpec((1, D_HEAD), lambda b, g: (0, 0)))
    args = [main, main, main, main, grow, conv_w, conv_w, conv_w, norm_g.reshape(1, D_HEAD)]
    state_spec = pl.BlockSpec((1, 2, hb, D_HEAD, D_HEAD), lambda b, g: (b, 0, g, 0, 0))
    if has_init:
        in_specs.append(state_spec)
        args.append(s0)
    out_specs = [pl.BlockSpec((t_len, hb * D_HEAD), lambda b, g: (b, g))]
    out_shape = [jax.ShapeDtypeStruct((batch * t_len, N_HEADS * D_HEAD), BF16)]
    if emit_state:
        out_specs.append(state_spec)
        out_shape.append(jax.ShapeDtypeStruct((batch, 2, N_HEADS, D_HEAD, D_HEAD), F32))
    scratch = ([pltpu.VMEM((t_len, hb * D_HEAD), F32) for _ in range(4)]
               + [pltpu.VMEM((2 * hb, t_len, D_HEAD), BF16),
                  pltpu.VMEM((2 * hb, t_len, D_HEAD), F32),
                  pltpu.VMEM((2 * hb * nc, D_HEAD, D_HEAD), BF16),
                  pltpu.VMEM((2 * hb * nc, D_HEAD, D_HEAD), F32),
                  pltpu.VMEM((2 * hb * nc, 8, D_HEAD), F32)])
    kern = functools.partial(_gdn_kernel, t_len=t_len, hb=hb, has_init=has_init, emit_state=emit_state)
    return _scan_call(kern, "gdn_scan_t%d" % t_len, args, in_specs, out_specs, out_shape, scratch, batch, hb)


def _mlstm_kernel(*refs, t_len, hb, has_init, emit_state):
    it = iter(refs)
    q_ref, k_ref, v_ref, og_ref, gr_ref, ng_ref = (next(it) for _ in range(6))
    if has_init:
        c0_ref, n0_ref, m0_ref = (next(it) for _ in range(3))
    o_ref = next(it)
    if emit_state:
        c_ref, n_ref, m_ref = (next(it) for _ in range(3))
    hacc, q16_s, v1_s, qk_s, ld_s, bb_s, lm_s, lw_s, sc_s = (next(it) for _ in range(9))
    nc = t_len // CHUNK

    hacc[...] = jnp.zeros_like(hacc)
    ones = jnp.ones((t_len, D_HEAD), BF16)
    for h in range(hb):
        hc = _head_cols(h)
        q16_s[:, hc] = (q_ref[:, hc] * (D_HEAD ** -0.5)).astype(BF16)
        v1_s[:, h * 2 * D_HEAD:(h * 2 + 1) * D_HEAD] = v_ref[:, hc].astype(BF16)
        v1_s[:, (h * 2 + 1) * D_HEAD:(h * 2 + 2) * D_HEAD] = ones
    lo_i, _, up_i, _ = _chunk_masks()

    def slot(h, d, c):
        return (h * 2 + d) * nc + c

    def intra(chunks):
        for h in range(hb):
            for c in chunks:
                sl, hc = _chunk_rows(c), _head_cols(h)
                qk_s[h, sl, :] = _dot_nt(q16_s[sl, hc], k_ref[sl, hc].astype(BF16))
                gr4 = gr_ref[h, c]
                col4 = _gate_columns(gr4)
                for d in range(2):
                    i = h * 2 + d
                    b_col, i_col = _gate_column(col4, 2 + d), _gate_column(col4, d)
                    b_last = _last_row(b_col, d)
                    lwe = b_last - b_col + i_col
                    log_d = jnp.where(lo_i if d == 0 else up_i,
                                      b_col[:, :CHUNK] - gr4[2 + d:3 + d, :] + gr4[d:d + 1, :], -jnp.inf)
                    ld_s[i, sl, :] = log_d
                    lm_s[i, sl, :] = jnp.broadcast_to(jnp.max(log_d, axis=-1, keepdims=True), (CHUNK, D_HEAD))
                    bb_s[i, sl, :] = b_col
                    lw_s[i, sl, :] = lwe
                    sc_s[slot(h, d, c), 0:1, :] = b_last
                    sc_s[slot(h, d, c), 1:2, :] = jnp.max(lwe, axis=0, keepdims=True)

    _for_chunk_groups(nc, min(nc, 4), intra)

    chains = [(h, d) for h in range(hb) for d in range(2)]

    def body(step, carry):
        cs = (step, nc - 1 - step)
        state, m = zip(*carry)
        sls = [_chunk_rows(cs[d]) for h, d in chains]
        idx = [h * 2 + d for h, d in chains]
        v1 = [v1_s[sl, h * 2 * D_HEAD:(h * 2 + 2) * D_HEAD] for (h, d), sl in zip(chains, sls)]
        log_last = [sc_s[slot(h, d, cs[d]), 0:1, :] + mi for (h, d), mi in zip(chains, m)]
        m_new = [jnp.maximum(ll, sc_s[slot(h, d, cs[d]), 1:2, :]) for (h, d), ll in zip(chains, log_last)]
        dec = [jnp.exp(ll - mn) for ll, mn in zip(log_last, m_new)]
        kw = [(k_ref[sl, _head_cols(h)] * jnp.exp(lw_s[i, sl, :] - mn)).astype(BF16)
              for (h, d), sl, i, mn in zip(chains, sls, idx, m_new)]
        upd = [_dot_tn(a, b) for a, b in zip(kw, v1)]
        qc = [_dot(q16_s[sl, _head_cols(h)], s.astype(BF16)) for (h, d), sl, s in zip(chains, sls, state)]
        log_inter = [bb_s[i, sl, :] + mi for i, sl, mi in zip(idx, sls, m)]
        mt = [jnp.maximum(li, lm_s[i, sl, :]) for li, i, sl in zip(log_inter, idx, sls)]
        s_inter = [jnp.exp(li - t) for li, t in zip(log_inter, mt)]
        p = [(jnp.exp(ld_s[i, sl, :] - t[:, :CHUNK]) * qk_s[h, sl, :]).astype(BF16)
             for (h, d), i, sl, t in zip(chains, idx, sls, mt)]
        pv = [_dot(a, b) for a, b in zip(p, v1)]
        for (h, d), sl, si, qci, pvi, t in zip(chains, sls, s_inter, qc, pv, mt):
            num = si * qci[:, :D_HEAD] + pvi[:, :D_HEAD]
            den = si * qci[:, D_HEAD:] + pvi[:, D_HEAD:]
            hacc[sl, _head_cols(h)] += num / jnp.maximum(jnp.abs(den), jnp.exp(-t))
        return tuple((jnp.concatenate([dc, dc], axis=1) * s + u, mn)
                     for dc, s, u, mn in zip(dec, state, upd, m_new))

    if has_init:
        init = tuple((jnp.concatenate([c0_ref[0, d, h],
                                       jnp.transpose(jnp.broadcast_to(n0_ref[0, h, d:d + 1, :], (D_HEAD, D_HEAD)))],
                                      axis=1),
                      m0_ref[0, h, d:d + 1, :]) for h, d in chains)
    else:
        init = tuple((jnp.zeros((D_HEAD, 2 * D_HEAD), F32), jnp.zeros((1, D_HEAD), F32)) for _ in chains)
    fin = lax.fori_loop(0, nc, body, init)
    if emit_state:
        for (h, d), (s, m) in zip(chains, fin):
            c_ref[0, d, h] = s[:, :D_HEAD]
            n_ref[0, h, d:d + 1, :] = jnp.transpose(s[:, D_HEAD:])[0:1, :]
            m_ref[0, h, d:d + 1, :] = m
    for h in range(hb):
        hc = _head_cols(h)
        o_ref[:, hc] = (_rms(hacc[:, hc]) * ng_ref[...] * jax.nn.sigmoid(og_ref[:, hc])).astype(o_ref.dtype)


def _mlstm_scan(main, grow, norm_g, init, *, batch, t_len, row0, hb, emit_state):
    nc = t_len // CHUNK
    has_init = init is not None
    in_specs = _scan_specs(t_len, row0, hb, 4)
    in_specs.append(pl.BlockSpec((1, D_HEAD), lambda b, g: (0, 0)))
    args = [main, main, main, main, grow, norm_g.reshape(1, D_HEAD)]
    c_spec = pl.BlockSpec((1, 2, hb, D_HEAD, D_HEAD), lambda b, g: (b, 0, g, 0, 0))
    v_spec = pl.BlockSpec((1, hb, 2, D_HEAD), lambda b, g: (b, g, 0, 0))
    if has_init:
        in_specs += [c_spec, v_spec, v_spec]
        args += list(init)
    out_specs = [pl.BlockSpec((t_len, hb * D_HEAD), lambda b, g: (b, g))]
    out_shape = [jax.ShapeDtypeStruct((batch * t_len, N_HEADS * D_HEAD), BF16)]
    if emit_state:
        out_specs += [c_spec, v_spec, v_spec]
        out_shape += [jax.ShapeDtypeStruct((batch, 2, N_HEADS, D_HEAD, D_HEAD), F32),
                      jax.ShapeDtypeStruct((batch, N_HEADS, 2, D_HEAD), F32),
                      jax.ShapeDtypeStruct((batch, N_HEADS, 2, D_HEAD), F32)]
    scratch = [pltpu.VMEM((t_len, hb * D_HEAD), F32),
               pltpu.VMEM((t_len, hb * D_HEAD), BF16),
               pltpu.VMEM((t_len, hb * 2 * D_HEAD), BF16),
               pltpu.VMEM((hb, t_len, CHUNK), F32),
               pltpu.VMEM((2 * hb, t_len, CHUNK), F32),
               pltpu.VMEM((2 * hb, t_len, D_HEAD), F32),
               pltpu.VMEM((2 * hb, t_len, D_HEAD), F32),
               pltpu.VMEM((2 * hb, t_len, D_HEAD), F32),
               pltpu.VMEM((2 * hb * nc, 8, D_HEAD), F32)]
    kern = functools.partial(_mlstm_kernel, t_len=t_len, hb=hb, has_init=has_init, emit_state=emit_state)
    return _scan_call(kern, "mlstm_scan_t%d" % t_len, args, in_specs, out_specs, out_shape, scratch, batch, hb)


def _top2_of4(v):
    best, i1 = v[0], jnp.zeros(v[0].shape, jnp.int32)
    for j in range(1, 4):
        take = v[j] > best
        i1 = jnp.where(take, j, i1)
        best = jnp.where(take, v[j], best)
    best2, i2 = None, None
    for j in range(4):
        vj = jnp.where(i1 == j, -jnp.inf, v[j])
        if best2 is None:
            best2, i2 = vj, jnp.zeros(v[0].shape, jnp.int32)
        else:
            take = vj > best2
            i2 = jnp.where(take, j, i2)
            best2 = jnp.where(take, vj, best2)
    return i1, i2


def _pick(rows, idx):
    out = rows[0]
    for j in range(1, len(rows)):
        out = jnp.where(idx == j, rows[j], out)
    return out


def _outproj_kernel(x_ref, oc_ref, od_ref, w_ref, g1_ref, ng_ref, sc_ref, sh_ref, rw_ref, rb_ref, xo_ref, hn_ref,
                    ri_ref, wt_ref, cnt_ref, run_ref, *, ctx_tiles):
    @pl.when(pl.program_id(0) == 0)
    def _():
        run_ref[...] = jnp.zeros_like(run_ref)

    o = jnp.where(pl.program_id(0) < ctx_tiles, oc_ref[...], od_ref[...])
    xn = x_ref[...] + g1_ref[0] * _dot(o, w_ref[...])
    xo_ref[...] = xn
    hn = _rms(xn) * ng_ref[...]
    hn = hn * (1.0 + sc_ref[0]) + sh_ref[0]
    hn_ref[...] = hn

    logits = _dot_nt(rw_ref[...], hn, HIGHEST)
    ex = jnp.exp(logits - jnp.max(logits, axis=0, keepdims=True))
    probs = ex / jnp.sum(ex, axis=0, keepdims=True)
    sel = probs + rb_ref[...]
    sel_rows = [sel[e:e + 1, :] for e in range(N_EXPERTS)]
    prob_rows = [probs[e:e + 1, :] for e in range(N_EXPERTS)]
    scores = []
    for g in range(N_GROUPS):
        r = sel_rows[4 * g:4 * g + 4]
        a, b = jnp.maximum(r[0], r[1]), jnp.minimum(r[0], r[1])
        c, d = jnp.maximum(r[2], r[3]), jnp.minimum(r[2], r[3])
        scores.append(jnp.maximum(a, c) + jnp.maximum(jnp.minimum(a, c), jnp.maximum(b, d)))
    best, grp = scores[0], jnp.zeros(scores[0].shape, jnp.int32)
    for g in range(1, N_GROUPS):
        take = scores[g] > best
        grp = jnp.where(take, g, grp)
        best = jnp.where(take, scores[g], best)
    sel_in = [_pick([sel_rows[4 * g + j] for g in range(N_GROUPS)], grp) for j in range(4)]
    prob_in = [_pick([prob_rows[4 * g + j] for g in range(N_GROUPS)], grp) for j in range(4)]
    i1, i2 = _top2_of4(sel_in)
    w1, w2 = _pick(prob_in, i1), _pick(prob_in, i2)
    tot = w1 + w2
    e1, e2 = grp * 4 + i1, grp * 4 + i2
    wt_ref[...] = jnp.concatenate([w1 / tot, w2 / tot], axis=0)

    tm = logits.shape[1]
    eidx = lax.broadcasted_iota(jnp.int32, logits.shape, 0)
    onehot = jnp.where(eidx == e1, 1.0, 0.0) + jnp.where(eidx == e2, 1.0, 0.0)
    r = lax.broadcasted_iota(jnp.int32, (tm, tm), 0)
    c = lax.broadcasted_iota(jnp.int32, (tm, tm), 1)
    earlier = jnp.where(r < c, 1.0, 0.0).astype(BF16)
    before = _dot(onehot.astype(BF16), earlier) + run_ref[:, 0:1]
    rank = [jnp.sum(jnp.where(eidx == e, before, 0.0), axis=0, keepdims=True) for e in (e1, e2)]
    ri_ref[...] = jnp.concatenate([e1, e2, rank[0].astype(jnp.int32), rank[1].astype(jnp.int32)], axis=0)
    run_ref[...] += jnp.sum(onehot, axis=1, keepdims=True)
    cnt_ref[...] = run_ref[...]


def _outproj_route(x, o_ctx, o_dec, w_out, mods, mod_base, norm_g, router_w, router_bias, n_ctx, t_dec):
    n, d = x.shape
    tm = TM_PROJ
    ctx_tiles = n_ctx // tm
    mrow = _mod_row_of_tile(tm, n_ctx, t_dec)
    mod_map = lambda k: (lambda i: (mod_base + mrow(i) * 6 + k, 0, 0))
    const2 = lambda i: (0, 0)
    return pl.pallas_call(
        functools.partial(_outproj_kernel, ctx_tiles=ctx_tiles),
        grid=(n // tm,),
        in_specs=[
            pl.BlockSpec((tm, d), lambda i: (i, 0)),
            pl.BlockSpec((tm, d), lambda i: (jnp.minimum(i, ctx_tiles - 1), 0)),
            pl.BlockSpec((tm, d), lambda i: (jnp.maximum(i - ctx_tiles, 0), 0)),
            pl.BlockSpec((d, d), const2),
            pl.BlockSpec((1, 1, d), mod_map(2)),
            pl.BlockSpec((1, d), const2),
            pl.BlockSpec((1, 1, d), mod_map(4)),
            pl.BlockSpec((1, 1, d), mod_map(3)),
            pl.BlockSpec((N_EXPERTS, d), const2),
            pl.BlockSpec((N_EXPERTS, 1), const2),
        ],
        out_specs=[
            pl.BlockSpec((tm, d), lambda i: (i, 0)),
            pl.BlockSpec((tm, d), lambda i: (i, 0)),
            pl.BlockSpec((4, tm), lambda i: (0, i)),
            pl.BlockSpec((2, tm), lambda i: (0, i)),
            pl.BlockSpec((N_EXPERTS, D_HEAD), const2),
        ],
        out_shape=[
            jax.ShapeDtypeStruct((n, d), F32),
            jax.ShapeDtypeStruct((n, d), F32),
            jax.ShapeDtypeStruct((4, n), jnp.int32),
            jax.ShapeDtypeStruct((2, n), F32),
            jax.ShapeDtypeStruct((N_EXPERTS, D_HEAD), F32),
        ],
        scratch_shapes=[pltpu.VMEM((N_EXPERTS, D_HEAD), F32)],
        compiler_params=pltpu.CompilerParams(dimension_semantics=("arbitrary",), vmem_limit_bytes=VMEM_LIMIT),
        name="outproj_route",
    )(x, o_ctx, o_dec, w_out.astype(BF16), mods, norm_g.reshape(1, d), mods, mods, router_w.T, router_bias.reshape(N_EXPERTS, 1))


def _route_tables(ri, counts, n_tiles):
    cnt = counts[:, 0].astype(jnp.int32)
    padded = (cnt + TM_MOE - 1) // TM_MOE * TM_MOE
    ends = jnp.cumsum(padded)
    offs = ends - padded
    expert, rank = ri[0:2], ri[2:4]
    pos = rank + sum(jnp.where(expert == e, offs[e], 0) for e in range(N_EXPERTS))
    tile_row0 = jnp.arange(n_tiles, dtype=jnp.int32) * TM_MOE
    tile_expert = jnp.minimum(jnp.sum(tile_row0[:, None] >= ends[None, :], axis=1), N_EXPERTS - 1)
    return pos.astype(jnp.int32), tile_expert.astype(jnp.int32), (tile_row0 < ends[-1]).astype(jnp.int32)


def _row_copy(src, src_row, dst, dst_row, sem):
    return pltpu.make_async_copy(src.at[pl.ds(src_row, 1)], dst.at[pl.ds(dst_row, 1)], sem)


def _dispatch_kernel(pos_ref, hn_ref, xs0_ref, xs_ref, sem):
    del xs0_ref
    tm = pos_ref.shape[2]
    t0 = pl.program_id(0) * tm

    def issue(j, carry):
        for s in range(2):
            _row_copy(hn_ref, t0 + j, xs_ref, pos_ref[0, s, j], sem).start()
        return carry

    def drain(j, carry):
        for s in range(2):
            _row_copy(hn_ref, 0, xs_ref, 0, sem).wait()
        return carry

    lax.fori_loop(0, tm, issue, 0)
    lax.fori_loop(0, tm, drain, 0)


def _dispatch(hn, pos, n_rows):
    n, d = hn.shape
    tm = TM_PROJ
    return pl.pallas_call(
        _dispatch_kernel,
        grid=(n // tm,),
        in_specs=[
            pl.BlockSpec((1, 2, tm), lambda i: (i, 0, 0), memory_space=pltpu.SMEM),
            pl.BlockSpec(memory_space=pl.ANY),
            pl.BlockSpec(memory_space=pl.ANY),
        ],
        out_specs=pl.BlockSpec(memory_space=pl.ANY),
        out_shape=jax.ShapeDtypeStruct((n_rows, d), F32),
        scratch_shapes=[pltpu.SemaphoreType.DMA(())],
        input_output_aliases={2: 0},
        compiler_params=pltpu.CompilerParams(dimension_semantics=("arbitrary",), vmem_limit_bytes=VMEM_LIMIT),
        name="moe_dispatch",
    )(pos.reshape(2, n // tm, tm).transpose(1, 0, 2), hn, jnp.zeros((n_rows, d), F32))


def _expert_kernel(te_ref, tv_ref, xs_ref, wg_ref, wu_ref, wd_ref, ys_ref, wg16, wu16, wd16):
    i = pl.program_id(0)
    fresh = jnp.logical_or(i == 0, te_ref[i] != te_ref[jnp.maximum(i - 1, 0)])

    @pl.when(fresh)
    def _():
        wg16[...] = wg_ref[0].astype(BF16)
        wu16[...] = wu_ref[0].astype(BF16)
        wd16[...] = wd_ref[0].astype(BF16)

    @pl.when(tv_ref[i] != 0)
    def _():
        x = xs_ref[...].astype(BF16)
        hid = _silu(_dot(x, wg16[...])) * _dot(x, wu16[...])
        ys_ref[...] = _dot(hid.astype(BF16), wd16[...])

    @pl.when(tv_ref[i] == 0)
    def _():
        ys_ref[...] = jnp.zeros_like(ys_ref)


def _experts(xs, tile_expert, tile_valid, w_gate, w_up, w_down):
    n_rows, d = xs.shape
    tm = TM_MOE
    w_in_spec = pl.BlockSpec((1, d, D_FF), lambda i, te, tv: (te[i], 0, 0))
    return pl.pallas_call(
        _expert_kernel,
        grid_spec=pltpu.PrefetchScalarGridSpec(
            num_scalar_prefetch=2,
            grid=(n_rows // tm,),
            in_specs=[
                pl.BlockSpec((tm, d), lambda i, te, tv: (i, 0)),
                w_in_spec,
                w_in_spec,
                pl.BlockSpec((1, D_FF, d), lambda i, te, tv: (te[i], 0, 0)),
            ],
            out_specs=pl.BlockSpec((tm, d), lambda i, te, tv: (i, 0)),
            scratch_shapes=[pltpu.VMEM((d, D_FF), BF16), pltpu.VMEM((d, D_FF), BF16), pltpu.VMEM((D_FF, d), BF16)],
        ),
        out_shape=jax.ShapeDtypeStruct((n_rows, d), F32),
        compiler_params=pltpu.CompilerParams(dimension_semantics=("arbitrary",), vmem_limit_bytes=VMEM_LIMIT),
        name="moe_experts",
    )(tile_expert, tile_valid, xs, w_gate, w_up, w_down)


def _combine_kernel(pos_ref, x_ref, wt_ref, g2_ref, fg_ref, ys_ref, *rest, ctx_tiles):
    *o_refs, buf, sem = rest
    tm = pos_ref.shape[2]

    def issue(j, carry):
        for s in range(2):
            _row_copy(ys_ref, pos_ref[0, s, j], buf.at[s], j, sem).start()
        return carry

    def drain(j, carry):
        for s in range(2):
            _row_copy(ys_ref, 0, buf.at[s], 0, sem).wait()
        return carry

    lax.fori_loop(0, tm, issue, 0)
    lax.fori_loop(0, tm, drain, 0)
    w = wt_ref[...]
    xn = x_ref[...] + g2_ref[0] * (w[:, 0:1] * buf[0] + w[:, 1:2] * buf[1])
    if ctx_tiles is None:
        o_refs[0][...] = xn
        return
    xn = _rms(xn) * fg_ref[...]

    @pl.when(pl.program_id(0) < ctx_tiles)
    def _():
        o_refs[0][...] = xn

    @pl.when(pl.program_id(0) >= ctx_tiles)
    def _():
        o_refs[1][...] = xn


def _combine(x, ys, pos, wts, mods, mod_base, final_g, final_norm, n_ctx, t_dec):
    n, d = x.shape
    tm = TM_PROJ
    mrow = _mod_row_of_tile(tm, n_ctx, t_dec)
    ctx_tiles = n_ctx // tm if final_norm else None
    if final_norm:
        out_specs = [pl.BlockSpec((tm, d), lambda i: (jnp.minimum(i, ctx_tiles - 1), 0)),
                     pl.BlockSpec((tm, d), lambda i: (jnp.maximum(i - ctx_tiles, 0), 0))]
        out_shape = [jax.ShapeDtypeStruct((n_ctx, d), F32), jax.ShapeDtypeStruct((n - n_ctx, d), F32)]
    else:
        out_specs = pl.BlockSpec((tm, d), lambda i: (i, 0))
        out_shape = jax.ShapeDtypeStruct((n, d), F32)
    return pl.pallas_call(
        functools.partial(_combine_kernel, ctx_tiles=ctx_tiles),
        grid=(n // tm,),
        in_specs=[
            pl.BlockSpec((1, 2, tm), lambda i: (i, 0, 0), memory_space=pltpu.SMEM),
            pl.BlockSpec((tm, d), lambda i: (i, 0)),
            pl.BlockSpec((tm, 2), lambda i: (i, 0)),
            pl.BlockSpec((1, 1, d), lambda i: (mod_base + mrow(i) * 6 + 5, 0, 0)),
            pl.BlockSpec((1, d), lambda i: (0, 0)),
            pl.BlockSpec(memory_space=pl.ANY),
        ],
        out_specs=out_specs,
        out_shape=out_shape,
        scratch_shapes=[pltpu.VMEM((2, tm, d), F32), pltpu.SemaphoreType.DMA(())],
        compiler_params=pltpu.CompilerParams(dimension_semantics=("arbitrary",), vmem_limit_bytes=VMEM_LIMIT),
        name="moe_combine",
    )(pos.reshape(2, n // tm, tm).transpose(1, 0, 2), x, wts.T, mods, final_g.reshape(1, d), ys)


def _moe(x, hn, ri, wts, counts, w_gate, w_up, w_down, mods, mod_base, final_g, final_norm, n_ctx, t_dec):
    n = x.shape[0]
    n_tiles = (2 * n) // TM_MOE + N_EXPERTS
    pos, tile_expert, tile_valid = _route_tables(ri, counts, n_tiles)
    xs = _dispatch(hn, pos, n_tiles * TM_MOE)
    ys = _experts(xs, tile_expert, tile_valid, w_gate, w_up, w_down)
    return _combine(x, ys, pos, wts, mods, mod_base, final_g, final_norm, n_ctx, t_dec)


def _grid_pos_embed(n_tokens):
    rows = n_tokens // GRID_W
    r = jnp.repeat(jnp.arange(rows, dtype=F32), GRID_W)
    col = jnp.tile(jnp.arange(GRID_W, dtype=F32), rows)
    quarter = D_MODEL // 4
    freq = jnp.exp(jnp.arange(quarter, dtype=F32) * (-math.log(POS_BASE) / quarter))

    def axis_embed(pos):
        a = pos[:, None] * freq[None, :]
        return jnp.concatenate([jnp.sin(a), jnp.cos(a)], axis=-1)

    return jnp.concatenate([axis_embed(r), axis_embed(col)], axis=-1)


def _split_in_weights(w_in):
    pm = 4 * N_HEADS * D_HEAD
    wg = w_in[:, pm:].reshape(-1, 2, 2, N_HEADS)
    return w_in[:, :pm], wg.transpose(0, 3, 1, 2).reshape(-1, 4 * N_HEADS)


def _head_params(first, second):
    return jnp.stack([first, second], axis=0).transpose(2, 0, 1).reshape(-1).astype(F32)


def _heads_per_step(t_len):
    return 4 if t_len <= 256 else 2


def kernel(x_prompt, x_sample, state_gdn_S, state_mlstm_C, state_mlstm_n, state_mlstm_m, c, c_ctx, ada_w, ada_b,
           norm1_g, norm2_g, gdn_w_in, gdn_conv_w, gdn_a_log, gdn_dt_bias, gdn_norm_g, gdn_w_out, mlstm_w_in,
           mlstm_gate_b, mlstm_norm_g, mlstm_w_out, router_w, router_bias, exp_w_gate, exp_w_up, exp_w_down,
           final_norm_g):
    bp, tp, d = x_prompt.shape
    bs, ts, _ = x_sample.shape
    n_ctx = bp * tp
    depth = ada_w.shape[0]
    assert n_ctx % ts == 0 and ts % TM_MOE == 0 and tp % TM_PROJ == 0 and bs + 1 <= N_MOD_ROWS

    pos = _grid_pos_embed(ts).astype(F32)
    x = jnp.concatenate([x_prompt.reshape(n_ctx, d), (x_sample + pos[None]).reshape(bs * ts, d)], axis=0)

    conds = jnp.concatenate([c_ctx[None, :], c, jnp.zeros((N_MOD_ROWS - 1 - bs, d), F32)], axis=0)
    mods = _ada_mods(conds, ada_w, ada_b).reshape(depth * N_MOD_ROWS * 6, 1, d)

    zeros_dh = jnp.zeros_like(gdn_a_log[0])
    ctx = dict(batch=bp, t_len=tp, row0=0, hb=_heads_per_step(tp), emit_state=True)
    dec = dict(batch=bs, t_len=ts, row0=n_ctx, hb=_heads_per_step(ts), emit_state=False)
    outs = {}
    for layer in range(depth):
        j = layer // 2
        mod_base = layer * N_MOD_ROWS * 6
        if layer % 2 == 0:
            w_main, w_gate = _split_in_weights(gdn_w_in[j])
            bias = _head_params(gdn_dt_bias[j], zeros_dh)
            mul = _head_params(-jnp.exp(gdn_a_log[j].astype(F32)), zeros_dh)
            main, grow = _inproj(x, mods, mod_base, norm1_g[layer], w_main, w_gate, bias, mul, "gdn", n_ctx, ts)
            o_ctx, s_new = _gdn_scan(main, grow, gdn_conv_w[j], gdn_norm_g[j], None, **ctx)
            (o_dec,) = _gdn_scan(main, grow, gdn_conv_w[j], gdn_norm_g[j], state_gdn_S[:, j].astype(F32), **dec)
            outs.setdefault("gdn", []).append(s_new)
            w_out = gdn_w_out[j]
        else:
            w_main, w_gate = _split_in_weights(mlstm_w_in[j])
            bias = _head_params(mlstm_gate_b[j, 0], mlstm_gate_b[j, 1])
            main, grow = _inproj(x, mods, mod_base, norm1_g[layer], w_main, w_gate, bias, jnp.zeros_like(bias),
                                 "mlstm", n_ctx, ts)
            o_ctx, c_new, n_new, m_new = _mlstm_scan(main, grow, mlstm_norm_g[j], None, **ctx)
            init = (state_mlstm_C[:, j].astype(F32),
                    state_mlstm_n[:, j].astype(F32).transpose(0, 2, 1, 3),
                    jnp.broadcast_to(state_mlstm_m[:, j].astype(F32).transpose(0, 2, 1)[..., None],
                                     (bs, N_HEADS, 2, D_HEAD)))
            (o_dec,) = _mlstm_scan(main, grow, mlstm_norm_g[j], init, **dec)
            outs.setdefault("mC", []).append(c_new)
            outs.setdefault("mn", []).append(n_new.transpose(0, 2, 1, 3))
            outs.setdefault("mm", []).append(m_new[..., 0].transpose(0, 2, 1))
            w_out = mlstm_w_out[j]
        x, hn, ri, wts, counts = _outproj_route(x, o_ctx, o_dec, w_out, mods, mod_base, norm2_g[layer], router_w, router_bias,
                                                n_ctx, ts)
        x = _moe(x, hn, ri, wts, counts, exp_w_gate[layer], exp_w_up[layer], exp_w_down[layer], mods, mod_base,
                 final_norm_g, layer == depth - 1, n_ctx, ts)

    y_prompt = x[0].reshape(bp, tp, d)
    y_sample = x[1].reshape(bs, ts, d)
    return (y_prompt, y_sample, jnp.stack(outs["gdn"], axis=1), jnp.stack(outs["mC"], axis=1),
            jnp.stack(outs["mn"], axis=1), jnp.stack(outs["mm"], axis=1))
```

```python
import functools
import math

import jax
import jax.numpy as jnp
from jax import lax
from jax.experimental import pallas as pl
from jax.experimental.pallas import tpu as pltpu

F32 = jnp.float32
BF16 = jnp.bfloat16
HIGHEST = lax.Precision.HIGHEST

D_MODEL = 1024
N_HEADS = 8
D_HEAD = 128
CHUNK = 64
N_EXPERTS = 16
N_GROUPS = 4
EXP_PER_GROUP = 4
D_FF = 512
EPS = 1e-6
GRID_W = 64
POS_BASE = 10000.0
N_MOD_ROWS = 8
VMEM_LIMIT = 56 * 1024 * 1024

TM_PROJ = 256
TM_MOE = 512
SCAN_CHAINS = 32


def _silu(x):
    return x * jax.nn.sigmoid(x)


def _softplus(x):
    return jnp.maximum(x, 0.0) + jnp.log(1.0 + jnp.exp(-jnp.abs(x)))


def _dot(a, b, precision=None):
    return jnp.dot(a, b, preferred_element_type=F32, precision=precision)


def _dot_nt(a, b, precision=None):
    return lax.dot_general(a, b, (((1,), (1,)), ((), ())), preferred_element_type=F32, precision=precision)


def _dot_tn(a, b, precision=None):
    return lax.dot_general(a, b, (((0,), (0,)), ((), ())), preferred_element_type=F32, precision=precision)


def _rms(x):
    return x * lax.rsqrt(jnp.mean(x * x, axis=-1, keepdims=True) + EPS)


def _ada_kernel(c_ref, w_ref, b_ref, o_ref):
    cs = _silu(c_ref[...]).astype(BF16)
    o_ref[0] = _dot(cs, w_ref[0].astype(BF16)) + b_ref[0]


def _ada_mods(conds, ada_w, ada_b):
    depth, d, n6 = ada_w.shape
    tn = 1536
    return pl.pallas_call(
        _ada_kernel,
        grid=(depth, n6 // tn),
        in_specs=[
            pl.BlockSpec((N_MOD_ROWS, d), lambda l, j: (0, 0)),
            pl.BlockSpec((1, d, tn), lambda l, j: (l, 0, j)),
            pl.BlockSpec((1, 1, tn), lambda l, j: (l, 0, j)),
        ],
        out_specs=pl.BlockSpec((1, N_MOD_ROWS, tn), lambda l, j: (l, 0, j)),
        out_shape=jax.ShapeDtypeStruct((depth, N_MOD_ROWS, n6), F32),
        compiler_params=pltpu.CompilerParams(dimension_semantics=("arbitrary", "arbitrary"),
                                             vmem_limit_bytes=VMEM_LIMIT),
        name="ada_mods",
    )(conds, ada_w, ada_b.reshape(depth, 1, n6))


def _split3(x):
    hi = x.astype(BF16)
    r1 = x - hi.astype(F32)
    mid = r1.astype(BF16)
    lo = (r1 - mid.astype(F32)).astype(BF16)
    return hi, mid, lo


def _inproj_kernel(x_ref, ng_ref, sc_ref, sh_ref, w_ref, wgt_ref, gb_ref, gm_ref, main_ref, gr_ref, *, kind):
    x = x_ref[...]
    hn = _rms(x) * ng_ref[...]
    hn = hn * (1.0 + sc_ref[0]) + sh_ref[0]
    hb = hn.astype(BF16)
    main_ref[...] = _dot(hb, w_ref[...])

    pre = _dot_nt(wgt_ref[...], hb) + gb_ref[...]
    row = lax.broadcasted_iota(jnp.int32, pre.shape, 0)
    first = (row % 4) < 2
    if kind == "gdn":
        act = jnp.where(first, gm_ref[...] * _softplus(pre), jax.nn.sigmoid(pre))
        scanned_kind = 0
    else:
        act = jnp.where(first, pre, -_softplus(-pre))
        scanned_kind = 1

    tm = x.shape[0]
    r = lax.broadcasted_iota(jnp.int32, (tm, tm), 0)
    c = lax.broadcasted_iota(jnp.int32, (tm, tm), 1)
    same = (r // CHUNK) == (c // CHUNK)
    before = jnp.where(same, jnp.where(r <= c, 1.0, 0.0), 0.0).astype(BF16)
    after = jnp.where(same, jnp.where(r >= c, 1.0, 0.0), 0.0).astype(BF16)
    parts = _split3(act)
    fwd = sum(_dot(p, before) for p in parts)
    bwd = sum(_dot(p, after) for p in parts)
    gt = jnp.where((row % 4) // 2 == scanned_kind, jnp.where(row % 2 == 0, fwd, bwd), act)
    for h in range(N_HEADS):
        for ch in range(tm // CHUNK):
            gr_ref[h, ch] = gt[4 * h:4 * h + 4, ch * CHUNK:(ch + 1) * CHUNK]


def _mod_row_of_tile(tm, n_ctx, t_dec):
    def f(i):
        r = i * tm
        return jnp.where(r < n_ctx, 0, 1 + (r - n_ctx) // t_dec)
    return f


def _inproj(x, mods, mod_base, norm_g, w_main, w_gate, gate_bias, gate_mul, kind, n_ctx, t_dec):
    n, d = x.shape
    tm = TM_PROJ
    pm = w_main.shape[1]
    ng = w_gate.shape[1]
    mrow = _mod_row_of_tile(tm, n_ctx, t_dec)
    sc_map = lambda i: (mod_base + mrow(i) * 6 + 1, 0, 0)
    sh_map = lambda i: (mod_base + mrow(i) * 6 + 0, 0, 0)
    const2 = lambda i: (0, 0)
    return pl.pallas_call(
        functools.partial(_inproj_kernel, kind=kind),
        grid=(n // tm,),
        in_specs=[
            pl.BlockSpec((tm, d), lambda i: (i, 0)),
            pl.BlockSpec((1, d), const2),
            pl.BlockSpec((1, 1, d), sc_map),
            pl.BlockSpec((1, 1, d), sh_map),
            pl.BlockSpec((d, pm), const2),
            pl.BlockSpec((ng, d), const2),
            pl.BlockSpec((ng, 1), const2),
            pl.BlockSpec((ng, 1), const2),
        ],
        out_specs=[
            pl.BlockSpec((tm, pm), lambda i: (i, 0)),
            pl.BlockSpec((N_HEADS, tm // CHUNK, 4, CHUNK), lambda i: (0, i, 0, 0)),
        ],
        out_shape=[
            jax.ShapeDtypeStruct((n, pm), F32),
            jax.ShapeDtypeStruct((N_HEADS, n // CHUNK, 4, CHUNK), F32),
        ],
        compiler_params=pltpu.CompilerParams(dimension_semantics=("arbitrary",), vmem_limit_bytes=VMEM_LIMIT),
        name="inproj_" + kind,
    )(x, norm_g.reshape(1, d), mods, mods, w_main.astype(BF16), w_gate.T.astype(BF16),
      gate_bias.reshape(ng, 1), gate_mul.reshape(ng, 1))


def _chunk_masks():
    r = lax.broadcasted_iota(jnp.int32, (CHUNK, CHUNK), 0)
    c = lax.broadcasted_iota(jnp.int32, (CHUNK, CHUNK), 1)
    return r >= c, r > c, r <= c, r < c


def _chunk_rows(c):
    if isinstance(c, int):
        return pl.ds(c * CHUNK, CHUNK)
    return pl.ds(pl.multiple_of(c * CHUNK, CHUNK), CHUNK)


def _head_cols(h):
    return slice(h * D_HEAD, (h + 1) * D_HEAD)


def _last_row(x, d):
    return x[CHUNK - 1:CHUNK, :] if d == 0 else x[0:1, :]


def _gate_columns(gr4):
    a = jnp.concatenate([gr4, jnp.zeros_like(gr4)], axis=0)
    hi = a.astype(BF16).astype(F32)
    mid = (a - hi).astype(BF16).astype(F32)
    lo = a - hi - mid
    parts = jnp.concatenate([hi, mid, lo, jnp.zeros_like(a)], axis=0).astype(BF16)
    j = lax.broadcasted_iota(jnp.int32, (32, 4 * D_HEAD), 0) % 8
    lane = lax.broadcasted_iota(jnp.int32, (32, 4 * D_HEAD), 1)
    return _dot_tn(parts, jnp.where(lane // D_HEAD == j, 1.0, 0.0).astype(BF16))


def _gate_column(cols, j):
    return cols[:, j * D_HEAD:(j + 1) * D_HEAD]


def _unit_tri_solve(a_list, rhs_list):
    mm = lambda x, y: _dot(x.astype(BF16), y.astype(BF16))
    r = lax.broadcasted_iota(jnp.int32, (CHUNK, CHUNK), 0)
    c = lax.broadcasted_iota(jnp.int32, (CHUNK, CHUNK), 1)
    same = (r // 16) == (c // 16)
    eye = jnp.where(r == c, 1.0, 0.0)
    d = [jnp.where(same, a, 0.0) for a in a_list]
    t = [eye - di for di in d]
    for _ in range(3):
        d = [mm(di, di) for di in d]
        t = [ti + mm(ti, di) for ti, di in zip(t, d)]
    b = [mm(ti, jnp.where(same, 0.0, a)) for ti, a in zip(t, a_list)]
    x = [mm(ti, ri) for ti, ri in zip(t, rhs_list)]
    b2 = [mm(bi, bi) for bi in b]
    x = [xi - mm(bi, xi) for xi, bi in zip(x, b)]
    return [xi + mm(bi, xi) for xi, bi in zip(x, b2)]


def _for_chunk_groups(nc, group, fn):
    if nc == group:
        fn(list(range(nc)))
        return

    def body(g, carry):
        fn([g * group + j for j in range(group)])
        return carry
    lax.fori_loop(0, nc // group, body, 0)


def _scan_specs(t_len, row0, hb, n_proj):
    rb = row0 // t_len
    nc = t_len // CHUNK
    ngrp = N_HEADS // hb
    specs = [pl.BlockSpec((t_len, hb * D_HEAD), functools.partial(lambda b, g, j: (rb + b, j * ngrp + g), j=j))
             for j in range(n_proj)]
    specs.append(pl.BlockSpec((hb, nc, 4, CHUNK), lambda b, g: (g, rb + b, 0, 0)))
    return specs


def _scan_call(kern, name, args, in_specs, out_specs, out_shape, scratch, batch, hb):
    return pl.pallas_call(
        kern,
        grid=(batch, N_HEADS // hb),
        in_specs=in_specs,
        out_specs=out_specs,
        out_shape=out_shape,
        scratch_shapes=scratch,
        compiler_params=pltpu.CompilerParams(dimension_semantics=("arbitrary", "arbitrary"),
                                             vmem_limit_bytes=VMEM_LIMIT),
        name=name,
    )(*args)


def _gdn_kernel(*refs, t_len, hb, has_init, emit_state):
    it = iter(refs)
    q_ref, k_ref, v_ref, z_ref, gr_ref, cwq_ref, cwk_ref, cwv_ref, ng_ref = (next(it) for _ in range(9))
    s0_ref = next(it) if has_init else None
    o_ref = next(it)
    s_ref = next(it) if emit_state else None
    qs, ks, vs, oacc, qp_s, op_s, km_s, nm_s, ge_s = (next(it) for _ in range(9))
    nc = t_len // CHUNK

    rows = lax.broadcasted_iota(jnp.int32, (t_len, 1), 0)

    def conv_silu(x, cw):
        xm = jnp.where(rows == 0, 0.0, pltpu.roll(x, 1, axis=0))
        xp = jnp.where(rows == t_len - 1, 0.0, pltpu.roll(x, t_len - 1, axis=0))
        return _silu(xm * cw[0:1] + x * cw[1:2] + xp * cw[2:3])

    def l2n(x):
        return x * lax.rsqrt(jnp.sum(x * x, axis=-1, keepdims=True) + EPS)

    for h in range(hb):
        hc = _head_cols(h)
        qs[:, hc] = l2n(conv_silu(q_ref[:, hc], cwq_ref[:, hc])) * (D_HEAD ** -0.5)
        ks[:, hc] = l2n(conv_silu(k_ref[:, hc], cwk_ref[:, hc]))
        vs[:, hc] = conv_silu(v_ref[:, hc], cwv_ref[:, hc])
    oacc[...] = jnp.zeros_like(oacc)

    lo_i, lo_s, up_i, up_s = _chunk_masks()

    def slot(h, d, c):
        return (h * 2 + d) * nc + c

    def intra(chunks):
        items = [(h, c, d) for h in range(hb) for c in chunks for d in range(2)]
        cols = [[_gate_columns(gr_ref[h, c]) for c in chunks] for h in range(hb)]
        a_list, rhs_list, keep = [], [], []
        for n_item, (h, c, d) in enumerate(items):
            sl, hc = _chunk_rows(c), _head_cols(h)
            q, k, v = qs[sl, hc], ks[sl, hc], vs[sl, hc]
            gr4 = gr_ref[h, c]
            col4 = cols[h][(n_item // 2) % len(chunks)]
            g_col, beta = _gate_column(col4, d), _gate_column(col4, 2 + d)
            incl, strict = (lo_i, lo_s) if d == 0 else (up_i, up_s)
            decay = jnp.exp(jnp.where(incl, g_col[:, :CHUNK] - gr4[d:d + 1, :], -jnp.inf))
            kb = k * beta
            k16 = k.astype(BF16)
            a_list.append(jnp.where(strict, _dot_nt(kb.astype(BF16), k16) * decay, 0.0))
            e_g = jnp.exp(g_col)
            rhs_list.append(jnp.concatenate([v * beta, kb * e_g], axis=1))
            qk16 = jnp.where(incl, _dot_nt(q.astype(BF16), k16) * decay, 0.0).astype(BF16)
            g_last = _last_row(g_col, d)
            ge_s[slot(h, d, c)] = jnp.broadcast_to(jnp.exp(g_last), (8, D_HEAD))
            keep.append((qk16, q * e_g, (k * jnp.exp(g_last - g_col)).astype(BF16)))
        sols = [s.astype(BF16) for s in _unit_tri_solve(a_list, rhs_list)]
        qw = [_dot(kp[0], s) for kp, s in zip(keep, sols)]
        kw = [_dot_tn(kp[2], s) for kp, s in zip(keep, sols)]
        for (h, c, d), kp, qwi, kwi in zip(items, keep, qw, kw):
            i = h * 2 + d
            sl = _chunk_rows(c)
            op_s[i, sl, :] = qwi[:, :D_HEAD]
            qp_s[i, sl, :] = (kp[1] - qwi[:, D_HEAD:]).astype(BF16)
            nm_s[slot(h, d, c)] = kwi[:, :D_HEAD]
            km_s[slot(h, d, c)] = (-kwi[:, D_HEAD:]).astype(BF16)

    _for_chunk_groups(nc, min(nc, max(1, SCAN_CHAINS // (2 * hb))), intra)

    chains = [(h, d) for h in range(hb) for d in range(2)]

    def inter_body(i, carry):
        cs = (i, nc - 1 - i)
        s16 = [s.astype(BF16) for s in carry]
        outs = [_dot(qp_s[h * 2 + d, _chunk_rows(cs[d]), :], s) for (h, d), s in zip(chains, s16)]
        upds = [_dot(km_s[slot(h, d, cs[d])], s) for (h, d), s in zip(chains, s16)]
        for (h, d), o in zip(chains, outs):
            oacc[_chunk_rows(cs[d]), _head_cols(h)] += o + op_s[h * 2 + d, _chunk_rows(cs[d]), :]
        return tuple(ge_s[slot(h, d, cs[d])][0:1, :] * s + u + nm_s[slot(h, d, cs[d])]
                     for (h, d), s, u in zip(chains, carry, upds))

    if has_init:
        init = tuple(s0_ref[0, d, h] for h, d in chains)
    else:
        init = tuple(jnp.zeros((D_HEAD, D_HEAD), F32) for _ in chains)
    fin = lax.fori_loop(0, nc, inter_body, init)
    if emit_state:
        for (h, d), s in zip(chains, fin):
            s_ref[0, d, h] = s
    for h in range(hb):
        hc = _head_cols(h)
        o_ref[:, hc] = (_rms(oacc[:, hc]) * ng_ref[...] * _silu(z_ref[:, hc])).astype(o_ref.dtype)


def _gdn_scan(main, grow, conv_w, norm_g, s0, *, batch, t_len, row0, hb, emit_state):
    nc = t_len // CHUNK
    ngrp = N_HEADS // hb
    has_init = s0 is not None
    in_specs = _scan_specs(t_len, row0, hb, 4)
    in_specs += [pl.BlockSpec((3, hb * D_HEAD), functools.partial(lambda b, g, j: (0, j * ngrp + g), j=j))
                 for j in range(3)]
    in_specs.append(pl.BlockSpec((1, D_HEAD), lambda b, g: (0, 0)))
    args = [main, main, main, main, grow, conv_w, conv_w, conv_w, norm_g.reshape(1, D_HEAD)]
    state_spec = pl.BlockSpec((1, 2, hb, D_HEAD, D_HEAD), lambda b, g: (b, 0, g, 0, 0))
    if has_init:
        in_specs.append(state_spec)
        args.append(s0)
    out_specs = [pl.BlockSpec((t_len, hb * D_HEAD), lambda b, g: (b, g))]
    out_shape = [jax.ShapeDtypeStruct((batch * t_len, N_HEADS * D_HEAD), BF16)]
    if emit_state:
        out_specs.append(state_spec)
        out_shape.append(jax.ShapeDtypeStruct((batch, 2, N_HEADS, D_HEAD, D_HEAD), F32))
    scratch = ([pltpu.VMEM((t_len, hb * D_HEAD), F32) for _ in range(4)]
               + [pltpu.VMEM((2 * hb, t_len, D_HEAD), BF16),
                  pltpu.VMEM((2 * hb, t_len, D_HEAD), F32),
                  pltpu.VMEM((2 * hb * nc, D_HEAD, D_HEAD), BF16),
                  pltpu.VMEM((2 * hb * nc, D_HEAD, D_HEAD), F32),
                  pltpu.VMEM((2 * hb * nc, 8, D_HEAD), F32)])
    kern = functools.partial(_gdn_kernel, t_len=t_len, hb=hb, has_init=has_init, emit_state=emit_state)
    return _scan_call(kern, "gdn_scan_t%d" % t_len, args, in_specs, out_specs, out_shape, scratch, batch, hb)


def _mlstm_kernel(*refs, t_len, hb, has_init, emit_state):
    it = iter(refs)
    q_ref, k_ref, v_ref, og_ref, gr_ref, ng_ref = (next(it) for _ in range(6))
    if has_init:
        c0_ref, n0_ref, m0_ref = (next(it) for _ in range(3))
    o_ref = next(it)
    if emit_state:
        c_ref, n_ref, m_ref = (next(it) for _ in range(3))
    hacc, q16_s, v1_s, qk_s, ld_s, bb_s, lm_s, lw_s, sc_s = (next(it) for _ in range(9))
    nc = t_len // CHUNK

    hacc[...] = jnp.zeros_like(hacc)
    ones = jnp.ones((t_len, D_HEAD), BF16)
    for h in range(hb):
        hc = _head_cols(h)
        q16_s[:, hc] = (q_ref[:, hc] * (D_HEAD ** -0.5)).astype(BF16)
        v1_s[:, h * 2 * D_HEAD:(h * 2 + 1) * D_HEAD] = v_ref[:, hc].astype(BF16)
        v1_s[:, (h * 2 + 1) * D_HEAD:(h * 2 + 2) * D_HEAD] = ones
    lo_i, _, up_i, _ = _chunk_masks()

    def slot(h, d, c):
        return (h * 2 + d) * nc + c

    def intra(chunks):
        for h in range(hb):
            for c in chunks:
                sl, hc = _chunk_rows(c), _head_cols(h)
                qk_s[h, sl, :] = _dot_nt(q16_s[sl, hc], k_ref[sl, hc].astype(BF16))
                gr4 = gr_ref[h, c]
                col4 = _gate_columns(gr4)
                for d in range(2):
                    i = h * 2 + d
                    b_col, i_col = _gate_column(col4, 2 + d), _gate_column(col4, d)
                    b_last = _last_row(b_col, d)
                    lwe = b_last - b_col + i_col
                    log_d = jnp.where(lo_i if d == 0 else up_i,
                                      b_col[:, :CHUNK] - gr4[2 + d:3 + d, :] + gr4[d:d + 1, :], -jnp.inf)
                    ld_s[i, sl, :] = log_d
                    lm_s[i, sl, :] = jnp.broadcast_to(jnp.max(log_d, axis=-1, keepdims=True), (CHUNK, D_HEAD))
                    bb_s[i, sl, :] = b_col
                    lw_s[i, sl, :] = lwe
                    sc_s[slot(h, d, c), 0:1, :] = b_last
                    sc_s[slot(h, d, c), 1:2, :] = jnp.max(lwe, axis=0, keepdims=True)

    _for_chunk_groups(nc, min(nc, 4), intra)

    chains = [(h, d) for h in range(hb) for d in range(2)]

    def body(step, carry):
        cs = (step, nc - 1 - step)
        state, m = zip(*carry)
        sls = [_chunk_rows(cs[d]) for h, d in chains]
        idx = [h * 2 + d for h, d in chains]
        v1 = [v1_s[sl, h * 2 * D_HEAD:(h * 2 + 2) * D_HEAD] for (h, d), sl in zip(chains, sls)]
        log_last = [sc_s[slot(h, d, cs[d]), 0:1, :] + mi for (h, d), mi in zip(chains, m)]
        m_new = [jnp.maximum(ll, sc_s[slot(h, d, cs[d]), 1:2, :]) for (h, d), ll in zip(chains, log_last)]
        dec = [jnp.exp(ll - mn) for ll, mn in zip(log_last, m_new)]
        kw = [(k_ref[sl, _head_cols(h)] * jnp.exp(lw_s[i, sl, :] - mn)).astype(BF16)
              for (h, d), sl, i, mn in zip(chains, sls, idx, m_new)]
        upd = [_dot_tn(a, b) for a, b in zip(kw, v1)]
        qc = [_dot(q16_s[sl, _head_cols(h)], s.astype(BF16)) for (h, d), sl, s in zip(chains, sls, state)]
        log_inter = [bb_s[i, sl, :] + mi for i, sl, mi in zip(idx, sls, m)]
        mt = [jnp.maximum(li, lm_s[i, sl, :]) for li, i, sl in zip(log_inter, idx, sls)]
        s_inter = [jnp.exp(li - t) for li, t in zip(log_inter, mt)]
        p = [(jnp.exp(ld_s[i, sl, :] - t[:, :CHUNK]) * qk_s[h, sl, :]).astype(BF16)
             for (h, d), i, sl, t in zip(chains, idx, sls, mt)]
        pv = [_dot(a, b) for a, b in zip(p, v1)]
        for (h, d), sl, si, qci, pvi, t in zip(chains, sls, s_inter, qc, pv, mt):
            num = si * qci[:, :D_HEAD] + pvi[:, :D_HEAD]
            den = si * qci[:, D_HEAD:] + pvi[:, D_HEAD:]
            hacc[sl, _head_cols(h)] += num / jnp.maximum(jnp.abs(den), jnp.exp(-t))
        return tuple((jnp.concatenate([dc, dc], axis=1) * s + u, mn)
                     for dc, s, u, mn in zip(dec, state, upd, m_new))

    if has_init:
        init = tuple((jnp.concatenate([c0_ref[0, d, h],
                                       jnp.transpose(jnp.broadcast_to(n0_ref[0, h, d:d + 1, :], (D_HEAD, D_HEAD)))],
                                      axis=1),
                      m0_ref[0, h, d:d + 1, :]) for h, d in chains)
    else:
        init = tuple((jnp.zeros((D_HEAD, 2 * D_HEAD), F32), jnp.zeros((1, D_HEAD), F32)) for _ in chains)
    fin = lax.fori_loop(0, nc, body, init)
    if emit_state:
        for (h, d), (s, m) in zip(chains, fin):
            c_ref[0, d, h] = s[:, :D_HEAD]
            n_ref[0, h, d:d + 1, :] = jnp.transpose(s[:, D_HEAD:])[0:1, :]
            m_ref[0, h, d:d + 1, :] = m
    for h in range(hb):
        hc = _head_cols(h)
        o_ref[:, hc] = (_rms(hacc[:, hc]) * ng_ref[...] * jax.nn.sigmoid(og_ref[:, hc])).astype(o_ref.dtype)


def _mlstm_scan(main, grow, norm_g, init, *, batch, t_len, row0, hb, emit_state):
    nc = t_len // CHUNK
    has_init = init is not None
    in_specs = _scan_specs(t_len, row0, hb, 4)
    in_specs.append(pl.BlockSpec((1, D_HEAD), lambda b, g: (0, 0)))
    args = [main, main, main, main, grow, norm_g.reshape(1, D_HEAD)]
    c_spec = pl.BlockSpec((1, 2, hb, D_HEAD, D_HEAD), lambda b, g: (b, 0, g, 0, 0))
    v_spec = pl.BlockSpec((1, hb, 2, D_HEAD), lambda b, g: (b, g, 0, 0))
    if has_init:
        in_specs += [c_spec, v_spec, v_spec]
        args += list(init)
    out_specs = [pl.BlockSpec((t_len, hb * D_HEAD), lambda b, g: (b, g))]
    out_shape = [jax.ShapeDtypeStruct((batch * t_len, N_HEADS * D_HEAD), BF16)]
    if emit_state:
        out_specs += [c_spec, v_spec, v_spec]
        out_shape += [jax.ShapeDtypeStruct((batch, 2, N_HEADS, D_HEAD, D_HEAD), F32),
                      jax.ShapeDtypeStruct((batch, N_HEADS, 2, D_HEAD), F32),
                      jax.ShapeDtypeStruct((batch, N_HEADS, 2, D_HEAD), F32)]
    scratch = [pltpu.VMEM((t_len, hb * D_HEAD), F32),
               pltpu.VMEM((t_len, hb * D_HEAD), BF16),
               pltpu.VMEM((t_len, hb * 2 * D_HEAD), BF16),
               pltpu.VMEM((hb, t_len, CHUNK), F32),
               pltpu.VMEM((2 * hb, t_len, CHUNK), F32),
               pltpu.VMEM((2 * hb, t_len, D_HEAD), F32),
               pltpu.VMEM((2 * hb, t_len, D_HEAD), F32),
               pltpu.VMEM((2 * hb, t_len, D_HEAD), F32),
               pltpu.VMEM((2 * hb * nc, 8, D_HEAD), F32)]
    kern = functools.partial(_mlstm_kernel, t_len=t_len, hb=hb, has_init=has_init, emit_state=emit_state)
    return _scan_call(kern, "mlstm_scan_t%d" % t_len, args, in_specs, out_specs, out_shape, scratch, batch, hb)


def _top2_of4(v):
    best, i1 = v[0], jnp.zeros(v[0].shape, jnp.int32)
    for j in range(1, 4):
        take = v[j] > best
        i1 = jnp.where(take, j, i1)
        best = jnp.where(take, v[j], best)
    best2, i2 = None, None
    for j in range(4):
        vj = jnp.where(i1 == j, -jnp.inf, v[j])
        if best2 is None:
            best2, i2 = vj, jnp.zeros(v[0].shape, jnp.int32)
        else:
            take = vj > best2
            i2 = jnp.where(take, j, i2)
            best2 = jnp.where(take, vj, best2)
    return i1, i2


def _pick(rows, idx):
    out = rows[0]
    for j in range(1, len(rows)):
        out = jnp.where(idx == j, rows[j], out)
    return out


def _outproj_kernel(x_ref, oc_ref, od_ref, w_ref, g1_ref, ng_ref, sc_ref, sh_ref, rw_ref, rb_ref, xo_ref, hn_ref,
                    gt_ref, *, ctx_tiles):
    o = jnp.where(pl.program_id(0) < ctx_tiles, oc_ref[...], od_ref[...])
    xn = x_ref[...] + g1_ref[0] * _dot(o, w_ref[...])
    xo_ref[...] = xn
    hn = _rms(xn) * ng_ref[...]
    hn = hn * (1.0 + sc_ref[0]) + sh_ref[0]
    hn_ref[...] = hn.astype(BF16)

    logits = _dot_nt(rw_ref[...], hn, HIGHEST)
    ex = jnp.exp(logits - jnp.max(logits, axis=0, keepdims=True))
    probs = ex / jnp.sum(ex, axis=0, keepdims=True)
    sel = probs + rb_ref[...]
    sel_rows = [sel[e:e + 1, :] for e in range(N_EXPERTS)]
    prob_rows = [probs[e:e + 1, :] for e in range(N_EXPERTS)]
    scores = []
    for g in range(N_GROUPS):
        r = sel_rows[4 * g:4 * g + 4]
        a, b = jnp.maximum(r[0], r[1]), jnp.minimum(r[0], r[1])
        c, d = jnp.maximum(r[2], r[3]), jnp.minimum(r[2], r[3])
        scores.append(jnp.maximum(a, c) + jnp.maximum(jnp.minimum(a, c), jnp.maximum(b, d)))
    best, grp = scores[0], jnp.zeros(scores[0].shape, jnp.int32)
    for g in range(1, N_GROUPS):
        take = scores[g] > best
        grp = jnp.where(take, g, grp)
        best = jnp.where(take, scores[g], best)
    sel_in = [_pick([sel_rows[4 * g + j] for g in range(N_GROUPS)], grp) for j in range(4)]
    prob_in = [_pick([prob_rows[4 * g + j] for g in range(N_GROUPS)], grp) for j in range(4)]
    i1, i2 = _top2_of4(sel_in)
    w1, w2 = _pick(prob_in, i1), _pick(prob_in, i2)
    tot = w1 + w2
    eidx = lax.broadcasted_iota(jnp.int32, logits.shape, 0)
    gt_ref[...] = (jnp.where(eidx == grp * 4 + i1, w1 / tot, 0.0)
                   + jnp.where(eidx == grp * 4 + i2, w2 / tot, 0.0))


def _outproj_route(x, o_ctx, o_dec, w_out, mods, mod_base, norm_g, router_w, router_bias, n_ctx, t_dec):
    n, d = x.shape
    tm = TM_PROJ
    ctx_tiles = n_ctx // tm
    mrow = _mod_row_of_tile(tm, n_ctx, t_dec)
    mod_map = lambda k: (lambda i: (mod_base + mrow(i) * 6 + k, 0, 0))
    const2 = lambda i: (0, 0)
    return pl.pallas_call(
        functools.partial(_outproj_kernel, ctx_tiles=ctx_tiles),
        grid=(n // tm,),
        in_specs=[
            pl.BlockSpec((tm, d), lambda i: (i, 0)),
            pl.BlockSpec((tm, d), lambda i: (jnp.minimum(i, ctx_tiles - 1), 0)),
            pl.BlockSpec((tm, d), lambda i: (jnp.maximum(i - ctx_tiles, 0), 0)),
            pl.BlockSpec((d, d), const2),
            pl.BlockSpec((1, 1, d), mod_map(2)),
            pl.BlockSpec((1, d), const2),
            pl.BlockSpec((1, 1, d), mod_map(4)),
            pl.BlockSpec((1, 1, d), mod_map(3)),
            pl.BlockSpec((N_EXPERTS, d), const2),
            pl.BlockSpec((N_EXPERTS, 1), const2),
        ],
        out_specs=[
            pl.BlockSpec((tm, d), lambda i: (i, 0)),
            pl.BlockSpec((tm, d), lambda i: (i, 0)),
            pl.BlockSpec((N_EXPERTS, tm), lambda i: (0, i)),
        ],
        out_shape=[
            jax.ShapeDtypeStruct((n, d), F32),
            jax.ShapeDtypeStruct((n, d), BF16),
            jax.ShapeDtypeStruct((N_EXPERTS, n), F32),
        ],
        compiler_params=pltpu.CompilerParams(dimension_semantics=("arbitrary",), vmem_limit_bytes=VMEM_LIMIT),
        name="outproj_route",
    )(x, o_ctx, o_dec, w_out.astype(BF16), mods, norm_g.reshape(1, d), mods, mods, router_w.T, router_bias.reshape(N_EXPERTS, 1))


def _moe_kernel(hn_ref, gate_ref, wg_ref, wu_ref, wd_ref, x_ref, g2_ref, fg_ref, *rest, ctx_tiles):
    *o_refs, acc_ref = rest
    e = pl.program_id(1)

    @pl.when(e == 0)
    def _():
        acc_ref[...] = jnp.zeros_like(acc_ref)

    h = hn_ref[...]
    hid = _silu(_dot(h, wg_ref[0])) * _dot(h, wu_ref[0])
    acc_ref[...] += gate_ref[0] * _dot(hid.astype(BF16), wd_ref[0])

    @pl.when(e == N_EXPERTS - 1)
    def _():
        xn = x_ref[...] + g2_ref[0] * acc_ref[...]
        if ctx_tiles is None:
            o_refs[0][...] = xn
            return
        xn = _rms(xn) * fg_ref[...]

        @pl.when(pl.program_id(0) < ctx_tiles)
        def _():
            o_refs[0][...] = xn

        @pl.when(pl.program_id(0) >= ctx_tiles)
        def _():
            o_refs[1][...] = xn


def _moe(x, hn, gates_t, w_gate, w_up, w_down, mods, mod_base, final_g, final_norm, n_ctx, t_dec):
    n, d = x.shape
    tm = TM_MOE
    mrow = _mod_row_of_tile(tm, n_ctx, t_dec)
    ctx_tiles = n_ctx // tm if final_norm else None
    if final_norm:
        out_specs = [pl.BlockSpec((tm, d), lambda i, e: (jnp.minimum(i, ctx_tiles - 1), 0)),
                     pl.BlockSpec((tm, d), lambda i, e: (jnp.maximum(i - ctx_tiles, 0), 0))]
        out_shape = [jax.ShapeDtypeStruct((n_ctx, d), F32), jax.ShapeDtypeStruct((n - n_ctx, d), F32)]
    else:
        out_specs = pl.BlockSpec((tm, d), lambda i, e: (i, 0))
        out_shape = jax.ShapeDtypeStruct((n, d), F32)
    return pl.pallas_call(
        functools.partial(_moe_kernel, ctx_tiles=ctx_tiles),
        grid=(n // tm, N_EXPERTS),
        in_specs=[
            pl.BlockSpec((tm, d), lambda i, e: (i, 0)),
            pl.BlockSpec((1, tm, 1), lambda i, e: (e, i, 0)),
            pl.BlockSpec((1, d, D_FF), lambda i, e: (e, 0, 0)),
            pl.BlockSpec((1, d, D_FF), lambda i, e: (e, 0, 0)),
            pl.BlockSpec((1, D_FF, d), lambda i, e: (e, 0, 0)),
            pl.BlockSpec((tm, d), lambda i, e: (i, 0)),
            pl.BlockSpec((1, 1, d), lambda i, e: (mod_base + mrow(i) * 6 + 5, 0, 0)),
            pl.BlockSpec((1, d), lambda i, e: (0, 0)),
        ],
        out_specs=out_specs,
        out_shape=out_shape,
        scratch_shapes=[pltpu.VMEM((tm, d), F32)],
        compiler_params=pltpu.CompilerParams(dimension_semantics=("arbitrary", "arbitrary"),
                                             vmem_limit_bytes=VMEM_LIMIT),
        name="moe_dense",
    )(hn, gates_t.reshape(N_EXPERTS, n, 1), w_gate.astype(BF16), w_up.astype(BF16), w_down.astype(BF16), x, mods,
      final_g.reshape(1, d))


def _grid_pos_embed(n_tokens):
    rows = n_tokens // GRID_W
    r = jnp.repeat(jnp.arange(rows, dtype=F32), GRID_W)
    col = jnp.tile(jnp.arange(GRID_W, dtype=F32), rows)
    quarter = D_MODEL // 4
    freq = jnp.exp(jnp.arange(quarter, dtype=F32) * (-math.log(POS_BASE) / quarter))

    def axis_embed(pos):
        a = pos[:, None] * freq[None, :]
        return jnp.concatenate([jnp.sin(a), jnp.cos(a)], axis=-1)

    return jnp.concatenate([axis_embed(r), axis_embed(col)], axis=-1)


def _split_in_weights(w_in):
    pm = 4 * N_HEADS * D_HEAD
    wg = w_in[:, pm:].reshape(-1, 2, 2, N_HEADS)
    return w_in[:, :pm], wg.transpose(0, 3, 1, 2).reshape(-1, 4 * N_HEADS)


def _head_params(first, second):
    return jnp.stack([first, second], axis=0).transpose(2, 0, 1).reshape(-1).astype(F32)


def _heads_per_step(t_len):
    return 4 if t_len <= 256 else 2


def kernel(x_prompt, x_sample, state_gdn_S, state_mlstm_C, state_mlstm_n, state_mlstm_m, c, c_ctx, ada_w, ada_b,
           norm1_g, norm2_g, gdn_w_in, gdn_conv_w, gdn_a_log, gdn_dt_bias, gdn_norm_g, gdn_w_out, mlstm_w_in,
           mlstm_gate_b, mlstm_norm_g, mlstm_w_out, router_w, router_bias, exp_w_gate, exp_w_up, exp_w_down,
           final_norm_g):
    bp, tp, d = x_prompt.shape
    bs, ts, _ = x_sample.shape
    n_ctx = bp * tp
    depth = ada_w.shape[0]
    assert n_ctx % ts == 0 and ts % TM_MOE == 0 and tp % TM_PROJ == 0 and bs + 1 <= N_MOD_ROWS

    pos = _grid_pos_embed(ts).astype(F32)
    x = jnp.concatenate([x_prompt.reshape(n_ctx, d), (x_sample + pos[None]).reshape(bs * ts, d)], axis=0)

    conds = jnp.concatenate([c_ctx[None, :], c, jnp.zeros((N_MOD_ROWS - 1 - bs, d), F32)], axis=0)
    mods = _ada_mods(conds, ada_w, ada_b).reshape(depth * N_MOD_ROWS * 6, 1, d)

    zeros_dh = jnp.zeros_like(gdn_a_log[0])
    ctx = dict(batch=bp, t_len=tp, row0=0, hb=_heads_per_step(tp), emit_state=True)
    dec = dict(batch=bs, t_len=ts, row0=n_ctx, hb=_heads_per_step(ts), emit_state=False)
    outs = {}
    for layer in range(depth):
        j = layer // 2
        mod_base = layer * N_MOD_ROWS * 6
        if layer % 2 == 0:
            w_main, w_gate = _split_in_weights(gdn_w_in[j])
            bias = _head_params(gdn_dt_bias[j], zeros_dh)
            mul = _head_params(-jnp.exp(gdn_a_log[j].astype(F32)), zeros_dh)
            main, grow = _inproj(x, mods, mod_base, norm1_g[layer], w_main, w_gate, bias, mul, "gdn", n_ctx, ts)
            o_ctx, s_new = _gdn_scan(main, grow, gdn_conv_w[j], gdn_norm_g[j], None, **ctx)
            (o_dec,) = _gdn_scan(main, grow, gdn_conv_w[j], gdn_norm_g[j], state_gdn_S[:, j].astype(F32), **dec)
            outs.setdefault("gdn", []).append(s_new)
            w_out = gdn_w_out[j]
        else:
            w_main, w_gate = _split_in_weights(mlstm_w_in[j])
            bias = _head_params(mlstm_gate_b[j, 0], mlstm_gate_b[j, 1])
            main, grow = _inproj(x, mods, mod_base, norm1_g[layer], w_main, w_gate, bias, jnp.zeros_like(bias),
                                 "mlstm", n_ctx, ts)
            o_ctx, c_new, n_new, m_new = _mlstm_scan(main, grow, mlstm_norm_g[j], None, **ctx)
            init = (state_mlstm_C[:, j].astype(F32),
                    state_mlstm_n[:, j].astype(F32).transpose(0, 2, 1, 3),
                    jnp.broadcast_to(state_mlstm_m[:, j].astype(F32).transpose(0, 2, 1)[..., None],
                                     (bs, N_HEADS, 2, D_HEAD)))
            (o_dec,) = _mlstm_scan(main, grow, mlstm_norm_g[j], init, **dec)
            outs.setdefault("mC", []).append(c_new)
            outs.setdefault("mn", []).append(n_new.transpose(0, 2, 1, 3))
            outs.setdefault("mm", []).append(m_new[..., 0].transpose(0, 2, 1))
            w_out = mlstm_w_out[j]
        x, hn, gates_t = _outproj_route(x, o_ctx, o_dec, w_out, mods, mod_base, norm2_g[layer], router_w, router_bias,
                                        n_ctx, ts)
        x = _moe(x, hn, gates_t, exp_w_gate[layer], exp_w_up[layer], exp_w_down[layer], mods, mod_base,
                 final_norm_g, layer == depth - 1, n_ctx, ts)

    y_prompt = x[0].reshape(bp, tp, d)
    y_sample = x[1].reshape(bs, ts, d)
    return (y_prompt, y_sample, jnp.stack(outs["gdn"], axis=1), jnp.stack(outs["mC"], axis=1),
            jnp.stack(outs["mn"], axis=1), jnp.stack(outs["mm"], axis=1))
```

```python
import functools
import math

import jax
import jax.numpy as jnp
from jax import lax
from jax.experimental import pallas as pl
from jax.experimental.pallas import tpu as pltpu

F32 = jnp.float32
BF16 = jnp.bfloat16
HIGHEST = lax.Precision.HIGHEST

D_MODEL = 1024
N_HEADS = 8
D_HEAD = 128
CHUNK = 64
N_EXPERTS = 16
N_GROUPS = 4
EXP_PER_GROUP = 4
D_FF = 512
EPS = 1e-6
GRID_W = 64
POS_BASE = 10000.0
N_MOD_ROWS = 8
VMEM_LIMIT = 56 * 1024 * 1024

TM_PROJ = 256
TM_MOE = 256
GRANULE = 8
TILE_GRANULES = 2 * TM_PROJ // GRANULE + N_EXPERTS
SCAN_CHAINS = 32


def _silu(x):
    return x * jax.nn.sigmoid(x)


def _softplus(x):
    return jnp.maximum(x, 0.0) + jnp.log(1.0 + jnp.exp(-jnp.abs(x)))


def _dot(a, b, precision=None):
    return jnp.dot(a, b, preferred_element_type=F32, precision=precision)


def _dot_nt(a, b, precision=None):
    return lax.dot_general(a, b, (((1,), (1,)), ((), ())), preferred_element_type=F32, precision=precision)


def _dot_tn(a, b, precision=None):
    return lax.dot_general(a, b, (((0,), (0,)), ((), ())), preferred_element_type=F32, precision=precision)


def _rms(x):
    return x * lax.rsqrt(jnp.mean(x * x, axis=-1, keepdims=True) + EPS)


def _ada_kernel(c_ref, w_ref, b_ref, o_ref):
    cs = _silu(c_ref[...]).astype(BF16)
    o_ref[0] = _dot(cs, w_ref[0].astype(BF16)) + b_ref[0]


def _ada_mods(conds, ada_w, ada_b):
    depth, d, n6 = ada_w.shape
    tn = 1536
    return pl.pallas_call(
        _ada_kernel,
        grid=(depth, n6 // tn),
        in_specs=[
            pl.BlockSpec((N_MOD_ROWS, d), lambda l, j: (0, 0)),
            pl.BlockSpec((1, d, tn), lambda l, j: (l, 0, j)),
            pl.BlockSpec((1, 1, tn), lambda l, j: (l, 0, j)),
        ],
        out_specs=pl.BlockSpec((1, N_MOD_ROWS, tn), lambda l, j: (l, 0, j)),
        out_shape=jax.ShapeDtypeStruct((depth, N_MOD_ROWS, n6), F32),
        compiler_params=pltpu.CompilerParams(dimension_semantics=("arbitrary", "arbitrary"),
                                             vmem_limit_bytes=VMEM_LIMIT),
        name="ada_mods",
    )(conds, ada_w, ada_b.reshape(depth, 1, n6))


def _split3(x):
    hi = x.astype(BF16)
    r1 = x - hi.astype(F32)
    mid = r1.astype(BF16)
    lo = (r1 - mid.astype(F32)).astype(BF16)
    return hi, mid, lo


def _inproj_kernel(x_ref, ng_ref, sc_ref, sh_ref, w_ref, wgt_ref, gb_ref, gm_ref, main_ref, gr_ref, *, kind):
    x = x_ref[...]
    hn = _rms(x) * ng_ref[...]
    hn = hn * (1.0 + sc_ref[0]) + sh_ref[0]
    hb = hn.astype(BF16)
    main_ref[...] = _dot(hb, w_ref[...])

    pre = _dot_nt(wgt_ref[...], hb) + gb_ref[...]
    row = lax.broadcasted_iota(jnp.int32, pre.shape, 0)
    first = (row % 4) < 2
    if kind == "gdn":
        act = jnp.where(first, gm_ref[...] * _softplus(pre), jax.nn.sigmoid(pre))
        scanned_kind = 0
    else:
        act = jnp.where(first, pre, -_softplus(-pre))
        scanned_kind = 1

    tm = x.shape[0]
    r = lax.broadcasted_iota(jnp.int32, (tm, tm), 0)
    c = lax.broadcasted_iota(jnp.int32, (tm, tm), 1)
    same = (r // CHUNK) == (c // CHUNK)
    before = jnp.where(same, jnp.where(r <= c, 1.0, 0.0), 0.0).astype(BF16)
    after = jnp.where(same, jnp.where(r >= c, 1.0, 0.0), 0.0).astype(BF16)
    parts = _split3(act)
    fwd = sum(_dot(p, before) for p in parts)
    bwd = sum(_dot(p, after) for p in parts)
    gt = jnp.where((row % 4) // 2 == scanned_kind, jnp.where(row % 2 == 0, fwd, bwd), act)
    for h in range(N_HEADS):
        for ch in range(tm // CHUNK):
            gr_ref[h, ch] = gt[4 * h:4 * h + 4, ch * CHUNK:(ch + 1) * CHUNK]


def _mod_row_of_tile(tm, n_ctx, t_dec):
    def f(i):
        r = i * tm
        return jnp.where(r < n_ctx, 0, 1 + (r - n_ctx) // t_dec)
    return f


def _inproj(x, mods, mod_base, norm_g, w_main, w_gate, gate_bias, gate_mul, kind, n_ctx, t_dec):
    n, d = x.shape
    tm = TM_PROJ
    pm = w_main.shape[1]
    ng = w_gate.shape[1]
    mrow = _mod_row_of_tile(tm, n_ctx, t_dec)
    sc_map = lambda i: (mod_base + mrow(i) * 6 + 1, 0, 0)
    sh_map = lambda i: (mod_base + mrow(i) * 6 + 0, 0, 0)
    const2 = lambda i: (0, 0)
    return pl.pallas_call(
        functools.partial(_inproj_kernel, kind=kind),
        grid=(n // tm,),
        in_specs=[
            pl.BlockSpec((tm, d), lambda i: (i, 0)),
            pl.BlockSpec((1, d), const2),
            pl.BlockSpec((1, 1, d), sc_map),
            pl.BlockSpec((1, 1, d), sh_map),
            pl.BlockSpec((d, pm), const2),
            pl.BlockSpec((ng, d), const2),
            pl.BlockSpec((ng, 1), const2),
            pl.BlockSpec((ng, 1), const2),
        ],
        out_specs=[
            pl.BlockSpec((tm, pm), lambda i: (i, 0)),
            pl.BlockSpec((N_HEADS, tm // CHUNK, 4, CHUNK), lambda i: (0, i, 0, 0)),
        ],
        out_shape=[
            jax.ShapeDtypeStruct((n, pm), F32),
            jax.ShapeDtypeStruct((N_HEADS, n // CHUNK, 4, CHUNK), F32),
        ],
        compiler_params=pltpu.CompilerParams(dimension_semantics=("arbitrary",), vmem_limit_bytes=VMEM_LIMIT),
        name="inproj_" + kind,
    )(x, norm_g.reshape(1, d), mods, mods, w_main.astype(BF16), w_gate.T.astype(BF16),
      gate_bias.reshape(ng, 1), gate_mul.reshape(ng, 1))


def _chunk_masks():
    r = lax.broadcasted_iota(jnp.int32, (CHUNK, CHUNK), 0)
    c = lax.broadcasted_iota(jnp.int32, (CHUNK, CHUNK), 1)
    return r >= c, r > c, r <= c, r < c


def _chunk_rows(c):
    if isinstance(c, int):
        return pl.ds(c * CHUNK, CHUNK)
    return pl.ds(pl.multiple_of(c * CHUNK, CHUNK), CHUNK)


def _head_cols(h):
    return slice(h * D_HEAD, (h + 1) * D_HEAD)


def _last_row(x, d):
    return x[CHUNK - 1:CHUNK, :] if d == 0 else x[0:1, :]


def _gate_columns(gr4):
    a = jnp.concatenate([gr4, jnp.zeros_like(gr4)], axis=0)
    hi = a.astype(BF16).astype(F32)
    mid = (a - hi).astype(BF16).astype(F32)
    lo = a - hi - mid
    parts = jnp.concatenate([hi, mid, lo, jnp.zeros_like(a)], axis=0).astype(BF16)
    j = lax.broadcasted_iota(jnp.int32, (32, 4 * D_HEAD), 0) % 8
    lane = lax.broadcasted_iota(jnp.int32, (32, 4 * D_HEAD), 1)
    return _dot_tn(parts, jnp.where(lane // D_HEAD == j, 1.0, 0.0).astype(BF16))


def _gate_column(cols, j):
    return cols[:, j * D_HEAD:(j + 1) * D_HEAD]


def _unit_tri_solve(a_list, rhs_list):
    mm = lambda x, y: _dot(x.astype(BF16), y.astype(BF16))
    r = lax.broadcasted_iota(jnp.int32, (CHUNK, CHUNK), 0)
    c = lax.broadcasted_iota(jnp.int32, (CHUNK, CHUNK), 1)
    same = (r // 16) == (c // 16)
    eye = jnp.where(r == c, 1.0, 0.0)
    d = [jnp.where(same, a, 0.0) for a in a_list]
    t = [eye - di for di in d]
    for _ in range(3):
        d = [mm(di, di) for di in d]
        t = [ti + mm(ti, di) for ti, di in zip(t, d)]
    b = [mm(ti, jnp.where(same, 0.0, a)) for ti, a in zip(t, a_list)]
    x = [mm(ti, ri) for ti, ri in zip(t, rhs_list)]
    b2 = [mm(bi, bi) for bi in b]
    x = [xi - mm(bi, xi) for xi, bi in zip(x, b)]
    return [xi + mm(bi, xi) for xi, bi in zip(x, b2)]


def _for_chunk_groups(nc, group, fn):
    if nc == group:
        fn(list(range(nc)))
        return

    def body(g, carry):
        fn([g * group + j for j in range(group)])
        return carry
    lax.fori_loop(0, nc // group, body, 0)


def _scan_specs(t_len, row0, hb, n_proj):
    rb = row0 // t_len
    nc = t_len // CHUNK
    ngrp = N_HEADS // hb
    specs = [pl.BlockSpec((t_len, hb * D_HEAD), functools.partial(lambda b, g, j: (rb + b, j * ngrp + g), j=j))
             for j in range(n_proj)]
    specs.append(pl.BlockSpec((hb, nc, 4, CHUNK), lambda b, g: (g, rb + b, 0, 0)))
    return specs


def _scan_call(kern, name, args, in_specs, out_specs, out_shape, scratch, batch, hb):
    return pl.pallas_call(
        kern,
        grid=(batch, N_HEADS // hb),
        in_specs=in_specs,
        out_specs=out_specs,
        out_shape=out_shape,
        scratch_shapes=scratch,
        compiler_params=pltpu.CompilerParams(dimension_semantics=("arbitrary", "arbitrary"),
                                             vmem_limit_bytes=VMEM_LIMIT),
        name=name,
    )(*args)


def _gdn_kernel(*refs, t_len, hb, has_init, emit_state):
    it = iter(refs)
    q_ref, k_ref, v_ref, z_ref, gr_ref, cwq_ref, cwk_ref, cwv_ref, ng_ref = (next(it) for _ in range(9))
    s0_ref = next(it) if has_init else None
    o_ref = next(it)
    s_ref = next(it) if emit_state else None
    qs, ks, vs, oacc, qp_s, op_s, km_s, nm_s, ge_s = (next(it) for _ in range(9))
    nc = t_len // CHUNK

    rows = lax.broadcasted_iota(jnp.int32, (t_len, 1), 0)

    def conv_silu(x, cw):
        xm = jnp.where(rows == 0, 0.0, pltpu.roll(x, 1, axis=0))
        xp = jnp.where(rows == t_len - 1, 0.0, pltpu.roll(x, t_len - 1, axis=0))
        return _silu(xm * cw[0:1] + x * cw[1:2] + xp * cw[2:3])

    def l2n(x):
        return x * lax.rsqrt(jnp.sum(x * x, axis=-1, keepdims=True) + EPS)

    for h in range(hb):
        hc = _head_cols(h)
        qs[:, hc] = l2n(conv_silu(q_ref[:, hc], cwq_ref[:, hc])) * (D_HEAD ** -0.5)
        ks[:, hc] = l2n(conv_silu(k_ref[:, hc], cwk_ref[:, hc]))
        vs[:, hc] = conv_silu(v_ref[:, hc], cwv_ref[:, hc])
    oacc[...] = jnp.zeros_like(oacc)

    lo_i, lo_s, up_i, up_s = _chunk_masks()

    def slot(h, d, c):
        return (h * 2 + d) * nc + c

    def intra(chunks):
        items = [(h, c, d) for h in range(hb) for c in chunks for d in range(2)]
        cols = [[_gate_columns(gr_ref[h, c]) for c in chunks] for h in range(hb)]
        a_list, rhs_list, keep = [], [], []
        for n_item, (h, c, d) in enumerate(items):
            sl, hc = _chunk_rows(c), _head_cols(h)
            q, k, v = qs[sl, hc], ks[sl, hc], vs[sl, hc]
            gr4 = gr_ref[h, c]
            col4 = cols[h][(n_item // 2) % len(chunks)]
            g_col, beta = _gate_column(col4, d), _gate_column(col4, 2 + d)
            incl, strict = (lo_i, lo_s) if d == 0 else (up_i, up_s)
            decay = jnp.exp(jnp.where(incl, g_col[:, :CHUNK] - gr4[d:d + 1, :], -jnp.inf))
            kb = k * beta
            k16 = k.astype(BF16)
            a_list.append(jnp.where(strict, _dot_nt(kb.astype(BF16), k16) * decay, 0.0))
            e_g = jnp.exp(g_col)
            rhs_list.append(jnp.concatenate([v * beta, kb * e_g], axis=1))
            qk16 = jnp.where(incl, _dot_nt(q.astype(BF16), k16) * decay, 0.0).astype(BF16)
            g_last = _last_row(g_col, d)
            ge_s[slot(h, d, c)] = jnp.broadcast_to(jnp.exp(g_last), (8, D_HEAD))
            keep.append((qk16, q * e_g, (k * jnp.exp(g_last - g_col)).astype(BF16)))
        sols = [s.astype(BF16) for s in _unit_tri_solve(a_list, rhs_list)]
        qw = [_dot(kp[0], s) for kp, s in zip(keep, sols)]
        kw = [_dot_tn(kp[2], s) for kp, s in zip(keep, sols)]
        for (h, c, d), kp, qwi, kwi in zip(items, keep, qw, kw):
            i = h * 2 + d
            sl = _chunk_rows(c)
            op_s[i, sl, :] = qwi[:, :D_HEAD]
            qp_s[i, sl, :] = (kp[1] - qwi[:, D_HEAD:]).astype(BF16)
            nm_s[slot(h, d, c)] = kwi[:, :D_HEAD]
            km_s[slot(h, d, c)] = (-kwi[:, D_HEAD:]).astype(BF16)

    _for_chunk_groups(nc, min(nc, max(1, SCAN_CHAINS // (2 * hb))), intra)

    chains = [(h, d) for h in range(hb) for d in range(2)]

    def inter_body(i, carry):
        cs = (i, nc - 1 - i)
        s16 = [s.astype(BF16) for s in carry]
        outs = [_dot(qp_s[h * 2 + d, _chunk_rows(cs[d]), :], s) for (h, d), s in zip(chains, s16)]
        upds = [_dot(km_s[slot(h, d, cs[d])], s) for (h, d), s in zip(chains, s16)]
        for (h, d), o in zip(chains, outs):
            oacc[_chunk_rows(cs[d]), _head_cols(h)] += o + op_s[h * 2 + d, _chunk_rows(cs[d]), :]
        return tuple(ge_s[slot(h, d, cs[d])][0:1, :] * s + u + nm_s[slot(h, d, cs[d])]
                     for (h, d), s, u in zip(chains, carry, upds))

    if has_init:
        init = tuple(s0_ref[0, d, h] for h, d in chains)
    else:
        init = tuple(jnp.zeros((D_HEAD, D_HEAD), F32) for _ in chains)
    fin = lax.fori_loop(0, nc, inter_body, init)
    if emit_state:
        for (h, d), s in zip(chains, fin):
            s_ref[0, d, h] = s
    for h in range(hb):
        hc = _head_cols(h)
        o_ref[:, hc] = (_rms(oacc[:, hc]) * ng_ref[...] * _silu(z_ref[:, hc])).astype(o_ref.dtype)


def _gdn_scan(main, grow, conv_w, norm_g, s0, *, batch, t_len, row0, hb, emit_state):
    nc = t_len // CHUNK
    ngrp = N_HEADS // hb
    has_init = s0 is not None
    in_specs = _scan_specs(t_len, row0, hb, 4)
    in_specs += [pl.BlockSpec((3, hb * D_HEAD), functools.partial(lambda b, g, j: (0, j * ngrp + g), j=j))
                 for j in range(3)]
    in_specs.append(pl.BlockSpec((1, D_HEAD), lambda b, g: (0, 0)))
    args = [main, main, main, main, grow, conv_w, conv_w, conv_w, norm_g.reshape(1, D_HEAD)]
    state_spec = pl.BlockSpec((1, 2, hb, D_HEAD, D_HEAD), lambda b, g: (b, 0, g, 0, 0))
    if has_init:
        in_specs.append(state_spec)
        args.append(s0)
    out_specs = [pl.BlockSpec((t_len, hb * D_HEAD), lambda b, g: (b, g))]
    out_shape = [jax.ShapeDtypeStruct((batch * t_len, N_HEADS * D_HEAD), BF16)]
    if emit_state:
        out_specs.append(state_spec)
        out_shape.append(jax.ShapeDtypeStruct((batch, 2, N_HEADS, D_HEAD, D_HEAD), F32))
    scratch = ([pltpu.VMEM((t_len, hb * D_HEAD), F32) for _ in range(4)]
               + [pltpu.VMEM((2 * hb, t_len, D_HEAD), BF16),
                  pltpu.VMEM((2 * hb, t_len, D_HEAD), F32),
                  pltpu.VMEM((2 * hb * nc, D_HEAD, D_HEAD), BF16),
                  pltpu.VMEM((2 * hb * nc, D_HEAD, D_HEAD), F32),
                  pltpu.VMEM((2 * hb * nc, 8, D_HEAD), F32)])
    kern = functools.partial(_gdn_kernel, t_len=t_len, hb=hb, has_init=has_init, emit_state=emit_state)
    return _scan_call(kern, "gdn_scan_t%d" % t_len, args, in_specs, out_specs, out_shape, scratch, batch, hb)


def _mlstm_kernel(*refs, t_len, hb, has_init, emit_state):
    it = iter(refs)
    q_ref, k_ref, v_ref, og_ref, gr_ref, ng_ref = (next(it) for _ in range(6))
    if has_init:
        c0_ref, n0_ref, m0_ref = (next(it) for _ in range(3))
    o_ref = next(it)
    if emit_state:
        c_ref, n_ref, m_ref = (next(it) for _ in range(3))
    hacc, q16_s, v1_s, qk_s, ld_s, bb_s, lm_s, lw_s, sc_s = (next(it) for _ in range(9))
    nc = t_len // CHUNK

    hacc[...] = jnp.zeros_like(hacc)
    ones = jnp.ones((t_len, D_HEAD), BF16)
    for h in range(hb):
        hc = _head_cols(h)
        q16_s[:, hc] = (q_ref[:, hc] * (D_HEAD ** -0.5)).astype(BF16)
        v1_s[:, h * 2 * D_HEAD:(h * 2 + 1) * D_HEAD] = v_ref[:, hc].astype(BF16)
        v1_s[:, (h * 2 + 1) * D_HEAD:(h * 2 + 2) * D_HEAD] = ones
    lo_i, _, up_i, _ = _chunk_masks()

    def slot(h, d, c):
        return (h * 2 + d) * nc + c

    def intra(chunks):
        for h in range(hb):
            for c in chunks:
                sl, hc = _chunk_rows(c), _head_cols(h)
                qk_s[h, sl, :] = _dot_nt(q16_s[sl, hc], k_ref[sl, hc].astype(BF16))
                gr4 = gr_ref[h, c]
                col4 = _gate_columns(gr4)
                for d in range(2):
                    i = h * 2 + d
                    b_col, i_col = _gate_column(col4, 2 + d), _gate_column(col4, d)
                    b_last = _last_row(b_col, d)
                    lwe = b_last - b_col + i_col
                    log_d = jnp.where(lo_i if d == 0 else up_i,
                                      b_col[:, :CHUNK] - gr4[2 + d:3 + d, :] + gr4[d:d + 1, :], -jnp.inf)
                    ld_s[i, sl, :] = log_d
                    lm_s[i, sl, :] = jnp.broadcast_to(jnp.max(log_d, axis=-1, keepdims=True), (CHUNK, D_HEAD))
                    bb_s[i, sl, :] = b_col
                    lw_s[i, sl, :] = lwe
                    sc_s[slot(h, d, c), 0:1, :] = b_last
                    sc_s[slot(h, d, c), 1:2, :] = jnp.max(lwe, axis=0, keepdims=True)

    _for_chunk_groups(nc, min(nc, 4), intra)

    chains = [(h, d) for h in range(hb) for d in range(2)]

    def body(step, carry):
        cs = (step, nc - 1 - step)
        state, nrow, m = zip(*carry)
        sls = [_chunk_rows(cs[d]) for h, d in chains]
        idx = [h * 2 + d for h, d in chains]
        v1 = [v1_s[sl, h * 2 * D_HEAD:(h * 2 + 2) * D_HEAD] for (h, d), sl in zip(chains, sls)]
        log_last = [sc_s[slot(h, d, cs[d]), 0:1, :] + mi for (h, d), mi in zip(chains, m)]
        m_new = [jnp.maximum(ll, sc_s[slot(h, d, cs[d]), 1:2, :]) for (h, d), ll in zip(chains, log_last)]
        dec = [jnp.exp(ll - mn) for ll, mn in zip(log_last, m_new)]
        kw = [k_ref[sl, _head_cols(h)] * jnp.exp(lw_s[i, sl, :] - mn)
              for (h, d), sl, i, mn in zip(chains, sls, idx, m_new)]
        upd = [_dot_tn(a.astype(BF16), b) for a, b in zip(kw, v1)]
        qc = [_dot(q16_s[sl, _head_cols(h)], s.astype(BF16)) for (h, d), sl, s in zip(chains, sls, state)]
        log_inter = [bb_s[i, sl, :] + mi for i, sl, mi in zip(idx, sls, m)]
        mt = [jnp.maximum(li, lm_s[i, sl, :]) for li, i, sl in zip(log_inter, idx, sls)]
        s_inter = [jnp.exp(li - t) for li, t in zip(log_inter, mt)]
        p = [(jnp.exp(ld_s[i, sl, :] - t[:, :CHUNK]) * qk_s[h, sl, :]).astype(BF16)
             for (h, d), i, sl, t in zip(chains, idx, sls, mt)]
        pv = [_dot(a, b) for a, b in zip(p, v1)]
        for (h, d), sl, si, qci, pvi, t in zip(chains, sls, s_inter, qc, pv, mt):
            num = si * qci[:, :D_HEAD] + pvi[:, :D_HEAD]
            den = si * qci[:, D_HEAD:] + pvi[:, D_HEAD:]
            hacc[sl, _head_cols(h)] += num / jnp.maximum(jnp.abs(den), jnp.exp(-t))
        return tuple((jnp.concatenate([dc, dc], axis=1) * s + u, dc * nr + jnp.sum(kwi, axis=0, keepdims=True), mn)
                     for dc, s, u, nr, kwi, mn in zip(dec, state, upd, nrow, kw, m_new))

    if has_init:
        init = tuple((jnp.concatenate([c0_ref[0, d, h],
                                       jnp.transpose(jnp.broadcast_to(n0_ref[0, h, d:d + 1, :], (D_HEAD, D_HEAD)))],
                                      axis=1),
                      n0_ref[0, h, d:d + 1, :], m0_ref[0, h, d:d + 1, :]) for h, d in chains)
    else:
        init = tuple((jnp.zeros((D_HEAD, 2 * D_HEAD), F32), jnp.zeros((1, D_HEAD), F32), jnp.zeros((1, D_HEAD), F32))
                     for _ in chains)
    fin = lax.fori_loop(0, nc, body, init)
    if emit_state:
        for (h, d), (s, nr, m) in zip(chains, fin):
            c_ref[0, d, h] = s[:, :D_HEAD]
            n_ref[0, h, d:d + 1, :] = nr
            m_ref[0, h, d:d + 1, :] = m
    for h in range(hb):
        hc = _head_cols(h)
        o_ref[:, hc] = (_rms(hacc[:, hc]) * ng_ref[...] * jax.nn.sigmoid(og_ref[:, hc])).astype(o_ref.dtype)


def _mlstm_scan(main, grow, norm_g, init, *, batch, t_len, row0, hb, emit_state):
    nc = t_len // CHUNK
    has_init = init is not None
    in_specs = _scan_specs(t_len, row0, hb, 4)
    in_specs.append(pl.BlockSpec((1, D_HEAD), lambda b, g: (0, 0)))
    args = [main, main, main, main, grow, norm_g.reshape(1, D_HEAD)]
    c_spec = pl.BlockSpec((1, 2, hb, D_HEAD, D_HEAD), lambda b, g: (b, 0, g, 0, 0))
    v_spec = pl.BlockSpec((1, hb, 2, D_HEAD), lambda b, g: (b, g, 0, 0))
    if has_init:
        in_specs += [c_spec, v_spec, v_spec]
        args += list(init)
    out_specs = [pl.BlockSpec((t_len, hb * D_HEAD), lambda b, g: (b, g))]
    out_shape = [jax.ShapeDtypeStruct((batch * t_len, N_HEADS * D_HEAD), BF16)]
    if emit_state:
        out_specs += [c_spec, v_spec, v_spec]
        out_shape += [jax.ShapeDtypeStruct((batch, 2, N_HEADS, D_HEAD, D_HEAD), F32),
                      jax.ShapeDtypeStruct((batch, N_HEADS, 2, D_HEAD), F32),
                      jax.ShapeDtypeStruct((batch, N_HEADS, 2, D_HEAD), F32)]
    scratch = [pltpu.VMEM((t_len, hb * D_HEAD), F32),
               pltpu.VMEM((t_len, hb * D_HEAD), BF16),
               pltpu.VMEM((t_len, hb * 2 * D_HEAD), BF16),
               pltpu.VMEM((hb, t_len, CHUNK), F32),
               pltpu.VMEM((2 * hb, t_len, CHUNK), F32),
               pltpu.VMEM((2 * hb, t_len, D_HEAD), F32),
               pltpu.VMEM((2 * hb, t_len, D_HEAD), F32),
               pltpu.VMEM((2 * hb, t_len, D_HEAD), F32),
               pltpu.VMEM((2 * hb * nc, 8, D_HEAD), F32)]
    kern = functools.partial(_mlstm_kernel, t_len=t_len, hb=hb, has_init=has_init, emit_state=emit_state)
    return _scan_call(kern, "mlstm_scan_t%d" % t_len, args, in_specs, out_specs, out_shape, scratch, batch, hb)


def _top2_of4(v):
    best, i1 = v[0], jnp.zeros(v[0].shape, jnp.int32)
    for j in range(1, 4):
        take = v[j] > best
        i1 = jnp.where(take, j, i1)
        best = jnp.where(take, v[j], best)
    best2, i2 = None, None
    for j in range(4):
        vj = jnp.where(i1 == j, -jnp.inf, v[j])
        if best2 is None:
            best2, i2 = vj, jnp.zeros(v[0].shape, jnp.int32)
        else:
            take = vj > best2
            i2 = jnp.where(take, j, i2)
            best2 = jnp.where(take, vj, best2)
    return i1, i2


def _pick(rows, idx):
    out = rows[0]
    for j in range(1, len(rows)):
        out = jnp.where(idx == j, rows[j], out)
    return out


def _outproj_kernel(x_ref, oc_ref, od_ref, w_ref, g1_ref, ng_ref, sc_ref, sh_ref, rw_ref, rb_ref, xo_ref, hn_ref,
                    slot_ref, wt_ref, gran_ref, cnt_ref, run_ref, *, ctx_tiles):
    @pl.when(pl.program_id(0) == 0)
    def _():
        run_ref[...] = jnp.zeros_like(run_ref)

    o = jnp.where(pl.program_id(0) < ctx_tiles, oc_ref[...], od_ref[...])
    xn = x_ref[...] + g1_ref[0] * _dot(o, w_ref[...])
    xo_ref[...] = xn
    hn = _rms(xn) * ng_ref[...]
    hn = hn * (1.0 + sc_ref[0]) + sh_ref[0]
    hn_ref[...] = hn.astype(BF16)

    logits = _dot_nt(rw_ref[...], hn, HIGHEST)
    ex = jnp.exp(logits - jnp.max(logits, axis=0, keepdims=True))
    probs = ex / jnp.sum(ex, axis=0, keepdims=True)
    sel = probs + rb_ref[...]
    sel_rows = [sel[e:e + 1, :] for e in range(N_EXPERTS)]
    prob_rows = [probs[e:e + 1, :] for e in range(N_EXPERTS)]
    scores = []
    for g in range(N_GROUPS):
        r = sel_rows[4 * g:4 * g + 4]
        a, b = jnp.maximum(r[0], r[1]), jnp.minimum(r[0], r[1])
        c, d = jnp.maximum(r[2], r[3]), jnp.minimum(r[2], r[3])
        scores.append(jnp.maximum(a, c) + jnp.maximum(jnp.minimum(a, c), jnp.maximum(b, d)))
    best, grp = scores[0], jnp.zeros(scores[0].shape, jnp.int32)
    for g in range(1, N_GROUPS):
        take = scores[g] > best
        grp = jnp.where(take, g, grp)
        best = jnp.where(take, scores[g], best)
    sel_in = [_pick([sel_rows[4 * g + j] for g in range(N_GROUPS)], grp) for j in range(4)]
    prob_in = [_pick([prob_rows[4 * g + j] for g in range(N_GROUPS)], grp) for j in range(4)]
    i1, i2 = _top2_of4(sel_in)
    w1, w2 = _pick(prob_in, i1), _pick(prob_in, i2)
    tot = w1 + w2
    e1, e2 = grp * 4 + i1, grp * 4 + i2
    wt_ref[...] = jnp.concatenate([w1 / tot, w2 / tot], axis=0)

    tm = logits.shape[1]
    eidx = lax.broadcasted_iota(jnp.int32, logits.shape, 0)
    onehot = jnp.where(eidx == e1, 1.0, 0.0) + jnp.where(eidx == e2, 1.0, 0.0)
    r = lax.broadcasted_iota(jnp.int32, (tm, tm), 0)
    c = lax.broadcasted_iota(jnp.int32, (tm, tm), 1)
    earlier = jnp.where(r < c, 1.0, 0.0).astype(BF16)
    ahead = _dot(onehot.astype(BF16), earlier)
    n_gran = jnp.floor((jnp.sum(onehot, axis=1, keepdims=True) + (GRANULE - 1)) * (1.0 / GRANULE))
    n_gran = jnp.broadcast_to(n_gran, (N_EXPERTS, D_HEAD))
    er = lax.broadcasted_iota(jnp.int32, (N_EXPERTS, N_EXPERTS), 0)
    ec = lax.broadcasted_iota(jnp.int32, (N_EXPERTS, N_EXPERTS), 1)
    first = _dot(jnp.where(ec < er, 1.0, 0.0).astype(BF16), n_gran.astype(BF16))
    row0 = GRANULE * first[:, 0:1] + ahead
    slot = [jnp.sum(jnp.where(eidx == e, row0, 0.0), axis=0, keepdims=True) for e in (e1, e2)]
    slot_ref[...] = jnp.concatenate(slot, axis=0).astype(jnp.int32)

    g = lax.broadcasted_iota(jnp.int32, (N_EXPERTS, D_HEAD), 1).astype(F32)
    owner = jnp.sum(jnp.where(g >= first + n_gran, 1.0, 0.0), axis=0, keepdims=True)
    ge = lax.broadcasted_iota(jnp.int32, (N_EXPERTS, D_HEAD), 0).astype(F32)
    index = g[0:1, :] + jnp.sum(jnp.where(ge == owner, run_ref[...] - first, 0.0), axis=0, keepdims=True)
    gran_ref[0] = jnp.concatenate([owner, index], axis=0).astype(jnp.int32)
    run_ref[...] += n_gran
    cnt_ref[...] = run_ref[...]


def _outproj_route(x, o_ctx, o_dec, w_out, mods, mod_base, norm_g, router_w, router_bias, n_ctx, t_dec):
    n, d = x.shape
    tm = TM_PROJ
    ctx_tiles = n_ctx // tm
    mrow = _mod_row_of_tile(tm, n_ctx, t_dec)
    mod_map = lambda k: (lambda i: (mod_base + mrow(i) * 6 + k, 0, 0))
    const2 = lambda i: (0, 0)
    return pl.pallas_call(
        functools.partial(_outproj_kernel, ctx_tiles=ctx_tiles),
        grid=(n // tm,),
        in_specs=[
            pl.BlockSpec((tm, d), lambda i: (i, 0)),
            pl.BlockSpec((tm, d), lambda i: (jnp.minimum(i, ctx_tiles - 1), 0)),
            pl.BlockSpec((tm, d), lambda i: (jnp.maximum(i - ctx_tiles, 0), 0)),
            pl.BlockSpec((d, d), const2),
            pl.BlockSpec((1, 1, d), mod_map(2)),
            pl.BlockSpec((1, d), const2),
            pl.BlockSpec((1, 1, d), mod_map(4)),
            pl.BlockSpec((1, 1, d), mod_map(3)),
            pl.BlockSpec((N_EXPERTS, d), const2),
            pl.BlockSpec((N_EXPERTS, 1), const2),
        ],
        out_specs=[
            pl.BlockSpec((tm, d), lambda i: (i, 0)),
            pl.BlockSpec((tm, d), lambda i: (i, 0)),
            pl.BlockSpec((2, tm), lambda i: (0, i)),
            pl.BlockSpec((2, tm), lambda i: (0, i)),
            pl.BlockSpec((1, 2, D_HEAD), lambda i: (i, 0, 0)),
            pl.BlockSpec((N_EXPERTS, D_HEAD), const2),
        ],
        out_shape=[
            jax.ShapeDtypeStruct((n, d), F32),
            jax.ShapeDtypeStruct((n, d), BF16),
            jax.ShapeDtypeStruct((2, n), jnp.int32),
            jax.ShapeDtypeStruct((2, n), F32),
            jax.ShapeDtypeStruct((n // tm, 2, D_HEAD), jnp.int32),
            jax.ShapeDtypeStruct((N_EXPERTS, D_HEAD), F32),
        ],
        scratch_shapes=[pltpu.VMEM((N_EXPERTS, D_HEAD), F32)],
        compiler_params=pltpu.CompilerParams(dimension_semantics=("arbitrary",), vmem_limit_bytes=VMEM_LIMIT),
        name="outproj_route",
    )(x, o_ctx, o_dec, w_out.astype(BF16), mods, norm_g.reshape(1, d), mods, mods, router_w.T, router_bias.reshape(N_EXPERTS, 1))


def _route_tables(gran, counts, n_tiles):
    tile_gran = TM_MOE // GRANULE
    cnt = counts[:, 0].astype(jnp.int32)
    padded = (cnt + tile_gran - 1) // tile_gran * tile_gran
    ends = jnp.cumsum(padded)
    offs = ends - padded
    owner, index = gran[:, 0, :], gran[:, 1, :]
    first_gran = index + sum(jnp.where(owner == e, offs[e], 0) for e in range(N_EXPERTS))
    rows = jnp.where(owner < N_EXPERTS, first_gran * GRANULE, -1).astype(jnp.int32)
    tile_g0 = jnp.arange(n_tiles, dtype=jnp.int32) * tile_gran
    tile_expert = jnp.minimum(jnp.sum(tile_g0[:, None] >= ends[None, :], axis=1), N_EXPERTS - 1)
    tile_valid = jnp.sum((tile_g0[:, None] >= offs[None, :]) & (tile_g0[:, None] < (offs + cnt)[None, :]), axis=1)
    return rows, tile_expert.astype(jnp.int32), tile_valid.astype(jnp.int32)


def _granule_copy(src, src_row, dst, dst_row, sem):
    return pltpu.make_async_copy(src.at[pl.ds(pl.multiple_of(src_row, GRANULE), GRANULE)],
                                 dst.at[pl.ds(pl.multiple_of(dst_row, GRANULE), GRANULE)], sem)


def _for_granules(rows_ref, fn):
    def body(g, carry):
        row = rows_ref[0, 0, g]

        @pl.when(row >= 0)
        def _():
            fn(g, row)
        return carry
    lax.fori_loop(0, TILE_GRANULES, body, 0)


def _pair_matrix(slot, first_value, second_value):
    p = lax.broadcasted_iota(jnp.int32, (TILE_GRANULES * GRANULE, slot.shape[1]), 0)
    return jnp.where(p == slot[0:1, :], first_value, 0.0) + jnp.where(p == slot[1:2, :], second_value, 0.0)


def _dispatch_kernel(rows_ref, hn_ref, slot_ref, xs0_ref, xs_ref, buf, sem):
    del xs0_ref
    buf[...] = _dot(_pair_matrix(slot_ref[...], 1.0, 1.0).astype(BF16), hn_ref[...])
    _for_granules(rows_ref, lambda g, row: _granule_copy(buf, g * GRANULE, xs_ref, row, sem).start())
    _for_granules(rows_ref, lambda g, row: _granule_copy(buf, g * GRANULE, xs_ref, row, sem).wait())


def _dispatch(hn, slot, rows, n_rows):
    n, d = hn.shape
    tm = TM_PROJ
    return pl.pallas_call(
        _dispatch_kernel,
        grid=(n // tm,),
        in_specs=[
            pl.BlockSpec((1, 1, D_HEAD), lambda i: (i, 0, 0), memory_space=pltpu.SMEM),
            pl.BlockSpec((tm, d), lambda i: (i, 0)),
            pl.BlockSpec((2, tm), lambda i: (0, i)),
            pl.BlockSpec(memory_space=pl.ANY),
        ],
        out_specs=pl.BlockSpec(memory_space=pl.ANY),
        out_shape=jax.ShapeDtypeStruct((n_rows, d), F32),
        scratch_shapes=[pltpu.VMEM((TILE_GRANULES * GRANULE, d), F32), pltpu.SemaphoreType.DMA(())],
        input_output_aliases={3: 0},
        compiler_params=pltpu.CompilerParams(dimension_semantics=("arbitrary",), vmem_limit_bytes=VMEM_LIMIT),
        name="moe_dispatch",
    )(rows.reshape(n // tm, 1, D_HEAD), hn, slot, jnp.zeros((n_rows, d), F32))


def _expert_kernel(te_ref, tv_ref, xs_ref, wg_ref, wu_ref, wd_ref, ys_ref, wg16, wu16, wd16):
    i = pl.program_id(0)
    fresh = jnp.logical_or(i == 0, te_ref[i] != te_ref[jnp.maximum(i - 1, 0)])

    @pl.when(fresh)
    def _():
        wg16[...] = wg_ref[0].astype(BF16)
        wu16[...] = wu_ref[0].astype(BF16)
        wd16[...] = wd_ref[0].astype(BF16)

    @pl.when(tv_ref[i] != 0)
    def _():
        x = xs_ref[...].astype(BF16)
        hid = _silu(_dot(x, wg16[...])) * _dot(x, wu16[...])
        ys_ref[...] = _dot(hid.astype(BF16), wd16[...])

    @pl.when(tv_ref[i] == 0)
    def _():
        ys_ref[...] = jnp.zeros_like(ys_ref)


def _experts(xs, tile_expert, tile_valid, w_gate, w_up, w_down):
    n_rows, d = xs.shape
    tm = TM_MOE
    w_in_spec = pl.BlockSpec((1, d, D_FF), lambda i, te, tv: (te[i], 0, 0))
    return pl.pallas_call(
        _expert_kernel,
        grid_spec=pltpu.PrefetchScalarGridSpec(
            num_scalar_prefetch=2,
            grid=(n_rows // tm,),
            in_specs=[
                pl.BlockSpec((tm, d), lambda i, te, tv: (i, 0)),
                w_in_spec,
                w_in_spec,
                pl.BlockSpec((1, D_FF, d), lambda i, te, tv: (te[i], 0, 0)),
            ],
            out_specs=pl.BlockSpec((tm, d), lambda i, te, tv: (i, 0)),
            scratch_shapes=[pltpu.VMEM((d, D_FF), BF16), pltpu.VMEM((d, D_FF), BF16), pltpu.VMEM((D_FF, d), BF16)],
        ),
        out_shape=jax.ShapeDtypeStruct((n_rows, d), F32),
        compiler_params=pltpu.CompilerParams(dimension_semantics=("arbitrary",), vmem_limit_bytes=VMEM_LIMIT),
        name="moe_experts",
    )(tile_expert, tile_valid, xs, w_gate, w_up, w_down)


def _combine_kernel(rows_ref, x_ref, slot_ref, wt_ref, g2_ref, fg_ref, ys_ref, *rest, ctx_tiles):
    *o_refs, buf, sem = rest
    buf[...] = jnp.zeros_like(buf)
    _for_granules(rows_ref, lambda g, row: _granule_copy(ys_ref, row, buf, g * GRANULE, sem).start())
    _for_granules(rows_ref, lambda g, row: _granule_copy(ys_ref, row, buf, g * GRANULE, sem).wait())
    w = wt_ref[...]
    mix = _pair_matrix(slot_ref[...], w[0:1, :], w[1:2, :]).astype(BF16)
    xn = x_ref[...] + g2_ref[0] * _dot_tn(mix, buf[...].astype(BF16))
    if ctx_tiles is None:
        o_refs[0][...] = xn
        return
    xn = _rms(xn) * fg_ref[...]

    @pl.when(pl.program_id(0) < ctx_tiles)
    def _():
        o_refs[0][...] = xn

    @pl.when(pl.program_id(0) >= ctx_tiles)
    def _():
        o_refs[1][...] = xn


def _combine(x, ys, slot, wts, rows, mods, mod_base, final_g, final_norm, n_ctx, t_dec):
    n, d = x.shape
    tm = TM_PROJ
    mrow = _mod_row_of_tile(tm, n_ctx, t_dec)
    ctx_tiles = n_ctx // tm if final_norm else None
    if final_norm:
        out_specs = [pl.BlockSpec((tm, d), lambda i: (jnp.minimum(i, ctx_tiles - 1), 0)),
                     pl.BlockSpec((tm, d), lambda i: (jnp.maximum(i - ctx_tiles, 0), 0))]
        out_shape = [jax.ShapeDtypeStruct((n_ctx, d), F32), jax.ShapeDtypeStruct((n - n_ctx, d), F32)]
    else:
        out_specs = pl.BlockSpec((tm, d), lambda i: (i, 0))
        out_shape = jax.ShapeDtypeStruct((n, d), F32)
    return pl.pallas_call(
        functools.partial(_combine_kernel, ctx_tiles=ctx_tiles),
        grid=(n // tm,),
        in_specs=[
            pl.BlockSpec((1, 1, D_HEAD), lambda i: (i, 0, 0), memory_space=pltpu.SMEM),
            pl.BlockSpec((tm, d), lambda i: (i, 0)),
            pl.BlockSpec((2, tm), lambda i: (0, i)),
            pl.BlockSpec((2, tm), lambda i: (0, i)),
            pl.BlockSpec((1, 1, d), lambda i: (mod_base + mrow(i) * 6 + 5, 0, 0)),
            pl.BlockSpec((1, d), lambda i: (0, 0)),
            pl.BlockSpec(memory_space=pl.ANY),
        ],
        out_specs=out_specs,
        out_shape=out_shape,
        scratch_shapes=[pltpu.VMEM((TILE_GRANULES * GRANULE, d), F32), pltpu.SemaphoreType.DMA(())],
        compiler_params=pltpu.CompilerParams(dimension_semantics=("arbitrary",), vmem_limit_bytes=VMEM_LIMIT),
        name="moe_combine",
    )(rows.reshape(n // tm, 1, D_HEAD), x, slot, wts, mods, final_g.reshape(1, d), ys)


def _moe(x, hn, slot, wts, gran, counts, w_gate, w_up, w_down, mods, mod_base, final_g, final_norm, n_ctx, t_dec):
    n = x.shape[0]
    n_rows = 2 * n + (n // TM_PROJ) * N_EXPERTS * GRANULE + N_EXPERTS * TM_MOE
    n_tiles = -(-n_rows // TM_MOE)
    rows, tile_expert, tile_valid = _route_tables(gran, counts, n_tiles)
    xs = _dispatch(hn, slot, rows, n_tiles * TM_MOE)
    ys = _experts(xs, tile_expert, tile_valid, w_gate, w_up, w_down)
    return _combine(x, ys, slot, wts, rows, mods, mod_base, final_g, final_norm, n_ctx, t_dec)


def _grid_pos_embed(n_tokens):
    rows = n_tokens // GRID_W
    r = jnp.repeat(jnp.arange(rows, dtype=F32), GRID_W)
    col = jnp.tile(jnp.arange(GRID_W, dtype=F32), rows)
    quarter = D_MODEL // 4
    freq = jnp.exp(jnp.arange(quarter, dtype=F32) * (-math.log(POS_BASE) / quarter))

    def axis_embed(pos):
        a = pos[:, None] * freq[None, :]
        return jnp.concatenate([jnp.sin(a), jnp.cos(a)], axis=-1)

    return jnp.concatenate([axis_embed(r), axis_embed(col)], axis=-1)


def _split_in_weights(w_in):
    pm = 4 * N_HEADS * D_HEAD
    wg = w_in[:, pm:].reshape(-1, 2, 2, N_HEADS)
    return w_in[:, :pm], wg.transpose(0, 3, 1, 2).reshape(-1, 4 * N_HEADS)


def _head_params(first, second):
    return jnp.stack([first, second], axis=0).transpose(2, 0, 1).reshape(-1).astype(F32)


def _heads_per_step(t_len):
    return 4 if t_len <= 256 else 2


def kernel(x_prompt, x_sample, state_gdn_S, state_mlstm_C, state_mlstm_n, state_mlstm_m, c, c_ctx, ada_w, ada_b,
           norm1_g, norm2_g, gdn_w_in, gdn_conv_w, gdn_a_log, gdn_dt_bias, gdn_norm_g, gdn_w_out, mlstm_w_in,
           mlstm_gate_b, mlstm_norm_g, mlstm_w_out, router_w, router_bias, exp_w_gate, exp_w_up, exp_w_down,
           final_norm_g):
    bp, tp, d = x_prompt.shape
    bs, ts, _ = x_sample.shape
    n_ctx = bp * tp
    depth = ada_w.shape[0]
    assert n_ctx % ts == 0 and ts % TM_MOE == 0 and tp % TM_PROJ == 0 and bs + 1 <= N_MOD_ROWS

    pos = _grid_pos_embed(ts).astype(F32)
    x = jnp.concatenate([x_prompt.reshape(n_ctx, d), (x_sample + pos[None]).reshape(bs * ts, d)], axis=0)

    conds = jnp.concatenate([c_ctx[None, :], c, jnp.zeros((N_MOD_ROWS - 1 - bs, d), F32)], axis=0)
    mods = _ada_mods(conds, ada_w, ada_b).reshape(depth * N_MOD_ROWS * 6, 1, d)

    zeros_dh = jnp.zeros_like(gdn_a_log[0])
    ctx = dict(batch=bp, t_len=tp, row0=0, hb=_heads_per_step(tp), emit_state=True)
    dec = dict(batch=bs, t_len=ts, row0=n_ctx, hb=_heads_per_step(ts), emit_state=False)
    outs = {}
    for layer in range(depth):
        j = layer // 2
        mod_base = layer * N_MOD_ROWS * 6
        if layer % 2 == 0:
            w_main, w_gate = _split_in_weights(gdn_w_in[j])
            bias = _head_params(gdn_dt_bias[j], zeros_dh)
            mul = _head_params(-jnp.exp(gdn_a_log[j].astype(F32)), zeros_dh)
            main, grow = _inproj(x, mods, mod_base, norm1_g[layer], w_main, w_gate, bias, mul, "gdn", n_ctx, ts)
            o_ctx, s_new = _gdn_scan(main, grow, gdn_conv_w[j], gdn_norm_g[j], None, **ctx)
            (o_dec,) = _gdn_scan(main, grow, gdn_conv_w[j], gdn_norm_g[j], state_gdn_S[:, j].astype(F32), **dec)
            outs.setdefault("gdn", []).append(s_new)
            w_out = gdn_w_out[j]
        else:
            w_main, w_gate = _split_in_weights(mlstm_w_in[j])
            bias = _head_params(mlstm_gate_b[j, 0], mlstm_gate_b[j, 1])
            main, grow = _inproj(x, mods, mod_base, norm1_g[layer], w_main, w_gate, bias, jnp.zeros_like(bias),
                                 "mlstm", n_ctx, ts)
            o_ctx, c_new, n_new, m_new = _mlstm_scan(main, grow, mlstm_norm_g[j], None, **ctx)
            init = (state_mlstm_C[:, j].astype(F32),
                    state_mlstm_n[:, j].astype(F32).transpose(0, 2, 1, 3),
                    jnp.broadcast_to(state_mlstm_m[:, j].astype(F32).transpose(0, 2, 1)[..., None],
                                     (bs, N_HEADS, 2, D_HEAD)))
            (o_dec,) = _mlstm_scan(main, grow, mlstm_norm_g[j], init, **dec)
            outs.setdefault("mC", []).append(c_new)
            outs.setdefault("mn", []).append(n_new.transpose(0, 2, 1, 3))
            outs.setdefault("mm", []).append(m_new[..., 0].transpose(0, 2, 1))
            w_out = mlstm_w_out[j]
        x, hn, slot, wts, gran, counts = _outproj_route(x, o_ctx, o_dec, w_out, mods, mod_base, norm2_g[layer],
                                                        router_w, router_bias, n_ctx, ts)
        x = _moe(x, hn, slot, wts, gran, counts, exp_w_gate[layer], exp_w_up[layer], exp_w_down[layer], mods,
                 mod_base, final_norm_g, layer == depth - 1, n_ctx, ts)

    y_prompt = x[0].reshape(bp, tp, d)
    y_sample = x[1].reshape(bs, ts, d)
    return (y_prompt, y_sample, jnp.stack(outs["gdn"], axis=1), jnp.stack(outs["mC"], axis=1),
            jnp.stack(outs["mn"], axis=1), jnp.stack(outs["mm"], axis=1))
```

```python
import functools
import math

import jax
import jax.numpy as jnp
from jax import lax
from jax.experimental import pallas as pl
from jax.experimental.pallas import tpu as pltpu

F32 = jnp.float32
BF16 = jnp.bfloat16
HIGHEST = lax.Precision.HIGHEST

D_MODEL = 1024
N_HEADS = 8
D_HEAD = 128
CHUNK = 64
N_EXPERTS = 16
N_GROUPS = 4
EXP_PER_GROUP = 4
D_FF = 512
EPS = 1e-6
GRID_W = 64
POS_BASE = 10000.0
N_MOD_ROWS = 8
VMEM_LIMIT = 56 * 1024 * 1024

TM_PROJ = 256
TM_MOE = 256
GRANULE = 8
TILE_GRANULES = 2 * TM_PROJ // GRANULE + N_EXPERTS
SCAN_CHAINS = 32


def _silu(x):
    return x * jax.nn.sigmoid(x)


def _softplus(x):
    return jnp.maximum(x, 0.0) + jnp.log(1.0 + jnp.exp(-jnp.abs(x)))


def _dot(a, b, precision=None):
    return jnp.dot(a, b, preferred_element_type=F32, precision=precision)


def _dot_nt(a, b, precision=None):
    return lax.dot_general(a, b, (((1,), (1,)), ((), ())), preferred_element_type=F32, precision=precision)


def _dot_tn(a, b, precision=None):
    return lax.dot_general(a, b, (((0,), (0,)), ((), ())), preferred_element_type=F32, precision=precision)


def _rms(x):
    return x * lax.rsqrt(jnp.mean(x * x, axis=-1, keepdims=True) + EPS)


def _ada_kernel(c_ref, w_ref, b_ref, o_ref):
    cs = _silu(c_ref[...]).astype(BF16)
    o_ref[0] = _dot(cs, w_ref[0].astype(BF16)) + b_ref[0]


def _ada_mods(conds, ada_w, ada_b):
    depth, d, n6 = ada_w.shape
    tn = 1536
    return pl.pallas_call(
        _ada_kernel,
        grid=(depth, n6 // tn),
        in_specs=[
            pl.BlockSpec((N_MOD_ROWS, d), lambda l, j: (0, 0)),
            pl.BlockSpec((1, d, tn), lambda l, j: (l, 0, j)),
            pl.BlockSpec((1, 1, tn), lambda l, j: (l, 0, j)),
        ],
        out_specs=pl.BlockSpec((1, N_MOD_ROWS, tn), lambda l, j: (l, 0, j)),
        out_shape=jax.ShapeDtypeStruct((depth, N_MOD_ROWS, n6), F32),
        compiler_params=pltpu.CompilerParams(dimension_semantics=("arbitrary", "arbitrary"),
                                             vmem_limit_bytes=VMEM_LIMIT),
        name="ada_mods",
    )(conds, ada_w, ada_b.reshape(depth, 1, n6))


def _split3(x):
    hi = x.astype(BF16)
    r1 = x - hi.astype(F32)
    mid = r1.astype(BF16)
    lo = (r1 - mid.astype(F32)).astype(BF16)
    return hi, mid, lo


def _inproj_kernel(x_ref, ng_ref, sc_ref, sh_ref, w_ref, wgt_ref, gb_ref, gm_ref, main_ref, gr_ref, *, kind):
    x = x_ref[...]
    hn = _rms(x) * ng_ref[...]
    hn = hn * (1.0 + sc_ref[0]) + sh_ref[0]
    hb = hn.astype(BF16)
    main_ref[...] = _dot(hb, w_ref[...])

    pre = _dot_nt(wgt_ref[...], hb) + gb_ref[...]
    row = lax.broadcasted_iota(jnp.int32, pre.shape, 0)
    first = (row % 4) < 2
    if kind == "gdn":
        act = jnp.where(first, gm_ref[...] * _softplus(pre), jax.nn.sigmoid(pre))
        scanned_kind = 0
    else:
        act = jnp.where(first, pre, -_softplus(-pre))
        scanned_kind = 1

    tm = x.shape[0]
    r = lax.broadcasted_iota(jnp.int32, (tm, tm), 0)
    c = lax.broadcasted_iota(jnp.int32, (tm, tm), 1)
    same = (r // CHUNK) == (c // CHUNK)
    before = jnp.where(same, jnp.where(r <= c, 1.0, 0.0), 0.0).astype(BF16)
    after = jnp.where(same, jnp.where(r >= c, 1.0, 0.0), 0.0).astype(BF16)
    parts = _split3(act)
    fwd = sum(_dot(p, before) for p in parts)
    bwd = sum(_dot(p, after) for p in parts)
    gt = jnp.where((row % 4) // 2 == scanned_kind, jnp.where(row % 2 == 0, fwd, bwd), act)
    for h in range(N_HEADS):
        for ch in range(tm // CHUNK):
            gr_ref[h, ch] = gt[4 * h:4 * h + 4, ch * CHUNK:(ch + 1) * CHUNK]


def _mod_row_of_tile(tm, n_ctx, t_dec):
    def f(i):
        r = i * tm
        return jnp.where(r < n_ctx, 0, 1 + (r - n_ctx) // t_dec)
    return f


def _inproj(x, mods, mod_base, norm_g, w_main, w_gate, gate_bias, gate_mul, kind, n_ctx, t_dec):
    n, d = x.shape
    tm = TM_PROJ
    pm = w_main.shape[1]
    ng = w_gate.shape[1]
    mrow = _mod_row_of_tile(tm, n_ctx, t_dec)
    sc_map = lambda i: (mod_base + mrow(i) * 6 + 1, 0, 0)
    sh_map = lambda i: (mod_base + mrow(i) * 6 + 0, 0, 0)
    const2 = lambda i: (0, 0)
    return pl.pallas_call(
        functools.partial(_inproj_kernel, kind=kind),
        grid=(n // tm,),
        in_specs=[
            pl.BlockSpec((tm, d), lambda i: (i, 0)),
            pl.BlockSpec((1, d), const2),
            pl.BlockSpec((1, 1, d), sc_map),
            pl.BlockSpec((1, 1, d), sh_map),
            pl.BlockSpec((d, pm), const2),
            pl.BlockSpec((ng, d), const2),
            pl.BlockSpec((ng, 1), const2),
            pl.BlockSpec((ng, 1), const2),
        ],
        out_specs=[
            pl.BlockSpec((tm, pm), lambda i: (i, 0)),
            pl.BlockSpec((N_HEADS, tm // CHUNK, 4, CHUNK), lambda i: (0, i, 0, 0)),
        ],
        out_shape=[
            jax.ShapeDtypeStruct((n, pm), F32),
            jax.ShapeDtypeStruct((N_HEADS, n // CHUNK, 4, CHUNK), F32),
        ],
        compiler_params=pltpu.CompilerParams(dimension_semantics=("arbitrary",), vmem_limit_bytes=VMEM_LIMIT),
        name="inproj_" + kind,
    )(x, norm_g.reshape(1, d), mods, mods, w_main.astype(BF16), w_gate.T.astype(BF16),
      gate_bias.reshape(ng, 1), gate_mul.reshape(ng, 1))


def _chunk_masks():
    r = lax.broadcasted_iota(jnp.int32, (CHUNK, CHUNK), 0)
    c = lax.broadcasted_iota(jnp.int32, (CHUNK, CHUNK), 1)
    return r >= c, r > c, r <= c, r < c


def _chunk_rows(c):
    if isinstance(c, int):
        return pl.ds(c * CHUNK, CHUNK)
    return pl.ds(pl.multiple_of(c * CHUNK, CHUNK), CHUNK)


def _head_cols(h):
    return slice(h * D_HEAD, (h + 1) * D_HEAD)


def _last_row(x, d):
    return x[CHUNK - 1:CHUNK, :] if d == 0 else x[0:1, :]


def _gate_columns(gr4):
    a = jnp.concatenate([gr4, jnp.zeros_like(gr4)], axis=0)
    hi = a.astype(BF16).astype(F32)
    mid = (a - hi).astype(BF16).astype(F32)
    lo = a - hi - mid
    parts = jnp.concatenate([hi, mid, lo, jnp.zeros_like(a)], axis=0).astype(BF16)
    j = lax.broadcasted_iota(jnp.int32, (32, 4 * D_HEAD), 0) % 8
    lane = lax.broadcasted_iota(jnp.int32, (32, 4 * D_HEAD), 1)
    return _dot_tn(parts, jnp.where(lane // D_HEAD == j, 1.0, 0.0).astype(BF16))


def _gate_column(cols, j):
    return cols[:, j * D_HEAD:(j + 1) * D_HEAD]


def _unit_tri_solve(a_list, rhs_list):
    mm = lambda x, y: _dot(x.astype(BF16), y.astype(BF16))
    r = lax.broadcasted_iota(jnp.int32, (CHUNK, CHUNK), 0)
    c = lax.broadcasted_iota(jnp.int32, (CHUNK, CHUNK), 1)
    same = (r // 16) == (c // 16)
    eye = jnp.where(r == c, 1.0, 0.0)
    d = [jnp.where(same, a, 0.0) for a in a_list]
    t = [eye - di for di in d]
    for _ in range(3):
        d = [mm(di, di) for di in d]
        t = [ti + mm(ti, di) for ti, di in zip(t, d)]
    b = [mm(ti, jnp.where(same, 0.0, a)) for ti, a in zip(t, a_list)]
    x = [mm(ti, ri) for ti, ri in zip(t, rhs_list)]
    b2 = [mm(bi, bi) for bi in b]
    x = [xi - mm(bi, xi) for xi, bi in zip(x, b)]
    return [xi + mm(bi, xi) for xi, bi in zip(x, b2)]


def _for_chunk_groups(nc, group, fn):
    if nc == group:
        fn(list(range(nc)))
        return

    def body(g, carry):
        fn([g * group + j for j in range(group)])
        return carry
    lax.fori_loop(0, nc // group, body, 0)


def _scan_specs(t_len, row0, hb, n_proj):
    rb = row0 // t_len
    nc = t_len // CHUNK
    ngrp = N_HEADS // hb
    specs = [pl.BlockSpec((t_len, hb * D_HEAD), functools.partial(lambda b, g, j: (rb + b, j * ngrp + g), j=j))
             for j in range(n_proj)]
    specs.append(pl.BlockSpec((hb, nc, 4, CHUNK), lambda b, g: (g, rb + b, 0, 0)))
    return specs


def _scan_call(kern, name, args, in_specs, out_specs, out_shape, scratch, batch, hb):
    return pl.pallas_call(
        kern,
        grid=(batch, N_HEADS // hb),
        in_specs=in_specs,
        out_specs=out_specs,
        out_shape=out_shape,
        scratch_shapes=scratch,
        compiler_params=pltpu.CompilerParams(dimension_semantics=("arbitrary", "arbitrary"),
                                             vmem_limit_bytes=VMEM_LIMIT),
        name=name,
    )(*args)


def _gdn_kernel(*refs, t_len, hb, has_init, emit_state):
    it = iter(refs)
    q_ref, k_ref, v_ref, z_ref, gr_ref, cwq_ref, cwk_ref, cwv_ref, ng_ref = (next(it) for _ in range(9))
    s0_ref = next(it) if has_init else None
    o_ref = next(it)
    s_ref = next(it) if emit_state else None
    qs, ks, vs, oacc, qp_s, op_s, km_s, nm_s, ge_s = (next(it) for _ in range(9))
    nc = t_len // CHUNK

    rows = lax.broadcasted_iota(jnp.int32, (t_len, 1), 0)

    def conv_silu(x, cw):
        xm = jnp.where(rows == 0, 0.0, pltpu.roll(x, 1, axis=0))
        xp = jnp.where(rows == t_len - 1, 0.0, pltpu.roll(x, t_len - 1, axis=0))
        return _silu(xm * cw[0:1] + x * cw[1:2] + xp * cw[2:3])

    def l2n(x):
        return x * lax.rsqrt(jnp.sum(x * x, axis=-1, keepdims=True) + EPS)

    for h in range(hb):
        hc = _head_cols(h)
        qs[:, hc] = l2n(conv_silu(q_ref[:, hc], cwq_ref[:, hc])) * (D_HEAD ** -0.5)
        ks[:, hc] = l2n(conv_silu(k_ref[:, hc], cwk_ref[:, hc]))
        vs[:, hc] = conv_silu(v_ref[:, hc], cwv_ref[:, hc])
    oacc[...] = jnp.zeros_like(oacc)

    lo_i, lo_s, up_i, up_s = _chunk_masks()

    def slot(h, d, c):
        return (h * 2 + d) * nc + c

    def intra(chunks):
        items = [(h, c, d) for h in range(hb) for c in chunks for d in range(2)]
        cols = [[_gate_columns(gr_ref[h, c]) for c in chunks] for h in range(hb)]
        a_list, rhs_list, keep = [], [], []
        for n_item, (h, c, d) in enumerate(items):
            sl, hc = _chunk_rows(c), _head_cols(h)
            q, k, v = qs[sl, hc], ks[sl, hc], vs[sl, hc]
            gr4 = gr_ref[h, c]
            col4 = cols[h][(n_item // 2) % len(chunks)]
            g_col, beta = _gate_column(col4, d), _gate_column(col4, 2 + d)
            incl, strict = (lo_i, lo_s) if d == 0 else (up_i, up_s)
            decay = jnp.exp(jnp.where(incl, g_col[:, :CHUNK] - gr4[d:d + 1, :], -jnp.inf))
            kb = k * beta
            k16 = k.astype(BF16)
            a_list.append(jnp.where(strict, _dot_nt(kb.astype(BF16), k16) * decay, 0.0))
            e_g = jnp.exp(g_col)
            rhs_list.append(jnp.concatenate([v * beta, kb * e_g], axis=1))
            qk16 = jnp.where(incl, _dot_nt(q.astype(BF16), k16) * decay, 0.0).astype(BF16)
            g_last = _last_row(g_col, d)
            ge_s[slot(h, d, c)] = jnp.broadcast_to(jnp.exp(g_last), (8, D_HEAD))
            keep.append((qk16, q * e_g, (k * jnp.exp(g_last - g_col)).astype(BF16)))
        sols = [s.astype(BF16) for s in _unit_tri_solve(a_list, rhs_list)]
        qw = [_dot(kp[0], s) for kp, s in zip(keep, sols)]
        kw = [_dot_tn(kp[2], s) for kp, s in zip(keep, sols)]
        for (h, c, d), kp, qwi, kwi in zip(items, keep, qw, kw):
            i = h * 2 + d
            sl = _chunk_rows(c)
            op_s[i, sl, :] = qwi[:, :D_HEAD]
            qp_s[i, sl, :] = (kp[1] - qwi[:, D_HEAD:]).astype(BF16)
            nm_s[slot(h, d, c)] = kwi[:, :D_HEAD]
            km_s[slot(h, d, c)] = (-kwi[:, D_HEAD:]).astype(BF16)

    _for_chunk_groups(nc, min(nc, max(1, SCAN_CHAINS // (2 * hb))), intra)

    chains = [(h, d) for h in range(hb) for d in range(2)]

    def inter_body(i, carry):
        cs = (i, nc - 1 - i)
        s16 = [s.astype(BF16) for s in carry]
        outs = [_dot(qp_s[h * 2 + d, _chunk_rows(cs[d]), :], s) for (h, d), s in zip(chains, s16)]
        upds = [_dot(km_s[slot(h, d, cs[d])], s) for (h, d), s in zip(chains, s16)]
        for (h, d), o in zip(chains, outs):
            oacc[_chunk_rows(cs[d]), _head_cols(h)] += o + op_s[h * 2 + d, _chunk_rows(cs[d]), :]
        return tuple(ge_s[slot(h, d, cs[d])][0:1, :] * s + u + nm_s[slot(h, d, cs[d])]
                     for (h, d), s, u in zip(chains, carry, upds))

    if has_init:
        init = tuple(s0_ref[0, d, h] for h, d in chains)
    else:
        init = tuple(jnp.zeros((D_HEAD, D_HEAD), F32) for _ in chains)
    fin = lax.fori_loop(0, nc, inter_body, init)
    if emit_state:
        for (h, d), s in zip(chains, fin):
            s_ref[0, d, h] = s
    for h in range(hb):
        hc = _head_cols(h)
        o_ref[:, hc] = (_rms(oacc[:, hc]) * ng_ref[...] * _silu(z_ref[:, hc])).astype(o_ref.dtype)


def _gdn_scan(main, grow, conv_w, norm_g, s0, *, batch, t_len, row0, hb, emit_state):
    nc = t_len // CHUNK
    ngrp = N_HEADS // hb
    has_init = s0 is not None
    in_specs = _scan_specs(t_len, row0, hb, 4)
    in_specs += [pl.BlockSpec((3, hb * D_HEAD), functools.partial(lambda b, g, j: (0, j * ngrp + g), j=j))
                 for j in range(3)]
    in_specs.append(pl.BlockSpec((1, D_HEAD), lambda b, g: (0, 0)))
    args = [main, main, main, main, grow, conv_w, conv_w, conv_w, norm_g.reshape(1, D_HEAD)]
    state_spec = pl.BlockSpec((1, 2, hb, D_HEAD, D_HEAD), lambda b, g: (b, 0, g, 0, 0))
    if has_init:
        in_specs.append(state_spec)
        args.append(s0)
    out_specs = [pl.BlockSpec((t_len, hb * D_HEAD), lambda b, g: (b, g))]
    out_shape = [jax.ShapeDtypeStruct((batch * t_len, N_HEADS * D_HEAD), BF16)]
    if emit_state:
        out_specs.append(state_spec)
        out_shape.append(jax.ShapeDtypeStruct((batch, 2, N_HEADS, D_HEAD, D_HEAD), F32))
    scratch = ([pltpu.VMEM((t_len, hb * D_HEAD), F32) for _ in range(4)]
               + [pltpu.VMEM((2 * hb, t_len, D_HEAD), BF16),
                  pltpu.VMEM((2 * hb, t_len, D_HEAD), F32),
                  pltpu.VMEM((2 * hb * nc, D_HEAD, D_HEAD), BF16),
                  pltpu.VMEM((2 * hb * nc, D_HEAD, D_HEAD), F32),
                  pltpu.VMEM((2 * hb * nc, 8, D_HEAD), F32)])
    kern = functools.partial(_gdn_kernel, t_len=t_len, hb=hb, has_init=has_init, emit_state=emit_state)
    return _scan_call(kern, "gdn_scan_t%d" % t_len, args, in_specs, out_specs, out_shape, scratch, batch, hb)


def _mlstm_kernel(*refs, t_len, hb, has_init, emit_state):
    it = iter(refs)
    q_ref, k_ref, v_ref, og_ref, gr_ref, ng_ref = (next(it) for _ in range(6))
    if has_init:
        c0_ref, n0_ref, m0_ref = (next(it) for _ in range(3))
    o_ref = next(it)
    if emit_state:
        c_ref, n_ref, m_ref = (next(it) for _ in range(3))
    hacc, q16_s, v1_s, qk_s, ld_s, bb_s, lm_s, lw_s, sc_s = (next(it) for _ in range(9))
    nc = t_len // CHUNK

    hacc[...] = jnp.zeros_like(hacc)
    ones = jnp.ones((t_len, D_HEAD), BF16)
    for h in range(hb):
        hc = _head_cols(h)
        q16_s[:, hc] = (q_ref[:, hc] * (D_HEAD ** -0.5)).astype(BF16)
        v1_s[:, h * 2 * D_HEAD:(h * 2 + 1) * D_HEAD] = v_ref[:, hc].astype(BF16)
        v1_s[:, (h * 2 + 1) * D_HEAD:(h * 2 + 2) * D_HEAD] = ones
    lo_i, _, up_i, _ = _chunk_masks()

    def slot(h, d, c):
        return (h * 2 + d) * nc + c

    def intra(chunks):
        for h in range(hb):
            for c in chunks:
                sl, hc = _chunk_rows(c), _head_cols(h)
                qk_s[h, sl, :] = _dot_nt(q16_s[sl, hc], k_ref[sl, hc].astype(BF16))
                gr4 = gr_ref[h, c]
                col4 = _gate_columns(gr4)
                for d in range(2):
                    i = h * 2 + d
                    b_col, i_col = _gate_column(col4, 2 + d), _gate_column(col4, d)
                    b_last = _last_row(b_col, d)
                    lwe = b_last - b_col + i_col
                    log_d = jnp.where(lo_i if d == 0 else up_i,
                                      b_col[:, :CHUNK] - gr4[2 + d:3 + d, :] + gr4[d:d + 1, :], -jnp.inf)
                    ld_s[i, sl, :] = log_d
                    lm_s[i, sl, :] = jnp.broadcast_to(jnp.max(log_d, axis=-1, keepdims=True), (CHUNK, D_HEAD))
                    bb_s[i, sl, :] = b_col
                    lw_s[i, sl, :] = lwe
                    sc_s[slot(h, d, c), 0:1, :] = b_last
                    sc_s[slot(h, d, c), 1:2, :] = jnp.max(lwe, axis=0, keepdims=True)

    _for_chunk_groups(nc, min(nc, 4), intra)

    chains = [(h, d) for h in range(hb) for d in range(2)]

    def body(step, carry):
        cs = (step, nc - 1 - step)
        state, nrow, m = zip(*carry)
        sls = [_chunk_rows(cs[d]) for h, d in chains]
        idx = [h * 2 + d for h, d in chains]
        v1 = [v1_s[sl, h * 2 * D_HEAD:(h * 2 + 2) * D_HEAD] for (h, d), sl in zip(chains, sls)]
        log_last = [sc_s[slot(h, d, cs[d]), 0:1, :] + mi for (h, d), mi in zip(chains, m)]
        m_new = [jnp.maximum(ll, sc_s[slot(h, d, cs[d]), 1:2, :]) for (h, d), ll in zip(chains, log_last)]
        dec = [jnp.exp(ll - mn) for ll, mn in zip(log_last, m_new)]
        kw = [k_ref[sl, _head_cols(h)] * jnp.exp(lw_s[i, sl, :] - mn)
              for (h, d), sl, i, mn in zip(chains, sls, idx, m_new)]
        upd = [_dot_tn(a.astype(BF16), b) for a, b in zip(kw, v1)]
        qc = [_dot(q16_s[sl, _head_cols(h)], s.astype(BF16)) for (h, d), sl, s in zip(chains, sls, state)]
        log_inter = [bb_s[i, sl, :] + mi for i, sl, mi in zip(idx, sls, m)]
        mt = [jnp.maximum(li, lm_s[i, sl, :]) for li, i, sl in zip(log_inter, idx, sls)]
        s_inter = [jnp.exp(li - t) for li, t in zip(log_inter, mt)]
        p = [(jnp.exp(ld_s[i, sl, :] - t[:, :CHUNK]) * qk_s[h, sl, :]).astype(BF16)
             for (h, d), i, sl, t in zip(chains, idx, sls, mt)]
        pv = [_dot(a, b) for a, b in zip(p, v1)]
        for (h, d), sl, si, qci, pvi, t in zip(chains, sls, s_inter, qc, pv, mt):
            num = si * qci[:, :D_HEAD] + pvi[:, :D_HEAD]
            den = si * qci[:, D_HEAD:] + pvi[:, D_HEAD:]
            hacc[sl, _head_cols(h)] += num / jnp.maximum(jnp.abs(den), jnp.exp(-t))
        return tuple((jnp.concatenate([dc, dc], axis=1) * s + u, dc * nr + jnp.sum(kwi, axis=0, keepdims=True), mn)
                     for dc, s, u, nr, kwi, mn in zip(dec, state, upd, nrow, kw, m_new))

    if has_init:
        init = tuple((jnp.concatenate([c0_ref[0, d, h],
                                       jnp.transpose(jnp.broadcast_to(n0_ref[0, h, d:d + 1, :], (D_HEAD, D_HEAD)))],
                                      axis=1),
                      n0_ref[0, h, d:d + 1, :], m0_ref[0, h, d:d + 1, :]) for h, d in chains)
    else:
        init = tuple((jnp.zeros((D_HEAD, 2 * D_HEAD), F32), jnp.zeros((1, D_HEAD), F32), jnp.zeros((1, D_HEAD), F32))
                     for _ in chains)
    fin = lax.fori_loop(0, nc, body, init)
    if emit_state:
        for (h, d), (s, nr, m) in zip(chains, fin):
            c_ref[0, d, h] = s[:, :D_HEAD]
            n_ref[0, h, d:d + 1, :] = nr
            m_ref[0, h, d:d + 1, :] = m
    for h in range(hb):
        hc = _head_cols(h)
        o_ref[:, hc] = (_rms(hacc[:, hc]) * ng_ref[...] * jax.nn.sigmoid(og_ref[:, hc])).astype(o_ref.dtype)


def _mlstm_scan(main, grow, norm_g, init, *, batch, t_len, row0, hb, emit_state):
    nc = t_len // CHUNK
    has_init = init is not None
    in_specs = _scan_specs(t_len, row0, hb, 4)
    in_specs.append(pl.BlockSpec((1, D_HEAD), lambda b, g: (0, 0)))
    args = [main, main, main, main, grow, norm_g.reshape(1, D_HEAD)]
    c_spec = pl.BlockSpec((1, 2, hb, D_HEAD, D_HEAD), lambda b, g: (b, 0, g, 0, 0))
    v_spec = pl.BlockSpec((1, hb, 2, D_HEAD), lambda b, g: (b, g, 0, 0))
    if has_init:
        in_specs += [c_spec, v_spec, v_spec]
        args += list(init)
    out_specs = [pl.BlockSpec((t_len, hb * D_HEAD), lambda b, g: (b, g))]
    out_shape = [jax.ShapeDtypeStruct((batch * t_len, N_HEADS * D_HEAD), BF16)]
    if emit_state:
        out_specs += [c_spec, v_spec, v_spec]
        out_shape += [jax.ShapeDtypeStruct((batch, 2, N_HEADS, D_HEAD, D_HEAD), F32),
                      jax.ShapeDtypeStruct((batch, N_HEADS, 2, D_HEAD), F32),
                      jax.ShapeDtypeStruct((batch, N_HEADS, 2, D_HEAD), F32)]
    scratch = [pltpu.VMEM((t_len, hb * D_HEAD), F32),
               pltpu.VMEM((t_len, hb * D_HEAD), BF16),
               pltpu.VMEM((t_len, hb * 2 * D_HEAD), BF16),
               pltpu.VMEM((hb, t_len, CHUNK), F32),
               pltpu.VMEM((2 * hb, t_len, CHUNK), F32),
               pltpu.VMEM((2 * hb, t_len, D_HEAD), F32),
               pltpu.VMEM((2 * hb, t_len, D_HEAD), F32),
               pltpu.VMEM((2 * hb, t_len, D_HEAD), F32),
               pltpu.VMEM((2 * hb * nc, 8, D_HEAD), F32)]
    kern = functools.partial(_mlstm_kernel, t_len=t_len, hb=hb, has_init=has_init, emit_state=emit_state)
    return _scan_call(kern, "mlstm_scan_t%d" % t_len, args, in_specs, out_specs, out_shape, scratch, batch, hb)


def _top2_of4(v):
    best, i1 = v[0], jnp.zeros(v[0].shape, jnp.int32)
    for j in range(1, 4):
        take = v[j] > best
        i1 = jnp.where(take, j, i1)
        best = jnp.where(take, v[j], best)
    best2, i2 = None, None
    for j in range(4):
        vj = jnp.where(i1 == j, -jnp.inf, v[j])
        if best2 is None:
            best2, i2 = vj, jnp.zeros(v[0].shape, jnp.int32)
        else:
            take = vj > best2
            i2 = jnp.where(take, j, i2)
            best2 = jnp.where(take, vj, best2)
    return i1, i2


def _pick(rows, idx):
    out = rows[0]
    for j in range(1, len(rows)):
        out = jnp.where(idx == j, rows[j], out)
    return out


def _outproj_kernel(x_ref, oc_ref, od_ref, w_ref, g1_ref, ng_ref, sc_ref, sh_ref, rw_ref, rb_ref, xo_ref, hn_ref,
                    slot_ref, wt_ref, gran_ref, cnt_ref, run_ref, *, ctx_tiles):
    @pl.when(pl.program_id(0) == 0)
    def _():
        run_ref[...] = jnp.zeros_like(run_ref)

    o = jnp.where(pl.program_id(0) < ctx_tiles, oc_ref[...], od_ref[...])
    xn = x_ref[...] + g1_ref[0] * _dot(o, w_ref[...])
    xo_ref[...] = xn
    hn = _rms(xn) * ng_ref[...]
    hn = hn * (1.0 + sc_ref[0]) + sh_ref[0]
    hn_ref[...] = hn.astype(BF16)

    logits = _dot_nt(rw_ref[...], hn, HIGHEST)
    ex = jnp.exp(logits - jnp.max(logits, axis=0, keepdims=True))
    probs = ex / jnp.sum(ex, axis=0, keepdims=True)
    sel = probs + rb_ref[...]
    sel_rows = [sel[e:e + 1, :] for e in range(N_EXPERTS)]
    prob_rows = [probs[e:e + 1, :] for e in range(N_EXPERTS)]
    scores = []
    for g in range(N_GROUPS):
        r = sel_rows[4 * g:4 * g + 4]
        a, b = jnp.maximum(r[0], r[1]), jnp.minimum(r[0], r[1])
        c, d = jnp.maximum(r[2], r[3]), jnp.minimum(r[2], r[3])
        scores.append(jnp.maximum(a, c) + jnp.maximum(jnp.minimum(a, c), jnp.maximum(b, d)))
    best, grp = scores[0], jnp.zeros(scores[0].shape, jnp.int32)
    for g in range(1, N_GROUPS):
        take = scores[g] > best
        grp = jnp.where(take, g, grp)
        best = jnp.where(take, scores[g], best)
    sel_in = [_pick([sel_rows[4 * g + j] for g in range(N_GROUPS)], grp) for j in range(4)]
    prob_in = [_pick([prob_rows[4 * g + j] for g in range(N_GROUPS)], grp) for j in range(4)]
    i1, i2 = _top2_of4(sel_in)
    w1, w2 = _pick(prob_in, i1), _pick(prob_in, i2)
    tot = w1 + w2
    e1, e2 = grp * 4 + i1, grp * 4 + i2
    wt_ref[...] = jnp.concatenate([w1 / tot, w2 / tot], axis=0)

    tm = logits.shape[1]
    eidx = lax.broadcasted_iota(jnp.int32, logits.shape, 0)
    onehot = jnp.where(eidx == e1, 1.0, 0.0) + jnp.where(eidx == e2, 1.0, 0.0)
    r = lax.broadcasted_iota(jnp.int32, (tm, tm), 0)
    c = lax.broadcasted_iota(jnp.int32, (tm, tm), 1)
    earlier = jnp.where(r < c, 1.0, 0.0).astype(BF16)
    ahead = _dot(onehot.astype(BF16), earlier)
    n_gran = jnp.floor((jnp.sum(onehot, axis=1, keepdims=True) + (GRANULE - 1)) * (1.0 / GRANULE))
    n_gran = jnp.broadcast_to(n_gran, (N_EXPERTS, D_HEAD))
    er = lax.broadcasted_iota(jnp.int32, (N_EXPERTS, N_EXPERTS), 0)
    ec = lax.broadcasted_iota(jnp.int32, (N_EXPERTS, N_EXPERTS), 1)
    first = _dot(jnp.where(ec < er, 1.0, 0.0).astype(BF16), n_gran.astype(BF16))
    row0 = GRANULE * first[:, 0:1] + ahead
    slot = [jnp.sum(jnp.where(eidx == e, row0, 0.0), axis=0, keepdims=True) for e in (e1, e2)]
    slot_ref[...] = jnp.concatenate(slot, axis=0).astype(jnp.int32)

    g = lax.broadcasted_iota(jnp.int32, (N_EXPERTS, D_HEAD), 1).astype(F32)
    owner = jnp.sum(jnp.where(g >= first + n_gran, 1.0, 0.0), axis=0, keepdims=True)
    ge = lax.broadcasted_iota(jnp.int32, (N_EXPERTS, D_HEAD), 0).astype(F32)
    index = g[0:1, :] + jnp.sum(jnp.where(ge == owner, run_ref[...] - first, 0.0), axis=0, keepdims=True)
    gran_ref[0] = jnp.concatenate([owner, index], axis=0).astype(jnp.int32)
    run_ref[...] += n_gran
    cnt_ref[...] = run_ref[...]


def _outproj_route(x, o_ctx, o_dec, w_out, mods, mod_base, norm_g, router_w, router_bias, n_ctx, t_dec):
    n, d = x.shape
    tm = TM_PROJ
    ctx_tiles = n_ctx // tm
    mrow = _mod_row_of_tile(tm, n_ctx, t_dec)
    mod_map = lambda k: (lambda i: (mod_base + mrow(i) * 6 + k, 0, 0))
    const2 = lambda i: (0, 0)
    return pl.pallas_call(
        functools.partial(_outproj_kernel, ctx_tiles=ctx_tiles),
        grid=(n // tm,),
        in_specs=[
            pl.BlockSpec((tm, d), lambda i: (i, 0)),
            pl.BlockSpec((tm, d), lambda i: (jnp.minimum(i, ctx_tiles - 1), 0)),
            pl.BlockSpec((tm, d), lambda i: (jnp.maximum(i - ctx_tiles, 0), 0)),
            pl.BlockSpec((d, d), const2),
            pl.BlockSpec((1, 1, d), mod_map(2)),
            pl.BlockSpec((1, d), const2),
            pl.BlockSpec((1, 1, d), mod_map(4)),
            pl.BlockSpec((1, 1, d), mod_map(3)),
            pl.BlockSpec((N_EXPERTS, d), const2),
            pl.BlockSpec((N_EXPERTS, 1), const2),
        ],
        out_specs=[
            pl.BlockSpec((tm, d), lambda i: (i, 0)),
            pl.BlockSpec((tm, d), lambda i: (i, 0)),
            pl.BlockSpec((2, tm), lambda i: (0, i)),
            pl.BlockSpec((2, tm), lambda i: (0, i)),
            pl.BlockSpec((1, 2, D_HEAD), lambda i: (i, 0, 0)),
            pl.BlockSpec((N_EXPERTS, D_HEAD), const2),
        ],
        out_shape=[
            jax.ShapeDtypeStruct((n, d), F32),
            jax.ShapeDtypeStruct((n, d), BF16),
            jax.ShapeDtypeStruct((2, n), jnp.int32),
            jax.ShapeDtypeStruct((2, n), F32),
            jax.ShapeDtypeStruct((n // tm, 2, D_HEAD), jnp.int32),
            jax.ShapeDtypeStruct((N_EXPERTS, D_HEAD), F32),
        ],
        scratch_shapes=[pltpu.VMEM((N_EXPERTS, D_HEAD), F32)],
        compiler_params=pltpu.CompilerParams(dimension_semantics=("arbitrary",), vmem_limit_bytes=VMEM_LIMIT),
        name="outproj_route",
    )(x, o_ctx, o_dec, w_out.astype(BF16), mods, norm_g.reshape(1, d), mods, mods, router_w.T, router_bias.reshape(N_EXPERTS, 1))


def _route_tables(gran, counts, n_tiles):
    tile_gran = TM_MOE // GRANULE
    cnt = counts[:, 0].astype(jnp.int32)
    padded = (cnt + tile_gran - 1) // tile_gran * tile_gran
    ends = jnp.cumsum(padded)
    offs = ends - padded
    owner, index = gran[:, 0, :], gran[:, 1, :]
    first_gran = index + sum(jnp.where(owner == e, offs[e], 0) for e in range(N_EXPERTS))
    rows = jnp.where(owner < N_EXPERTS, first_gran * GRANULE, -1).astype(jnp.int32)
    tile_g0 = jnp.arange(n_tiles, dtype=jnp.int32) * tile_gran
    tile_expert = jnp.minimum(jnp.sum(tile_g0[:, None] >= ends[None, :], axis=1), N_EXPERTS - 1)
    tile_valid = jnp.sum((tile_g0[:, None] >= offs[None, :]) & (tile_g0[:, None] < (offs + cnt)[None, :]), axis=1)
    j = jnp.arange(tile_gran, dtype=jnp.int32)[None, :]
    tails = jnp.where(j < (padded - cnt)[:, None], (offs + cnt)[:, None] + j, -1)
    tails = jnp.where(tails >= 0, tails * GRANULE, -1).astype(jnp.int32).reshape(1, 1, N_EXPERTS * tile_gran)
    return rows, tails, tile_expert.astype(jnp.int32), tile_valid.astype(jnp.int32)


def _granule_copy(src, src_row, dst, dst_row, sem):
    return pltpu.make_async_copy(src.at[pl.ds(pl.multiple_of(src_row, GRANULE), GRANULE)],
                                 dst.at[pl.ds(pl.multiple_of(dst_row, GRANULE), GRANULE)], sem)


def _for_granules(rows_ref, fn, count=TILE_GRANULES):
    def body(g, carry):
        row = rows_ref[0, 0, g]

        @pl.when(row >= 0)
        def _():
            fn(g, row)
        return carry
    lax.fori_loop(0, count, body, 0)


def _pair_matrix(slot, first_value, second_value):
    p = lax.broadcasted_iota(jnp.int32, (TILE_GRANULES * GRANULE, slot.shape[1]), 0)
    return jnp.where(p == slot[0:1, :], first_value, 0.0) + jnp.where(p == slot[1:2, :], second_value, 0.0)


def _dispatch_kernel(rows_ref, prev_ref, tail_ref, valid_ref, hn_ref, slot_ref, xs_ref, buf, zero, sems):
    i = pl.program_id(0)
    cur = i % 2
    buf[cur] = _dot(_pair_matrix(slot_ref[...], 1.0, 1.0).astype(BF16), hn_ref[...])

    def copies(table_ref, b, act):
        _for_granules(table_ref, lambda g, row: act(_granule_copy(buf.at[b], g * GRANULE, xs_ref, row, sems.at[b])))

    copies(rows_ref, cur, lambda c: c.start())

    @pl.when(i > 0)
    def _():
        copies(prev_ref, 1 - cur, lambda c: c.wait())

    @pl.when(i == pl.num_programs(0) - 1)
    def _():
        copies(rows_ref, cur, lambda c: c.wait())
        zero[...] = jnp.zeros_like(zero)
        n_tail = tail_ref.shape[2]
        _for_granules(tail_ref, lambda g, row: _granule_copy(zero, 0, xs_ref, row, sems.at[0]).start(), n_tail)
        _for_granules(tail_ref, lambda g, row: _granule_copy(zero, 0, xs_ref, row, sems.at[0]).wait(), n_tail)

        def fill_tiles(act):
            def body(t, carry):
                @pl.when(valid_ref[0, 0, t] == 0)
                def _():
                    act(pltpu.make_async_copy(zero, xs_ref.at[pl.ds(pl.multiple_of(t * TM_MOE, TM_MOE), TM_MOE)],
                                              sems.at[1]))
                return carry
            lax.fori_loop(0, valid_ref.shape[2], body, 0)

        fill_tiles(lambda c: c.start())
        fill_tiles(lambda c: c.wait())


def _dispatch(hn, slot, rows, tails, tile_valid, n_rows):
    n, d = hn.shape
    tm = TM_PROJ
    rows = rows.reshape(n // tm, 1, D_HEAD)
    table = lambda index_map: pl.BlockSpec((1, 1, D_HEAD), index_map, memory_space=pltpu.SMEM)
    return pl.pallas_call(
        _dispatch_kernel,
        grid=(n // tm,),
        in_specs=[
            table(lambda i: (i, 0, 0)),
            table(lambda i: (jnp.maximum(i - 1, 0), 0, 0)),
            pl.BlockSpec(memory_space=pltpu.SMEM),
            pl.BlockSpec(memory_space=pltpu.SMEM),
            pl.BlockSpec((tm, d), lambda i: (i, 0)),
            pl.BlockSpec((2, tm), lambda i: (0, i)),
        ],
        out_specs=pl.BlockSpec(memory_space=pl.ANY),
        out_shape=jax.ShapeDtypeStruct((n_rows, d), F32),
        scratch_shapes=[pltpu.VMEM((2, TILE_GRANULES * GRANULE, d), F32), pltpu.VMEM((TM_MOE, d), F32),
                        pltpu.SemaphoreType.DMA((2,))],
        compiler_params=pltpu.CompilerParams(dimension_semantics=("arbitrary",), vmem_limit_bytes=VMEM_LIMIT),
        name="moe_dispatch",
    )(rows, rows, tails, tile_valid.reshape(1, 1, -1), hn, slot)


def _expert_kernel(te_ref, tv_ref, xs_ref, wg_ref, wu_ref, wd_ref, ys_ref, wg16, wu16, wd16):
    i = pl.program_id(0)
    fresh = jnp.logical_or(i == 0, te_ref[i] != te_ref[jnp.maximum(i - 1, 0)])

    @pl.when(fresh)
    def _():
        wg16[...] = wg_ref[0, 0].astype(BF16)
        wu16[...] = wu_ref[0, 0].astype(BF16)
        wd16[...] = wd_ref[0, 0].astype(BF16)

    @pl.when(tv_ref[i] != 0)
    def _():
        x = xs_ref[...].astype(BF16)
        hid = _silu(_dot(x, wg16[...])) * _dot(x, wu16[...])
        ys_ref[...] = _dot(hid.astype(BF16), wd16[...])

    @pl.when(tv_ref[i] == 0)
    def _():
        ys_ref[...] = jnp.zeros_like(ys_ref)


def _experts(xs, tile_expert, tile_valid, w_gate, w_up, w_down, layer):
    n_rows, d = xs.shape
    tm = TM_MOE
    w_in_spec = pl.BlockSpec((1, 1, d, D_FF), lambda i, te, tv: (layer, te[i], 0, 0))
    return pl.pallas_call(
        _expert_kernel,
        grid_spec=pltpu.PrefetchScalarGridSpec(
            num_scalar_prefetch=2,
            grid=(n_rows // tm,),
            in_specs=[
                pl.BlockSpec((tm, d), lambda i, te, tv: (jnp.where(tv[i] != 0, i, 0), 0)),
                w_in_spec,
                w_in_spec,
                pl.BlockSpec((1, 1, D_FF, d), lambda i, te, tv: (layer, te[i], 0, 0)),
            ],
            out_specs=pl.BlockSpec((tm, d), lambda i, te, tv: (i, 0)),
            scratch_shapes=[pltpu.VMEM((d, D_FF), BF16), pltpu.VMEM((d, D_FF), BF16), pltpu.VMEM((D_FF, d), BF16)],
        ),
        out_shape=jax.ShapeDtypeStruct((n_rows, d), F32),
        compiler_params=pltpu.CompilerParams(dimension_semantics=("arbitrary",), vmem_limit_bytes=VMEM_LIMIT),
        name="moe_experts",
    )(tile_expert, tile_valid, xs, w_gate, w_up, w_down)


def _combine_kernel(rows_ref, next_ref, x_ref, slot_ref, wt_ref, g2_ref, fg_ref, ys_ref, *rest, ctx_tiles):
    *o_refs, buf, sems = rest
    i = pl.program_id(0)
    cur = i % 2

    def copies(table_ref, b, act):
        _for_granules(table_ref, lambda g, row: act(_granule_copy(ys_ref, row, buf.at[b], g * GRANULE, sems.at[b])))

    @pl.when(i == 0)
    def _():
        buf[...] = jnp.zeros_like(buf)
        copies(rows_ref, 0, lambda c: c.start())

    @pl.when(i + 1 < pl.num_programs(0))
    def _():
        copies(next_ref, 1 - cur, lambda c: c.start())

    copies(rows_ref, cur, lambda c: c.wait())
    w = wt_ref[...]
    mix = _pair_matrix(slot_ref[...], w[0:1, :], w[1:2, :]).astype(BF16)
    xn = x_ref[...] + g2_ref[0] * _dot_tn(mix, buf[cur].astype(BF16))
    if ctx_tiles is None:
        o_refs[0][...] = xn
        return
    xn = _rms(xn) * fg_ref[...]

    @pl.when(pl.program_id(0) < ctx_tiles)
    def _():
        o_refs[0][...] = xn

    @pl.when(pl.program_id(0) >= ctx_tiles)
    def _():
        o_refs[1][...] = xn


def _combine(x, ys, slot, wts, rows, mods, mod_base, final_g, final_norm, n_ctx, t_dec):
    n, d = x.shape
    tm = TM_PROJ
    mrow = _mod_row_of_tile(tm, n_ctx, t_dec)
    ctx_tiles = n_ctx // tm if final_norm else None
    if final_norm:
        out_specs = [pl.BlockSpec((tm, d), lambda i: (jnp.minimum(i, ctx_tiles - 1), 0)),
                     pl.BlockSpec((tm, d), lambda i: (jnp.maximum(i - ctx_tiles, 0), 0))]
        out_shape = [jax.ShapeDtypeStruct((n_ctx, d), F32), jax.ShapeDtypeStruct((n - n_ctx, d), F32)]
    else:
        out_specs = pl.BlockSpec((tm, d), lambda i: (i, 0))
        out_shape = jax.ShapeDtypeStruct((n, d), F32)
    rows = rows.reshape(n // tm, 1, D_HEAD)
    last = n // tm - 1
    return pl.pallas_call(
        functools.partial(_combine_kernel, ctx_tiles=ctx_tiles),
        grid=(n // tm,),
        in_specs=[
            pl.BlockSpec((1, 1, D_HEAD), lambda i: (i, 0, 0), memory_space=pltpu.SMEM),
            pl.BlockSpec((1, 1, D_HEAD), lambda i: (jnp.minimum(i + 1, last), 0, 0), memory_space=pltpu.SMEM),
            pl.BlockSpec((tm, d), lambda i: (i, 0)),
            pl.BlockSpec((2, tm), lambda i: (0, i)),
            pl.BlockSpec((2, tm), lambda i: (0, i)),
            pl.BlockSpec((1, 1, d), lambda i: (mod_base + mrow(i) * 6 + 5, 0, 0)),
            pl.BlockSpec((1, d), lambda i: (0, 0)),
            pl.BlockSpec(memory_space=pl.ANY),
        ],
        out_specs=out_specs,
        out_shape=out_shape,
        scratch_shapes=[pltpu.VMEM((2, TILE_GRANULES * GRANULE, d), F32), pltpu.SemaphoreType.DMA((2,))],
        compiler_params=pltpu.CompilerParams(dimension_semantics=("arbitrary",), vmem_limit_bytes=VMEM_LIMIT),
        name="moe_combine",
    )(rows, rows, x, slot, wts, mods, final_g.reshape(1, d), ys)


def _moe(x, hn, slot, wts, gran, counts, w_gate, w_up, w_down, layer, mods, mod_base, final_g, final_norm, n_ctx,
         t_dec):
    n = x.shape[0]
    n_rows = 2 * n + (n // TM_PROJ) * N_EXPERTS * GRANULE + N_EXPERTS * TM_MOE
    n_tiles = -(-n_rows // TM_MOE)
    rows, tails, tile_expert, tile_valid = _route_tables(gran, counts, n_tiles)
    xs = _dispatch(hn, slot, rows, tails, tile_valid, n_tiles * TM_MOE)
    ys = _experts(xs, tile_expert, tile_valid, w_gate, w_up, w_down, layer)
    return _combine(x, ys, slot, wts, rows, mods, mod_base, final_g, final_norm, n_ctx, t_dec)


def _grid_pos_embed(n_tokens):
    rows = n_tokens // GRID_W
    r = jnp.repeat(jnp.arange(rows, dtype=F32), GRID_W)
    col = jnp.tile(jnp.arange(GRID_W, dtype=F32), rows)
    quarter = D_MODEL // 4
    freq = jnp.exp(jnp.arange(quarter, dtype=F32) * (-math.log(POS_BASE) / quarter))

    def axis_embed(pos):
        a = pos[:, None] * freq[None, :]
        return jnp.concatenate([jnp.sin(a), jnp.cos(a)], axis=-1)

    return jnp.concatenate([axis_embed(r), axis_embed(col)], axis=-1)


def _split_in_weights(w_in):
    pm = 4 * N_HEADS * D_HEAD
    wg = w_in[:, pm:].reshape(-1, 2, 2, N_HEADS)
    return w_in[:, :pm], wg.transpose(0, 3, 1, 2).reshape(-1, 4 * N_HEADS)


def _head_params(first, second):
    return jnp.stack([first, second], axis=0).transpose(2, 0, 1).reshape(-1).astype(F32)


def _heads_per_step(t_len):
    return 4 if t_len <= 256 else 2


def kernel(x_prompt, x_sample, state_gdn_S, state_mlstm_C, state_mlstm_n, state_mlstm_m, c, c_ctx, ada_w, ada_b,
           norm1_g, norm2_g, gdn_w_in, gdn_conv_w, gdn_a_log, gdn_dt_bias, gdn_norm_g, gdn_w_out, mlstm_w_in,
           mlstm_gate_b, mlstm_norm_g, mlstm_w_out, router_w, router_bias, exp_w_gate, exp_w_up, exp_w_down,
           final_norm_g):
    bp, tp, d = x_prompt.shape
    bs, ts, _ = x_sample.shape
    n_ctx = bp * tp
    depth = ada_w.shape[0]
    assert n_ctx % ts == 0 and ts % TM_MOE == 0 and tp % TM_PROJ == 0 and bs + 1 <= N_MOD_ROWS

    pos = _grid_pos_embed(ts).astype(F32)
    x = jnp.concatenate([x_prompt.reshape(n_ctx, d), (x_sample + pos[None]).reshape(bs * ts, d)], axis=0)

    conds = jnp.concatenate([c_ctx[None, :], c, jnp.zeros((N_MOD_ROWS - 1 - bs, d), F32)], axis=0)
    mods = _ada_mods(conds, ada_w, ada_b).reshape(depth * N_MOD_ROWS * 6, 1, d)

    zeros_dh = jnp.zeros_like(gdn_a_log[0])
    ctx = dict(batch=bp, t_len=tp, row0=0, hb=_heads_per_step(tp), emit_state=True)
    dec = dict(batch=bs, t_len=ts, row0=n_ctx, hb=_heads_per_step(ts), emit_state=False)
    outs = {}
    for layer in range(depth):
        j = layer // 2
        mod_base = layer * N_MOD_ROWS * 6
        if layer % 2 == 0:
            w_main, w_gate = _split_in_weights(gdn_w_in[j])
            bias = _head_params(gdn_dt_bias[j], zeros_dh)
            mul = _head_params(-jnp.exp(gdn_a_log[j].astype(F32)), zeros_dh)
            main, grow = _inproj(x, mods, mod_base, norm1_g[layer], w_main, w_gate, bias, mul, "gdn", n_ctx, ts)
            o_ctx, s_new = _gdn_scan(main, grow, gdn_conv_w[j], gdn_norm_g[j], None, **ctx)
            (o_dec,) = _gdn_scan(main, grow, gdn_conv_w[j], gdn_norm_g[j], state_gdn_S[:, j].astype(F32), **dec)
            outs.setdefault("gdn", []).append(s_new)
            w_out = gdn_w_out[j]
        else:
            w_main, w_gate = _split_in_weights(mlstm_w_in[j])
            bias = _head_params(mlstm_gate_b[j, 0], mlstm_gate_b[j, 1])
            main, grow = _inproj(x, mods, mod_base, norm1_g[layer], w_main, w_gate, bias, jnp.zeros_like(bias),
                                 "mlstm", n_ctx, ts)
            o_ctx, c_new, n_new, m_new = _mlstm_scan(main, grow, mlstm_norm_g[j], None, **ctx)
            init = (state_mlstm_C[:, j].astype(F32),
                    state_mlstm_n[:, j].astype(F32).transpose(0, 2, 1, 3),
                    jnp.broadcast_to(state_mlstm_m[:, j].astype(F32).transpose(0, 2, 1)[..., None],
                                     (bs, N_HEADS, 2, D_HEAD)))
            (o_dec,) = _mlstm_scan(main, grow, mlstm_norm_g[j], init, **dec)
            outs.setdefault("mC", []).append(c_new)
            outs.setdefault("mn", []).append(n_new.transpose(0, 2, 1, 3))
            outs.setdefault("mm", []).append(m_new[..., 0].transpose(0, 2, 1))
            w_out = mlstm_w_out[j]
        x, hn, slot, wts, gran, counts = _outproj_route(x, o_ctx, o_dec, w_out, mods, mod_base, norm2_g[layer],
                                                        router_w, router_bias, n_ctx, ts)
        x = _moe(x, hn, slot, wts, gran, counts, exp_w_gate, exp_w_up, exp_w_down, layer, mods, mod_base,
                 final_norm_g, layer == depth - 1, n_ctx, ts)

    y_prompt = x[0].reshape(bp, tp, d)
    y_sample = x[1].reshape(bs, ts, d)
    return (y_prompt, y_sample, jnp.stack(outs["gdn"], axis=1), jnp.stack(outs["mC"], axis=1),
            jnp.stack(outs["mn"], axis=1), jnp.stack(outs["mm"], axis=1))
```

```python
import functools
import math

import jax
import jax.numpy as jnp
from jax import lax
from jax.experimental import pallas as pl
from jax.experimental.pallas import tpu as pltpu

F32 = jnp.float32
BF16 = jnp.bfloat16
HIGHEST = lax.Precision.HIGHEST

D_MODEL = 1024
N_HEADS = 8
D_HEAD = 128
CHUNK = 64
N_EXPERTS = 16
N_GROUPS = 4
EXP_PER_GROUP = 4
D_FF = 512
EPS = 1e-6
GRID_W = 64
POS_BASE = 10000.0
N_MOD_ROWS = 8
VMEM_LIMIT = 56 * 1024 * 1024

TM_PROJ = 256
TM_IN = 512
TM_OUT = 512
TM_MOE = 512
GRANULE = 8
TILE_GRANULES = 2 * TM_PROJ // GRANULE + N_EXPERTS
SCAN_CHAINS = 64


def _silu(x):
    return x * jax.nn.sigmoid(x)


def _softplus(x):
    return jnp.maximum(x, 0.0) + jnp.log(1.0 + jnp.exp(-jnp.abs(x)))


def _dot(a, b, precision=None):
    return jnp.dot(a, b, preferred_element_type=F32, precision=precision)


def _dot_nt(a, b, precision=None):
    return lax.dot_general(a, b, (((1,), (1,)), ((), ())), preferred_element_type=F32, precision=precision)


def _dot_tn(a, b, precision=None):
    return lax.dot_general(a, b, (((0,), (0,)), ((), ())), preferred_element_type=F32, precision=precision)


def _rms(x):
    return x * lax.rsqrt(jnp.mean(x * x, axis=-1, keepdims=True) + EPS)


def _ada_kernel(c_ref, w_ref, b_ref, o_ref):
    cs = _silu(c_ref[...]).astype(BF16)
    o_ref[0] = _dot(cs, w_ref[0].astype(BF16)) + b_ref[0]


def _ada_mods(conds, ada_w, ada_b):
    depth, d, n6 = ada_w.shape
    tn = 1536
    return pl.pallas_call(
        _ada_kernel,
        grid=(depth, n6 // tn),
        in_specs=[
            pl.BlockSpec((N_MOD_ROWS, d), lambda l, j: (0, 0)),
            pl.BlockSpec((1, d, tn), lambda l, j: (l, 0, j)),
            pl.BlockSpec((1, 1, tn), lambda l, j: (l, 0, j)),
        ],
        out_specs=pl.BlockSpec((1, N_MOD_ROWS, tn), lambda l, j: (l, 0, j)),
        out_shape=jax.ShapeDtypeStruct((depth, N_MOD_ROWS, n6), F32),
        compiler_params=pltpu.CompilerParams(dimension_semantics=("arbitrary", "arbitrary"),
                                             vmem_limit_bytes=VMEM_LIMIT),
        name="ada_mods",
    )(conds, ada_w, ada_b.reshape(depth, 1, n6))


def _split3(x):
    hi = x.astype(BF16)
    r1 = x - hi.astype(F32)
    mid = r1.astype(BF16)
    lo = (r1 - mid.astype(F32)).astype(BF16)
    return hi, mid, lo


def _inproj_kernel(x_ref, ng_ref, sc_ref, sh_ref, w_ref, wgt_ref, gb_ref, gm_ref, main_ref, gr_ref, *, kind):
    x = x_ref[...]
    hn = _rms(x) * ng_ref[...]
    hn = hn * (1.0 + sc_ref[0]) + sh_ref[0]
    hb = hn.astype(BF16)
    main_ref[...] = _dot(hb, w_ref[...])

    pre = _dot_nt(wgt_ref[...], hb) + gb_ref[...]
    row = lax.broadcasted_iota(jnp.int32, pre.shape, 0)
    first = (row % 4) < 2
    if kind == "gdn":
        act = jnp.where(first, gm_ref[...] * _softplus(pre), jax.nn.sigmoid(pre))
        scanned_kind = 0
    else:
        act = jnp.where(first, pre, -_softplus(-pre))
        scanned_kind = 1

    tm = x.shape[0]
    r = lax.broadcasted_iota(jnp.int32, (tm, tm), 0)
    c = lax.broadcasted_iota(jnp.int32, (tm, tm), 1)
    same = (r // CHUNK) == (c // CHUNK)
    before = jnp.where(same, jnp.where(r <= c, 1.0, 0.0), 0.0).astype(BF16)
    after = jnp.where(same, jnp.where(r >= c, 1.0, 0.0), 0.0).astype(BF16)
    parts = _split3(act)
    fwd = sum(_dot(p, before) for p in parts)
    bwd = sum(_dot(p, after) for p in parts)
    gt = jnp.where((row % 4) // 2 == scanned_kind, jnp.where(row % 2 == 0, fwd, bwd), act)
    for h in range(N_HEADS):
        for ch in range(tm // CHUNK):
            gr_ref[h, ch] = gt[4 * h:4 * h + 4, ch * CHUNK:(ch + 1) * CHUNK]


def _mod_row_of_tile(tm, n_ctx, t_dec):
    def f(i):
        r = i * tm
        return jnp.where(r < n_ctx, 0, 1 + (r - n_ctx) // t_dec)
    return f


def _inproj(x, mods, mod_base, norm_g, w_main, w_gate, gate_bias, gate_mul, kind, n_ctx, t_dec):
    n, d = x.shape
    tm = TM_IN
    pm = w_main.shape[1]
    ng = w_gate.shape[1]
    mrow = _mod_row_of_tile(tm, n_ctx, t_dec)
    sc_map = lambda i: (mod_base + mrow(i) * 6 + 1, 0, 0)
    sh_map = lambda i: (mod_base + mrow(i) * 6 + 0, 0, 0)
    const2 = lambda i: (0, 0)
    return pl.pallas_call(
        functools.partial(_inproj_kernel, kind=kind),
        grid=(n // tm,),
        in_specs=[
            pl.BlockSpec((tm, d), lambda i: (i, 0)),
            pl.BlockSpec((1, d), const2),
            pl.BlockSpec((1, 1, d), sc_map),
            pl.BlockSpec((1, 1, d), sh_map),
            pl.BlockSpec((d, pm), const2),
            pl.BlockSpec((ng, d), const2),
            pl.BlockSpec((ng, 1), const2),
            pl.BlockSpec((ng, 1), const2),
        ],
        out_specs=[
            pl.BlockSpec((tm, pm), lambda i: (i, 0)),
            pl.BlockSpec((N_HEADS, tm // CHUNK, 4, CHUNK), lambda i: (0, i, 0, 0)),
        ],
        out_shape=[
            jax.ShapeDtypeStruct((n, pm), F32),
            jax.ShapeDtypeStruct((N_HEADS, n // CHUNK, 4, CHUNK), F32),
        ],
        compiler_params=pltpu.CompilerParams(dimension_semantics=("arbitrary",), vmem_limit_bytes=VMEM_LIMIT),
        name="inproj_" + kind,
    )(x, norm_g.reshape(1, d), mods, mods, w_main.astype(BF16), w_gate.T.astype(BF16),
      gate_bias.reshape(ng, 1), gate_mul.reshape(ng, 1))


def _chunk_masks():
    r = lax.broadcasted_iota(jnp.int32, (CHUNK, CHUNK), 0)
    c = lax.broadcasted_iota(jnp.int32, (CHUNK, CHUNK), 1)
    return r >= c, r > c, r <= c, r < c


def _chunk_rows(c):
    if isinstance(c, int):
        return pl.ds(c * CHUNK, CHUNK)
    return pl.ds(pl.multiple_of(c * CHUNK, CHUNK), CHUNK)


def _head_cols(h):
    return slice(h * D_HEAD, (h + 1) * D_HEAD)


def _last_row(x, d):
    return x[CHUNK - 1:CHUNK, :] if d == 0 else x[0:1, :]


def _gate_columns(gr4):
    a = jnp.concatenate([gr4, jnp.zeros_like(gr4)], axis=0)
    hi = a.astype(BF16).astype(F32)
    mid = (a - hi).astype(BF16).astype(F32)
    lo = a - hi - mid
    parts = jnp.concatenate([hi, mid, lo, jnp.zeros_like(a)], axis=0).astype(BF16)
    j = lax.broadcasted_iota(jnp.int32, (32, 4 * D_HEAD), 0) % 8
    lane = lax.broadcasted_iota(jnp.int32, (32, 4 * D_HEAD), 1)
    return _dot_tn(parts, jnp.where(lane // D_HEAD == j, 1.0, 0.0).astype(BF16))


def _gate_column(cols, j):
    return cols[:, j * D_HEAD:(j + 1) * D_HEAD]


def _unit_tri_solve(a_list, rhs_list):
    pack = 4
    groups = [a_list[i:i + pack] for i in range(0, len(a_list), pack)]
    r = lax.broadcasted_iota(jnp.int32, (CHUNK, pack * CHUNK), 0)
    c = lax.broadcasted_iota(jnp.int32, (CHUNK, pack * CHUNK), 1) % CHUNK
    same = (r // 16) == (c // 16)
    eye = jnp.where(r == c, 1.0, 0.0)
    br = lax.broadcasted_iota(jnp.int32, (pack * CHUNK, pack * CHUNK), 0) // CHUNK
    bc = lax.broadcasted_iota(jnp.int32, (pack * CHUNK, pack * CHUNK), 1) // CHUNK
    zero16 = jnp.zeros((), BF16)

    def blockdiag(y):
        return jnp.where(br == bc, jnp.concatenate([y.astype(BF16)] * pack, axis=0), zero16)

    def mm(xs, ys):
        return [_dot(x.astype(BF16), blockdiag(y)) for x, y in zip(xs, ys)]

    ap = [jnp.concatenate(g, axis=1) for g in groups]
    d = [jnp.where(same, a, 0.0) for a in ap]
    t = [eye - di for di in d]
    for _ in range(3):
        d = mm(d, d)
        t = [ti + pi for ti, pi in zip(t, mm(t, d))]
    b = mm(t, [jnp.where(same, 0.0, a) for a in ap])
    b2 = mm(b, b)
    y = [ti - pi for ti, pi in zip(t, mm(b, t))]
    inv = [(yi + pi).astype(BF16) for yi, pi in zip(y, mm(b2, y))]

    def placed(rhs, j):
        z = jnp.zeros(rhs.shape, BF16)
        return jnp.concatenate([z] * j + [rhs.astype(BF16)] + [z] * (pack - 1 - j), axis=0)

    return [_dot(inv[i // pack], placed(rhs, i % pack)) for i, rhs in enumerate(rhs_list)]


def _for_chunk_groups(nc, group, fn):
    if nc == group:
        fn(list(range(nc)))
        return

    def body(g, carry):
        fn([g * group + j for j in range(group)])
        return carry
    lax.fori_loop(0, nc // group, body, 0)


def _scan_specs(t_len, row0, hb, n_proj):
    rb = row0 // t_len
    nc = t_len // CHUNK
    ngrp = N_HEADS // hb
    specs = [pl.BlockSpec((t_len, hb * D_HEAD), functools.partial(lambda b, g, j: (rb + b, j * ngrp + g), j=j))
             for j in range(n_proj)]
    specs.append(pl.BlockSpec((hb, nc, 4, CHUNK), lambda b, g: (g, rb + b, 0, 0)))
    return specs


def _scan_call(kern, name, args, in_specs, out_specs, out_shape, scratch, batch, hb):
    return pl.pallas_call(
        kern,
        grid=(batch, N_HEADS // hb),
        in_specs=in_specs,
        out_specs=out_specs,
        out_shape=out_shape,
        scratch_shapes=scratch,
        compiler_params=pltpu.CompilerParams(dimension_semantics=("arbitrary", "arbitrary"),
                                             vmem_limit_bytes=VMEM_LIMIT),
        name=name,
    )(*args)


def _gdn_kernel(*refs, t_len, hb, has_init, emit_state):
    it = iter(refs)
    q_ref, k_ref, v_ref, z_ref, gr_ref, cwq_ref, cwk_ref, cwv_ref, ng_ref = (next(it) for _ in range(9))
    s0_ref = next(it) if has_init else None
    o_ref = next(it)
    s_ref = next(it) if emit_state else None
    qs, ks, vs, oacc, qp_s, op_s, km_s, nm_s, ge_s = (next(it) for _ in range(9))
    nc = t_len // CHUNK

    rows = lax.broadcasted_iota(jnp.int32, (t_len, 1), 0)

    def conv_silu(x, cw):
        xm = jnp.where(rows == 0, 0.0, pltpu.roll(x, 1, axis=0))
        xp = jnp.where(rows == t_len - 1, 0.0, pltpu.roll(x, t_len - 1, axis=0))
        return _silu(xm * cw[0:1] + x * cw[1:2] + xp * cw[2:3])

    def l2n(x):
        return x * lax.rsqrt(jnp.sum(x * x, axis=-1, keepdims=True) + EPS)

    for h in range(hb):
        hc = _head_cols(h)
        qs[:, hc] = l2n(conv_silu(q_ref[:, hc], cwq_ref[:, hc])) * (D_HEAD ** -0.5)
        ks[:, hc] = l2n(conv_silu(k_ref[:, hc], cwk_ref[:, hc]))
        vs[:, hc] = conv_silu(v_ref[:, hc], cwv_ref[:, hc])
    oacc[...] = jnp.zeros_like(oacc)

    lo_i, lo_s, up_i, up_s = _chunk_masks()

    def slot(h, d, c):
        return (h * 2 + d) * nc + c

    def intra(chunks):
        items = [(h, c, d) for h in range(hb) for c in chunks for d in range(2)]
        qkk, cols = [], []
        for h in range(hb):
            for c in chunks:
                q16, k16 = qs[_chunk_rows(c), _head_cols(h)].astype(BF16), ks[_chunk_rows(c), _head_cols(h)].astype(BF16)
                qkk.append(_dot_nt(jnp.concatenate([q16, k16], axis=0), k16))
                cols.append(_gate_columns(gr_ref[h, c]))
        a_list, rhs_list, keep = [], [], []
        for n_item, (h, c, d) in enumerate(items):
            sl, hc = _chunk_rows(c), _head_cols(h)
            q, k, v = qs[sl, hc], ks[sl, hc], vs[sl, hc]
            gr4 = gr_ref[h, c]
            col4, qk, kk = cols[n_item // 2], qkk[n_item // 2][:CHUNK], qkk[n_item // 2][CHUNK:]
            g_col, beta = _gate_column(col4, d), _gate_column(col4, 2 + d)
            incl, strict = (lo_i, lo_s) if d == 0 else (up_i, up_s)
            decay = jnp.exp(jnp.where(incl, g_col[:, :CHUNK] - gr4[d:d + 1, :], -jnp.inf))
            a_list.append(jnp.where(strict, beta[:, :CHUNK] * kk * decay, 0.0))
            kb = k * beta
            e_g = jnp.exp(g_col)
            rhs_list.append(jnp.concatenate([v * beta, kb * e_g], axis=1))
            qk16 = jnp.where(incl, qk * decay, 0.0).astype(BF16)
            g_last = _last_row(g_col, d)
            ge_s[slot(h, d, c)] = jnp.broadcast_to(jnp.exp(g_last), (8, D_HEAD))
            keep.append((qk16, q * e_g, (k * jnp.exp(g_last - g_col)).astype(BF16)))
        sols = [s.astype(BF16) for s in _unit_tri_solve(a_list, rhs_list)]
        qw = [_dot(kp[0], s) for kp, s in zip(keep, sols)]
        kw = [_dot_tn(kp[2], s) for kp, s in zip(keep, sols)]
        for (h, c, d), kp, qwi, kwi in zip(items, keep, qw, kw):
            i = h * 2 + d
            sl = _chunk_rows(c)
            op_s[i, sl, :] = qwi[:, :D_HEAD]
            qp_s[i, sl, :] = (kp[1] - qwi[:, D_HEAD:]).astype(BF16)
            nm_s[slot(h, d, c)] = kwi[:, :D_HEAD]
            km_s[slot(h, d, c)] = (-kwi[:, D_HEAD:]).astype(BF16)

    _for_chunk_groups(nc, min(nc, max(1, SCAN_CHAINS // (2 * hb))), intra)

    chains = [(h, d) for h in range(hb) for d in range(2)]

    def inter_body(i, carry):
        cs = (i, nc - 1 - i)
        s16 = [s.astype(BF16) for s in carry]
        outs = [_dot(qp_s[h * 2 + d, _chunk_rows(cs[d]), :], s) for (h, d), s in zip(chains, s16)]
        upds = [_dot(km_s[slot(h, d, cs[d])], s) for (h, d), s in zip(chains, s16)]
        for (h, d), o in zip(chains, outs):
            oacc[_chunk_rows(cs[d]), _head_cols(h)] += o + op_s[h * 2 + d, _chunk_rows(cs[d]), :]
        return tuple(ge_s[slot(h, d, cs[d])][0:1, :] * s + u + nm_s[slot(h, d, cs[d])]
                     for (h, d), s, u in zip(chains, carry, upds))

    if has_init:
        init = tuple(s0_ref[0, d, h] for h, d in chains)
    else:
        init = tuple(jnp.zeros((D_HEAD, D_HEAD), F32) for _ in chains)
    fin = lax.fori_loop(0, nc, inter_body, init)
    if emit_state:
        for (h, d), s in zip(chains, fin):
            s_ref[0, d, h] = s
    for h in range(hb):
        hc = _head_cols(h)
        o_ref[:, hc] = (_rms(oacc[:, hc]) * ng_ref[...] * _silu(z_ref[:, hc])).astype(o_ref.dtype)


def _gdn_scan(main, grow, conv_w, norm_g, s0, *, batch, t_len, row0, hb, emit_state):
    nc = t_len // CHUNK
    ngrp = N_HEADS // hb
    has_init = s0 is not None
    in_specs = _scan_specs(t_len, row0, hb, 4)
    in_specs += [pl.BlockSpec((3, hb * D_HEAD), functools.partial(lambda b, g, j: (0, j * ngrp + g), j=j))
                 for j in range(3)]
    in_specs.append(pl.BlockSpec((1, D_HEAD), lambda b, g: (0, 0)))
    args = [main, main, main, main, grow, conv_w, conv_w, conv_w, norm_g.reshape(1, D_HEAD)]
    state_spec = pl.BlockSpec((1, 2, hb, D_HEAD, D_HEAD), lambda b, g: (b, 0, g, 0, 0))
    if has_init:
        in_specs.append(state_spec)
        args.append(s0)
    out_specs = [pl.BlockSpec((t_len, hb * D_HEAD), lambda b, g: (b, g))]
    out_shape = [jax.ShapeDtypeStruct((batch * t_len, N_HEADS * D_HEAD), BF16)]
    if emit_state:
        out_specs.append(state_spec)
        out_shape.append(jax.ShapeDtypeStruct((batch, 2, N_HEADS, D_HEAD, D_HEAD), F32))
    scratch = ([pltpu.VMEM((t_len, hb * D_HEAD), F32) for _ in range(4)]
               + [pltpu.VMEM((2 * hb, t_len, D_HEAD), BF16),
                  pltpu.VMEM((2 * hb, t_len, D_HEAD), F32),
                  pltpu.VMEM((2 * hb * nc, D_HEAD, D_HEAD), BF16),
                  pltpu.VMEM((2 * hb * nc, D_HEAD, D_HEAD), F32),
                  pltpu.VMEM((2 * hb * nc, 8, D_HEAD), F32)])
    kern = functools.partial(_gdn_kernel, t_len=t_len, hb=hb, has_init=has_init, emit_state=emit_state)
    return _scan_call(kern, "gdn_scan_t%d" % t_len, args, in_specs, out_specs, out_shape, scratch, batch, hb)


def _mlstm_kernel(*refs, t_len, hb, has_init, emit_state):
    it = iter(refs)
    q_ref, k_ref, v_ref, og_ref, gr_ref, ng_ref = (next(it) for _ in range(6))
    if has_init:
        c0_ref, n0_ref, m0_ref = (next(it) for _ in range(3))
    o_ref = next(it)
    if emit_state:
        c_ref, n_ref, m_ref = (next(it) for _ in range(3))
    hacc, q16_s, v1_s, qk_s, ld_s, bb_s, lm_s, lw_s, sc_s = (next(it) for _ in range(9))
    nc = t_len // CHUNK

    hacc[...] = jnp.zeros_like(hacc)
    ones = jnp.ones((t_len, D_HEAD), BF16)
    for h in range(hb):
        hc = _head_cols(h)
        q16_s[:, hc] = (q_ref[:, hc] * (D_HEAD ** -0.5)).astype(BF16)
        v1_s[:, h * 2 * D_HEAD:(h * 2 + 1) * D_HEAD] = v_ref[:, hc].astype(BF16)
        v1_s[:, (h * 2 + 1) * D_HEAD:(h * 2 + 2) * D_HEAD] = ones
    lo_i, _, up_i, _ = _chunk_masks()

    def slot(h, d, c):
        return (h * 2 + d) * nc + c

    def intra(chunks):
        for h in range(hb):
            for c in chunks:
                sl, hc = _chunk_rows(c), _head_cols(h)
                qk_s[h, sl, :] = _dot_nt(q16_s[sl, hc], k_ref[sl, hc].astype(BF16))
                gr4 = gr_ref[h, c]
                col4 = _gate_columns(gr4)
                for d in range(2):
                    i = h * 2 + d
                    b_col, i_col = _gate_column(col4, 2 + d), _gate_column(col4, d)
                    b_last = _last_row(b_col, d)
                    lwe = b_last - b_col + i_col
                    log_d = jnp.where(lo_i if d == 0 else up_i,
                                      b_col[:, :CHUNK] - gr4[2 + d:3 + d, :] + gr4[d:d + 1, :], -jnp.inf)
                    ld_s[i, sl, :] = log_d
                    lm_s[i, sl, :] = jnp.broadcast_to(jnp.max(log_d, axis=-1, keepdims=True), (CHUNK, D_HEAD))
                    bb_s[i, sl, :] = b_col
                    lw_s[i, sl, :] = lwe
                    sc_s[slot(h, d, c), 0:1, :] = b_last
                    sc_s[slot(h, d, c), 1:2, :] = jnp.max(lwe, axis=0, keepdims=True)

    _for_chunk_groups(nc, min(nc, 4), intra)

    chains = [(h, d) for h in range(hb) for d in range(2)]

    def body(step, carry):
        cs = (step, nc - 1 - step)
        state, nrow, m = zip(*carry)
        sls = [_chunk_rows(cs[d]) for h, d in chains]
        idx = [h * 2 + d for h, d in chains]
        v1 = [v1_s[sl, h * 2 * D_HEAD:(h * 2 + 2) * D_HEAD] for (h, d), sl in zip(chains, sls)]
        log_last = [sc_s[slot(h, d, cs[d]), 0:1, :] + mi for (h, d), mi in zip(chains, m)]
        m_new = [jnp.maximum(ll, sc_s[slot(h, d, cs[d]), 1:2, :]) for (h, d), ll in zip(chains, log_last)]
        dec = [jnp.exp(ll - mn) for ll, mn in zip(log_last, m_new)]
        kw = [k_ref[sl, _head_cols(h)] * jnp.exp(lw_s[i, sl, :] - mn)
              for (h, d), sl, i, mn in zip(chains, sls, idx, m_new)]
        upd = [_dot_tn(a.astype(BF16), b) for a, b in zip(kw, v1)]
        qc = [_dot(q16_s[sl, _head_cols(h)], s.astype(BF16)) for (h, d), sl, s in zip(chains, sls, state)]
        log_inter = [bb_s[i, sl, :] + mi for i, sl, mi in zip(idx, sls, m)]
        mt = [jnp.maximum(li, lm_s[i, sl, :]) for li, i, sl in zip(log_inter, idx, sls)]
        s_inter = [jnp.exp(li - t) for li, t in zip(log_inter, mt)]
        p = [(jnp.exp(ld_s[i, sl, :] - t[:, :CHUNK]) * qk_s[h, sl, :]).astype(BF16)
             for (h, d), i, sl, t in zip(chains, idx, sls, mt)]
        pv = [_dot(a, b) for a, b in zip(p, v1)]
        for (h, d), sl, si, qci, pvi, t in zip(chains, sls, s_inter, qc, pv, mt):
            num = si * qci[:, :D_HEAD] + pvi[:, :D_HEAD]
            den = si * qci[:, D_HEAD:] + pvi[:, D_HEAD:]
            hacc[sl, _head_cols(h)] += num / jnp.maximum(jnp.abs(den), jnp.exp(-t))
        return tuple((jnp.concatenate([dc, dc], axis=1) * s + u, dc * nr + jnp.sum(kwi, axis=0, keepdims=True), mn)
                     for dc, s, u, nr, kwi, mn in zip(dec, state, upd, nrow, kw, m_new))

    if has_init:
        init = tuple((jnp.concatenate([c0_ref[0, d, h],
                                       jnp.transpose(jnp.broadcast_to(n0_ref[0, h, d:d + 1, :], (D_HEAD, D_HEAD)))],
                                      axis=1),
                      n0_ref[0, h, d:d + 1, :], m0_ref[0, h, d:d + 1, :]) for h, d in chains)
    else:
        init = tuple((jnp.zeros((D_HEAD, 2 * D_HEAD), F32), jnp.zeros((1, D_HEAD), F32), jnp.zeros((1, D_HEAD), F32))
                     for _ in chains)
    fin = lax.fori_loop(0, nc, body, init)
    if emit_state:
        for (h, d), (s, nr, m) in zip(chains, fin):
            c_ref[0, d, h] = s[:, :D_HEAD]
            n_ref[0, h, d:d + 1, :] = nr
            m_ref[0, h, d:d + 1, :] = m
    for h in range(hb):
        hc = _head_cols(h)
        o_ref[:, hc] = (_rms(hacc[:, hc]) * ng_ref[...] * jax.nn.sigmoid(og_ref[:, hc])).astype(o_ref.dtype)


def _mlstm_scan(main, grow, norm_g, init, *, batch, t_len, row0, hb, emit_state):
    nc = t_len // CHUNK
    has_init = init is not None
    in_specs = _scan_specs(t_len, row0, hb, 4)
    in_specs.append(pl.BlockSpec((1, D_HEAD), lambda b, g: (0, 0)))
    args = [main, main, main, main, grow, norm_g.reshape(1, D_HEAD)]
    c_spec = pl.BlockSpec((1, 2, hb, D_HEAD, D_HEAD), lambda b, g: (b, 0, g, 0, 0))
    v_spec = pl.BlockSpec((1, hb, 2, D_HEAD), lambda b, g: (b, g, 0, 0))
    if has_init:
        in_specs += [c_spec, v_spec, v_spec]
        args += list(init)
    out_specs = [pl.BlockSpec((t_len, hb * D_HEAD), lambda b, g: (b, g))]
    out_shape = [jax.ShapeDtypeStruct((batch * t_len, N_HEADS * D_HEAD), BF16)]
    if emit_state:
        out_specs += [c_spec, v_spec, v_spec]
        out_shape += [jax.ShapeDtypeStruct((batch, 2, N_HEADS, D_HEAD, D_HEAD), F32),
                      jax.ShapeDtypeStruct((batch, N_HEADS, 2, D_HEAD), F32),
                      jax.ShapeDtypeStruct((batch, N_HEADS, 2, D_HEAD), F32)]
    scratch = [pltpu.VMEM((t_len, hb * D_HEAD), F32),
               pltpu.VMEM((t_len, hb * D_HEAD), BF16),
               pltpu.VMEM((t_len, hb * 2 * D_HEAD), BF16),
               pltpu.VMEM((hb, t_len, CHUNK), F32),
               pltpu.VMEM((2 * hb, t_len, CHUNK), F32),
               pltpu.VMEM((2 * hb, t_len, D_HEAD), F32),
               pltpu.VMEM((2 * hb, t_len, D_HEAD), F32),
               pltpu.VMEM((2 * hb, t_len, D_HEAD), F32),
               pltpu.VMEM((2 * hb * nc, 8, D_HEAD), F32)]
    kern = functools.partial(_mlstm_kernel, t_len=t_len, hb=hb, has_init=has_init, emit_state=emit_state)
    return _scan_call(kern, "mlstm_scan_t%d" % t_len, args, in_specs, out_specs, out_shape, scratch, batch, hb)


def _top2_of4(v):
    best, i1 = v[0], jnp.zeros(v[0].shape, jnp.int32)
    for j in range(1, 4):
        take = v[j] > best
        i1 = jnp.where(take, j, i1)
        best = jnp.where(take, v[j], best)
    best2, i2 = None, None
    for j in range(4):
        vj = jnp.where(i1 == j, -jnp.inf, v[j])
        if best2 is None:
            best2, i2 = vj, jnp.zeros(v[0].shape, jnp.int32)
        else:
            take = vj > best2
            i2 = jnp.where(take, j, i2)
            best2 = jnp.where(take, vj, best2)
    return i1, i2


def _pick(rows, idx):
    out = rows[0]
    for j in range(1, len(rows)):
        out = jnp.where(idx == j, rows[j], out)
    return out


def _outproj_kernel(x_ref, oc_ref, od_ref, w_ref, g1_ref, ng_ref, sc_ref, sh_ref, rw_ref, rb_ref, xo_ref, hn_ref,
                    slot_ref, wt_ref, gran_ref, cnt_ref, run_ref, *, ctx_tiles):
    @pl.when(pl.program_id(0) == 0)
    def _():
        run_ref[...] = jnp.zeros_like(run_ref)

    o = jnp.where(pl.program_id(0) < ctx_tiles, oc_ref[...], od_ref[...])
    xn = x_ref[...] + g1_ref[0] * _dot(o, w_ref[...])
    xo_ref[...] = xn
    hn = _rms(xn) * ng_ref[...]
    hn = hn * (1.0 + sc_ref[0]) + sh_ref[0]
    hn_ref[...] = hn.astype(BF16)

    tm = TM_PROJ
    tiles = range(hn.shape[0] // tm)
    lanes = [slice(j * tm, (j + 1) * tm) for j in tiles]
    logits = [_dot_nt(rw_ref[...], hn[ln, :], HIGHEST) for ln in lanes]

    def top2(lg):
        ex = jnp.exp(lg - jnp.max(lg, axis=0, keepdims=True))
        probs = ex / jnp.sum(ex, axis=0, keepdims=True)
        sel = probs + rb_ref[...]
        sel_rows = [sel[e:e + 1, :] for e in range(N_EXPERTS)]
        prob_rows = [probs[e:e + 1, :] for e in range(N_EXPERTS)]
        scores = []
        for g in range(N_GROUPS):
            r = sel_rows[4 * g:4 * g + 4]
            a, b = jnp.maximum(r[0], r[1]), jnp.minimum(r[0], r[1])
            c, d = jnp.maximum(r[2], r[3]), jnp.minimum(r[2], r[3])
            scores.append(jnp.maximum(a, c) + jnp.maximum(jnp.minimum(a, c), jnp.maximum(b, d)))
        best, grp = scores[0], jnp.zeros(scores[0].shape, jnp.int32)
        for g in range(1, N_GROUPS):
            take = scores[g] > best
            grp = jnp.where(take, g, grp)
            best = jnp.where(take, scores[g], best)
        sel_in = [_pick([sel_rows[4 * g + j] for g in range(N_GROUPS)], grp) for j in range(4)]
        prob_in = [_pick([prob_rows[4 * g + j] for g in range(N_GROUPS)], grp) for j in range(4)]
        i1, i2 = _top2_of4(sel_in)
        w1, w2 = _pick(prob_in, i1), _pick(prob_in, i2)
        tot = w1 + w2
        return grp * 4 + i1, grp * 4 + i2, w1 / tot, w2 / tot

    picks = [top2(lg) for lg in logits]
    eidx = lax.broadcasted_iota(jnp.int32, (N_EXPERTS, tm), 0)
    onehot = [jnp.where(eidx == e1, 1.0, 0.0) + jnp.where(eidx == e2, 1.0, 0.0) for e1, e2, _, _ in picks]
    r = lax.broadcasted_iota(jnp.int32, (tm, tm), 0)
    c = lax.broadcasted_iota(jnp.int32, (tm, tm), 1)
    earlier = jnp.where(r < c, 1.0, 0.0).astype(BF16)
    ahead = [_dot(oh.astype(BF16), earlier) for oh in onehot]
    n_gran = [jnp.broadcast_to(jnp.floor((jnp.sum(oh, axis=1, keepdims=True) + (GRANULE - 1)) * (1.0 / GRANULE)),
                               (N_EXPERTS, D_HEAD)) for oh in onehot]
    er = lax.broadcasted_iota(jnp.int32, (N_EXPERTS, N_EXPERTS), 0)
    ec = lax.broadcasted_iota(jnp.int32, (N_EXPERTS, N_EXPERTS), 1)
    before = jnp.where(ec < er, 1.0, 0.0).astype(BF16)
    first = [_dot(before, ng.astype(BF16)) for ng in n_gran]
    g = lax.broadcasted_iota(jnp.int32, (N_EXPERTS, D_HEAD), 1).astype(F32)
    ge = lax.broadcasted_iota(jnp.int32, (N_EXPERTS, D_HEAD), 0).astype(F32)
    run = run_ref[...]
    for j in tiles:
        e1, e2, w1, w2 = picks[j]
        wt_ref[:, lanes[j]] = jnp.concatenate([w1, w2], axis=0)
        row0 = GRANULE * first[j][:, 0:1] + ahead[j]
        slot = [jnp.sum(jnp.where(eidx == e, row0, 0.0), axis=0, keepdims=True) for e in (e1, e2)]
        slot_ref[:, lanes[j]] = jnp.concatenate(slot, axis=0).astype(jnp.int32)
        owner = jnp.sum(jnp.where(g >= first[j] + n_gran[j], 1.0, 0.0), axis=0, keepdims=True)
        index = g[0:1, :] + jnp.sum(jnp.where(ge == owner, run - first[j], 0.0), axis=0, keepdims=True)
        gran_ref[j] = jnp.concatenate([owner, index], axis=0).astype(jnp.int32)
        run = run + n_gran[j]
    run_ref[...] = run
    cnt_ref[...] = run


def _outproj_route(x, o_ctx, o_dec, w_out, mods, mod_base, norm_g, router_w, router_bias, n_ctx, t_dec):
    n, d = x.shape
    tm = TM_OUT
    ctx_tiles = n_ctx // tm
    mrow = _mod_row_of_tile(tm, n_ctx, t_dec)
    mod_map = lambda k: (lambda i: (mod_base + mrow(i) * 6 + k, 0, 0))
    const2 = lambda i: (0, 0)
    return pl.pallas_call(
        functools.partial(_outproj_kernel, ctx_tiles=ctx_tiles),
        grid=(n // tm,),
        in_specs=[
            pl.BlockSpec((tm, d), lambda i: (i, 0)),
            pl.BlockSpec((tm, d), lambda i: (jnp.minimum(i, ctx_tiles - 1), 0)),
            pl.BlockSpec((tm, d), lambda i: (jnp.maximum(i - ctx_tiles, 0), 0)),
            pl.BlockSpec((d, d), const2),
            pl.BlockSpec((1, 1, d), mod_map(2)),
            pl.BlockSpec((1, d), const2),
            pl.BlockSpec((1, 1, d), mod_map(4)),
            pl.BlockSpec((1, 1, d), mod_map(3)),
            pl.BlockSpec((N_EXPERTS, d), const2),
            pl.BlockSpec((N_EXPERTS, 1), const2),
        ],
        out_specs=[
            pl.BlockSpec((tm, d), lambda i: (i, 0)),
            pl.BlockSpec((tm, d), lambda i: (i, 0)),
            pl.BlockSpec((2, tm), lambda i: (0, i)),
            pl.BlockSpec((2, tm), lambda i: (0, i)),
            pl.BlockSpec((tm // TM_PROJ, 2, D_HEAD), lambda i: (i, 0, 0)),
            pl.BlockSpec((N_EXPERTS, D_HEAD), const2),
        ],
        out_shape=[
            jax.ShapeDtypeStruct((n, d), F32),
            jax.ShapeDtypeStruct((n, d), BF16),
            jax.ShapeDtypeStruct((2, n), jnp.int32),
            jax.ShapeDtypeStruct((2, n), F32),
            jax.ShapeDtypeStruct((n // TM_PROJ, 2, D_HEAD), jnp.int32),
            jax.ShapeDtypeStruct((N_EXPERTS, D_HEAD), F32),
        ],
        scratch_shapes=[pltpu.VMEM((N_EXPERTS, D_HEAD), F32)],
        compiler_params=pltpu.CompilerParams(dimension_semantics=("arbitrary",), vmem_limit_bytes=VMEM_LIMIT),
        name="outproj_route",
    )(x, o_ctx, o_dec, w_out.astype(BF16), mods, norm_g.reshape(1, d), mods, mods, router_w.T, router_bias.reshape(N_EXPERTS, 1))


def _route_tables(gran, counts, n_tiles):
    tile_gran = TM_MOE // GRANULE
    cnt = counts[:, 0].astype(jnp.int32)
    padded = (cnt + tile_gran - 1) // tile_gran * tile_gran
    ends = jnp.cumsum(padded)
    offs = ends - padded
    owner, index = gran[:, 0, :], gran[:, 1, :]
    first_gran = index + sum(jnp.where(owner == e, offs[e], 0) for e in range(N_EXPERTS))
    rows = jnp.where(owner < N_EXPERTS, first_gran * GRANULE, -1).astype(jnp.int32)
    tile_g0 = jnp.arange(n_tiles, dtype=jnp.int32) * tile_gran
    tile_expert = jnp.minimum(jnp.sum(tile_g0[:, None] >= ends[None, :], axis=1), N_EXPERTS - 1)
    tile_valid = jnp.sum((tile_g0[:, None] >= offs[None, :]) & (tile_g0[:, None] < (offs + cnt)[None, :]), axis=1)
    j = jnp.arange(tile_gran, dtype=jnp.int32)[None, :]
    tails = jnp.where(j < (padded - cnt)[:, None], (offs + cnt)[:, None] + j, -1)
    tails = jnp.where(tails >= 0, tails * GRANULE, -1).astype(jnp.int32).reshape(1, 1, N_EXPERTS * tile_gran)
    return rows, tails, tile_expert.astype(jnp.int32), tile_valid.astype(jnp.int32)


def _granule_copy(src, src_row, dst, dst_row, sem):
    return pltpu.make_async_copy(src.at[pl.ds(pl.multiple_of(src_row, GRANULE), GRANULE)],
                                 dst.at[pl.ds(pl.multiple_of(dst_row, GRANULE), GRANULE)], sem)


def _for_granules(rows_ref, fn, count=TILE_GRANULES):
    def body(g, carry):
        row = rows_ref[0, 0, g]

        @pl.when(row >= 0)
        def _():
            fn(g, row)
        return carry
    lax.fori_loop(0, count, body, 0)


def _pair_matrix(slot, first_value, second_value):
    p = lax.broadcasted_iota(jnp.int32, (TILE_GRANULES * GRANULE, slot.shape[1]), 0)
    return jnp.where(p == slot[0:1, :], first_value, 0.0) + jnp.where(p == slot[1:2, :], second_value, 0.0)


def _dispatch_kernel(rows_ref, prev_ref, tail_ref, valid_ref, hn_ref, slot_ref, xs_ref, buf, zero, sems):
    i = pl.program_id(0)
    cur = i % 2
    buf[cur] = _dot(_pair_matrix(slot_ref[...], 1.0, 1.0).astype(BF16), hn_ref[...])

    def copies(table_ref, b, act):
        _for_granules(table_ref, lambda g, row: act(_granule_copy(buf.at[b], g * GRANULE, xs_ref, row, sems.at[b])))

    copies(rows_ref, cur, lambda c: c.start())

    @pl.when(i > 0)
    def _():
        copies(prev_ref, 1 - cur, lambda c: c.wait())

    @pl.when(i == pl.num_programs(0) - 1)
    def _():
        copies(rows_ref, cur, lambda c: c.wait())
        zero[...] = jnp.zeros_like(zero)
        n_tail = tail_ref.shape[2]
        _for_granules(tail_ref, lambda g, row: _granule_copy(zero, 0, xs_ref, row, sems.at[0]).start(), n_tail)
        _for_granules(tail_ref, lambda g, row: _granule_copy(zero, 0, xs_ref, row, sems.at[0]).wait(), n_tail)

        def fill_tiles(act):
            def body(t, carry):
                @pl.when(valid_ref[0, 0, t] == 0)
                def _():
                    act(pltpu.make_async_copy(zero, xs_ref.at[pl.ds(pl.multiple_of(t * TM_MOE, TM_MOE), TM_MOE)],
                                              sems.at[1]))
                return carry
            lax.fori_loop(0, valid_ref.shape[2], body, 0)

        fill_tiles(lambda c: c.start())
        fill_tiles(lambda c: c.wait())


def _dispatch(hn, slot, rows, tails, tile_valid, n_rows):
    n, d = hn.shape
    tm = TM_PROJ
    rows = rows.reshape(n // tm, 1, D_HEAD)
    table = lambda index_map: pl.BlockSpec((1, 1, D_HEAD), index_map, memory_space=pltpu.SMEM)
    return pl.pallas_call(
        _dispatch_kernel,
        grid=(n // tm,),
        in_specs=[
            table(lambda i: (i, 0, 0)),
            table(lambda i: (jnp.maximum(i - 1, 0), 0, 0)),
            pl.BlockSpec(memory_space=pltpu.SMEM),
            pl.BlockSpec(memory_space=pltpu.SMEM),
            pl.BlockSpec((tm, d), lambda i: (i, 0)),
            pl.BlockSpec((2, tm), lambda i: (0, i)),
        ],
        out_specs=pl.BlockSpec(memory_space=pl.ANY),
        out_shape=jax.ShapeDtypeStruct((n_rows, d), F32),
        scratch_shapes=[pltpu.VMEM((2, TILE_GRANULES * GRANULE, d), F32), pltpu.VMEM((TM_MOE, d), F32),
                        pltpu.SemaphoreType.DMA((2,))],
        compiler_params=pltpu.CompilerParams(dimension_semantics=("arbitrary",), vmem_limit_bytes=VMEM_LIMIT),
        name="moe_dispatch",
    )(rows, rows, tails, tile_valid.reshape(1, 1, -1), hn, slot)


def _expert_kernel(te_ref, tv_ref, xs_ref, wg_ref, wu_ref, wd_ref, ys_ref, wg16, wu16, wd16):
    i = pl.program_id(0)
    fresh = jnp.logical_or(i == 0, te_ref[i] != te_ref[jnp.maximum(i - 1, 0)])

    @pl.when(fresh)
    def _():
        wg16[...] = wg_ref[0, 0].astype(BF16)
        wu16[...] = wu_ref[0, 0].astype(BF16)
        wd16[...] = wd_ref[0, 0].astype(BF16)

    @pl.when(tv_ref[i] != 0)
    def _():
        x = xs_ref[...].astype(BF16)
        hid = _silu(_dot(x, wg16[...])) * _dot(x, wu16[...])
        ys_ref[...] = _dot(hid.astype(BF16), wd16[...])

    @pl.when(tv_ref[i] == 0)
    def _():
        ys_ref[...] = jnp.zeros_like(ys_ref)


def _experts(xs, tile_expert, tile_valid, w_gate, w_up, w_down, layer):
    n_rows, d = xs.shape
    tm = TM_MOE
    w_in_spec = pl.BlockSpec((1, 1, d, D_FF), lambda i, te, tv: (layer, te[i], 0, 0))
    return pl.pallas_call(
        _expert_kernel,
        grid_spec=pltpu.PrefetchScalarGridSpec(
            num_scalar_prefetch=2,
            grid=(n_rows // tm,),
            in_specs=[
                pl.BlockSpec((tm, d), lambda i, te, tv: (jnp.where(tv[i] != 0, i, 0), 0)),
                w_in_spec,
                w_in_spec,
                pl.BlockSpec((1, 1, D_FF, d), lambda i, te, tv: (layer, te[i], 0, 0)),
            ],
            out_specs=pl.BlockSpec((tm, d), lambda i, te, tv: (i, 0)),
            scratch_shapes=[pltpu.VMEM((d, D_FF), BF16), pltpu.VMEM((d, D_FF), BF16), pltpu.VMEM((D_FF, d), BF16)],
        ),
        out_shape=jax.ShapeDtypeStruct((n_rows, d), F32),
        compiler_params=pltpu.CompilerParams(dimension_semantics=("arbitrary",), vmem_limit_bytes=VMEM_LIMIT),
        name="moe_experts",
    )(tile_expert, tile_valid, xs, w_gate, w_up, w_down)


def _combine_kernel(rows_ref, next_ref, x_ref, slot_ref, wt_ref, g2_ref, fg_ref, ys_ref, *rest, ctx_tiles):
    *o_refs, buf, sems = rest
    i = pl.program_id(0)
    cur = i % 2

    def copies(table_ref, b, act):
        _for_granules(table_ref, lambda g, row: act(_granule_copy(ys_ref, row, buf.at[b], g * GRANULE, sems.at[b])))

    @pl.when(i == 0)
    def _():
        buf[...] = jnp.zeros_like(buf)
        copies(rows_ref, 0, lambda c: c.start())

    @pl.when(i + 1 < pl.num_programs(0))
    def _():
        copies(next_ref, 1 - cur, lambda c: c.start())

    copies(rows_ref, cur, lambda c: c.wait())
    w = wt_ref[...]
    mix = _pair_matrix(slot_ref[...], w[0:1, :], w[1:2, :]).astype(BF16)
    xn = x_ref[...] + g2_ref[0] * _dot_tn(mix, buf[cur].astype(BF16))
    if ctx_tiles is None:
        o_refs[0][...] = xn
        return
    xn = _rms(xn) * fg_ref[...]

    @pl.when(pl.program_id(0) < ctx_tiles)
    def _():
        o_refs[0][...] = xn

    @pl.when(pl.program_id(0) >= ctx_tiles)
    def _():
        o_refs[1][...] = xn


def _combine(x, ys, slot, wts, rows, mods, mod_base, final_g, final_norm, n_ctx, t_dec):
    n, d = x.shape
    tm = TM_PROJ
    mrow = _mod_row_of_tile(tm, n_ctx, t_dec)
    ctx_tiles = n_ctx // tm if final_norm else None
    if final_norm:
        out_specs = [pl.BlockSpec((tm, d), lambda i: (jnp.minimum(i, ctx_tiles - 1), 0)),
                     pl.BlockSpec((tm, d), lambda i: (jnp.maximum(i - ctx_tiles, 0), 0))]
        out_shape = [jax.ShapeDtypeStruct((n_ctx, d), F32), jax.ShapeDtypeStruct((n - n_ctx, d), F32)]
    else:
        out_specs = pl.BlockSpec((tm, d), lambda i: (i, 0))
        out_shape = jax.ShapeDtypeStruct((n, d), F32)
    rows = rows.reshape(n // tm, 1, D_HEAD)
    last = n // tm - 1
    return pl.pallas_call(
        functools.partial(_combine_kernel, ctx_tiles=ctx_tiles),
        grid=(n // tm,),
        in_specs=[
            pl.BlockSpec((1, 1, D_HEAD), lambda i: (i, 0, 0), memory_space=pltpu.SMEM),
            pl.BlockSpec((1, 1, D_HEAD), lambda i: (jnp.minimum(i + 1, last), 0, 0), memory_space=pltpu.SMEM),
            pl.BlockSpec((tm, d), lambda i: (i, 0)),
            pl.BlockSpec((2, tm), lambda i: (0, i)),
            pl.BlockSpec((2, tm), lambda i: (0, i)),
            pl.BlockSpec((1, 1, d), lambda i: (mod_base + mrow(i) * 6 + 5, 0, 0)),
            pl.BlockSpec((1, d), lambda i: (0, 0)),
            pl.BlockSpec(memory_space=pl.ANY),
        ],
        out_specs=out_specs,
        out_shape=out_shape,
        scratch_shapes=[pltpu.VMEM((2, TILE_GRANULES * GRANULE, d), F32), pltpu.SemaphoreType.DMA((2,))],
        compiler_params=pltpu.CompilerParams(dimension_semantics=("arbitrary",), vmem_limit_bytes=VMEM_LIMIT),
        name="moe_combine",
    )(rows, rows, x, slot, wts, mods, final_g.reshape(1, d), ys)


def _moe(x, hn, slot, wts, gran, counts, w_gate, w_up, w_down, layer, mods, mod_base, final_g, final_norm, n_ctx,
         t_dec):
    n = x.shape[0]
    n_rows = 2 * n + (n // TM_PROJ) * N_EXPERTS * GRANULE + N_EXPERTS * TM_MOE
    n_tiles = -(-n_rows // TM_MOE)
    rows, tails, tile_expert, tile_valid = _route_tables(gran, counts, n_tiles)
    xs = _dispatch(hn, slot, rows, tails, tile_valid, n_tiles * TM_MOE)
    ys = _experts(xs, tile_expert, tile_valid, w_gate, w_up, w_down, layer)
    return _combine(x, ys, slot, wts, rows, mods, mod_base, final_g, final_norm, n_ctx, t_dec)


def _grid_pos_embed(n_tokens):
    rows = n_tokens // GRID_W
    r = jnp.repeat(jnp.arange(rows, dtype=F32), GRID_W)
    col = jnp.tile(jnp.arange(GRID_W, dtype=F32), rows)
    quarter = D_MODEL // 4
    freq = jnp.exp(jnp.arange(quarter, dtype=F32) * (-math.log(POS_BASE) / quarter))

    def axis_embed(pos):
        a = pos[:, None] * freq[None, :]
        return jnp.concatenate([jnp.sin(a), jnp.cos(a)], axis=-1)

    return jnp.concatenate([axis_embed(r), axis_embed(col)], axis=-1)


def _split_in_weights(w_in):
    pm = 4 * N_HEADS * D_HEAD
    wg = w_in[:, pm:].reshape(-1, 2, 2, N_HEADS)
    return w_in[:, :pm], wg.transpose(0, 3, 1, 2).reshape(-1, 4 * N_HEADS)


def _head_params(first, second):
    return jnp.stack([first, second], axis=0).transpose(2, 0, 1).reshape(-1).astype(F32)


def _heads_per_step(t_len, chains_per_head_chunk=2):
    return max(2, min(N_HEADS, SCAN_CHAINS // (chains_per_head_chunk * (t_len // CHUNK))))


def kernel(x_prompt, x_sample, state_gdn_S, state_mlstm_C, state_mlstm_n, state_mlstm_m, c, c_ctx, ada_w, ada_b,
           norm1_g, norm2_g, gdn_w_in, gdn_conv_w, gdn_a_log, gdn_dt_bias, gdn_norm_g, gdn_w_out, mlstm_w_in,
           mlstm_gate_b, mlstm_norm_g, mlstm_w_out, router_w, router_bias, exp_w_gate, exp_w_up, exp_w_down,
           final_norm_g):
    bp, tp, d = x_prompt.shape
    bs, ts, _ = x_sample.shape
    n_ctx = bp * tp
    depth = ada_w.shape[0]
    assert n_ctx % ts == 0 and ts % max(TM_IN, TM_OUT) == 0 and tp % TM_PROJ == 0 and bs + 1 <= N_MOD_ROWS
    assert TILE_GRANULES <= D_HEAD

    pos = _grid_pos_embed(ts).astype(F32)
    x = jnp.concatenate([x_prompt.reshape(n_ctx, d), (x_sample + pos[None]).reshape(bs * ts, d)], axis=0)

    conds = jnp.concatenate([c_ctx[None, :], c, jnp.zeros((N_MOD_ROWS - 1 - bs, d), F32)], axis=0)
    mods = _ada_mods(conds, ada_w, ada_b).reshape(depth * N_MOD_ROWS * 6, 1, d)

    zeros_dh = jnp.zeros_like(gdn_a_log[0])
    ctx = dict(batch=bp, t_len=tp, row0=0, emit_state=True)
    dec = dict(batch=bs, t_len=ts, row0=n_ctx, emit_state=False)
    gdn_hb = dict(ctx=_heads_per_step(tp), dec=_heads_per_step(ts))
    mlstm_hb = dict(ctx=_heads_per_step(tp, 4), dec=_heads_per_step(ts, 4))
    outs = {}
    for layer in range(depth):
        j = layer // 2
        mod_base = layer * N_MOD_ROWS * 6
        if layer % 2 == 0:
            w_main, w_gate = _split_in_weights(gdn_w_in[j])
            bias = _head_params(gdn_dt_bias[j], zeros_dh)
            mul = _head_params(-jnp.exp(gdn_a_log[j].astype(F32)), zeros_dh)
            main, grow = _inproj(x, mods, mod_base, norm1_g[layer], w_main, w_gate, bias, mul, "gdn", n_ctx, ts)
            o_ctx, s_new = _gdn_scan(main, grow, gdn_conv_w[j], gdn_norm_g[j], None, hb=gdn_hb["ctx"], **ctx)
            (o_dec,) = _gdn_scan(main, grow, gdn_conv_w[j], gdn_norm_g[j], state_gdn_S[:, j].astype(F32),
                                 hb=gdn_hb["dec"], **dec)
            outs.setdefault("gdn", []).append(s_new)
            w_out = gdn_w_out[j]
        else:
            w_main, w_gate = _split_in_weights(mlstm_w_in[j])
            bias = _head_params(mlstm_gate_b[j, 0], mlstm_gate_b[j, 1])
            main, grow = _inproj(x, mods, mod_base, norm1_g[layer], w_main, w_gate, bias, jnp.zeros_like(bias),
                                 "mlstm", n_ctx, ts)
            o_ctx, c_new, n_new, m_new = _mlstm_scan(main, grow, mlstm_norm_g[j], None, hb=mlstm_hb["ctx"], **ctx)
            init = (state_mlstm_C[:, j].astype(F32),
                    state_mlstm_n[:, j].astype(F32).transpose(0, 2, 1, 3),
                    jnp.broadcast_to(state_mlstm_m[:, j].astype(F32).transpose(0, 2, 1)[..., None],
                                     (bs, N_HEADS, 2, D_HEAD)))
            (o_dec,) = _mlstm_scan(main, grow, mlstm_norm_g[j], init, hb=mlstm_hb["dec"], **dec)
            outs.setdefault("mC", []).append(c_new)
            outs.setdefault("mn", []).append(n_new.transpose(0, 2, 1, 3))
            outs.setdefault("mm", []).append(m_new[..., 0].transpose(0, 2, 1))
            w_out = mlstm_w_out[j]
        x, hn, slot, wts, gran, counts = _outproj_route(x, o_ctx, o_dec, w_out, mods, mod_base, norm2_g[layer],
                                                        router_w, router_bias, n_ctx, ts)
        x = _moe(x, hn, slot, wts, gran, counts, exp_w_gate, exp_w_up, exp_w_down, layer, mods, mod_base,
                 final_norm_g, layer == depth - 1, n_ctx, ts)

    y_prompt = x[0].reshape(bp, tp, d)
    y_sample = x[1].reshape(bs, ts, d)
    return (y_prompt, y_sample, jnp.stack(outs["gdn"], axis=1), jnp.stack(outs["mC"], axis=1),
            jnp.stack(outs["mn"], axis=1), jnp.stack(outs["mm"], axis=1))
```

```python
import functools
import math

import jax
import jax.numpy as jnp
from jax import lax
from jax.experimental import pallas as pl
from jax.experimental.pallas import tpu as pltpu

F32 = jnp.float32
BF16 = jnp.bfloat16
HIGHEST = lax.Precision.HIGHEST

D_MODEL = 1024
N_HEADS = 8
D_HEAD = 128
CHUNK = 64
N_EXPERTS = 16
N_GROUPS = 4
EXP_PER_GROUP = 4
D_FF = 512
EPS = 1e-6
GRID_W = 64
POS_BASE = 10000.0
N_MOD_ROWS = 8
VMEM_LIMIT = 56 * 1024 * 1024

TM_PROJ = 256
TM_IN = 512
TM_OUT = 512
TM_MOE = 512
GRANULE = 8
TILE_GRANULES = 2 * TM_PROJ // GRANULE + N_EXPERTS
SCAN_CHAINS = 64


def _silu(x):
    return x * jax.nn.sigmoid(x)


def _softplus(x):
    return jnp.maximum(x, 0.0) + jnp.log(1.0 + jnp.exp(-jnp.abs(x)))


def _dot(a, b, precision=None):
    return jnp.dot(a, b, preferred_element_type=F32, precision=precision)


def _dot_nt(a, b, precision=None):
    return lax.dot_general(a, b, (((1,), (1,)), ((), ())), preferred_element_type=F32, precision=precision)


def _dot_tn(a, b, precision=None):
    return lax.dot_general(a, b, (((0,), (0,)), ((), ())), preferred_element_type=F32, precision=precision)


def _rms(x):
    return x * lax.rsqrt(jnp.mean(x * x, axis=-1, keepdims=True) + EPS)


def _ada_kernel(c_ref, w_ref, b_ref, o_ref):
    cs = _silu(c_ref[...]).astype(BF16)
    o_ref[0] = _dot(cs, w_ref[0].astype(BF16)) + b_ref[0]


def _ada_mods(conds, ada_w, ada_b):
    depth, d, n6 = ada_w.shape
    tn = 1536
    return pl.pallas_call(
        _ada_kernel,
        grid=(depth, n6 // tn),
        in_specs=[
            pl.BlockSpec((N_MOD_ROWS, d), lambda l, j: (0, 0)),
            pl.BlockSpec((1, d, tn), lambda l, j: (l, 0, j)),
            pl.BlockSpec((1, 1, tn), lambda l, j: (l, 0, j)),
        ],
        out_specs=pl.BlockSpec((1, N_MOD_ROWS, tn), lambda l, j: (l, 0, j)),
        out_shape=jax.ShapeDtypeStruct((depth, N_MOD_ROWS, n6), F32),
        compiler_params=pltpu.CompilerParams(dimension_semantics=("arbitrary", "arbitrary"),
                                             vmem_limit_bytes=VMEM_LIMIT),
        name="ada_mods",
    )(conds, ada_w, ada_b.reshape(depth, 1, n6))


def _split3(x):
    hi = x.astype(BF16)
    r1 = x - hi.astype(F32)
    mid = r1.astype(BF16)
    lo = (r1 - mid.astype(F32)).astype(BF16)
    return hi, mid, lo


def _token_specs(x, tm, n_ctx):
    if not isinstance(x, tuple):
        return [pl.BlockSpec((tm, x.shape[1]), lambda i, *_: (i, 0))], [x], None
    xc, xd, pos = x
    d = xc.shape[1]
    ctx_tiles, pos_tiles = n_ctx // tm, pos.shape[0] // tm
    specs = [pl.BlockSpec((tm, d), lambda i, *_: (jnp.minimum(i, ctx_tiles - 1), 0)),
             pl.BlockSpec((tm, d), lambda i, *_: (jnp.maximum(i - ctx_tiles, 0), 0)),
             pl.BlockSpec((tm, d), lambda i, *_: (jnp.maximum(i - ctx_tiles, 0) % pos_tiles, 0))]
    return specs, [xc, xd, pos], ctx_tiles


def _token_tile(refs, ctx_tiles):
    if ctx_tiles is None:
        return refs[0][...], refs[1:]
    xc_ref, xd_ref, pos_ref = refs[:3]
    return jnp.where(pl.program_id(0) < ctx_tiles, xc_ref[...], xd_ref[...] + pos_ref[...]), refs[3:]


def _inproj_kernel(*refs, kind, x_ctx_tiles):
    x, (ng_ref, sc_ref, sh_ref, w_ref, wgt_ref, gb_ref, gm_ref, main_ref, gr_ref) = _token_tile(refs, x_ctx_tiles)
    hn = _rms(x) * ng_ref[...]
    hn = hn * (1.0 + sc_ref[0]) + sh_ref[0]
    hb = hn.astype(BF16)
    main_ref[...] = _dot(hb, w_ref[...])

    pre = _dot_nt(wgt_ref[...], hb) + gb_ref[...]
    row = lax.broadcasted_iota(jnp.int32, pre.shape, 0)
    first = (row % 4) < 2
    if kind == "gdn":
        act = jnp.where(first, gm_ref[...] * _softplus(pre), jax.nn.sigmoid(pre))
        scanned_kind = 0
    else:
        act = jnp.where(first, pre, -_softplus(-pre))
        scanned_kind = 1

    tm = x.shape[0]
    r = lax.broadcasted_iota(jnp.int32, (tm, tm), 0)
    c = lax.broadcasted_iota(jnp.int32, (tm, tm), 1)
    same = (r // CHUNK) == (c // CHUNK)
    before = jnp.where(same, jnp.where(r <= c, 1.0, 0.0), 0.0).astype(BF16)
    after = jnp.where(same, jnp.where(r >= c, 1.0, 0.0), 0.0).astype(BF16)
    parts = _split3(act)
    fwd = sum(_dot(p, before) for p in parts)
    bwd = sum(_dot(p, after) for p in parts)
    gt = jnp.where((row % 4) // 2 == scanned_kind, jnp.where(row % 2 == 0, fwd, bwd), act)
    for h in range(N_HEADS):
        for ch in range(tm // CHUNK):
            gr_ref[h, ch] = gt[4 * h:4 * h + 4, ch * CHUNK:(ch + 1) * CHUNK]


def _mod_row_of_tile(tm, n_ctx, t_dec):
    def f(i):
        r = i * tm
        return jnp.where(r < n_ctx, 0, 1 + (r - n_ctx) // t_dec)
    return f


def _inproj(x, n, mods, mod_base, norm_g, w_main, w_gate, gate_bias, gate_mul, kind, n_ctx, t_dec):
    d = w_main.shape[0]
    tm = TM_IN
    pm = w_main.shape[1]
    ng = w_gate.shape[1]
    mrow = _mod_row_of_tile(tm, n_ctx, t_dec)
    sc_map = lambda i: (mod_base + mrow(i) * 6 + 1, 0, 0)
    sh_map = lambda i: (mod_base + mrow(i) * 6 + 0, 0, 0)
    const2 = lambda i: (0, 0)
    x_specs, x_args, x_ctx_tiles = _token_specs(x, tm, n_ctx)
    return pl.pallas_call(
        functools.partial(_inproj_kernel, kind=kind, x_ctx_tiles=x_ctx_tiles),
        grid=(n // tm,),
        in_specs=x_specs + [
            pl.BlockSpec((1, d), const2),
            pl.BlockSpec((1, 1, d), sc_map),
            pl.BlockSpec((1, 1, d), sh_map),
            pl.BlockSpec((d, pm), const2),
            pl.BlockSpec((ng, d), const2),
            pl.BlockSpec((ng, 1), const2),
            pl.BlockSpec((ng, 1), const2),
        ],
        out_specs=[
            pl.BlockSpec((tm, pm), lambda i: (i, 0)),
            pl.BlockSpec((N_HEADS, tm // CHUNK, 4, CHUNK), lambda i: (0, i, 0, 0)),
        ],
        out_shape=[
            jax.ShapeDtypeStruct((n, pm), F32),
            jax.ShapeDtypeStruct((N_HEADS, n // CHUNK, 4, CHUNK), F32),
        ],
        compiler_params=pltpu.CompilerParams(dimension_semantics=("arbitrary",), vmem_limit_bytes=VMEM_LIMIT),
        name="inproj_" + kind,
    )(*x_args, norm_g.reshape(1, d), mods, mods, w_main.astype(BF16), w_gate.T.astype(BF16),
      gate_bias.reshape(ng, 1), gate_mul.reshape(ng, 1))


def _chunk_masks():
    r = lax.broadcasted_iota(jnp.int32, (CHUNK, CHUNK), 0)
    c = lax.broadcasted_iota(jnp.int32, (CHUNK, CHUNK), 1)
    return r >= c, r > c, r <= c, r < c


def _chunk_rows(c):
    if isinstance(c, int):
        return pl.ds(c * CHUNK, CHUNK)
    return pl.ds(pl.multiple_of(c * CHUNK, CHUNK), CHUNK)


def _head_cols(h):
    return slice(h * D_HEAD, (h + 1) * D_HEAD)


def _last_row(x, d):
    return x[CHUNK - 1:CHUNK, :] if d == 0 else x[0:1, :]


def _gate_columns(gr4):
    a = jnp.concatenate([gr4, jnp.zeros_like(gr4)], axis=0)
    hi = a.astype(BF16).astype(F32)
    mid = (a - hi).astype(BF16).astype(F32)
    lo = a - hi - mid
    parts = jnp.concatenate([hi, mid, lo, jnp.zeros_like(a)], axis=0).astype(BF16)
    j = lax.broadcasted_iota(jnp.int32, (32, 4 * D_HEAD), 0) % 8
    lane = lax.broadcasted_iota(jnp.int32, (32, 4 * D_HEAD), 1)
    return _dot_tn(parts, jnp.where(lane // D_HEAD == j, 1.0, 0.0).astype(BF16))


def _gate_column(cols, j):
    return cols[:, j * D_HEAD:(j + 1) * D_HEAD]


def _unit_tri_solve(a_list, rhs_list):
    pack = 4
    groups = [a_list[i:i + pack] for i in range(0, len(a_list), pack)]
    r = lax.broadcasted_iota(jnp.int32, (CHUNK, pack * CHUNK), 0)
    c = lax.broadcasted_iota(jnp.int32, (CHUNK, pack * CHUNK), 1) % CHUNK
    same = (r // 16) == (c // 16)
    eye = jnp.where(r == c, 1.0, 0.0)
    br = lax.broadcasted_iota(jnp.int32, (pack * CHUNK, pack * CHUNK), 0) // CHUNK
    bc = lax.broadcasted_iota(jnp.int32, (pack * CHUNK, pack * CHUNK), 1) // CHUNK
    zero16 = jnp.zeros((), BF16)

    def blockdiag(y):
        return jnp.where(br == bc, jnp.concatenate([y.astype(BF16)] * pack, axis=0), zero16)

    def mm(xs, ys):
        return [_dot(x.astype(BF16), blockdiag(y)) for x, y in zip(xs, ys)]

    ap = [jnp.concatenate(g, axis=1) for g in groups]
    d = [jnp.where(same, a, 0.0) for a in ap]
    t = [eye - di for di in d]
    for _ in range(3):
        d = mm(d, d)
        t = [ti + pi for ti, pi in zip(t, mm(t, d))]
    b = mm(t, [jnp.where(same, 0.0, a) for a in ap])
    b2 = mm(b, b)
    y = [ti - pi for ti, pi in zip(t, mm(b, t))]
    inv = [(yi + pi).astype(BF16) for yi, pi in zip(y, mm(b2, y))]

    def placed(rhs, j):
        z = jnp.zeros(rhs.shape, BF16)
        return jnp.concatenate([z] * j + [rhs.astype(BF16)] + [z] * (pack - 1 - j), axis=0)

    return [_dot(inv[i // pack], placed(rhs, i % pack)) for i, rhs in enumerate(rhs_list)]


def _for_chunk_groups(nc, group, fn):
    if nc == group:
        fn(list(range(nc)))
        return

    def body(g, carry):
        fn([g * group + j for j in range(group)])
        return carry
    lax.fori_loop(0, nc // group, body, 0)


def _scan_specs(t_len, row0, hb, n_proj):
    rb = row0 // t_len
    nc = t_len // CHUNK
    ngrp = N_HEADS // hb
    specs = [pl.BlockSpec((t_len, hb * D_HEAD), functools.partial(lambda b, g, j: (rb + b, j * ngrp + g), j=j))
             for j in range(n_proj)]
    specs.append(pl.BlockSpec((hb, nc, 4, CHUNK), lambda b, g: (g, rb + b, 0, 0)))
    return specs


def _scan_call(kern, name, args, in_specs, out_specs, out_shape, scratch, batch, hb):
    return pl.pallas_call(
        kern,
        grid=(batch, N_HEADS // hb),
        in_specs=in_specs,
        out_specs=out_specs,
        out_shape=out_shape,
        scratch_shapes=scratch,
        compiler_params=pltpu.CompilerParams(dimension_semantics=("arbitrary", "arbitrary"),
                                             vmem_limit_bytes=VMEM_LIMIT),
        name=name,
    )(*args)


def _gdn_kernel(*refs, t_len, hb, has_init, emit_state):
    it = iter(refs)
    q_ref, k_ref, v_ref, z_ref, gr_ref, cwq_ref, cwk_ref, cwv_ref, ng_ref = (next(it) for _ in range(9))
    s0_ref = next(it) if has_init else None
    o_ref = next(it)
    s_ref = next(it) if emit_state else None
    qs, ks, vs, oacc, qp_s, op_s, km_s, nm_s, ge_s, st_s = (next(it) for _ in range(10))
    nc = t_len // CHUNK

    rows = lax.broadcasted_iota(jnp.int32, (t_len, 1), 0)

    def conv_silu(x, cw):
        xm = jnp.where(rows == 0, 0.0, pltpu.roll(x, 1, axis=0))
        xp = jnp.where(rows == t_len - 1, 0.0, pltpu.roll(x, t_len - 1, axis=0))
        return _silu(xm * cw[0:1] + x * cw[1:2] + xp * cw[2:3])

    def l2n(x):
        return x * lax.rsqrt(jnp.sum(x * x, axis=-1, keepdims=True) + EPS)

    for h in range(hb):
        hc = _head_cols(h)
        qs[:, hc] = l2n(conv_silu(q_ref[:, hc], cwq_ref[:, hc])) * (D_HEAD ** -0.5)
        ks[:, hc] = l2n(conv_silu(k_ref[:, hc], cwk_ref[:, hc]))
        vs[:, hc] = conv_silu(v_ref[:, hc], cwv_ref[:, hc])
    oacc[...] = jnp.zeros_like(oacc)

    lo_i, lo_s, up_i, up_s = _chunk_masks()

    def slot(h, d, c):
        return (h * 2 + d) * nc + c

    def intra(chunks):
        items = [(h, c, d) for h in range(hb) for c in chunks for d in range(2)]
        qkk, cols = [], []
        for h in range(hb):
            for c in chunks:
                q16, k16 = qs[_chunk_rows(c), _head_cols(h)].astype(BF16), ks[_chunk_rows(c), _head_cols(h)].astype(BF16)
                qkk.append(_dot_nt(jnp.concatenate([q16, k16], axis=0), k16))
                cols.append(_gate_columns(gr_ref[h, c]))
        a_list, rhs_list, keep = [], [], []
        for n_item, (h, c, d) in enumerate(items):
            sl, hc = _chunk_rows(c), _head_cols(h)
            q, k, v = qs[sl, hc], ks[sl, hc], vs[sl, hc]
            gr4 = gr_ref[h, c]
            col4, qk, kk = cols[n_item // 2], qkk[n_item // 2][:CHUNK], qkk[n_item // 2][CHUNK:]
            g_col, beta = _gate_column(col4, d), _gate_column(col4, 2 + d)
            incl, strict = (lo_i, lo_s) if d == 0 else (up_i, up_s)
            decay = jnp.exp(jnp.where(incl, g_col[:, :CHUNK] - gr4[d:d + 1, :], -jnp.inf))
            a_list.append(jnp.where(strict, beta[:, :CHUNK] * kk * decay, 0.0))
            kb = k * beta
            e_g = jnp.exp(g_col)
            rhs_list.append(jnp.concatenate([v * beta, kb * e_g], axis=1))
            qk16 = jnp.where(incl, qk * decay, 0.0).astype(BF16)
            g_last = _last_row(g_col, d)
            ge_s[slot(h, d, c)] = jnp.broadcast_to(jnp.exp(g_last), (8, D_HEAD))
            keep.append((qk16, q * e_g, (k * jnp.exp(g_last - g_col)).astype(BF16)))
        sols = [s.astype(BF16) for s in _unit_tri_solve(a_list, rhs_list)]
        qw = [_dot(kp[0], s) for kp, s in zip(keep, sols)]
        kw = [_dot_tn(kp[2], s) for kp, s in zip(keep, sols)]
        for (h, c, d), kp, qwi, kwi in zip(items, keep, qw, kw):
            i = h * 2 + d
            sl = _chunk_rows(c)
            op_s[i, sl, :] = qwi[:, :D_HEAD]
            qp_s[i, sl, :] = (kp[1] - qwi[:, D_HEAD:]).astype(BF16)
            nm_s[slot(h, d, c)] = kwi[:, :D_HEAD]
            km_s[slot(h, d, c)] = (-kwi[:, D_HEAD:]).astype(BF16)

    _for_chunk_groups(nc, min(nc, max(1, SCAN_CHAINS // (2 * hb))), intra)

    chains = [(h, d) for h in range(hb) for d in range(2)]

    def inter_body(i, carry):
        cs = (i, nc - 1 - i)
        s16 = [st_s[h * 2 + d].astype(BF16) for h, d in chains]
        outs = [_dot(qp_s[h * 2 + d, _chunk_rows(cs[d]), :], s) for (h, d), s in zip(chains, s16)]
        upds = [_dot(km_s[slot(h, d, cs[d])], s) for (h, d), s in zip(chains, s16)]
        for (h, d), o in zip(chains, outs):
            oacc[_chunk_rows(cs[d]), _head_cols(h)] += o + op_s[h * 2 + d, _chunk_rows(cs[d]), :]
        for (h, d), u in zip(chains, upds):
            st_s[h * 2 + d] = ge_s[slot(h, d, cs[d])][0:1, :] * st_s[h * 2 + d] + u + nm_s[slot(h, d, cs[d])]
        return carry

    for h, d in chains:
        st_s[h * 2 + d] = s0_ref[0, d, h] if has_init else jnp.zeros((D_HEAD, D_HEAD), F32)
    lax.fori_loop(0, nc, inter_body, 0)
    if emit_state:
        for h, d in chains:
            s_ref[0, d, h] = st_s[h * 2 + d]
    for h in range(hb):
        hc = _head_cols(h)
        o_ref[:, hc] = (_rms(oacc[:, hc]) * ng_ref[...] * _silu(z_ref[:, hc])).astype(o_ref.dtype)


def _gdn_scan(main, grow, conv_w, norm_g, s0, *, batch, t_len, row0, hb, emit_state):
    nc = t_len // CHUNK
    ngrp = N_HEADS // hb
    has_init = s0 is not None
    in_specs = _scan_specs(t_len, row0, hb, 4)
    in_specs += [pl.BlockSpec((3, hb * D_HEAD), functools.partial(lambda b, g, j: (0, j * ngrp + g), j=j))
                 for j in range(3)]
    in_specs.append(pl.BlockSpec((1, D_HEAD), lambda b, g: (0, 0)))
    args = [main, main, main, main, grow, conv_w, conv_w, conv_w, norm_g.reshape(1, D_HEAD)]
    state_spec = pl.BlockSpec((1, 2, hb, D_HEAD, D_HEAD), lambda b, g: (b, 0, g, 0, 0))
    if has_init:
        in_specs.append(state_spec)
        args.append(s0)
    out_specs = [pl.BlockSpec((t_len, hb * D_HEAD), lambda b, g: (b, g))]
    out_shape = [jax.ShapeDtypeStruct((batch * t_len, N_HEADS * D_HEAD), BF16)]
    if emit_state:
        out_specs.append(state_spec)
        out_shape.append(jax.ShapeDtypeStruct((batch, 2, N_HEADS, D_HEAD, D_HEAD), F32))
    scratch = ([pltpu.VMEM((t_len, hb * D_HEAD), F32) for _ in range(4)]
               + [pltpu.VMEM((2 * hb, t_len, D_HEAD), BF16),
                  pltpu.VMEM((2 * hb, t_len, D_HEAD), F32),
                  pltpu.VMEM((2 * hb * nc, D_HEAD, D_HEAD), BF16),
                  pltpu.VMEM((2 * hb * nc, D_HEAD, D_HEAD), F32),
                  pltpu.VMEM((2 * hb * nc, 8, D_HEAD), F32),
                  pltpu.VMEM((2 * hb, D_HEAD, D_HEAD), F32)])
    kern = functools.partial(_gdn_kernel, t_len=t_len, hb=hb, has_init=has_init, emit_state=emit_state)
    return _scan_call(kern, "gdn_scan_t%d" % t_len, args, in_specs, out_specs, out_shape, scratch, batch, hb)


def _mlstm_kernel(*refs, t_len, hb, has_init, emit_state):
    it = iter(refs)
    q_ref, k_ref, v_ref, og_ref, gr_ref, ng_ref = (next(it) for _ in range(6))
    if has_init:
        c0_ref, n0_ref, m0_ref = (next(it) for _ in range(3))
    o_ref = next(it)
    if emit_state:
        c_ref, n_ref, m_ref = (next(it) for _ in range(3))
    hacc, q16_s, v1_s, qk_s, ld_s, bb_s, lm_s, lw_s, sc_s, st_s = (next(it) for _ in range(10))
    nc = t_len // CHUNK

    hacc[...] = jnp.zeros_like(hacc)
    ones = jnp.ones((t_len, D_HEAD), BF16)
    for h in range(hb):
        hc = _head_cols(h)
        q16_s[:, hc] = (q_ref[:, hc] * (D_HEAD ** -0.5)).astype(BF16)
        v1_s[:, h * 2 * D_HEAD:(h * 2 + 1) * D_HEAD] = v_ref[:, hc].astype(BF16)
        v1_s[:, (h * 2 + 1) * D_HEAD:(h * 2 + 2) * D_HEAD] = ones
    lo_i, _, up_i, _ = _chunk_masks()

    def slot(h, d, c):
        return (h * 2 + d) * nc + c

    def intra(chunks):
        for h in range(hb):
            for c in chunks:
                sl, hc = _chunk_rows(c), _head_cols(h)
                qk_s[h, sl, :] = _dot_nt(q16_s[sl, hc], k_ref[sl, hc].astype(BF16))
                gr4 = gr_ref[h, c]
                col4 = _gate_columns(gr4)
                for d in range(2):
                    i = h * 2 + d
                    b_col, i_col = _gate_column(col4, 2 + d), _gate_column(col4, d)
                    b_last = _last_row(b_col, d)
                    lwe = b_last - b_col + i_col
                    log_d = jnp.where(lo_i if d == 0 else up_i,
                                      b_col[:, :CHUNK] - gr4[2 + d:3 + d, :] + gr4[d:d + 1, :], -jnp.inf)
                    ld_s[i, sl, :] = log_d
                    lm_s[i, sl, :] = jnp.broadcast_to(jnp.max(log_d, axis=-1, keepdims=True), (CHUNK, D_HEAD))
                    bb_s[i, sl, :] = b_col
                    lw_s[i, sl, :] = lwe
                    sc_s[slot(h, d, c), 0:1, :] = b_last
                    sc_s[slot(h, d, c), 1:2, :] = jnp.max(lwe, axis=0, keepdims=True)

    _for_chunk_groups(nc, min(nc, 4), intra)

    chains = [(h, d) for h in range(hb) for d in range(2)]

    def body(step, carry):
        cs = (step, nc - 1 - step)
        nrow, m = zip(*carry)
        sls = [_chunk_rows(cs[d]) for h, d in chains]
        idx = [h * 2 + d for h, d in chains]
        v1 = [v1_s[sl, h * 2 * D_HEAD:(h * 2 + 2) * D_HEAD] for (h, d), sl in zip(chains, sls)]
        log_last = [sc_s[slot(h, d, cs[d]), 0:1, :] + mi for (h, d), mi in zip(chains, m)]
        m_new = [jnp.maximum(ll, sc_s[slot(h, d, cs[d]), 1:2, :]) for (h, d), ll in zip(chains, log_last)]
        dec = [jnp.exp(ll - mn) for ll, mn in zip(log_last, m_new)]
        kw = [k_ref[sl, _head_cols(h)] * jnp.exp(lw_s[i, sl, :] - mn)
              for (h, d), sl, i, mn in zip(chains, sls, idx, m_new)]
        upd = [_dot_tn(a.astype(BF16), b) for a, b in zip(kw, v1)]
        qc = [_dot(q16_s[sl, _head_cols(h)], st_s[i].astype(BF16)) for (h, d), sl, i in zip(chains, sls, idx)]
        log_inter = [bb_s[i, sl, :] + mi for i, sl, mi in zip(idx, sls, m)]
        mt = [jnp.maximum(li, lm_s[i, sl, :]) for li, i, sl in zip(log_inter, idx, sls)]
        s_inter = [jnp.exp(li - t) for li, t in zip(log_inter, mt)]
        p = [(jnp.exp(ld_s[i, sl, :] - t[:, :CHUNK]) * qk_s[h, sl, :]).astype(BF16)
             for (h, d), i, sl, t in zip(chains, idx, sls, mt)]
        pv = [_dot(a, b) for a, b in zip(p, v1)]
        for (h, d), sl, si, qci, pvi, t in zip(chains, sls, s_inter, qc, pv, mt):
            num = si * qci[:, :D_HEAD] + pvi[:, :D_HEAD]
            den = si * qci[:, D_HEAD:] + pvi[:, D_HEAD:]
            hacc[sl, _head_cols(h)] += num / jnp.maximum(jnp.abs(den), jnp.exp(-t))
        for i, dc, u in zip(idx, dec, upd):
            st_s[i] = jnp.concatenate([dc, dc], axis=1) * st_s[i] + u
        return tuple((dc * nr + jnp.sum(kwi, axis=0, keepdims=True), mn)
                     for dc, nr, kwi, mn in zip(dec, nrow, kw, m_new))

    for h, d in chains:
        if has_init:
            st_s[h * 2 + d, :, :D_HEAD] = c0_ref[0, d, h]
            st_s[h * 2 + d, :, D_HEAD:] = jnp.transpose(jnp.broadcast_to(n0_ref[0, h, d:d + 1, :], (D_HEAD, D_HEAD)))
        else:
            st_s[h * 2 + d] = jnp.zeros((D_HEAD, 2 * D_HEAD), F32)
    if has_init:
        init = tuple((n0_ref[0, h, d:d + 1, :], m0_ref[0, h, d:d + 1, :]) for h, d in chains)
    else:
        init = tuple((jnp.zeros((1, D_HEAD), F32), jnp.zeros((1, D_HEAD), F32)) for _ in chains)
    fin = lax.fori_loop(0, nc, body, init)
    if emit_state:
        for (h, d), (nr, m) in zip(chains, fin):
            c_ref[0, d, h] = st_s[h * 2 + d, :, :D_HEAD]
            n_ref[0, h, d:d + 1, :] = nr
            m_ref[0, h, d:d + 1, :] = m
    for h in range(hb):
        hc = _head_cols(h)
        o_ref[:, hc] = (_rms(hacc[:, hc]) * ng_ref[...] * jax.nn.sigmoid(og_ref[:, hc])).astype(o_ref.dtype)


def _mlstm_scan(main, grow, norm_g, init, *, batch, t_len, row0, hb, emit_state):
    nc = t_len // CHUNK
    has_init = init is not None
    in_specs = _scan_specs(t_len, row0, hb, 4)
    in_specs.append(pl.BlockSpec((1, D_HEAD), lambda b, g: (0, 0)))
    args = [main, main, main, main, grow, norm_g.reshape(1, D_HEAD)]
    c_spec = pl.BlockSpec((1, 2, hb, D_HEAD, D_HEAD), lambda b, g: (b, 0, g, 0, 0))
    v_spec = pl.BlockSpec((1, hb, 2, D_HEAD), lambda b, g: (b, g, 0, 0))
    if has_init:
        in_specs += [c_spec, v_spec, v_spec]
        args += list(init)
    out_specs = [pl.BlockSpec((t_len, hb * D_HEAD), lambda b, g: (b, g))]
    out_shape = [jax.ShapeDtypeStruct((batch * t_len, N_HEADS * D_HEAD), BF16)]
    if emit_state:
        out_specs += [c_spec, v_spec, v_spec]
        out_shape += [jax.ShapeDtypeStruct((batch, 2, N_HEADS, D_HEAD, D_HEAD), F32),
                      jax.ShapeDtypeStruct((batch, N_HEADS, 2, D_HEAD), F32),
                      jax.ShapeDtypeStruct((batch, N_HEADS, 2, D_HEAD), F32)]
    scratch = [pltpu.VMEM((t_len, hb * D_HEAD), F32),
               pltpu.VMEM((t_len, hb * D_HEAD), BF16),
               pltpu.VMEM((t_len, hb * 2 * D_HEAD), BF16),
               pltpu.VMEM((hb, t_len, CHUNK), F32),
               pltpu.VMEM((2 * hb, t_len, CHUNK), F32),
               pltpu.VMEM((2 * hb, t_len, D_HEAD), F32),
               pltpu.VMEM((2 * hb, t_len, D_HEAD), F32),
               pltpu.VMEM((2 * hb, t_len, D_HEAD), F32),
               pltpu.VMEM((2 * hb * nc, 8, D_HEAD), F32),
               pltpu.VMEM((2 * hb, D_HEAD, 2 * D_HEAD), F32)]
    kern = functools.partial(_mlstm_kernel, t_len=t_len, hb=hb, has_init=has_init, emit_state=emit_state)
    return _scan_call(kern, "mlstm_scan_t%d" % t_len, args, in_specs, out_specs, out_shape, scratch, batch, hb)


def _top2_of4(v):
    best, i1 = v[0], jnp.zeros(v[0].shape, jnp.int32)
    for j in range(1, 4):
        take = v[j] > best
        i1 = jnp.where(take, j, i1)
        best = jnp.where(take, v[j], best)
    best2, i2 = None, None
    for j in range(4):
        vj = jnp.where(i1 == j, -jnp.inf, v[j])
        if best2 is None:
            best2, i2 = vj, jnp.zeros(v[0].shape, jnp.int32)
        else:
            take = vj > best2
            i2 = jnp.where(take, j, i2)
            best2 = jnp.where(take, vj, best2)
    return i1, i2


def _pick(rows, idx):
    out = rows[0]
    for j in range(1, len(rows)):
        out = jnp.where(idx == j, rows[j], out)
    return out


def _outproj_kernel(*refs, ctx_tiles, x_ctx_tiles):
    x, (oc_ref, od_ref, w_ref, g1_ref, ng_ref, sc_ref, sh_ref, rw_ref, rb_ref, xo_ref, hn_ref, slot_ref, wt_ref,
        gran_ref, cnt_ref, run_ref) = _token_tile(refs, x_ctx_tiles)
    _outproj_body(x, oc_ref, od_ref, w_ref, g1_ref, ng_ref, sc_ref, sh_ref, rw_ref, rb_ref, xo_ref, hn_ref, slot_ref,
                  wt_ref, gran_ref, cnt_ref, run_ref, ctx_tiles)


def _outproj_body(x, oc_ref, od_ref, w_ref, g1_ref, ng_ref, sc_ref, sh_ref, rw_ref, rb_ref, xo_ref, hn_ref,
                  slot_ref, wt_ref, gran_ref, cnt_ref, run_ref, ctx_tiles):
    @pl.when(pl.program_id(0) == 0)
    def _():
        run_ref[...] = jnp.zeros_like(run_ref)

    o = jnp.where(pl.program_id(0) < ctx_tiles, oc_ref[...], od_ref[...])
    xn = x + g1_ref[0] * _dot(o, w_ref[...])
    xo_ref[...] = xn
    hn = _rms(xn) * ng_ref[...]
    hn = hn * (1.0 + sc_ref[0]) + sh_ref[0]
    hn_ref[...] = hn.astype(BF16)

    tm = TM_PROJ
    tiles = range(hn.shape[0] // tm)
    lanes = [slice(j * tm, (j + 1) * tm) for j in tiles]
    logits = [_dot_nt(rw_ref[...], hn[ln, :], HIGHEST) for ln in lanes]

    def top2(lg):
        ex = jnp.exp(lg - jnp.max(lg, axis=0, keepdims=True))
        probs = ex / jnp.sum(ex, axis=0, keepdims=True)
        sel = probs + rb_ref[...]
        sel_rows = [sel[e:e + 1, :] for e in range(N_EXPERTS)]
        prob_rows = [probs[e:e + 1, :] for e in range(N_EXPERTS)]
        scores = []
        for g in range(N_GROUPS):
            r = sel_rows[4 * g:4 * g + 4]
            a, b = jnp.maximum(r[0], r[1]), jnp.minimum(r[0], r[1])
            c, d = jnp.maximum(r[2], r[3]), jnp.minimum(r[2], r[3])
            scores.append(jnp.maximum(a, c) + jnp.maximum(jnp.minimum(a, c), jnp.maximum(b, d)))
        best, grp = scores[0], jnp.zeros(scores[0].shape, jnp.int32)
        for g in range(1, N_GROUPS):
            take = scores[g] > best
            grp = jnp.where(take, g, grp)
            best = jnp.where(take, scores[g], best)
        sel_in = [_pick([sel_rows[4 * g + j] for g in range(N_GROUPS)], grp) for j in range(4)]
        prob_in = [_pick([prob_rows[4 * g + j] for g in range(N_GROUPS)], grp) for j in range(4)]
        i1, i2 = _top2_of4(sel_in)
        w1, w2 = _pick(prob_in, i1), _pick(prob_in, i2)
        tot = w1 + w2
        return grp * 4 + i1, grp * 4 + i2, w1 / tot, w2 / tot

    picks = [top2(lg) for lg in logits]
    eidx = lax.broadcasted_iota(jnp.int32, (N_EXPERTS, tm), 0)
    onehot = [jnp.where(eidx == e1, 1.0, 0.0) + jnp.where(eidx == e2, 1.0, 0.0) for e1, e2, _, _ in picks]
    r = lax.broadcasted_iota(jnp.int32, (tm, tm), 0)
    c = lax.broadcasted_iota(jnp.int32, (tm, tm), 1)
    earlier = jnp.where(r < c, 1.0, 0.0).astype(BF16)
    ahead = [_dot(oh.astype(BF16), earlier) for oh in onehot]
    n_gran = [jnp.broadcast_to(jnp.floor((jnp.sum(oh, axis=1, keepdims=True) + (GRANULE - 1)) * (1.0 / GRANULE)),
                               (N_EXPERTS, D_HEAD)) for oh in onehot]
    er = lax.broadcasted_iota(jnp.int32, (N_EXPERTS, N_EXPERTS), 0)
    ec = lax.broadcasted_iota(jnp.int32, (N_EXPERTS, N_EXPERTS), 1)
    before = jnp.where(ec < er, 1.0, 0.0).astype(BF16)
    first = [_dot(before, ng.astype(BF16)) for ng in n_gran]
    g = lax.broadcasted_iota(jnp.int32, (N_EXPERTS, D_HEAD), 1).astype(F32)
    ge = lax.broadcasted_iota(jnp.int32, (N_EXPERTS, D_HEAD), 0).astype(F32)
    run = run_ref[...]
    for j in tiles:
        e1, e2, w1, w2 = picks[j]
        wt_ref[:, lanes[j]] = jnp.concatenate([w1, w2], axis=0)
        row0 = GRANULE * first[j][:, 0:1] + ahead[j]
        slot = [jnp.sum(jnp.where(eidx == e, row0, 0.0), axis=0, keepdims=True) for e in (e1, e2)]
        slot_ref[:, lanes[j]] = jnp.concatenate(slot, axis=0).astype(jnp.int32)
        owner = jnp.sum(jnp.where(g >= first[j] + n_gran[j], 1.0, 0.0), axis=0, keepdims=True)
        index = g[0:1, :] + jnp.sum(jnp.where(ge == owner, run - first[j], 0.0), axis=0, keepdims=True)
        gran_ref[j] = jnp.concatenate([owner, index], axis=0).astype(jnp.int32)
        run = run + n_gran[j]
    run_ref[...] = run
    cnt_ref[...] = run


def _outproj_route(x, n, o_ctx, o_dec, w_out, mods, mod_base, norm_g, router_w, router_bias, n_ctx, t_dec):
    d = w_out.shape[1]
    tm = TM_OUT
    ctx_tiles = n_ctx // tm
    mrow = _mod_row_of_tile(tm, n_ctx, t_dec)
    mod_map = lambda k: (lambda i: (mod_base + mrow(i) * 6 + k, 0, 0))
    const2 = lambda i: (0, 0)
    x_specs, x_args, x_ctx_tiles = _token_specs(x, tm, n_ctx)
    return pl.pallas_call(
        functools.partial(_outproj_kernel, ctx_tiles=ctx_tiles, x_ctx_tiles=x_ctx_tiles),
        grid=(n // tm,),
        in_specs=x_specs + [
            pl.BlockSpec((tm, d), lambda i: (jnp.minimum(i, ctx_tiles - 1), 0)),
            pl.BlockSpec((tm, d), lambda i: (jnp.maximum(i - ctx_tiles, 0), 0)),
            pl.BlockSpec((d, d), const2),
            pl.BlockSpec((1, 1, d), mod_map(2)),
            pl.BlockSpec((1, d), const2),
            pl.BlockSpec((1, 1, d), mod_map(4)),
            pl.BlockSpec((1, 1, d), mod_map(3)),
            pl.BlockSpec((N_EXPERTS, d), const2),
            pl.BlockSpec((N_EXPERTS, 1), const2),
        ],
        out_specs=[
            pl.BlockSpec((tm, d), lambda i: (i, 0)),
            pl.BlockSpec((tm, d), lambda i: (i, 0)),
            pl.BlockSpec((2, tm), lambda i: (0, i)),
            pl.BlockSpec((2, tm), lambda i: (0, i)),
            pl.BlockSpec((tm // TM_PROJ, 2, D_HEAD), lambda i: (i, 0, 0)),
            pl.BlockSpec((N_EXPERTS, D_HEAD), const2),
        ],
        out_shape=[
            jax.ShapeDtypeStruct((n, d), F32),
            jax.ShapeDtypeStruct((n, d), BF16),
            jax.ShapeDtypeStruct((2, n), jnp.int32),
            jax.ShapeDtypeStruct((2, n), F32),
            jax.ShapeDtypeStruct((n // TM_PROJ, 2, D_HEAD), jnp.int32),
            jax.ShapeDtypeStruct((N_EXPERTS, D_HEAD), F32),
        ],
        scratch_shapes=[pltpu.VMEM((N_EXPERTS, D_HEAD), F32)],
        compiler_params=pltpu.CompilerParams(dimension_semantics=("arbitrary",), vmem_limit_bytes=VMEM_LIMIT),
        name="outproj_route",
    )(*x_args, o_ctx, o_dec, w_out.astype(BF16), mods, norm_g.reshape(1, d), mods, mods, router_w.T, router_bias.reshape(N_EXPERTS, 1))


def _route_tables(gran, counts, n_tiles):
    tile_gran = TM_MOE // GRANULE
    cnt = counts[:, 0].astype(jnp.int32)
    padded = (cnt + tile_gran - 1) // tile_gran * tile_gran
    ends = jnp.cumsum(padded)
    offs = ends - padded
    owner, index = gran[:, 0, :], gran[:, 1, :]
    first_gran = index + sum(jnp.where(owner == e, offs[e], 0) for e in range(N_EXPERTS))
    rows = jnp.where(owner < N_EXPERTS, first_gran * GRANULE, -1).astype(jnp.int32)
    tile_g0 = jnp.arange(n_tiles, dtype=jnp.int32) * tile_gran
    tile_expert = jnp.minimum(jnp.sum(tile_g0[:, None] >= ends[None, :], axis=1), N_EXPERTS - 1)
    tile_valid = jnp.sum((tile_g0[:, None] >= offs[None, :]) & (tile_g0[:, None] < (offs + cnt)[None, :]), axis=1)
    j = jnp.arange(tile_gran, dtype=jnp.int32)[None, :]
    tails = jnp.where(j < (padded - cnt)[:, None], (offs + cnt)[:, None] + j, -1)
    tails = jnp.where(tails >= 0, tails * GRANULE, -1).astype(jnp.int32).reshape(1, 1, N_EXPERTS * tile_gran)
    return rows, tails, tile_expert.astype(jnp.int32), tile_valid.astype(jnp.int32)


def _granule_copy(src, src_row, dst, dst_row, sem):
    return pltpu.make_async_copy(src.at[pl.ds(pl.multiple_of(src_row, GRANULE), GRANULE)],
                                 dst.at[pl.ds(pl.multiple_of(dst_row, GRANULE), GRANULE)], sem)


def _for_granules(rows_ref, fn, count=TILE_GRANULES):
    def body(g, carry):
        row = rows_ref[0, 0, g]

        @pl.when(row >= 0)
        def _():
            fn(g, row)
        return carry
    lax.fori_loop(0, count, body, 0, unroll=8)


def _pair_matrix(slot, first_value, second_value):
    p = lax.broadcasted_iota(jnp.int32, (TILE_GRANULES * GRANULE, slot.shape[1]), 0)
    return jnp.where(p == slot[0:1, :], first_value, 0.0) + jnp.where(p == slot[1:2, :], second_value, 0.0)


def _dispatch_kernel(rows_ref, prev_ref, tail_ref, valid_ref, hn_ref, slot_ref, xs_ref, buf, zero, sems):
    i = pl.program_id(0)
    cur = i % 2
    buf[cur] = _dot(_pair_matrix(slot_ref[...], 1.0, 1.0).astype(BF16), hn_ref[...])

    def copies(table_ref, b, act):
        _for_granules(table_ref, lambda g, row: act(_granule_copy(buf.at[b], g * GRANULE, xs_ref, row, sems.at[b])))

    copies(rows_ref, cur, lambda c: c.start())

    @pl.when(i > 0)
    def _():
        copies(prev_ref, 1 - cur, lambda c: c.wait())

    @pl.when(i == pl.num_programs(0) - 1)
    def _():
        copies(rows_ref, cur, lambda c: c.wait())
        zero[...] = jnp.zeros_like(zero)
        n_tail = tail_ref.shape[2]
        _for_granules(tail_ref, lambda g, row: _granule_copy(zero, 0, xs_ref, row, sems.at[0]).start(), n_tail)
        _for_granules(tail_ref, lambda g, row: _granule_copy(zero, 0, xs_ref, row, sems.at[0]).wait(), n_tail)

        def fill_tiles(act):
            def body(t, carry):
                @pl.when(valid_ref[0, 0, t] == 0)
                def _():
                    act(pltpu.make_async_copy(zero, xs_ref.at[pl.ds(pl.multiple_of(t * TM_MOE, TM_MOE), TM_MOE)],
                                              sems.at[1]))
                return carry
            lax.fori_loop(0, valid_ref.shape[2], body, 0)

        fill_tiles(lambda c: c.start())
        fill_tiles(lambda c: c.wait())


def _dispatch(hn, slot, rows, tails, tile_valid, n_rows):
    n, d = hn.shape
    tm = TM_PROJ
    rows = rows.reshape(n // tm, 1, D_HEAD)
    table = lambda index_map: pl.BlockSpec((1, 1, D_HEAD), index_map, memory_space=pltpu.SMEM)
    return pl.pallas_call(
        _dispatch_kernel,
        grid=(n // tm,),
        in_specs=[
            table(lambda i: (i, 0, 0)),
            table(lambda i: (jnp.maximum(i - 1, 0), 0, 0)),
            pl.BlockSpec(memory_space=pltpu.SMEM),
            pl.BlockSpec(memory_space=pltpu.SMEM),
            pl.BlockSpec((tm, d), lambda i: (i, 0)),
            pl.BlockSpec((2, tm), lambda i: (0, i)),
        ],
        out_specs=pl.BlockSpec(memory_space=pl.ANY),
        out_shape=jax.ShapeDtypeStruct((n_rows, d), F32),
        scratch_shapes=[pltpu.VMEM((2, TILE_GRANULES * GRANULE, d), F32), pltpu.VMEM((TM_MOE, d), F32),
                        pltpu.SemaphoreType.DMA((2,))],
        compiler_params=pltpu.CompilerParams(dimension_semantics=("arbitrary",), vmem_limit_bytes=VMEM_LIMIT),
        name="moe_dispatch",
    )(rows, rows, tails, tile_valid.reshape(1, 1, -1), hn, slot)


def _expert_kernel(te_ref, tv_ref, xs_ref, wg_ref, wu_ref, wd_ref, ys_ref, wg16, wu16, wd16):
    i = pl.program_id(0)
    fresh = jnp.logical_or(i == 0, te_ref[i] != te_ref[jnp.maximum(i - 1, 0)])

    @pl.when(fresh)
    def _():
        wg16[...] = wg_ref[0, 0].astype(BF16)
        wu16[...] = wu_ref[0, 0].astype(BF16)
        wd16[...] = wd_ref[0, 0].astype(BF16)

    @pl.when(tv_ref[i] != 0)
    def _():
        x = xs_ref[...].astype(BF16)
        hid = _silu(_dot(x, wg16[...])) * _dot(x, wu16[...])
        ys_ref[...] = _dot(hid.astype(BF16), wd16[...])

    @pl.when(tv_ref[i] == 0)
    def _():
        ys_ref[...] = jnp.zeros_like(ys_ref)


def _experts(xs, tile_expert, tile_valid, w_gate, w_up, w_down, layer):
    n_rows, d = xs.shape
    tm = TM_MOE
    w_in_spec = pl.BlockSpec((1, 1, d, D_FF), lambda i, te, tv: (layer, te[i], 0, 0))
    return pl.pallas_call(
        _expert_kernel,
        grid_spec=pltpu.PrefetchScalarGridSpec(
            num_scalar_prefetch=2,
            grid=(n_rows // tm,),
            in_specs=[
                pl.BlockSpec((tm, d), lambda i, te, tv: (jnp.where(tv[i] != 0, i, 0), 0)),
                w_in_spec,
                w_in_spec,
                pl.BlockSpec((1, 1, D_FF, d), lambda i, te, tv: (layer, te[i], 0, 0)),
            ],
            out_specs=pl.BlockSpec((tm, d), lambda i, te, tv: (i, 0)),
            scratch_shapes=[pltpu.VMEM((d, D_FF), BF16), pltpu.VMEM((d, D_FF), BF16), pltpu.VMEM((D_FF, d), BF16)],
        ),
        out_shape=jax.ShapeDtypeStruct((n_rows, d), F32),
        compiler_params=pltpu.CompilerParams(dimension_semantics=("arbitrary",), vmem_limit_bytes=VMEM_LIMIT),
        name="moe_experts",
    )(tile_expert, tile_valid, xs, w_gate, w_up, w_down)


def _combine_kernel(rows_ref, next_ref, x_ref, slot_ref, wt_ref, g2_ref, fg_ref, ys_ref, *rest, ctx_tiles):
    *o_refs, buf, sems = rest
    i = pl.program_id(0)
    cur = i % 2

    def copies(table_ref, b, act):
        _for_granules(table_ref, lambda g, row: act(_granule_copy(ys_ref, row, buf.at[b], g * GRANULE, sems.at[b])))

    @pl.when(i == 0)
    def _():
        buf[...] = jnp.zeros_like(buf)
        copies(rows_ref, 0, lambda c: c.start())

    @pl.when(i + 1 < pl.num_programs(0))
    def _():
        copies(next_ref, 1 - cur, lambda c: c.start())

    copies(rows_ref, cur, lambda c: c.wait())
    w = wt_ref[...]
    mix = _pair_matrix(slot_ref[...], w[0:1, :], w[1:2, :]).astype(BF16)
    xn = x_ref[...] + g2_ref[0] * _dot_tn(mix, buf[cur].astype(BF16))
    if ctx_tiles is None:
        o_refs[0][...] = xn
        return
    xn = _rms(xn) * fg_ref[...]

    @pl.when(pl.program_id(0) < ctx_tiles)
    def _():
        o_refs[0][...] = xn

    @pl.when(pl.program_id(0) >= ctx_tiles)
    def _():
        o_refs[1][...] = xn


def _combine(x, ys, slot, wts, rows, mods, mod_base, final_g, final_norm, n_ctx, t_dec):
    n, d = x.shape
    tm = TM_PROJ
    mrow = _mod_row_of_tile(tm, n_ctx, t_dec)
    ctx_tiles = n_ctx // tm if final_norm else None
    if final_norm:
        out_specs = [pl.BlockSpec((tm, d), lambda i: (jnp.minimum(i, ctx_tiles - 1), 0)),
                     pl.BlockSpec((tm, d), lambda i: (jnp.maximum(i - ctx_tiles, 0), 0))]
        out_shape = [jax.ShapeDtypeStruct((n_ctx, d), F32), jax.ShapeDtypeStruct((n - n_ctx, d), F32)]
    else:
        out_specs = pl.BlockSpec((tm, d), lambda i: (i, 0))
        out_shape = jax.ShapeDtypeStruct((n, d), F32)
    rows = rows.reshape(n // tm, 1, D_HEAD)
    last = n // tm - 1
    return pl.pallas_call(
        functools.partial(_combine_kernel, ctx_tiles=ctx_tiles),
        grid=(n // tm,),
        in_specs=[
            pl.BlockSpec((1, 1, D_HEAD), lambda i: (i, 0, 0), memory_space=pltpu.SMEM),
            pl.BlockSpec((1, 1, D_HEAD), lambda i: (jnp.minimum(i + 1, last), 0, 0), memory_space=pltpu.SMEM),
            pl.BlockSpec((tm, d), lambda i: (i, 0)),
            pl.BlockSpec((2, tm), lambda i: (0, i)),
            pl.BlockSpec((2, tm), lambda i: (0, i)),
            pl.BlockSpec((1, 1, d), lambda i: (mod_base + mrow(i) * 6 + 5, 0, 0)),
            pl.BlockSpec((1, d), lambda i: (0, 0)),
            pl.BlockSpec(memory_space=pl.ANY),
        ],
        out_specs=out_specs,
        out_shape=out_shape,
        scratch_shapes=[pltpu.VMEM((2, TILE_GRANULES * GRANULE, d), F32), pltpu.SemaphoreType.DMA((2,))],
        compiler_params=pltpu.CompilerParams(dimension_semantics=("arbitrary",), vmem_limit_bytes=VMEM_LIMIT),
        name="moe_combine",
    )(rows, rows, x, slot, wts, mods, final_g.reshape(1, d), ys)


def _moe(x, hn, slot, wts, gran, counts, w_gate, w_up, w_down, layer, mods, mod_base, final_g, final_norm, n_ctx,
         t_dec):
    n = x.shape[0]
    n_rows = 2 * n + (n // TM_PROJ) * N_EXPERTS * GRANULE + N_EXPERTS * TM_MOE
    n_tiles = -(-n_rows // TM_MOE)
    rows, tails, tile_expert, tile_valid = _route_tables(gran, counts, n_tiles)
    xs = _dispatch(hn, slot, rows, tails, tile_valid, n_tiles * TM_MOE)
    ys = _experts(xs, tile_expert, tile_valid, w_gate, w_up, w_down, layer)
    return _combine(x, ys, slot, wts, rows, mods, mod_base, final_g, final_norm, n_ctx, t_dec)


def _grid_pos_embed(n_tokens):
    rows = n_tokens // GRID_W
    r = jnp.repeat(jnp.arange(rows, dtype=F32), GRID_W)
    col = jnp.tile(jnp.arange(GRID_W, dtype=F32), rows)
    quarter = D_MODEL // 4
    freq = jnp.exp(jnp.arange(quarter, dtype=F32) * (-math.log(POS_BASE) / quarter))

    def axis_embed(pos):
        a = pos[:, None] * freq[None, :]
        return jnp.concatenate([jnp.sin(a), jnp.cos(a)], axis=-1)

    return jnp.concatenate([axis_embed(r), axis_embed(col)], axis=-1)


def _split_in_weights(w_in):
    pm = 4 * N_HEADS * D_HEAD
    wg = w_in[:, pm:].reshape(-1, 2, 2, N_HEADS)
    return w_in[:, :pm], wg.transpose(0, 3, 1, 2).reshape(-1, 4 * N_HEADS)


def _head_params(first, second):
    return jnp.stack([first, second], axis=0).transpose(2, 0, 1).reshape(-1).astype(F32)


def _heads_per_step(t_len, chains_per_head_chunk=2):
    return max(2, min(N_HEADS, SCAN_CHAINS // (chains_per_head_chunk * (t_len // CHUNK))))


def kernel(x_prompt, x_sample, state_gdn_S, state_mlstm_C, state_mlstm_n, state_mlstm_m, c, c_ctx, ada_w, ada_b,
           norm1_g, norm2_g, gdn_w_in, gdn_conv_w, gdn_a_log, gdn_dt_bias, gdn_norm_g, gdn_w_out, mlstm_w_in,
           mlstm_gate_b, mlstm_norm_g, mlstm_w_out, router_w, router_bias, exp_w_gate, exp_w_up, exp_w_down,
           final_norm_g):
    bp, tp, d = x_prompt.shape
    bs, ts, _ = x_sample.shape
    n_ctx = bp * tp
    depth = ada_w.shape[0]
    assert n_ctx % ts == 0 and ts % max(TM_IN, TM_OUT) == 0 and tp % TM_PROJ == 0 and bs + 1 <= N_MOD_ROWS
    assert TILE_GRANULES <= D_HEAD

    pos = _grid_pos_embed(ts).astype(F32)
    n = n_ctx + bs * ts
    x = (x_prompt.reshape(n_ctx, d), x_sample.reshape(bs * ts, d), pos)

    conds = jnp.concatenate([c_ctx[None, :], c, jnp.zeros((N_MOD_ROWS - 1 - bs, d), F32)], axis=0)
    mods = _ada_mods(conds, ada_w, ada_b).reshape(depth * N_MOD_ROWS * 6, 1, d)

    zeros_dh = jnp.zeros_like(gdn_a_log[0])
    ctx = dict(batch=bp, t_len=tp, row0=0, emit_state=True)
    dec = dict(batch=bs, t_len=ts, row0=n_ctx, emit_state=False)
    gdn_hb = dict(ctx=_heads_per_step(tp), dec=_heads_per_step(ts))
    mlstm_hb = dict(ctx=_heads_per_step(tp, 4), dec=_heads_per_step(ts, 4))
    outs = {}
    for layer in range(depth):
        j = layer // 2
        mod_base = layer * N_MOD_ROWS * 6
        if layer % 2 == 0:
            w_main, w_gate = _split_in_weights(gdn_w_in[j])
            bias = _head_params(gdn_dt_bias[j], zeros_dh)
            mul = _head_params(-jnp.exp(gdn_a_log[j].astype(F32)), zeros_dh)
            main, grow = _inproj(x, n, mods, mod_base, norm1_g[layer], w_main, w_gate, bias, mul, "gdn", n_ctx, ts)
            o_ctx, s_new = _gdn_scan(main, grow, gdn_conv_w[j], gdn_norm_g[j], None, hb=gdn_hb["ctx"], **ctx)
            (o_dec,) = _gdn_scan(main, grow, gdn_conv_w[j], gdn_norm_g[j], state_gdn_S[:, j].astype(F32),
                                 hb=gdn_hb["dec"], **dec)
            outs.setdefault("gdn", []).append(s_new)
            w_out = gdn_w_out[j]
        else:
            w_main, w_gate = _split_in_weights(mlstm_w_in[j])
            bias = _head_params(mlstm_gate_b[j, 0], mlstm_gate_b[j, 1])
            main, grow = _inproj(x, n, mods, mod_base, norm1_g[layer], w_main, w_gate, bias, jnp.zeros_like(bias),
                                 "mlstm", n_ctx, ts)
            o_ctx, c_new, n_new, m_new = _mlstm_scan(main, grow, mlstm_norm_g[j], None, hb=mlstm_hb["ctx"], **ctx)
            init = (state_mlstm_C[:, j].astype(F32),
                    state_mlstm_n[:, j].astype(F32).transpose(0, 2, 1, 3),
                    jnp.broadcast_to(state_mlstm_m[:, j].astype(F32).transpose(0, 2, 1)[..., None],
                                     (bs, N_HEADS, 2, D_HEAD)))
            (o_dec,) = _mlstm_scan(main, grow, mlstm_norm_g[j], init, hb=mlstm_hb["dec"], **dec)
            outs.setdefault("mC", []).append(c_new)
            outs.setdefault("mn", []).append(n_new.transpose(0, 2, 1, 3))
            outs.setdefault("mm", []).append(m_new[..., 0].transpose(0, 2, 1))
            w_out = mlstm_w_out[j]
        x, hn, slot, wts, gran, counts = _outproj_route(x, n, o_ctx, o_dec, w_out, mods, mod_base, norm2_g[layer],
                                                        router_w, router_bias, n_ctx, ts)
        x = _moe(x, hn, slot, wts, gran, counts, exp_w_gate, exp_w_up, exp_w_down, layer, mods, mod_base,
                 final_norm_g, layer == depth - 1, n_ctx, ts)

    y_prompt = x[0].reshape(bp, tp, d)
    y_sample = x[1].reshape(bs, ts, d)
    return (y_prompt, y_sample, jnp.stack(outs["gdn"], axis=1), jnp.stack(outs["mC"], axis=1),
            jnp.stack(outs["mn"], axis=1), jnp.stack(outs["mm"], axis=1))
```

```python
import functools
import math

import jax
import jax.numpy as jnp
from jax import lax
from jax.experimental import pallas as pl
from jax.experimental.pallas import tpu as pltpu

F32 = jnp.float32
BF16 = jnp.bfloat16
HIGHEST = lax.Precision.HIGHEST

D_MODEL = 1024
N_HEADS = 8
D_HEAD = 128
CHUNK = 64
N_EXPERTS = 16
N_GROUPS = 4
EXP_PER_GROUP = 4
D_FF = 512
EPS = 1e-6
GRID_W = 64
POS_BASE = 10000.0
N_MOD_ROWS = 8
VMEM_LIMIT = 56 * 1024 * 1024

TM_PROJ = 256
TM_IN = 512
TM_OUT = 512
TM_MOE = 512
GRANULE = 8
TILE_GRANULES = 2 * TM_PROJ // GRANULE + N_EXPERTS
SCAN_CHAINS = 64


def _silu(x):
    return x * jax.nn.sigmoid(x)


def _softplus(x):
    return jnp.maximum(x, 0.0) + jnp.log(1.0 + jnp.exp(-jnp.abs(x)))


def _dot(a, b, precision=None):
    return jnp.dot(a, b, preferred_element_type=F32, precision=precision)


def _dot_nt(a, b, precision=None):
    return lax.dot_general(a, b, (((1,), (1,)), ((), ())), preferred_element_type=F32, precision=precision)


def _dot_tn(a, b, precision=None):
    return lax.dot_general(a, b, (((0,), (0,)), ((), ())), preferred_element_type=F32, precision=precision)


def _rms(x):
    return x * lax.rsqrt(jnp.mean(x * x, axis=-1, keepdims=True) + EPS)


def _ada_kernel(c_ref, w_ref, b_ref, o_ref):
    cs = _silu(c_ref[...]).astype(BF16)
    o_ref[0] = _dot(cs, w_ref[0].astype(BF16)) + b_ref[0]


def _ada_mods(conds, ada_w, ada_b):
    depth, d, n6 = ada_w.shape
    tn = 1536
    return pl.pallas_call(
        _ada_kernel,
        grid=(depth, n6 // tn),
        in_specs=[
            pl.BlockSpec((N_MOD_ROWS, d), lambda l, j: (0, 0)),
            pl.BlockSpec((1, d, tn), lambda l, j: (l, 0, j)),
            pl.BlockSpec((1, 1, tn), lambda l, j: (l, 0, j)),
        ],
        out_specs=pl.BlockSpec((1, N_MOD_ROWS, tn), lambda l, j: (l, 0, j)),
        out_shape=jax.ShapeDtypeStruct((depth, N_MOD_ROWS, n6), F32),
        compiler_params=pltpu.CompilerParams(dimension_semantics=("arbitrary", "arbitrary"),
                                             vmem_limit_bytes=VMEM_LIMIT),
        name="ada_mods",
    )(conds, ada_w, ada_b.reshape(depth, 1, n6))


def _split3(x):
    hi = x.astype(BF16)
    r1 = x - hi.astype(F32)
    mid = r1.astype(BF16)
    lo = (r1 - mid.astype(F32)).astype(BF16)
    return hi, mid, lo


def _token_specs(x, tm, n_ctx):
    if not isinstance(x, tuple):
        return [pl.BlockSpec((tm, x.shape[1]), lambda i, *_: (i, 0))], [x], None
    xc, xd, pos = x
    d = xc.shape[1]
    ctx_tiles, pos_tiles = n_ctx // tm, pos.shape[0] // tm
    specs = [pl.BlockSpec((tm, d), lambda i, *_: (jnp.minimum(i, ctx_tiles - 1), 0)),
             pl.BlockSpec((tm, d), lambda i, *_: (jnp.maximum(i - ctx_tiles, 0), 0)),
             pl.BlockSpec((tm, d), lambda i, *_: (jnp.maximum(i - ctx_tiles, 0) % pos_tiles, 0))]
    return specs, [xc, xd, pos], ctx_tiles


def _token_tile(refs, ctx_tiles):
    if ctx_tiles is None:
        return refs[0][...], refs[1:]
    xc_ref, xd_ref, pos_ref = refs[:3]
    return jnp.where(pl.program_id(0) < ctx_tiles, xc_ref[...], xd_ref[...] + pos_ref[...]), refs[3:]


def _inproj_kernel(*refs, kind, x_ctx_tiles):
    x, (ng_ref, sc_ref, sh_ref, w_ref, wgt_ref, gb_ref, gm_ref, main_ref, gr_ref) = _token_tile(refs, x_ctx_tiles)
    hn = _rms(x) * ng_ref[...]
    hn = hn * (1.0 + sc_ref[0]) + sh_ref[0]
    hb = hn.astype(BF16)
    main_ref[...] = _dot(hb, w_ref[...])

    pre = _dot_nt(wgt_ref[...], hb) + gb_ref[...]
    row = lax.broadcasted_iota(jnp.int32, pre.shape, 0)
    first = (row % 4) < 2
    if kind == "gdn":
        act = jnp.where(first, gm_ref[...] * _softplus(pre), jax.nn.sigmoid(pre))
        scanned_kind = 0
    else:
        act = jnp.where(first, pre, -_softplus(-pre))
        scanned_kind = 1

    tm = x.shape[0]
    r = lax.broadcasted_iota(jnp.int32, (tm, tm), 0)
    c = lax.broadcasted_iota(jnp.int32, (tm, tm), 1)
    same = (r // CHUNK) == (c // CHUNK)
    before = jnp.where(same, jnp.where(r <= c, 1.0, 0.0), 0.0).astype(BF16)
    after = jnp.where(same, jnp.where(r >= c, 1.0, 0.0), 0.0).astype(BF16)
    parts = _split3(act)
    fwd = sum(_dot(p, before) for p in parts)
    bwd = sum(_dot(p, after) for p in parts)
    gt = jnp.where((row % 4) // 2 == scanned_kind, jnp.where(row % 2 == 0, fwd, bwd), act)
    for h in range(N_HEADS):
        for ch in range(tm // CHUNK):
            gr_ref[h, ch] = gt[4 * h:4 * h + 4, ch * CHUNK:(ch + 1) * CHUNK]


def _mod_row_of_tile(tm, n_ctx, t_dec):
    def f(i):
        r = i * tm
        return jnp.where(r < n_ctx, 0, 1 + (r - n_ctx) // t_dec)
    return f


def _inproj(x, n, mods, mod_base, norm_g, w_main, w_gate, gate_bias, gate_mul, kind, n_ctx, t_dec):
    d = w_main.shape[0]
    tm = TM_IN
    pm = w_main.shape[1]
    ng = w_gate.shape[1]
    mrow = _mod_row_of_tile(tm, n_ctx, t_dec)
    sc_map = lambda i: (mod_base + mrow(i) * 6 + 1, 0, 0)
    sh_map = lambda i: (mod_base + mrow(i) * 6 + 0, 0, 0)
    const2 = lambda i: (0, 0)
    x_specs, x_args, x_ctx_tiles = _token_specs(x, tm, n_ctx)
    return pl.pallas_call(
        functools.partial(_inproj_kernel, kind=kind, x_ctx_tiles=x_ctx_tiles),
        grid=(n // tm,),
        in_specs=x_specs + [
            pl.BlockSpec((1, d), const2),
            pl.BlockSpec((1, 1, d), sc_map),
            pl.BlockSpec((1, 1, d), sh_map),
            pl.BlockSpec((d, pm), const2),
            pl.BlockSpec((ng, d), const2),
            pl.BlockSpec((ng, 1), const2),
            pl.BlockSpec((ng, 1), const2),
        ],
        out_specs=[
            pl.BlockSpec((tm, pm), lambda i: (i, 0)),
            pl.BlockSpec((N_HEADS, tm // CHUNK, 4, CHUNK), lambda i: (0, i, 0, 0)),
        ],
        out_shape=[
            jax.ShapeDtypeStruct((n, pm), F32),
            jax.ShapeDtypeStruct((N_HEADS, n // CHUNK, 4, CHUNK), F32),
        ],
        compiler_params=pltpu.CompilerParams(dimension_semantics=("arbitrary",), vmem_limit_bytes=VMEM_LIMIT),
        name="inproj_" + kind,
    )(*x_args, norm_g.reshape(1, d), mods, mods, w_main.astype(BF16), w_gate.T.astype(BF16),
      gate_bias.reshape(ng, 1), gate_mul.reshape(ng, 1))


def _chunk_masks():
    r = lax.broadcasted_iota(jnp.int32, (CHUNK, CHUNK), 0)
    c = lax.broadcasted_iota(jnp.int32, (CHUNK, CHUNK), 1)
    return r >= c, r > c, r <= c, r < c


def _chunk_rows(c):
    if isinstance(c, int):
        return pl.ds(c * CHUNK, CHUNK)
    return pl.ds(pl.multiple_of(c * CHUNK, CHUNK), CHUNK)


def _head_cols(h):
    return slice(h * D_HEAD, (h + 1) * D_HEAD)


def _last_row(x, d):
    return x[CHUNK - 1:CHUNK, :] if d == 0 else x[0:1, :]


def _gate_selector():
    j = lax.broadcasted_iota(jnp.int32, (32, 4 * D_HEAD), 0) % 8
    lane = lax.broadcasted_iota(jnp.int32, (32, 4 * D_HEAD), 1)
    return jnp.where(lane // D_HEAD == j, 1.0, 0.0).astype(BF16)


def _gate_columns(gr4, selector):
    a = jnp.concatenate([gr4, jnp.zeros_like(gr4)], axis=0)
    hi = a.astype(BF16).astype(F32)
    mid = (a - hi).astype(BF16).astype(F32)
    lo = a - hi - mid
    parts = jnp.concatenate([hi, mid, lo, jnp.zeros_like(a)], axis=0).astype(BF16)
    return _dot_tn(parts, selector)


def _gate_column(cols, j):
    return cols[:, j * D_HEAD:(j + 1) * D_HEAD]


def _unit_tri_solve(a_list, rhs_list):
    pack = 4
    groups = [a_list[i:i + pack] for i in range(0, len(a_list), pack)]
    r = lax.broadcasted_iota(jnp.int32, (CHUNK, pack * CHUNK), 0)
    c = lax.broadcasted_iota(jnp.int32, (CHUNK, pack * CHUNK), 1) % CHUNK
    same = (r // 16) == (c // 16)
    eye = jnp.where(r == c, 1.0, 0.0)
    br = lax.broadcasted_iota(jnp.int32, (pack * CHUNK, pack * CHUNK), 0) // CHUNK
    bc = lax.broadcasted_iota(jnp.int32, (pack * CHUNK, pack * CHUNK), 1) // CHUNK
    zero16 = jnp.zeros((), BF16)

    def blockdiag(y):
        return jnp.where(br == bc, jnp.concatenate([y.astype(BF16)] * pack, axis=0), zero16)

    def mm(xs, ys):
        return [_dot(x.astype(BF16), blockdiag(y)) for x, y in zip(xs, ys)]

    ap = [jnp.concatenate(g, axis=1) for g in groups]
    d = [jnp.where(same, a, 0.0) for a in ap]
    t = [eye - di for di in d]
    for _ in range(3):
        d = mm(d, d)
        t = [ti + pi for ti, pi in zip(t, mm(t, d))]
    b = mm(t, [jnp.where(same, 0.0, a) for a in ap])
    b2 = mm(b, b)
    y = [ti - pi for ti, pi in zip(t, mm(b, t))]
    inv = [(yi + pi).astype(BF16) for yi, pi in zip(y, mm(b2, y))]

    def placed(rhs, j):
        z = jnp.zeros(rhs.shape, BF16)
        return jnp.concatenate([z] * j + [rhs.astype(BF16)] + [z] * (pack - 1 - j), axis=0)

    return [_dot(inv[i // pack], placed(rhs, i % pack)) for i, rhs in enumerate(rhs_list)]


def _for_chunk_groups(nc, group, fn):
    if nc == group:
        fn(list(range(nc)))
        return

    def body(g, carry):
        fn([g * group + j for j in range(group)])
        return carry
    lax.fori_loop(0, nc // group, body, 0)


def _scan_specs(t_len, row0, hb, n_proj):
    rb = row0 // t_len
    nc = t_len // CHUNK
    ngrp = N_HEADS // hb
    specs = [pl.BlockSpec((t_len, hb * D_HEAD), functools.partial(lambda b, g, j: (rb + b, j * ngrp + g), j=j))
             for j in range(n_proj)]
    specs.append(pl.BlockSpec((hb, nc, 4, CHUNK), lambda b, g: (g, rb + b, 0, 0)))
    return specs


def _scan_call(kern, name, args, in_specs, out_specs, out_shape, scratch, batch, hb):
    return pl.pallas_call(
        kern,
        grid=(batch, N_HEADS // hb),
        in_specs=in_specs,
        out_specs=out_specs,
        out_shape=out_shape,
        scratch_shapes=scratch,
        compiler_params=pltpu.CompilerParams(dimension_semantics=("arbitrary", "arbitrary"),
                                             vmem_limit_bytes=VMEM_LIMIT),
        name=name,
    )(*args)


def _gdn_kernel(*refs, t_len, hb, has_init, emit_state):
    it = iter(refs)
    q_ref, k_ref, v_ref, z_ref, gr_ref, cwq_ref, cwk_ref, cwv_ref, ng_ref = (next(it) for _ in range(9))
    s0_ref = next(it) if has_init else None
    o_ref = next(it)
    s_ref = next(it) if emit_state else None
    qs, ks, vs, oacc, qp_s, op_s, km_s, nm_s, ge_s, st_s = (next(it) for _ in range(10))
    nc = t_len // CHUNK

    rows = lax.broadcasted_iota(jnp.int32, (t_len, 1), 0)

    def conv_silu(x, cw):
        xm = jnp.where(rows == 0, 0.0, pltpu.roll(x, 1, axis=0))
        xp = jnp.where(rows == t_len - 1, 0.0, pltpu.roll(x, t_len - 1, axis=0))
        return _silu(xm * cw[0:1] + x * cw[1:2] + xp * cw[2:3])

    def l2n(x):
        return x * lax.rsqrt(jnp.sum(x * x, axis=-1, keepdims=True) + EPS)

    for h in range(hb):
        hc = _head_cols(h)
        qs[:, hc] = l2n(conv_silu(q_ref[:, hc], cwq_ref[:, hc])) * (D_HEAD ** -0.5)
        ks[:, hc] = l2n(conv_silu(k_ref[:, hc], cwk_ref[:, hc]))
        vs[:, hc] = conv_silu(v_ref[:, hc], cwv_ref[:, hc])
    oacc[...] = jnp.zeros_like(oacc)

    lo_i, lo_s, up_i, up_s = _chunk_masks()
    selector = _gate_selector()

    def slot(h, d, c):
        return (h * 2 + d) * nc + c

    def intra(chunks):
        items = [(h, c, d) for h in range(hb) for c in chunks for d in range(2)]
        qkk, cols = [], []
        for h in range(hb):
            for c in chunks:
                q16, k16 = qs[_chunk_rows(c), _head_cols(h)].astype(BF16), ks[_chunk_rows(c), _head_cols(h)].astype(BF16)
                qkk.append(_dot_nt(jnp.concatenate([q16, k16], axis=0), k16))
                cols.append(_gate_columns(gr_ref[h, c], selector))
        a_list, rhs_list, keep = [], [], []
        for n_item, (h, c, d) in enumerate(items):
            sl, hc = _chunk_rows(c), _head_cols(h)
            q, k, v = qs[sl, hc], ks[sl, hc], vs[sl, hc]
            gr4 = gr_ref[h, c]
            col4, qk, kk = cols[n_item // 2], qkk[n_item // 2][:CHUNK], qkk[n_item // 2][CHUNK:]
            g_col, beta = _gate_column(col4, d), _gate_column(col4, 2 + d)
            incl, strict = (lo_i, lo_s) if d == 0 else (up_i, up_s)
            decay = jnp.exp(jnp.where(incl, g_col[:, :CHUNK] - gr4[d:d + 1, :], -jnp.inf))
            a_list.append(jnp.where(strict, beta[:, :CHUNK] * kk * decay, 0.0))
            kb = k * beta
            e_g = jnp.exp(g_col)
            rhs_list.append(jnp.concatenate([v * beta, kb * e_g], axis=1))
            qk16 = jnp.where(incl, qk * decay, 0.0).astype(BF16)
            g_last = _last_row(g_col, d)
            ge_s[slot(h, d, c)] = jnp.broadcast_to(jnp.exp(g_last), (8, D_HEAD))
            keep.append((qk16, q * e_g, (k * jnp.exp(g_last - g_col)).astype(BF16)))
        sols = [s.astype(BF16) for s in _unit_tri_solve(a_list, rhs_list)]
        qw = [_dot(kp[0], s) for kp, s in zip(keep, sols)]
        kw = [_dot_tn(kp[2], s) for kp, s in zip(keep, sols)]
        for (h, c, d), kp, qwi, kwi in zip(items, keep, qw, kw):
            i = h * 2 + d
            sl = _chunk_rows(c)
            op_s[i, sl, :] = qwi[:, :D_HEAD]
            qp_s[i, sl, :] = (kp[1] - qwi[:, D_HEAD:]).astype(BF16)
            nm_s[slot(h, d, c)] = kwi[:, :D_HEAD]
            km_s[slot(h, d, c)] = (-kwi[:, D_HEAD:]).astype(BF16)

    _for_chunk_groups(nc, min(nc, max(1, SCAN_CHAINS // (2 * hb))), intra)

    chains = [(h, d) for h in range(hb) for d in range(2)]

    def inter_body(i, carry):
        cs = (i, nc - 1 - i)
        s16 = [st_s[h * 2 + d].astype(BF16) for h, d in chains]
        outs = [_dot(qp_s[h * 2 + d, _chunk_rows(cs[d]), :], s) for (h, d), s in zip(chains, s16)]
        upds = [_dot(km_s[slot(h, d, cs[d])], s) for (h, d), s in zip(chains, s16)]
        for (h, d), o in zip(chains, outs):
            oacc[_chunk_rows(cs[d]), _head_cols(h)] += o + op_s[h * 2 + d, _chunk_rows(cs[d]), :]
        for (h, d), u in zip(chains, upds):
            st_s[h * 2 + d] = ge_s[slot(h, d, cs[d])][0:1, :] * st_s[h * 2 + d] + u + nm_s[slot(h, d, cs[d])]
        return carry

    for h, d in chains:
        st_s[h * 2 + d] = s0_ref[0, d, h] if has_init else jnp.zeros((D_HEAD, D_HEAD), F32)
    lax.fori_loop(0, nc, inter_body, 0)
    if emit_state:
        for h, d in chains:
            s_ref[0, d, h] = st_s[h * 2 + d]
    for h in range(hb):
        hc = _head_cols(h)
        o_ref[:, hc] = (_rms(oacc[:, hc]) * ng_ref[...] * _silu(z_ref[:, hc])).astype(o_ref.dtype)


def _gdn_scan(main, grow, conv_w, norm_g, s0, *, batch, t_len, row0, hb, emit_state):
    nc = t_len // CHUNK
    ngrp = N_HEADS // hb
    has_init = s0 is not None
    in_specs = _scan_specs(t_len, row0, hb, 4)
    in_specs += [pl.BlockSpec((3, hb * D_HEAD), functools.partial(lambda b, g, j: (0, j * ngrp + g), j=j))
                 for j in range(3)]
    in_specs.append(pl.BlockSpec((1, D_HEAD), lambda b, g: (0, 0)))
    args = [main, main, main, main, grow, conv_w, conv_w, conv_w, norm_g.reshape(1, D_HEAD)]
    state_spec = pl.BlockSpec((1, 2, hb, D_HEAD, D_HEAD), lambda b, g: (b, 0, g, 0, 0))
    if has_init:
        in_specs.append(state_spec)
        args.append(s0)
    out_specs = [pl.BlockSpec((t_len, hb * D_HEAD), lambda b, g: (b, g))]
    out_shape = [jax.ShapeDtypeStruct((batch * t_len, N_HEADS * D_HEAD), BF16)]
    if emit_state:
        out_specs.append(state_spec)
        out_shape.append(jax.ShapeDtypeStruct((batch, 2, N_HEADS, D_HEAD, D_HEAD), F32))
    scratch = ([pltpu.VMEM((t_len, hb * D_HEAD), F32) for _ in range(4)]
               + [pltpu.VMEM((2 * hb, t_len, D_HEAD), BF16),
                  pltpu.VMEM((2 * hb, t_len, D_HEAD), F32),
                  pltpu.VMEM((2 * hb * nc, D_HEAD, D_HEAD), BF16),
                  pltpu.VMEM((2 * hb * nc, D_HEAD, D_HEAD), F32),
                  pltpu.VMEM((2 * hb * nc, 8, D_HEAD), F32),
                  pltpu.VMEM((2 * hb, D_HEAD, D_HEAD), F32)])
    kern = functools.partial(_gdn_kernel, t_len=t_len, hb=hb, has_init=has_init, emit_state=emit_state)
    return _scan_call(kern, "gdn_scan_t%d" % t_len, args, in_specs, out_specs, out_shape, scratch, batch, hb)


def _mlstm_kernel(*refs, t_len, hb, has_init, emit_state):
    it = iter(refs)
    q_ref, k_ref, v_ref, og_ref, gr_ref, ng_ref = (next(it) for _ in range(6))
    if has_init:
        c0_ref, n0_ref, m0_ref = (next(it) for _ in range(3))
    o_ref = next(it)
    if emit_state:
        c_ref, n_ref, m_ref = (next(it) for _ in range(3))
    hacc, q16_s, v1_s, qk_s, ld_s, bb_s, lm_s, lw_s, sc_s, st_s = (next(it) for _ in range(10))
    nc = t_len // CHUNK

    hacc[...] = jnp.zeros_like(hacc)
    ones = jnp.ones((t_len, D_HEAD), BF16)
    for h in range(hb):
        hc = _head_cols(h)
        q16_s[:, hc] = (q_ref[:, hc] * (D_HEAD ** -0.5)).astype(BF16)
        v1_s[:, h * 2 * D_HEAD:(h * 2 + 1) * D_HEAD] = v_ref[:, hc].astype(BF16)
        v1_s[:, (h * 2 + 1) * D_HEAD:(h * 2 + 2) * D_HEAD] = ones
    lo_i, _, up_i, _ = _chunk_masks()
    selector = _gate_selector()

    def slot(h, d, c):
        return (h * 2 + d) * nc + c

    def intra(chunks):
        for h in range(hb):
            for c in chunks:
                sl, hc = _chunk_rows(c), _head_cols(h)
                qk_s[h, sl, :] = _dot_nt(q16_s[sl, hc], k_ref[sl, hc].astype(BF16))
                gr4 = gr_ref[h, c]
                col4 = _gate_columns(gr4, selector)
                for d in range(2):
                    i = h * 2 + d
                    b_col, i_col = _gate_column(col4, 2 + d), _gate_column(col4, d)
                    b_last = _last_row(b_col, d)
                    lwe = b_last - b_col + i_col
                    log_d = jnp.where(lo_i if d == 0 else up_i,
                                      b_col[:, :CHUNK] - gr4[2 + d:3 + d, :] + gr4[d:d + 1, :], -jnp.inf)
                    ld_s[i, sl, :] = log_d
                    lm_s[i, sl, :] = jnp.broadcast_to(jnp.max(log_d, axis=-1, keepdims=True), (CHUNK, D_HEAD))
                    bb_s[i, sl, :] = b_col
                    lw_s[i, sl, :] = lwe
                    sc_s[slot(h, d, c), 0:1, :] = b_last
                    sc_s[slot(h, d, c), 1:2, :] = jnp.max(lwe, axis=0, keepdims=True)

    _for_chunk_groups(nc, min(nc, 4), intra)

    chains = [(h, d) for h in range(hb) for d in range(2)]

    def body(step, carry):
        cs = (step, nc - 1 - step)
        nrow, m = zip(*carry)
        sls = [_chunk_rows(cs[d]) for h, d in chains]
        idx = [h * 2 + d for h, d in chains]
        v1 = [v1_s[sl, h * 2 * D_HEAD:(h * 2 + 2) * D_HEAD] for (h, d), sl in zip(chains, sls)]
        log_last = [sc_s[slot(h, d, cs[d]), 0:1, :] + mi for (h, d), mi in zip(chains, m)]
        m_new = [jnp.maximum(ll, sc_s[slot(h, d, cs[d]), 1:2, :]) for (h, d), ll in zip(chains, log_last)]
        dec = [jnp.exp(ll - mn) for ll, mn in zip(log_last, m_new)]
        kw = [k_ref[sl, _head_cols(h)] * jnp.exp(lw_s[i, sl, :] - mn)
              for (h, d), sl, i, mn in zip(chains, sls, idx, m_new)]
        upd = [_dot_tn(a.astype(BF16), b) for a, b in zip(kw, v1)]
        qc = [_dot(q16_s[sl, _head_cols(h)], st_s[i].astype(BF16)) for (h, d), sl, i in zip(chains, sls, idx)]
        log_inter = [bb_s[i, sl, :] + mi for i, sl, mi in zip(idx, sls, m)]
        mt = [jnp.maximum(li, lm_s[i, sl, :]) for li, i, sl in zip(log_inter, idx, sls)]
        s_inter = [jnp.exp(li - t) for li, t in zip(log_inter, mt)]
        p = [(jnp.exp(ld_s[i, sl, :] - t[:, :CHUNK]) * qk_s[h, sl, :]).astype(BF16)
             for (h, d), i, sl, t in zip(chains, idx, sls, mt)]
        pv = [_dot(a, b) for a, b in zip(p, v1)]
        for (h, d), sl, si, qci, pvi, t in zip(chains, sls, s_inter, qc, pv, mt):
            num = si * qci[:, :D_HEAD] + pvi[:, :D_HEAD]
            den = si * qci[:, D_HEAD:] + pvi[:, D_HEAD:]
            hacc[sl, _head_cols(h)] += num / jnp.maximum(jnp.abs(den), jnp.exp(-t))
        for i, dc, u in zip(idx, dec, upd):
            st_s[i] = jnp.concatenate([dc, dc], axis=1) * st_s[i] + u
        return tuple((dc * nr + jnp.sum(kwi, axis=0, keepdims=True), mn)
                     for dc, nr, kwi, mn in zip(dec, nrow, kw, m_new))

    for h, d in chains:
        if has_init:
            st_s[h * 2 + d, :, :D_HEAD] = c0_ref[0, d, h]
            st_s[h * 2 + d, :, D_HEAD:] = jnp.transpose(jnp.broadcast_to(n0_ref[0, h, d:d + 1, :], (D_HEAD, D_HEAD)))
        else:
            st_s[h * 2 + d] = jnp.zeros((D_HEAD, 2 * D_HEAD), F32)
    if has_init:
        init = tuple((n0_ref[0, h, d:d + 1, :], m0_ref[0, h, d:d + 1, :]) for h, d in chains)
    else:
        init = tuple((jnp.zeros((1, D_HEAD), F32), jnp.zeros((1, D_HEAD), F32)) for _ in chains)
    fin = lax.fori_loop(0, nc, body, init)
    if emit_state:
        for (h, d), (nr, m) in zip(chains, fin):
            c_ref[0, d, h] = st_s[h * 2 + d, :, :D_HEAD]
            n_ref[0, h, d:d + 1, :] = nr
            m_ref[0, h, d:d + 1, :] = m
    for h in range(hb):
        hc = _head_cols(h)
        o_ref[:, hc] = (_rms(hacc[:, hc]) * ng_ref[...] * jax.nn.sigmoid(og_ref[:, hc])).astype(o_ref.dtype)


def _mlstm_scan(main, grow, norm_g, init, *, batch, t_len, row0, hb, emit_state):
    nc = t_len // CHUNK
    has_init = init is not None
    in_specs = _scan_specs(t_len, row0, hb, 4)
    in_specs.append(pl.BlockSpec((1, D_HEAD), lambda b, g: (0, 0)))
    args = [main, main, main, main, grow, norm_g.reshape(1, D_HEAD)]
    c_spec = pl.BlockSpec((1, 2, hb, D_HEAD, D_HEAD), lambda b, g: (b, 0, g, 0, 0))
    v_spec = pl.BlockSpec((1, hb, 2, D_HEAD), lambda b, g: (b, g, 0, 0))
    if has_init:
        in_specs += [c_spec, v_spec, v_spec]
        args += list(init)
    out_specs = [pl.BlockSpec((t_len, hb * D_HEAD), lambda b, g: (b, g))]
    out_shape = [jax.ShapeDtypeStruct((batch * t_len, N_HEADS * D_HEAD), BF16)]
    if emit_state:
        out_specs += [c_spec, v_spec, v_spec]
        out_shape += [jax.ShapeDtypeStruct((batch, 2, N_HEADS, D_HEAD, D_HEAD), F32),
                      jax.ShapeDtypeStruct((batch, N_HEADS, 2, D_HEAD), F32),
                      jax.ShapeDtypeStruct((batch, N_HEADS, 2, D_HEAD), F32)]
    scratch = [pltpu.VMEM((t_len, hb * D_HEAD), F32),
               pltpu.VMEM((t_len, hb * D_HEAD), BF16),
               pltpu.VMEM((t_len, hb * 2 * D_HEAD), BF16),
               pltpu.VMEM((hb, t_len, CHUNK), F32),
               pltpu.VMEM((2 * hb, t_len, CHUNK), F32),
               pltpu.VMEM((2 * hb, t_len, D_HEAD), F32),
               pltpu.VMEM((2 * hb, t_len, D_HEAD), F32),
               pltpu.VMEM((2 * hb, t_len, D_HEAD), F32),
               pltpu.VMEM((2 * hb * nc, 8, D_HEAD), F32),
               pltpu.VMEM((2 * hb, D_HEAD, 2 * D_HEAD), F32)]
    kern = functools.partial(_mlstm_kernel, t_len=t_len, hb=hb, has_init=has_init, emit_state=emit_state)
    return _scan_call(kern, "mlstm_scan_t%d" % t_len, args, in_specs, out_specs, out_shape, scratch, batch, hb)


def _top2_of4(v):
    best, i1 = v[0], jnp.zeros(v[0].shape, jnp.int32)
    for j in range(1, 4):
        take = v[j] > best
        i1 = jnp.where(take, j, i1)
        best = jnp.where(take, v[j], best)
    best2, i2 = None, None
    for j in range(4):
        vj = jnp.where(i1 == j, -jnp.inf, v[j])
        if best2 is None:
            best2, i2 = vj, jnp.zeros(v[0].shape, jnp.int32)
        else:
            take = vj > best2
            i2 = jnp.where(take, j, i2)
            best2 = jnp.where(take, vj, best2)
    return i1, i2


def _pick(rows, idx):
    out = rows[0]
    for j in range(1, len(rows)):
        out = jnp.where(idx == j, rows[j], out)
    return out


def _outproj_kernel(*refs, ctx_tiles, x_ctx_tiles):
    x, (oc_ref, od_ref, w_ref, g1_ref, ng_ref, sc_ref, sh_ref, rw_ref, rb_ref, xo_ref, hn_ref, slot_ref, wt_ref,
        gran_ref, cnt_ref, run_ref) = _token_tile(refs, x_ctx_tiles)
    _outproj_body(x, oc_ref, od_ref, w_ref, g1_ref, ng_ref, sc_ref, sh_ref, rw_ref, rb_ref, xo_ref, hn_ref, slot_ref,
                  wt_ref, gran_ref, cnt_ref, run_ref, ctx_tiles)


def _outproj_body(x, oc_ref, od_ref, w_ref, g1_ref, ng_ref, sc_ref, sh_ref, rw_ref, rb_ref, xo_ref, hn_ref,
                  slot_ref, wt_ref, gran_ref, cnt_ref, run_ref, ctx_tiles):
    @pl.when(pl.program_id(0) == 0)
    def _():
        run_ref[...] = jnp.zeros_like(run_ref)

    o = jnp.where(pl.program_id(0) < ctx_tiles, oc_ref[...], od_ref[...])
    xn = x + g1_ref[0] * _dot(o, w_ref[...])
    xo_ref[...] = xn
    hn = _rms(xn) * ng_ref[...]
    hn = hn * (1.0 + sc_ref[0]) + sh_ref[0]
    hn_ref[...] = hn.astype(BF16)

    tm = TM_PROJ
    tiles = range(hn.shape[0] // tm)
    lanes = [slice(j * tm, (j + 1) * tm) for j in tiles]
    logits = [_dot_nt(rw_ref[...], hn[ln, :], HIGHEST) for ln in lanes]

    def top2(lg):
        ex = jnp.exp(lg - jnp.max(lg, axis=0, keepdims=True))
        probs = ex / jnp.sum(ex, axis=0, keepdims=True)
        sel = probs + rb_ref[...]
        sel_rows = [sel[e:e + 1, :] for e in range(N_EXPERTS)]
        prob_rows = [probs[e:e + 1, :] for e in range(N_EXPERTS)]
        scores = []
        for g in range(N_GROUPS):
            r = sel_rows[4 * g:4 * g + 4]
            a, b = jnp.maximum(r[0], r[1]), jnp.minimum(r[0], r[1])
            c, d = jnp.maximum(r[2], r[3]), jnp.minimum(r[2], r[3])
            scores.append(jnp.maximum(a, c) + jnp.maximum(jnp.minimum(a, c), jnp.maximum(b, d)))
        best, grp = scores[0], jnp.zeros(scores[0].shape, jnp.int32)
        for g in range(1, N_GROUPS):
            take = scores[g] > best
            grp = jnp.where(take, g, grp)
            best = jnp.where(take, scores[g], best)
        sel_in = [_pick([sel_rows[4 * g + j] for g in range(N_GROUPS)], grp) for j in range(4)]
        prob_in = [_pick([prob_rows[4 * g + j] for g in range(N_GROUPS)], grp) for j in range(4)]
        i1, i2 = _top2_of4(sel_in)
        w1, w2 = _pick(prob_in, i1), _pick(prob_in, i2)
        tot = w1 + w2
        return grp * 4 + i1, grp * 4 + i2, w1 / tot, w2 / tot

    picks = [top2(lg) for lg in logits]
    eidx = lax.broadcasted_iota(jnp.int32, (N_EXPERTS, tm), 0)
    onehot = [jnp.where(eidx == e1, 1.0, 0.0) + jnp.where(eidx == e2, 1.0, 0.0) for e1, e2, _, _ in picks]
    r = lax.broadcasted_iota(jnp.int32, (tm, tm), 0)
    c = lax.broadcasted_iota(jnp.int32, (tm, tm), 1)
    earlier = jnp.where(r < c, 1.0, 0.0).astype(BF16)
    ahead = [_dot(oh.astype(BF16), earlier) for oh in onehot]
    n_gran = [jnp.broadcast_to(jnp.floor((jnp.sum(oh, axis=1, keepdims=True) + (GRANULE - 1)) * (1.0 / GRANULE)),
                               (N_EXPERTS, D_HEAD)) for oh in onehot]
    er = lax.broadcasted_iota(jnp.int32, (N_EXPERTS, N_EXPERTS), 0)
    ec = lax.broadcasted_iota(jnp.int32, (N_EXPERTS, N_EXPERTS), 1)
    before = jnp.where(ec < er, 1.0, 0.0).astype(BF16)
    first = [_dot(before, ng.astype(BF16)) for ng in n_gran]
    g = lax.broadcasted_iota(jnp.int32, (N_EXPERTS, D_HEAD), 1).astype(F32)
    ge = lax.broadcasted_iota(jnp.int32, (N_EXPERTS, D_HEAD), 0).astype(F32)
    run = run_ref[...]
    for j in tiles:
        e1, e2, w1, w2 = picks[j]
        wt_ref[:, lanes[j]] = jnp.concatenate([w1, w2], axis=0)
        row0 = GRANULE * first[j][:, 0:1] + ahead[j]
        slot = [jnp.sum(jnp.where(eidx == e, row0, 0.0), axis=0, keepdims=True) for e in (e1, e2)]
        slot_ref[:, lanes[j]] = jnp.concatenate(slot, axis=0).astype(jnp.int32)
        owner = jnp.sum(jnp.where(g >= first[j] + n_gran[j], 1.0, 0.0), axis=0, keepdims=True)
        index = g[0:1, :] + jnp.sum(jnp.where(ge == owner, run - first[j], 0.0), axis=0, keepdims=True)
        gran_ref[j] = jnp.concatenate([owner, index], axis=0).astype(jnp.int32)
        run = run + n_gran[j]
    run_ref[...] = run
    cnt_ref[...] = run


def _outproj_route(x, n, o_ctx, o_dec, w_out, mods, mod_base, norm_g, router_w, router_bias, n_ctx, t_dec):
    d = w_out.shape[1]
    tm = TM_OUT
    ctx_tiles = n_ctx // tm
    mrow = _mod_row_of_tile(tm, n_ctx, t_dec)
    mod_map = lambda k: (lambda i: (mod_base + mrow(i) * 6 + k, 0, 0))
    const2 = lambda i: (0, 0)
    x_specs, x_args, x_ctx_tiles = _token_specs(x, tm, n_ctx)
    return pl.pallas_call(
        functools.partial(_outproj_kernel, ctx_tiles=ctx_tiles, x_ctx_tiles=x_ctx_tiles),
        grid=(n // tm,),
        in_specs=x_specs + [
            pl.BlockSpec((tm, d), lambda i: (jnp.minimum(i, ctx_tiles - 1), 0)),
            pl.BlockSpec((tm, d), lambda i: (jnp.maximum(i - ctx_tiles, 0), 0)),
            pl.BlockSpec((d, d), const2),
            pl.BlockSpec((1, 1, d), mod_map(2)),
            pl.BlockSpec((1, d), const2),
            pl.BlockSpec((1, 1, d), mod_map(4)),
            pl.BlockSpec((1, 1, d), mod_map(3)),
            pl.BlockSpec((N_EXPERTS, d), const2),
            pl.BlockSpec((N_EXPERTS, 1), const2),
        ],
        out_specs=[
            pl.BlockSpec((tm, d), lambda i: (i, 0)),
            pl.BlockSpec((tm, d), lambda i: (i, 0)),
            pl.BlockSpec((2, tm), lambda i: (0, i)),
            pl.BlockSpec((2, tm), lambda i: (0, i)),
            pl.BlockSpec((tm // TM_PROJ, 2, D_HEAD), lambda i: (i, 0, 0)),
            pl.BlockSpec((N_EXPERTS, D_HEAD), const2),
        ],
        out_shape=[
            jax.ShapeDtypeStruct((n, d), F32),
            jax.ShapeDtypeStruct((n, d), BF16),
            jax.ShapeDtypeStruct((2, n), jnp.int32),
            jax.ShapeDtypeStruct((2, n), F32),
            jax.ShapeDtypeStruct((n // TM_PROJ, 2, D_HEAD), jnp.int32),
            jax.ShapeDtypeStruct((N_EXPERTS, D_HEAD), F32),
        ],
        scratch_shapes=[pltpu.VMEM((N_EXPERTS, D_HEAD), F32)],
        compiler_params=pltpu.CompilerParams(dimension_semantics=("arbitrary",), vmem_limit_bytes=VMEM_LIMIT),
        name="outproj_route",
    )(*x_args, o_ctx, o_dec, w_out.astype(BF16), mods, norm_g.reshape(1, d), mods, mods, router_w.T, router_bias.reshape(N_EXPERTS, 1))


def _route_tables(gran, counts, n_tiles):
    tile_gran = TM_MOE // GRANULE
    cnt = counts[:, 0].astype(jnp.int32)
    padded = (cnt + tile_gran - 1) // tile_gran * tile_gran
    ends = jnp.cumsum(padded)
    offs = ends - padded
    owner, index = gran[:, 0, :], gran[:, 1, :]
    first_gran = index + sum(jnp.where(owner == e, offs[e], 0) for e in range(N_EXPERTS))
    used = owner < N_EXPERTS
    parity = (jnp.arange(gran.shape[0], dtype=jnp.int32) % 2)[:, None]
    g = jnp.minimum(jnp.arange(D_HEAD, dtype=jnp.int32), TILE_GRANULES - 1)[None, :]
    dump = n_tiles * TM_MOE + (parity * TILE_GRANULES + g) * GRANULE
    rows = (jnp.where(used, first_gran * GRANULE, dump).astype(jnp.int32),
            jnp.where(used, first_gran * GRANULE, 0).astype(jnp.int32))
    tile_g0 = jnp.arange(n_tiles, dtype=jnp.int32) * tile_gran
    tile_expert = jnp.minimum(jnp.sum(tile_g0[:, None] >= ends[None, :], axis=1), N_EXPERTS - 1)
    tile_valid = jnp.sum((tile_g0[:, None] >= offs[None, :]) & (tile_g0[:, None] < (offs + cnt)[None, :]), axis=1)
    j = jnp.arange(tile_gran, dtype=jnp.int32)[None, :]
    tails = jnp.where(j < (padded - cnt)[:, None], (offs + cnt)[:, None] + j, -1)
    tails = jnp.where(tails >= 0, tails * GRANULE, -1).astype(jnp.int32).reshape(1, 1, N_EXPERTS * tile_gran)
    return rows, tails, tile_expert.astype(jnp.int32), tile_valid.astype(jnp.int32)


def _granule_copy(src, src_row, dst, dst_row, sem):
    return pltpu.make_async_copy(src.at[pl.ds(pl.multiple_of(src_row, GRANULE), GRANULE)],
                                 dst.at[pl.ds(pl.multiple_of(dst_row, GRANULE), GRANULE)], sem)


def _for_granules(rows_ref, fn, count=TILE_GRANULES):
    def body(g, carry):
        row = rows_ref[0, 0, g]

        @pl.when(row >= 0)
        def _():
            fn(g, row)
        return carry
    lax.fori_loop(0, count, body, 0, unroll=8)


def _pair_matrix(slot, first_value, second_value):
    p = lax.broadcasted_iota(jnp.int32, (TILE_GRANULES * GRANULE, slot.shape[1]), 0)
    return jnp.where(p == slot[0:1, :], first_value, 0.0) + jnp.where(p == slot[1:2, :], second_value, 0.0)


def _for_all_granules(rows_ref, fn):
    def body(g, carry):
        fn(g, rows_ref[0, 0, g])
        return carry
    lax.fori_loop(0, TILE_GRANULES, body, 0, unroll=8)


def _dispatch_kernel(rows_ref, tail_ref, valid_ref, hn_ref, slot_ref, xs_ref, buf, zero, sems):
    i = pl.program_id(0)
    cur = i % 2
    buf[cur] = _dot(_pair_matrix(slot_ref[...], 1.0, 1.0).astype(BF16), hn_ref[...])
    _for_all_granules(rows_ref, lambda g, row: _granule_copy(buf.at[cur], g * GRANULE, xs_ref, row,
                                                             sems.at[cur]).start())

    def retire(b):
        pltpu.make_async_copy(buf.at[b], xs_ref.at[pl.ds(0, TILE_GRANULES * GRANULE)], sems.at[b]).wait()

    @pl.when(i > 0)
    def _():
        retire(1 - cur)

    @pl.when(i == pl.num_programs(0) - 1)
    def _():
        retire(cur)
        zero[...] = jnp.zeros_like(zero)
        n_tail = tail_ref.shape[2]
        _for_granules(tail_ref, lambda g, row: _granule_copy(zero, 0, xs_ref, row, sems.at[0]).start(), n_tail)
        _for_granules(tail_ref, lambda g, row: _granule_copy(zero, 0, xs_ref, row, sems.at[0]).wait(), n_tail)

        def fill_tiles(act):
            def body(t, carry):
                @pl.when(valid_ref[0, 0, t] == 0)
                def _():
                    act(pltpu.make_async_copy(zero, xs_ref.at[pl.ds(pl.multiple_of(t * TM_MOE, TM_MOE), TM_MOE)],
                                              sems.at[1]))
                return carry
            lax.fori_loop(0, valid_ref.shape[2], body, 0)

        fill_tiles(lambda c: c.start())
        fill_tiles(lambda c: c.wait())


def _dispatch(hn, slot, rows, tails, tile_valid, n_rows):
    n, d = hn.shape
    tm = TM_PROJ
    rows = rows.reshape(n // tm, 1, D_HEAD)
    return pl.pallas_call(
        _dispatch_kernel,
        grid=(n // tm,),
        in_specs=[
            pl.BlockSpec((1, 1, D_HEAD), lambda i: (i, 0, 0), memory_space=pltpu.SMEM),
            pl.BlockSpec(memory_space=pltpu.SMEM),
            pl.BlockSpec(memory_space=pltpu.SMEM),
            pl.BlockSpec((tm, d), lambda i: (i, 0)),
            pl.BlockSpec((2, tm), lambda i: (0, i)),
        ],
        out_specs=pl.BlockSpec(memory_space=pl.ANY),
        out_shape=jax.ShapeDtypeStruct((n_rows, d), F32),
        scratch_shapes=[pltpu.VMEM((2, TILE_GRANULES * GRANULE, d), F32), pltpu.VMEM((TM_MOE, d), F32),
                        pltpu.SemaphoreType.DMA((2,))],
        compiler_params=pltpu.CompilerParams(dimension_semantics=("arbitrary",), vmem_limit_bytes=VMEM_LIMIT),
        name="moe_dispatch",
    )(rows, tails, tile_valid.reshape(1, 1, -1), hn, slot)


def _expert_kernel(te_ref, tv_ref, xs_ref, wg_ref, wu_ref, wd_ref, ys_ref, wg16, wu16, wd16):
    i = pl.program_id(0)
    fresh = jnp.logical_or(i == 0, te_ref[i] != te_ref[jnp.maximum(i - 1, 0)])

    @pl.when(fresh)
    def _():
        wg16[...] = wg_ref[0, 0].astype(BF16)
        wu16[...] = wu_ref[0, 0].astype(BF16)
        wd16[...] = wd_ref[0, 0].astype(BF16)

    @pl.when(tv_ref[i] != 0)
    def _():
        x = xs_ref[...].astype(BF16)
        hid = _silu(_dot(x, wg16[...])) * _dot(x, wu16[...])
        ys_ref[...] = _dot(hid.astype(BF16), wd16[...])

    @pl.when(tv_ref[i] == 0)
    def _():
        ys_ref[...] = jnp.zeros_like(ys_ref)


def _experts(xs, n_tiles, tile_expert, tile_valid, w_gate, w_up, w_down, layer):
    d = xs.shape[1]
    tm = TM_MOE
    n_rows = n_tiles * tm
    w_in_spec = pl.BlockSpec((1, 1, d, D_FF), lambda i, te, tv: (layer, te[i], 0, 0))
    return pl.pallas_call(
        _expert_kernel,
        grid_spec=pltpu.PrefetchScalarGridSpec(
            num_scalar_prefetch=2,
            grid=(n_rows // tm,),
            in_specs=[
                pl.BlockSpec((tm, d), lambda i, te, tv: (jnp.where(tv[i] != 0, i, 0), 0)),
                w_in_spec,
                w_in_spec,
                pl.BlockSpec((1, 1, D_FF, d), lambda i, te, tv: (layer, te[i], 0, 0)),
            ],
            out_specs=pl.BlockSpec((tm, d), lambda i, te, tv: (i, 0)),
            scratch_shapes=[pltpu.VMEM((d, D_FF), BF16), pltpu.VMEM((d, D_FF), BF16), pltpu.VMEM((D_FF, d), BF16)],
        ),
        out_shape=jax.ShapeDtypeStruct((n_rows, d), F32),
        compiler_params=pltpu.CompilerParams(dimension_semantics=("arbitrary",), vmem_limit_bytes=VMEM_LIMIT),
        name="moe_experts",
    )(tile_expert, tile_valid, xs, w_gate, w_up, w_down)


def _combine_kernel(rows_ref, next_ref, x_ref, slot_ref, wt_ref, g2_ref, fg_ref, ys_ref, *rest, ctx_tiles):
    *o_refs, buf, sems = rest
    i = pl.program_id(0)
    cur = i % 2

    def fetch(table_ref, b):
        _for_all_granules(table_ref, lambda g, row: _granule_copy(ys_ref, row, buf.at[b], g * GRANULE,
                                                                  sems.at[b]).start())

    @pl.when(i == 0)
    def _():
        fetch(rows_ref, 0)

    @pl.when(i + 1 < pl.num_programs(0))
    def _():
        fetch(next_ref, 1 - cur)

    pltpu.make_async_copy(ys_ref.at[pl.ds(0, TILE_GRANULES * GRANULE)], buf.at[cur], sems.at[cur]).wait()
    w = wt_ref[...]
    mix = _pair_matrix(slot_ref[...], w[0:1, :], w[1:2, :]).astype(BF16)
    xn = x_ref[...] + g2_ref[0] * _dot_tn(mix, buf[cur].astype(BF16))
    if ctx_tiles is None:
        o_refs[0][...] = xn
        return
    xn = _rms(xn) * fg_ref[...]

    @pl.when(pl.program_id(0) < ctx_tiles)
    def _():
        o_refs[0][...] = xn

    @pl.when(pl.program_id(0) >= ctx_tiles)
    def _():
        o_refs[1][...] = xn


def _combine(x, ys, slot, wts, rows, mods, mod_base, final_g, final_norm, n_ctx, t_dec):
    n, d = x.shape
    tm = TM_PROJ
    mrow = _mod_row_of_tile(tm, n_ctx, t_dec)
    ctx_tiles = n_ctx // tm if final_norm else None
    if final_norm:
        out_specs = [pl.BlockSpec((tm, d), lambda i: (jnp.minimum(i, ctx_tiles - 1), 0)),
                     pl.BlockSpec((tm, d), lambda i: (jnp.maximum(i - ctx_tiles, 0), 0))]
        out_shape = [jax.ShapeDtypeStruct((n_ctx, d), F32), jax.ShapeDtypeStruct((n - n_ctx, d), F32)]
    else:
        out_specs = pl.BlockSpec((tm, d), lambda i: (i, 0))
        out_shape = jax.ShapeDtypeStruct((n, d), F32)
    rows = rows.reshape(n // tm, 1, D_HEAD)
    last = n // tm - 1
    return pl.pallas_call(
        functools.partial(_combine_kernel, ctx_tiles=ctx_tiles),
        grid=(n // tm,),
        in_specs=[
            pl.BlockSpec((1, 1, D_HEAD), lambda i: (i, 0, 0), memory_space=pltpu.SMEM),
            pl.BlockSpec((1, 1, D_HEAD), lambda i: (jnp.minimum(i + 1, last), 0, 0), memory_space=pltpu.SMEM),
            pl.BlockSpec((tm, d), lambda i: (i, 0)),
            pl.BlockSpec((2, tm), lambda i: (0, i)),
            pl.BlockSpec((2, tm), lambda i: (0, i)),
            pl.BlockSpec((1, 1, d), lambda i: (mod_base + mrow(i) * 6 + 5, 0, 0)),
            pl.BlockSpec((1, d), lambda i: (0, 0)),
            pl.BlockSpec(memory_space=pl.ANY),
        ],
        out_specs=out_specs,
        out_shape=out_shape,
        scratch_shapes=[pltpu.VMEM((2, TILE_GRANULES * GRANULE, d), F32), pltpu.SemaphoreType.DMA((2,))],
        compiler_params=pltpu.CompilerParams(dimension_semantics=("arbitrary",), vmem_limit_bytes=VMEM_LIMIT),
        name="moe_combine",
    )(rows, rows, x, slot, wts, mods, final_g.reshape(1, d), ys)


def _moe(x, hn, slot, wts, gran, counts, w_gate, w_up, w_down, layer, mods, mod_base, final_g, final_norm, n_ctx,
         t_dec):
    n = x.shape[0]
    n_rows = 2 * n + (n // TM_PROJ) * N_EXPERTS * GRANULE + N_EXPERTS * TM_MOE
    n_tiles = -(-n_rows // TM_MOE)
    (rows_put, rows_get), tails, tile_expert, tile_valid = _route_tables(gran, counts, n_tiles)
    dump_tiles = -(-2 * TILE_GRANULES * GRANULE // TM_MOE)
    fill = jnp.concatenate([tile_valid, jnp.zeros((dump_tiles,), jnp.int32)])
    xs = _dispatch(hn, slot, rows_put, tails, fill, (n_tiles + dump_tiles) * TM_MOE)
    ys = _experts(xs, n_tiles, tile_expert, tile_valid, w_gate, w_up, w_down, layer)
    return _combine(x, ys, slot, wts, rows_get, mods, mod_base, final_g, final_norm, n_ctx, t_dec)


def _grid_pos_embed(n_tokens):
    rows = n_tokens // GRID_W
    r = jnp.repeat(jnp.arange(rows, dtype=F32), GRID_W)
    col = jnp.tile(jnp.arange(GRID_W, dtype=F32), rows)
    quarter = D_MODEL // 4
    freq = jnp.exp(jnp.arange(quarter, dtype=F32) * (-math.log(POS_BASE) / quarter))

    def axis_embed(pos):
        a = pos[:, None] * freq[None, :]
        return jnp.concatenate([jnp.sin(a), jnp.cos(a)], axis=-1)

    return jnp.concatenate([axis_embed(r), axis_embed(col)], axis=-1)


def _split_in_weights(w_in):
    pm = 4 * N_HEADS * D_HEAD
    wg = w_in[:, pm:].reshape(-1, 2, 2, N_HEADS)
    return w_in[:, :pm], wg.transpose(0, 3, 1, 2).reshape(-1, 4 * N_HEADS)


def _head_params(first, second):
    return jnp.stack([first, second], axis=0).transpose(2, 0, 1).reshape(-1).astype(F32)


def _heads_per_step(t_len, chains_per_head_chunk=2):
    return max(2, min(N_HEADS, SCAN_CHAINS // (chains_per_head_chunk * (t_len // CHUNK))))


def kernel(x_prompt, x_sample, state_gdn_S, state_mlstm_C, state_mlstm_n, state_mlstm_m, c, c_ctx, ada_w, ada_b,
           norm1_g, norm2_g, gdn_w_in, gdn_conv_w, gdn_a_log, gdn_dt_bias, gdn_norm_g, gdn_w_out, mlstm_w_in,
           mlstm_gate_b, mlstm_norm_g, mlstm_w_out, router_w, router_bias, exp_w_gate, exp_w_up, exp_w_down,
           final_norm_g):
    bp, tp, d = x_prompt.shape
    bs, ts, _ = x_sample.shape
    n_ctx = bp * tp
    depth = ada_w.shape[0]
    assert n_ctx % ts == 0 and ts % max(TM_IN, TM_OUT) == 0 and tp % TM_PROJ == 0 and bs + 1 <= N_MOD_ROWS
    assert TILE_GRANULES <= D_HEAD

    pos = _grid_pos_embed(ts).astype(F32)
    n = n_ctx + bs * ts
    x = (x_prompt.reshape(n_ctx, d), x_sample.reshape(bs * ts, d), pos)

    conds = jnp.concatenate([c_ctx[None, :], c, jnp.zeros((N_MOD_ROWS - 1 - bs, d), F32)], axis=0)
    mods = _ada_mods(conds, ada_w, ada_b).reshape(depth * N_MOD_ROWS * 6, 1, d)

    zeros_dh = jnp.zeros_like(gdn_a_log[0])
    ctx = dict(batch=bp, t_len=tp, row0=0, emit_state=True)
    dec = dict(batch=bs, t_len=ts, row0=n_ctx, emit_state=False)
    gdn_hb = dict(ctx=_heads_per_step(tp), dec=_heads_per_step(ts))
    mlstm_hb = dict(ctx=_heads_per_step(tp, 4), dec=_heads_per_step(ts, 4))
    outs = {}
    for layer in range(depth):
        j = layer // 2
        mod_base = layer * N_MOD_ROWS * 6
        if layer % 2 == 0:
            w_main, w_gate = _split_in_weights(gdn_w_in[j])
            bias = _head_params(gdn_dt_bias[j], zeros_dh)
            mul = _head_params(-jnp.exp(gdn_a_log[j].astype(F32)), zeros_dh)
            main, grow = _inproj(x, n, mods, mod_base, norm1_g[layer], w_main, w_gate, bias, mul, "gdn", n_ctx, ts)
            o_ctx, s_new = _gdn_scan(main, grow, gdn_conv_w[j], gdn_norm_g[j], None, hb=gdn_hb["ctx"], **ctx)
            (o_dec,) = _gdn_scan(main, grow, gdn_conv_w[j], gdn_norm_g[j], state_gdn_S[:, j].astype(F32),
                                 hb=gdn_hb["dec"], **dec)
            outs.setdefault("gdn", []).append(s_new)
            w_out = gdn_w_out[j]
        else:
            w_main, w_gate = _split_in_weights(mlstm_w_in[j])
            bias = _head_params(mlstm_gate_b[j, 0], mlstm_gate_b[j, 1])
            main, grow = _inproj(x, n, mods, mod_base, norm1_g[layer], w_main, w_gate, bias, jnp.zeros_like(bias),
                                 "mlstm", n_ctx, ts)
            o_ctx, c_new, n_new, m_new = _mlstm_scan(main, grow, mlstm_norm_g[j], None, hb=mlstm_hb["ctx"], **ctx)
            init = (state_mlstm_C[:, j].astype(F32),
                    state_mlstm_n[:, j].astype(F32).transpose(0, 2, 1, 3),
                    jnp.broadcast_to(state_mlstm_m[:, j].astype(F32).transpose(0, 2, 1)[..., None],
                                     (bs, N_HEADS, 2, D_HEAD)))
            (o_dec,) = _mlstm_scan(main, grow, mlstm_norm_g[j], init, hb=mlstm_hb["dec"], **dec)
            outs.setdefault("mC", []).append(c_new)
            outs.setdefault("mn", []).append(n_new.transpose(0, 2, 1, 3))
            outs.setdefault("mm", []).append(m_new[..., 0].transpose(0, 2, 1))
            w_out = mlstm_w_out[j]
        x, hn, slot, wts, gran, counts = _outproj_route(x, n, o_ctx, o_dec, w_out, mods, mod_base, norm2_g[layer],
                                                        router_w, router_bias, n_ctx, ts)
        x = _moe(x, hn, slot, wts, gran, counts, exp_w_gate, exp_w_up, exp_w_down, layer, mods, mod_base,
                 final_norm_g, layer == depth - 1, n_ctx, ts)

    y_prompt = x[0].reshape(bp, tp, d)
    y_sample = x[1].reshape(bs, ts, d)
    return (y_prompt, y_sample, jnp.stack(outs["gdn"], axis=1), jnp.stack(outs["mC"], axis=1),
            jnp.stack(outs["mn"], axis=1), jnp.stack(outs["mm"], axis=1))
```

```python
import functools
import math

import jax
import jax.numpy as jnp
from jax import lax
from jax.experimental import pallas as pl
from jax.experimental.pallas import tpu as pltpu

F32 = jnp.float32
BF16 = jnp.bfloat16
HIGHEST = lax.Precision.HIGHEST

D_MODEL = 1024
N_HEADS = 8
D_HEAD = 128
CHUNK = 64
N_EXPERTS = 16
N_GROUPS = 4
EXP_PER_GROUP = 4
D_FF = 512
EPS = 1e-6
GRID_W = 64
POS_BASE = 10000.0
N_MOD_ROWS = 8
VMEM_LIMIT = 56 * 1024 * 1024

TM_PROJ = 256
TM_IN = 512
TM_OUT = 512
TM_MOE = 512
GRANULE = 16
TILE_GRANULES = 2 * TM_PROJ // GRANULE + N_EXPERTS
SCAN_CHAINS = 64


def _silu(x):
    return x * jax.nn.sigmoid(x)


def _softplus(x):
    return jnp.maximum(x, 0.0) + jnp.log(1.0 + jnp.exp(-jnp.abs(x)))


def _dot(a, b, precision=None):
    return jnp.dot(a, b, preferred_element_type=F32, precision=precision)


def _dot_nt(a, b, precision=None):
    return lax.dot_general(a, b, (((1,), (1,)), ((), ())), preferred_element_type=F32, precision=precision)


def _dot_tn(a, b, precision=None):
    return lax.dot_general(a, b, (((0,), (0,)), ((), ())), preferred_element_type=F32, precision=precision)


def _rms(x):
    return x * lax.rsqrt(jnp.mean(x * x, axis=-1, keepdims=True) + EPS)


def _ada_kernel(c_ref, w_ref, b_ref, o_ref):
    cs = _silu(c_ref[...]).astype(BF16)
    o_ref[0] = _dot(cs, w_ref[0].astype(BF16)) + b_ref[0]


def _ada_mods(conds, ada_w, ada_b):
    depth, d, n6 = ada_w.shape
    tn = 1536
    return pl.pallas_call(
        _ada_kernel,
        grid=(depth, n6 // tn),
        in_specs=[
            pl.BlockSpec((N_MOD_ROWS, d), lambda l, j: (0, 0)),
            pl.BlockSpec((1, d, tn), lambda l, j: (l, 0, j)),
            pl.BlockSpec((1, 1, tn), lambda l, j: (l, 0, j)),
        ],
        out_specs=pl.BlockSpec((1, N_MOD_ROWS, tn), lambda l, j: (l, 0, j)),
        out_shape=jax.ShapeDtypeStruct((depth, N_MOD_ROWS, n6), F32),
        compiler_params=pltpu.CompilerParams(dimension_semantics=("arbitrary", "arbitrary"),
                                             vmem_limit_bytes=VMEM_LIMIT),
        name="ada_mods",
    )(conds, ada_w, ada_b.reshape(depth, 1, n6))


def _split3(x):
    hi = x.astype(BF16)
    r1 = x - hi.astype(F32)
    mid = r1.astype(BF16)
    lo = (r1 - mid.astype(F32)).astype(BF16)
    return hi, mid, lo


def _token_specs(x, tm, n_ctx):
    if not isinstance(x, tuple):
        return [pl.BlockSpec((tm, x.shape[1]), lambda i, *_: (i, 0))], [x], None
    xc, xd, pos = x
    d = xc.shape[1]
    ctx_tiles, pos_tiles = n_ctx // tm, pos.shape[0] // tm
    specs = [pl.BlockSpec((tm, d), lambda i, *_: (jnp.minimum(i, ctx_tiles - 1), 0)),
             pl.BlockSpec((tm, d), lambda i, *_: (jnp.maximum(i - ctx_tiles, 0), 0)),
             pl.BlockSpec((tm, d), lambda i, *_: (jnp.maximum(i - ctx_tiles, 0) % pos_tiles, 0))]
    return specs, [xc, xd, pos], ctx_tiles


def _token_tile(refs, ctx_tiles):
    if ctx_tiles is None:
        return refs[0][...], refs[1:]
    xc_ref, xd_ref, pos_ref = refs[:3]
    return jnp.where(pl.program_id(0) < ctx_tiles, xc_ref[...], xd_ref[...] + pos_ref[...]), refs[3:]


def _inproj_kernel(*refs, kind, x_ctx_tiles):
    x, (ng_ref, sc_ref, sh_ref, w_ref, wgt_ref, gb_ref, gm_ref, main_ref, gr_ref) = _token_tile(refs, x_ctx_tiles)
    hn = _rms(x) * ng_ref[...]
    hn = hn * (1.0 + sc_ref[0]) + sh_ref[0]
    hb = hn.astype(BF16)
    main_ref[...] = _dot(hb, w_ref[...])

    pre = _dot_nt(wgt_ref[...], hb) + gb_ref[...]
    row = lax.broadcasted_iota(jnp.int32, pre.shape, 0)
    first = (row % 4) < 2
    if kind == "gdn":
        act = jnp.where(first, gm_ref[...] * _softplus(pre), jax.nn.sigmoid(pre))
        scanned_kind = 0
    else:
        act = jnp.where(first, pre, -_softplus(-pre))
        scanned_kind = 1

    tm = x.shape[0]
    r = lax.broadcasted_iota(jnp.int32, (tm, tm), 0)
    c = lax.broadcasted_iota(jnp.int32, (tm, tm), 1)
    same = (r // CHUNK) == (c // CHUNK)
    before = jnp.where(same, jnp.where(r <= c, 1.0, 0.0), 0.0).astype(BF16)
    after = jnp.where(same, jnp.where(r >= c, 1.0, 0.0), 0.0).astype(BF16)
    parts = _split3(act)
    fwd = sum(_dot(p, before) for p in parts)
    bwd = sum(_dot(p, after) for p in parts)
    gt = jnp.where((row % 4) // 2 == scanned_kind, jnp.where(row % 2 == 0, fwd, bwd), act)
    for h in range(N_HEADS):
        for ch in range(tm // CHUNK):
            gr_ref[h, ch] = gt[4 * h:4 * h + 4, ch * CHUNK:(ch + 1) * CHUNK]


def _mod_row_of_tile(tm, n_ctx, t_dec):
    def f(i):
        r = i * tm
        return jnp.where(r < n_ctx, 0, 1 + (r - n_ctx) // t_dec)
    return f


def _inproj(x, n, mods, mod_base, norm_g, w_main, w_gate, gate_bias, gate_mul, kind, n_ctx, t_dec):
    d = w_main.shape[0]
    tm = TM_IN
    pm = w_main.shape[1]
    ng = w_gate.shape[1]
    mrow = _mod_row_of_tile(tm, n_ctx, t_dec)
    sc_map = lambda i: (mod_base + mrow(i) * 6 + 1, 0, 0)
    sh_map = lambda i: (mod_base + mrow(i) * 6 + 0, 0, 0)
    const2 = lambda i: (0, 0)
    x_specs, x_args, x_ctx_tiles = _token_specs(x, tm, n_ctx)
    return pl.pallas_call(
        functools.partial(_inproj_kernel, kind=kind, x_ctx_tiles=x_ctx_tiles),
        grid=(n // tm,),
        in_specs=x_specs + [
            pl.BlockSpec((1, d), const2),
            pl.BlockSpec((1, 1, d), sc_map),
            pl.BlockSpec((1, 1, d), sh_map),
            pl.BlockSpec((d, pm), const2),
            pl.BlockSpec((ng, d), const2),
            pl.BlockSpec((ng, 1), const2),
            pl.BlockSpec((ng, 1), const2),
        ],
        out_specs=[
            pl.BlockSpec((tm, pm), lambda i: (i, 0)),
            pl.BlockSpec((N_HEADS, tm // CHUNK, 4, CHUNK), lambda i: (0, i, 0, 0)),
        ],
        out_shape=[
            jax.ShapeDtypeStruct((n, pm), F32),
            jax.ShapeDtypeStruct((N_HEADS, n // CHUNK, 4, CHUNK), F32),
        ],
        compiler_params=pltpu.CompilerParams(dimension_semantics=("arbitrary",), vmem_limit_bytes=VMEM_LIMIT),
        name="inproj_" + kind,
    )(*x_args, norm_g.reshape(1, d), mods, mods, w_main.astype(BF16), w_gate.T.astype(BF16),
      gate_bias.reshape(ng, 1), gate_mul.reshape(ng, 1))


def _chunk_masks():
    r = lax.broadcasted_iota(jnp.int32, (CHUNK, CHUNK), 0)
    c = lax.broadcasted_iota(jnp.int32, (CHUNK, CHUNK), 1)
    return r >= c, r > c, r <= c, r < c


def _chunk_rows(c):
    if isinstance(c, int):
        return pl.ds(c * CHUNK, CHUNK)
    return pl.ds(pl.multiple_of(c * CHUNK, CHUNK), CHUNK)


def _head_cols(h):
    return slice(h * D_HEAD, (h + 1) * D_HEAD)


def _last_row(x, d):
    return x[CHUNK - 1:CHUNK, :] if d == 0 else x[0:1, :]


def _gate_selector():
    j = lax.broadcasted_iota(jnp.int32, (32, 4 * D_HEAD), 0) % 8
    lane = lax.broadcasted_iota(jnp.int32, (32, 4 * D_HEAD), 1)
    return jnp.where(lane // D_HEAD == j, 1.0, 0.0).astype(BF16)


def _gate_columns(gr4, selector):
    a = jnp.concatenate([gr4, jnp.zeros_like(gr4)], axis=0)
    hi = a.astype(BF16).astype(F32)
    mid = (a - hi).astype(BF16).astype(F32)
    lo = a - hi - mid
    parts = jnp.concatenate([hi, mid, lo, jnp.zeros_like(a)], axis=0).astype(BF16)
    return _dot_tn(parts, selector)


def _gate_column(cols, j):
    return cols[:, j * D_HEAD:(j + 1) * D_HEAD]


def _unit_tri_solve(a_list, rhs_list):
    pack = 4
    groups = [a_list[i:i + pack] for i in range(0, len(a_list), pack)]
    r = lax.broadcasted_iota(jnp.int32, (CHUNK, pack * CHUNK), 0)
    c = lax.broadcasted_iota(jnp.int32, (CHUNK, pack * CHUNK), 1) % CHUNK
    same = (r // 16) == (c // 16)
    eye = jnp.where(r == c, 1.0, 0.0)
    br = lax.broadcasted_iota(jnp.int32, (pack * CHUNK, pack * CHUNK), 0) // CHUNK
    bc = lax.broadcasted_iota(jnp.int32, (pack * CHUNK, pack * CHUNK), 1) // CHUNK
    zero16 = jnp.zeros((), BF16)

    def blockdiag(y):
        return jnp.where(br == bc, jnp.concatenate([y.astype(BF16)] * pack, axis=0), zero16)

    def mm(xs, ys):
        return [_dot(x.astype(BF16), blockdiag(y)) for x, y in zip(xs, ys)]

    ap = [jnp.concatenate(g, axis=1) for g in groups]
    d = [jnp.where(same, a, 0.0) for a in ap]
    t = [eye - di for di in d]
    for _ in range(3):
        d = mm(d, d)
        t = [ti + pi for ti, pi in zip(t, mm(t, d))]
    b = mm(t, [jnp.where(same, 0.0, a) for a in ap])
    b2 = mm(b, b)
    y = [ti - pi for ti, pi in zip(t, mm(b, t))]
    inv = [(yi + pi).astype(BF16) for yi, pi in zip(y, mm(b2, y))]

    def placed(rhs, j):
        z = jnp.zeros(rhs.shape, BF16)
        return jnp.concatenate([z] * j + [rhs.astype(BF16)] + [z] * (pack - 1 - j), axis=0)

    return [_dot(inv[i // pack], placed(rhs, i % pack)) for i, rhs in enumerate(rhs_list)]


def _for_chunk_groups(nc, group, fn):
    if nc == group:
        fn(list(range(nc)))
        return

    def body(g, carry):
        fn([g * group + j for j in range(group)])
        return carry
    lax.fori_loop(0, nc // group, body, 0)


def _scan_specs(t_len, row0, hb, n_proj):
    rb = row0 // t_len
    nc = t_len // CHUNK
    ngrp = N_HEADS // hb
    specs = [pl.BlockSpec((t_len, hb * D_HEAD), functools.partial(lambda b, g, j: (rb + b, j * ngrp + g), j=j))
             for j in range(n_proj)]
    specs.append(pl.BlockSpec((hb, nc, 4, CHUNK), lambda b, g: (g, rb + b, 0, 0)))
    return specs


def _scan_call(kern, name, args, in_specs, out_specs, out_shape, scratch, batch, hb):
    return pl.pallas_call(
        kern,
        grid=(batch, N_HEADS // hb),
        in_specs=in_specs,
        out_specs=out_specs,
        out_shape=out_shape,
        scratch_shapes=scratch,
        compiler_params=pltpu.CompilerParams(dimension_semantics=("arbitrary", "arbitrary"),
                                             vmem_limit_bytes=VMEM_LIMIT),
        name=name,
    )(*args)


def _gdn_kernel(*refs, t_len, hb, has_init, emit_state):
    it = iter(refs)
    q_ref, k_ref, v_ref, z_ref, gr_ref, cwq_ref, cwk_ref, cwv_ref, ng_ref = (next(it) for _ in range(9))
    s0_ref = next(it) if has_init else None
    o_ref = next(it)
    s_ref = next(it) if emit_state else None
    qs, ks, vs, oacc, qp_s, op_s, km_s, nm_s, ge_s, st_s = (next(it) for _ in range(10))
    nc = t_len // CHUNK

    rows = lax.broadcasted_iota(jnp.int32, (t_len, 1), 0)

    def conv_silu(x, cw):
        xm = jnp.where(rows == 0, 0.0, pltpu.roll(x, 1, axis=0))
        xp = jnp.where(rows == t_len - 1, 0.0, pltpu.roll(x, t_len - 1, axis=0))
        return _silu(xm * cw[0:1] + x * cw[1:2] + xp * cw[2:3])

    def l2n(x):
        return x * lax.rsqrt(jnp.sum(x * x, axis=-1, keepdims=True) + EPS)

    for h in range(hb):
        hc = _head_cols(h)
        qs[:, hc] = l2n(conv_silu(q_ref[:, hc], cwq_ref[:, hc])) * (D_HEAD ** -0.5)
        ks[:, hc] = l2n(conv_silu(k_ref[:, hc], cwk_ref[:, hc]))
        vs[:, hc] = conv_silu(v_ref[:, hc], cwv_ref[:, hc])
    oacc[...] = jnp.zeros_like(oacc)

    lo_i, lo_s, up_i, up_s = _chunk_masks()
    selector = _gate_selector()

    def slot(h, d, c):
        return (h * 2 + d) * nc + c

    def intra(chunks):
        items = [(h, c, d) for h in range(hb) for c in chunks for d in range(2)]
        qkk, cols = [], []
        for h in range(hb):
            for c in chunks:
                q16, k16 = qs[_chunk_rows(c), _head_cols(h)].astype(BF16), ks[_chunk_rows(c), _head_cols(h)].astype(BF16)
                qkk.append(_dot_nt(jnp.concatenate([q16, k16], axis=0), k16))
                cols.append(_gate_columns(gr_ref[h, c], selector))
        a_list, rhs_list, keep = [], [], []
        for n_item, (h, c, d) in enumerate(items):
            sl, hc = _chunk_rows(c), _head_cols(h)
            q, k, v = qs[sl, hc], ks[sl, hc], vs[sl, hc]
            gr4 = gr_ref[h, c]
            col4, qk, kk = cols[n_item // 2], qkk[n_item // 2][:CHUNK], qkk[n_item // 2][CHUNK:]
            g_col, beta = _gate_column(col4, d), _gate_column(col4, 2 + d)
            incl, strict = (lo_i, lo_s) if d == 0 else (up_i, up_s)
            decay = jnp.exp(jnp.where(incl, g_col[:, :CHUNK] - gr4[d:d + 1, :], -jnp.inf))
            a_list.append(jnp.where(strict, beta[:, :CHUNK] * kk * decay, 0.0))
            kb = k * beta
            e_g = jnp.exp(g_col)
            rhs_list.append(jnp.concatenate([v * beta, kb * e_g], axis=1))
            qk16 = jnp.where(incl, qk * decay, 0.0).astype(BF16)
            g_last = _last_row(g_col, d)
            ge_s[slot(h, d, c)] = jnp.broadcast_to(jnp.exp(g_last), (8, D_HEAD))
            keep.append((qk16, q * e_g, (k * jnp.exp(g_last - g_col)).astype(BF16)))
        sols = [s.astype(BF16) for s in _unit_tri_solve(a_list, rhs_list)]
        qw = [_dot(kp[0], s) for kp, s in zip(keep, sols)]
        kw = [_dot_tn(kp[2], s) for kp, s in zip(keep, sols)]
        for (h, c, d), kp, qwi, kwi in zip(items, keep, qw, kw):
            i = h * 2 + d
            sl = _chunk_rows(c)
            op_s[i, sl, :] = qwi[:, :D_HEAD]
            qp_s[i, sl, :] = (kp[1] - qwi[:, D_HEAD:]).astype(BF16)
            nm_s[slot(h, d, c)] = kwi[:, :D_HEAD]
            km_s[slot(h, d, c)] = (-kwi[:, D_HEAD:]).astype(BF16)

    _for_chunk_groups(nc, min(nc, max(1, SCAN_CHAINS // (2 * hb))), intra)

    chains = [(h, d) for h in range(hb) for d in range(2)]

    def inter_body(i, carry):
        cs = (i, nc - 1 - i)
        s16 = [st_s[h * 2 + d].astype(BF16) for h, d in chains]
        outs = [_dot(qp_s[h * 2 + d, _chunk_rows(cs[d]), :], s) for (h, d), s in zip(chains, s16)]
        upds = [_dot(km_s[slot(h, d, cs[d])], s) for (h, d), s in zip(chains, s16)]
        for (h, d), o in zip(chains, outs):
            oacc[_chunk_rows(cs[d]), _head_cols(h)] += o + op_s[h * 2 + d, _chunk_rows(cs[d]), :]
        for (h, d), u in zip(chains, upds):
            st_s[h * 2 + d] = ge_s[slot(h, d, cs[d])][0:1, :] * st_s[h * 2 + d] + u + nm_s[slot(h, d, cs[d])]
        return carry

    for h, d in chains:
        st_s[h * 2 + d] = s0_ref[0, d, h] if has_init else jnp.zeros((D_HEAD, D_HEAD), F32)
    lax.fori_loop(0, nc, inter_body, 0)
    if emit_state:
        for h, d in chains:
            s_ref[0, d, h] = st_s[h * 2 + d]
    for h in range(hb):
        hc = _head_cols(h)
        o_ref[:, hc] = (_rms(oacc[:, hc]) * ng_ref[...] * _silu(z_ref[:, hc])).astype(o_ref.dtype)


def _gdn_scan(main, grow, conv_w, norm_g, s0, *, batch, t_len, row0, hb, emit_state):
    nc = t_len // CHUNK
    ngrp = N_HEADS // hb
    has_init = s0 is not None
    in_specs = _scan_specs(t_len, row0, hb, 4)
    in_specs += [pl.BlockSpec((3, hb * D_HEAD), functools.partial(lambda b, g, j: (0, j * ngrp + g), j=j))
                 for j in range(3)]
    in_specs.append(pl.BlockSpec((1, D_HEAD), lambda b, g: (0, 0)))
    args = [main, main, main, main, grow, conv_w, conv_w, conv_w, norm_g.reshape(1, D_HEAD)]
    state_spec = pl.BlockSpec((1, 2, hb, D_HEAD, D_HEAD), lambda b, g: (b, 0, g, 0, 0))
    if has_init:
        in_specs.append(state_spec)
        args.append(s0)
    out_specs = [pl.BlockSpec((t_len, hb * D_HEAD), lambda b, g: (b, g))]
    out_shape = [jax.ShapeDtypeStruct((batch * t_len, N_HEADS * D_HEAD), BF16)]
    if emit_state:
        out_specs.append(state_spec)
        out_shape.append(jax.ShapeDtypeStruct((batch, 2, N_HEADS, D_HEAD, D_HEAD), F32))
    scratch = ([pltpu.VMEM((t_len, hb * D_HEAD), F32) for _ in range(4)]
               + [pltpu.VMEM((2 * hb, t_len, D_HEAD), BF16),
                  pltpu.VMEM((2 * hb, t_len, D_HEAD), F32),
                  pltpu.VMEM((2 * hb * nc, D_HEAD, D_HEAD), BF16),
                  pltpu.VMEM((2 * hb * nc, D_HEAD, D_HEAD), F32),
                  pltpu.VMEM((2 * hb * nc, 8, D_HEAD), F32),
                  pltpu.VMEM((2 * hb, D_HEAD, D_HEAD), F32)])
    kern = functools.partial(_gdn_kernel, t_len=t_len, hb=hb, has_init=has_init, emit_state=emit_state)
    return _scan_call(kern, "gdn_scan_t%d" % t_len, args, in_specs, out_specs, out_shape, scratch, batch, hb)


def _mlstm_kernel(*refs, t_len, hb, has_init, emit_state):
    it = iter(refs)
    q_ref, k_ref, v_ref, og_ref, gr_ref, ng_ref = (next(it) for _ in range(6))
    if has_init:
        c0_ref, n0_ref, m0_ref = (next(it) for _ in range(3))
    o_ref = next(it)
    if emit_state:
        c_ref, n_ref, m_ref = (next(it) for _ in range(3))
    hacc, q16_s, v1_s, qk_s, ld_s, bb_s, lm_s, lw_s, sc_s, st_s = (next(it) for _ in range(10))
    nc = t_len // CHUNK

    hacc[...] = jnp.zeros_like(hacc)
    ones = jnp.ones((t_len, D_HEAD), BF16)
    for h in range(hb):
        hc = _head_cols(h)
        q16_s[:, hc] = (q_ref[:, hc] * (D_HEAD ** -0.5)).astype(BF16)
        v1_s[:, h * 2 * D_HEAD:(h * 2 + 1) * D_HEAD] = v_ref[:, hc].astype(BF16)
        v1_s[:, (h * 2 + 1) * D_HEAD:(h * 2 + 2) * D_HEAD] = ones
    lo_i, _, up_i, _ = _chunk_masks()
    selector = _gate_selector()

    def slot(h, d, c):
        return (h * 2 + d) * nc + c

    def intra(chunks):
        for h in range(hb):
            for c in chunks:
                sl, hc = _chunk_rows(c), _head_cols(h)
                qk_s[h, sl, :] = _dot_nt(q16_s[sl, hc], k_ref[sl, hc].astype(BF16))
                gr4 = gr_ref[h, c]
                col4 = _gate_columns(gr4, selector)
                for d in range(2):
                    i = h * 2 + d
                    b_col, i_col = _gate_column(col4, 2 + d), _gate_column(col4, d)
                    b_last = _last_row(b_col, d)
                    lwe = b_last - b_col + i_col
                    log_d = jnp.where(lo_i if d == 0 else up_i,
                                      b_col[:, :CHUNK] - gr4[2 + d:3 + d, :] + gr4[d:d + 1, :], -jnp.inf)
                    ld_s[i, sl, :] = log_d
                    lm_s[i, sl, :] = jnp.broadcast_to(jnp.max(log_d, axis=-1, keepdims=True), (CHUNK, D_HEAD))
                    bb_s[i, sl, :] = b_col
                    lw_s[i, sl, :] = lwe
                    sc_s[slot(h, d, c), 0:1, :] = b_last
                    sc_s[slot(h, d, c), 1:2, :] = jnp.max(lwe, axis=0, keepdims=True)

    _for_chunk_groups(nc, min(nc, 4), intra)

    chains = [(h, d) for h in range(hb) for d in range(2)]

    def body(step, carry):
        cs = (step, nc - 1 - step)
        nrow, m = zip(*carry)
        sls = [_chunk_rows(cs[d]) for h, d in chains]
        idx = [h * 2 + d for h, d in chains]
        v1 = [v1_s[sl, h * 2 * D_HEAD:(h * 2 + 2) * D_HEAD] for (h, d), sl in zip(chains, sls)]
        log_last = [sc_s[slot(h, d, cs[d]), 0:1, :] + mi for (h, d), mi in zip(chains, m)]
        m_new = [jnp.maximum(ll, sc_s[slot(h, d, cs[d]), 1:2, :]) for (h, d), ll in zip(chains, log_last)]
        dec = [jnp.exp(ll - mn) for ll, mn in zip(log_last, m_new)]
        kw = [k_ref[sl, _head_cols(h)] * jnp.exp(lw_s[i, sl, :] - mn)
              for (h, d), sl, i, mn in zip(chains, sls, idx, m_new)]
        upd = [_dot_tn(a.astype(BF16), b) for a, b in zip(kw, v1)]
        qc = [_dot(q16_s[sl, _head_cols(h)], st_s[i].astype(BF16)) for (h, d), sl, i in zip(chains, sls, idx)]
        log_inter = [bb_s[i, sl, :] + mi for i, sl, mi in zip(idx, sls, m)]
        mt = [jnp.maximum(li, lm_s[i, sl, :]) for li, i, sl in zip(log_inter, idx, sls)]
        s_inter = [jnp.exp(li - t) for li, t in zip(log_inter, mt)]
        p = [(jnp.exp(ld_s[i, sl, :] - t[:, :CHUNK]) * qk_s[h, sl, :]).astype(BF16)
             for (h, d), i, sl, t in zip(chains, idx, sls, mt)]
        pv = [_dot(a, b) for a, b in zip(p, v1)]
        for (h, d), sl, si, qci, pvi, t in zip(chains, sls, s_inter, qc, pv, mt):
            num = si * qci[:, :D_HEAD] + pvi[:, :D_HEAD]
            den = si * qci[:, D_HEAD:] + pvi[:, D_HEAD:]
            hacc[sl, _head_cols(h)] += num / jnp.maximum(jnp.abs(den), jnp.exp(-t))
        for i, dc, u in zip(idx, dec, upd):
            st_s[i] = jnp.concatenate([dc, dc], axis=1) * st_s[i] + u
        return tuple((dc * nr + jnp.sum(kwi, axis=0, keepdims=True), mn)
                     for dc, nr, kwi, mn in zip(dec, nrow, kw, m_new))

    for h, d in chains:
        if has_init:
            st_s[h * 2 + d, :, :D_HEAD] = c0_ref[0, d, h]
            st_s[h * 2 + d, :, D_HEAD:] = jnp.transpose(jnp.broadcast_to(n0_ref[0, h, d:d + 1, :], (D_HEAD, D_HEAD)))
        else:
            st_s[h * 2 + d] = jnp.zeros((D_HEAD, 2 * D_HEAD), F32)
    if has_init:
        init = tuple((n0_ref[0, h, d:d + 1, :], m0_ref[0, h, d:d + 1, :]) for h, d in chains)
    else:
        init = tuple((jnp.zeros((1, D_HEAD), F32), jnp.zeros((1, D_HEAD), F32)) for _ in chains)
    fin = lax.fori_loop(0, nc, body, init)
    if emit_state:
        for (h, d), (nr, m) in zip(chains, fin):
            c_ref[0, d, h] = st_s[h * 2 + d, :, :D_HEAD]
            n_ref[0, h, d:d + 1, :] = nr
            m_ref[0, h, d:d + 1, :] = m
    for h in range(hb):
        hc = _head_cols(h)
        o_ref[:, hc] = (_rms(hacc[:, hc]) * ng_ref[...] * jax.nn.sigmoid(og_ref[:, hc])).astype(o_ref.dtype)


def _mlstm_scan(main, grow, norm_g, init, *, batch, t_len, row0, hb, emit_state):
    nc = t_len // CHUNK
    has_init = init is not None
    in_specs = _scan_specs(t_len, row0, hb, 4)
    in_specs.append(pl.BlockSpec((1, D_HEAD), lambda b, g: (0, 0)))
    args = [main, main, main, main, grow, norm_g.reshape(1, D_HEAD)]
    c_spec = pl.BlockSpec((1, 2, hb, D_HEAD, D_HEAD), lambda b, g: (b, 0, g, 0, 0))
    v_spec = pl.BlockSpec((1, hb, 2, D_HEAD), lambda b, g: (b, g, 0, 0))
    if has_init:
        in_specs += [c_spec, v_spec, v_spec]
        args += list(init)
    out_specs = [pl.BlockSpec((t_len, hb * D_HEAD), lambda b, g: (b, g))]
    out_shape = [jax.ShapeDtypeStruct((batch * t_len, N_HEADS * D_HEAD), BF16)]
    if emit_state:
        out_specs += [c_spec, v_spec, v_spec]
        out_shape += [jax.ShapeDtypeStruct((batch, 2, N_HEADS, D_HEAD, D_HEAD), F32),
                      jax.ShapeDtypeStruct((batch, N_HEADS, 2, D_HEAD), F32),
                      jax.ShapeDtypeStruct((batch, N_HEADS, 2, D_HEAD), F32)]
    scratch = [pltpu.VMEM((t_len, hb * D_HEAD), F32),
               pltpu.VMEM((t_len, hb * D_HEAD), BF16),
               pltpu.VMEM((t_len, hb * 2 * D_HEAD), BF16),
               pltpu.VMEM((hb, t_len, CHUNK), F32),
               pltpu.VMEM((2 * hb, t_len, CHUNK), F32),
               pltpu.VMEM((2 * hb, t_len, D_HEAD), F32),
               pltpu.VMEM((2 * hb, t_len, D_HEAD), F32),
               pltpu.VMEM((2 * hb, t_len, D_HEAD), F32),
               pltpu.VMEM((2 * hb * nc, 8, D_HEAD), F32),
               pltpu.VMEM((2 * hb, D_HEAD, 2 * D_HEAD), F32)]
    kern = functools.partial(_mlstm_kernel, t_len=t_len, hb=hb, has_init=has_init, emit_state=emit_state)
    return _scan_call(kern, "mlstm_scan_t%d" % t_len, args, in_specs, out_specs, out_shape, scratch, batch, hb)


def _top2_of4(v):
    best, i1 = v[0], jnp.zeros(v[0].shape, jnp.int32)
    for j in range(1, 4):
        take = v[j] > best
        i1 = jnp.where(take, j, i1)
        best = jnp.where(take, v[j], best)
    best2, i2 = None, None
    for j in range(4):
        vj = jnp.where(i1 == j, -jnp.inf, v[j])
        if best2 is None:
            best2, i2 = vj, jnp.zeros(v[0].shape, jnp.int32)
        else:
            take = vj > best2
            i2 = jnp.where(take, j, i2)
            best2 = jnp.where(take, vj, best2)
    return i1, i2


def _pick(rows, idx):
    out = rows[0]
    for j in range(1, len(rows)):
        out = jnp.where(idx == j, rows[j], out)
    return out


def _outproj_kernel(*refs, ctx_tiles, x_ctx_tiles):
    x, (oc_ref, od_ref, w_ref, g1_ref, ng_ref, sc_ref, sh_ref, rw_ref, rb_ref, xo_ref, hn_ref, slot_ref, wt_ref,
        gran_ref, cnt_ref, run_ref) = _token_tile(refs, x_ctx_tiles)
    _outproj_body(x, oc_ref, od_ref, w_ref, g1_ref, ng_ref, sc_ref, sh_ref, rw_ref, rb_ref, xo_ref, hn_ref, slot_ref,
                  wt_ref, gran_ref, cnt_ref, run_ref, ctx_tiles)


def _outproj_body(x, oc_ref, od_ref, w_ref, g1_ref, ng_ref, sc_ref, sh_ref, rw_ref, rb_ref, xo_ref, hn_ref,
                  slot_ref, wt_ref, gran_ref, cnt_ref, run_ref, ctx_tiles):
    @pl.when(pl.program_id(0) == 0)
    def _():
        run_ref[...] = jnp.zeros_like(run_ref)

    o = jnp.where(pl.program_id(0) < ctx_tiles, oc_ref[...], od_ref[...])
    xn = x + g1_ref[0] * _dot(o, w_ref[...])
    xo_ref[...] = xn
    hn = _rms(xn) * ng_ref[...]
    hn = hn * (1.0 + sc_ref[0]) + sh_ref[0]
    hn_ref[...] = hn.astype(BF16)

    tm = TM_PROJ
    tiles = range(hn.shape[0] // tm)
    lanes = [slice(j * tm, (j + 1) * tm) for j in tiles]
    logits = [_dot_nt(rw_ref[...], hn[ln, :], HIGHEST) for ln in lanes]

    def top2(lg):
        ex = jnp.exp(lg - jnp.max(lg, axis=0, keepdims=True))
        probs = ex / jnp.sum(ex, axis=0, keepdims=True)
        sel = probs + rb_ref[...]
        sel_rows = [sel[e:e + 1, :] for e in range(N_EXPERTS)]
        prob_rows = [probs[e:e + 1, :] for e in range(N_EXPERTS)]
        scores = []
        for g in range(N_GROUPS):
            r = sel_rows[4 * g:4 * g + 4]
            a, b = jnp.maximum(r[0], r[1]), jnp.minimum(r[0], r[1])
            c, d = jnp.maximum(r[2], r[3]), jnp.minimum(r[2], r[3])
            scores.append(jnp.maximum(a, c) + jnp.maximum(jnp.minimum(a, c), jnp.maximum(b, d)))
        best, grp = scores[0], jnp.zeros(scores[0].shape, jnp.int32)
        for g in range(1, N_GROUPS):
            take = scores[g] > best
            grp = jnp.where(take, g, grp)
            best = jnp.where(take, scores[g], best)
        sel_in = [_pick([sel_rows[4 * g + j] for g in range(N_GROUPS)], grp) for j in range(4)]
        prob_in = [_pick([prob_rows[4 * g + j] for g in range(N_GROUPS)], grp) for j in range(4)]
        i1, i2 = _top2_of4(sel_in)
        w1, w2 = _pick(prob_in, i1), _pick(prob_in, i2)
        tot = w1 + w2
        return grp * 4 + i1, grp * 4 + i2, w1 / tot, w2 / tot

    picks = [top2(lg) for lg in logits]
    eidx = lax.broadcasted_iota(jnp.int32, (N_EXPERTS, tm), 0)
    onehot = [jnp.where(eidx == e1, 1.0, 0.0) + jnp.where(eidx == e2, 1.0, 0.0) for e1, e2, _, _ in picks]
    r = lax.broadcasted_iota(jnp.int32, (tm, tm), 0)
    c = lax.broadcasted_iota(jnp.int32, (tm, tm), 1)
    earlier = jnp.where(r < c, 1.0, 0.0).astype(BF16)
    ahead = [_dot(oh.astype(BF16), earlier) for oh in onehot]
    n_gran = [jnp.broadcast_to(jnp.floor((jnp.sum(oh, axis=1, keepdims=True) + (GRANULE - 1)) * (1.0 / GRANULE)),
                               (N_EXPERTS, D_HEAD)) for oh in onehot]
    er = lax.broadcasted_iota(jnp.int32, (N_EXPERTS, N_EXPERTS), 0)
    ec = lax.broadcasted_iota(jnp.int32, (N_EXPERTS, N_EXPERTS), 1)
    before = jnp.where(ec < er, 1.0, 0.0).astype(BF16)
    first = [_dot(before, ng.astype(BF16)) for ng in n_gran]
    g = lax.broadcasted_iota(jnp.int32, (N_EXPERTS, D_HEAD), 1).astype(F32)
    ge = lax.broadcasted_iota(jnp.int32, (N_EXPERTS, D_HEAD), 0).astype(F32)
    run = run_ref[...]
    for j in tiles:
        e1, e2, w1, w2 = picks[j]
        wt_ref[:, lanes[j]] = jnp.concatenate([w1, w2], axis=0)
        row0 = GRANULE * first[j][:, 0:1] + ahead[j]
        slot = [jnp.sum(jnp.where(eidx == e, row0, 0.0), axis=0, keepdims=True) for e in (e1, e2)]
        slot_ref[:, lanes[j]] = jnp.concatenate(slot, axis=0).astype(jnp.int32)
        owner = jnp.sum(jnp.where(g >= first[j] + n_gran[j], 1.0, 0.0), axis=0, keepdims=True)
        index = g[0:1, :] + jnp.sum(jnp.where(ge == owner, run - first[j], 0.0), axis=0, keepdims=True)
        gran_ref[j] = jnp.concatenate([owner, index], axis=0).astype(jnp.int32)
        run = run + n_gran[j]
    run_ref[...] = run
    cnt_ref[...] = run


def _outproj_route(x, n, o_ctx, o_dec, w_out, mods, mod_base, norm_g, router_w, router_bias, n_ctx, t_dec):
    d = w_out.shape[1]
    tm = TM_OUT
    ctx_tiles = n_ctx // tm
    mrow = _mod_row_of_tile(tm, n_ctx, t_dec)
    mod_map = lambda k: (lambda i: (mod_base + mrow(i) * 6 + k, 0, 0))
    const2 = lambda i: (0, 0)
    x_specs, x_args, x_ctx_tiles = _token_specs(x, tm, n_ctx)
    return pl.pallas_call(
        functools.partial(_outproj_kernel, ctx_tiles=ctx_tiles, x_ctx_tiles=x_ctx_tiles),
        grid=(n // tm,),
        in_specs=x_specs + [
            pl.BlockSpec((tm, d), lambda i: (jnp.minimum(i, ctx_tiles - 1), 0)),
            pl.BlockSpec((tm, d), lambda i: (jnp.maximum(i - ctx_tiles, 0), 0)),
            pl.BlockSpec((d, d), const2),
            pl.BlockSpec((1, 1, d), mod_map(2)),
            pl.BlockSpec((1, d), const2),
            pl.BlockSpec((1, 1, d), mod_map(4)),
            pl.BlockSpec((1, 1, d), mod_map(3)),
            pl.BlockSpec((N_EXPERTS, d), const2),
            pl.BlockSpec((N_EXPERTS, 1), const2),
        ],
        out_specs=[
            pl.BlockSpec((tm, d), lambda i: (i, 0)),
            pl.BlockSpec((tm, d), lambda i: (i, 0)),
            pl.BlockSpec((2, tm), lambda i: (0, i)),
            pl.BlockSpec((2, tm), lambda i: (0, i)),
            pl.BlockSpec((tm // TM_PROJ, 2, D_HEAD), lambda i: (i, 0, 0)),
            pl.BlockSpec((N_EXPERTS, D_HEAD), const2),
        ],
        out_shape=[
            jax.ShapeDtypeStruct((n, d), F32),
            jax.ShapeDtypeStruct((n, d), BF16),
            jax.ShapeDtypeStruct((2, n), jnp.int32),
            jax.ShapeDtypeStruct((2, n), F32),
            jax.ShapeDtypeStruct((n // TM_PROJ, 2, D_HEAD), jnp.int32),
            jax.ShapeDtypeStruct((N_EXPERTS, D_HEAD), F32),
        ],
        scratch_shapes=[pltpu.VMEM((N_EXPERTS, D_HEAD), F32)],
        compiler_params=pltpu.CompilerParams(dimension_semantics=("arbitrary",), vmem_limit_bytes=VMEM_LIMIT),
        name="outproj_route",
    )(*x_args, o_ctx, o_dec, w_out.astype(BF16), mods, norm_g.reshape(1, d), mods, mods, router_w.T, router_bias.reshape(N_EXPERTS, 1))


def _route_tables(gran, counts, n_tiles):
    tile_gran = TM_MOE // GRANULE
    cnt = counts[:, 0].astype(jnp.int32)
    padded = (cnt + tile_gran - 1) // tile_gran * tile_gran
    ends = jnp.cumsum(padded)
    offs = ends - padded
    owner, index = gran[:, 0, :], gran[:, 1, :]
    first_gran = index + sum(jnp.where(owner == e, offs[e], 0) for e in range(N_EXPERTS))
    rows = jnp.where(owner < N_EXPERTS, first_gran * GRANULE, -1).astype(jnp.int32)
    tile_g0 = jnp.arange(n_tiles, dtype=jnp.int32) * tile_gran
    tile_expert = jnp.minimum(jnp.sum(tile_g0[:, None] >= ends[None, :], axis=1), N_EXPERTS - 1)
    tile_valid = jnp.sum((tile_g0[:, None] >= offs[None, :]) & (tile_g0[:, None] < (offs + cnt)[None, :]), axis=1)
    j = jnp.arange(tile_gran, dtype=jnp.int32)[None, :]
    tails = jnp.where(j < (padded - cnt)[:, None], (offs + cnt)[:, None] + j, -1)
    tails = jnp.where(tails >= 0, tails * GRANULE, -1).astype(jnp.int32).reshape(1, 1, N_EXPERTS * tile_gran)
    return rows, tails, tile_expert.astype(jnp.int32), tile_valid.astype(jnp.int32)


def _granule_copy(src, src_row, dst, dst_row, sem):
    return pltpu.make_async_copy(src.at[pl.ds(pl.multiple_of(src_row, GRANULE), GRANULE)],
                                 dst.at[pl.ds(pl.multiple_of(dst_row, GRANULE), GRANULE)], sem)


def _for_granules(rows_ref, fn, count=TILE_GRANULES):
    def body(g, carry):
        row = rows_ref[0, 0, g]

        @pl.when(row >= 0)
        def _():
            fn(g, row)
        return carry
    lax.fori_loop(0, count, body, 0, unroll=8)


def _pair_matrix(slot, first_value, second_value):
    p = lax.broadcasted_iota(jnp.int32, (TILE_GRANULES * GRANULE, slot.shape[1]), 0)
    return jnp.where(p == slot[0:1, :], first_value, 0.0) + jnp.where(p == slot[1:2, :], second_value, 0.0)


def _dispatch_kernel(rows_ref, prev_ref, tail_ref, valid_ref, hn_ref, slot_ref, xs_ref, buf, zero, sems):
    i = pl.program_id(0)
    cur = i % 2
    buf[cur] = _dot(_pair_matrix(slot_ref[...], 1.0, 1.0).astype(BF16), hn_ref[...]).astype(BF16)

    def copies(table_ref, b, act):
        _for_granules(table_ref, lambda g, row: act(_granule_copy(buf.at[b], g * GRANULE, xs_ref, row, sems.at[b])))

    copies(rows_ref, cur, lambda c: c.start())

    @pl.when(i > 0)
    def _():
        copies(prev_ref, 1 - cur, lambda c: c.wait())

    @pl.when(i == pl.num_programs(0) - 1)
    def _():
        copies(rows_ref, cur, lambda c: c.wait())
        zero[...] = jnp.zeros_like(zero)
        n_tail = tail_ref.shape[2]
        _for_granules(tail_ref, lambda g, row: _granule_copy(zero, 0, xs_ref, row, sems.at[0]).start(), n_tail)
        _for_granules(tail_ref, lambda g, row: _granule_copy(zero, 0, xs_ref, row, sems.at[0]).wait(), n_tail)

        def fill_tiles(act):
            def body(t, carry):
                @pl.when(valid_ref[0, 0, t] == 0)
                def _():
                    act(pltpu.make_async_copy(zero, xs_ref.at[pl.ds(pl.multiple_of(t * TM_MOE, TM_MOE), TM_MOE)],
                                              sems.at[1]))
                return carry
            lax.fori_loop(0, valid_ref.shape[2], body, 0)

        fill_tiles(lambda c: c.start())
        fill_tiles(lambda c: c.wait())


def _dispatch(hn, slot, rows, tails, tile_valid, n_rows):
    n, d = hn.shape
    tm = TM_PROJ
    rows = rows.reshape(n // tm, 1, D_HEAD)
    table = lambda index_map: pl.BlockSpec((1, 1, D_HEAD), index_map, memory_space=pltpu.SMEM)
    return pl.pallas_call(
        _dispatch_kernel,
        grid=(n // tm,),
        in_specs=[
            table(lambda i: (i, 0, 0)),
            table(lambda i: (jnp.maximum(i - 1, 0), 0, 0)),
            pl.BlockSpec(memory_space=pltpu.SMEM),
            pl.BlockSpec(memory_space=pltpu.SMEM),
            pl.BlockSpec((tm, d), lambda i: (i, 0)),
            pl.BlockSpec((2, tm), lambda i: (0, i)),
        ],
        out_specs=pl.BlockSpec(memory_space=pl.ANY),
        out_shape=jax.ShapeDtypeStruct((n_rows, d), BF16),
        scratch_shapes=[pltpu.VMEM((2, TILE_GRANULES * GRANULE, d), BF16), pltpu.VMEM((TM_MOE, d), BF16),
                        pltpu.SemaphoreType.DMA((2,))],
        compiler_params=pltpu.CompilerParams(dimension_semantics=("arbitrary",), vmem_limit_bytes=VMEM_LIMIT),
        name="moe_dispatch",
    )(rows, rows, tails, tile_valid.reshape(1, 1, -1), hn, slot)


def _expert_kernel(te_ref, tv_ref, xs_ref, wg_ref, wu_ref, wd_ref, ys_ref, wg16, wu16, wd16):
    i = pl.program_id(0)
    fresh = jnp.logical_or(i == 0, te_ref[i] != te_ref[jnp.maximum(i - 1, 0)])

    @pl.when(fresh)
    def _():
        wg16[...] = wg_ref[0, 0].astype(BF16)
        wu16[...] = wu_ref[0, 0].astype(BF16)
        wd16[...] = wd_ref[0, 0].astype(BF16)

    @pl.when(tv_ref[i] != 0)
    def _():
        x = xs_ref[...]
        hid = _silu(_dot(x, wg16[...])) * _dot(x, wu16[...])
        ys_ref[...] = _dot(hid.astype(BF16), wd16[...]).astype(BF16)

    @pl.when(tv_ref[i] == 0)
    def _():
        ys_ref[...] = jnp.zeros_like(ys_ref)


def _experts(xs, tile_expert, tile_valid, w_gate, w_up, w_down, layer):
    n_rows, d = xs.shape
    tm = TM_MOE
    w_in_spec = pl.BlockSpec((1, 1, d, D_FF), lambda i, te, tv: (layer, te[i], 0, 0))
    return pl.pallas_call(
        _expert_kernel,
        grid_spec=pltpu.PrefetchScalarGridSpec(
            num_scalar_prefetch=2,
            grid=(n_rows // tm,),
            in_specs=[
                pl.BlockSpec((tm, d), lambda i, te, tv: (jnp.where(tv[i] != 0, i, 0), 0)),
                w_in_spec,
                w_in_spec,
                pl.BlockSpec((1, 1, D_FF, d), lambda i, te, tv: (layer, te[i], 0, 0)),
            ],
            out_specs=pl.BlockSpec((tm, d), lambda i, te, tv: (i, 0)),
            scratch_shapes=[pltpu.VMEM((d, D_FF), BF16), pltpu.VMEM((d, D_FF), BF16), pltpu.VMEM((D_FF, d), BF16)],
        ),
        out_shape=jax.ShapeDtypeStruct((n_rows, d), BF16),
        compiler_params=pltpu.CompilerParams(dimension_semantics=("arbitrary",), vmem_limit_bytes=VMEM_LIMIT),
        name="moe_experts",
    )(tile_expert, tile_valid, xs, w_gate, w_up, w_down)


def _combine_kernel(rows_ref, next_ref, x_ref, slot_ref, wt_ref, g2_ref, fg_ref, ys_ref, *rest, ctx_tiles):
    *o_refs, buf, sems = rest
    i = pl.program_id(0)
    cur = i % 2

    def copies(table_ref, b, act):
        _for_granules(table_ref, lambda g, row: act(_granule_copy(ys_ref, row, buf.at[b], g * GRANULE, sems.at[b])))

    @pl.when(i == 0)
    def _():
        buf[...] = jnp.zeros_like(buf)
        copies(rows_ref, 0, lambda c: c.start())

    @pl.when(i + 1 < pl.num_programs(0))
    def _():
        copies(next_ref, 1 - cur, lambda c: c.start())

    copies(rows_ref, cur, lambda c: c.wait())
    w = wt_ref[...]
    mix = _pair_matrix(slot_ref[...], w[0:1, :], w[1:2, :]).astype(BF16)
    xn = x_ref[...] + g2_ref[0] * _dot_tn(mix, buf[cur])
    if ctx_tiles is None:
        o_refs[0][...] = xn
        return
    xn = _rms(xn) * fg_ref[...]

    @pl.when(pl.program_id(0) < ctx_tiles)
    def _():
        o_refs[0][...] = xn

    @pl.when(pl.program_id(0) >= ctx_tiles)
    def _():
        o_refs[1][...] = xn


def _combine(x, ys, slot, wts, rows, mods, mod_base, final_g, final_norm, n_ctx, t_dec):
    n, d = x.shape
    tm = TM_PROJ
    mrow = _mod_row_of_tile(tm, n_ctx, t_dec)
    ctx_tiles = n_ctx // tm if final_norm else None
    if final_norm:
        out_specs = [pl.BlockSpec((tm, d), lambda i: (jnp.minimum(i, ctx_tiles - 1), 0)),
                     pl.BlockSpec((tm, d), lambda i: (jnp.maximum(i - ctx_tiles, 0), 0))]
        out_shape = [jax.ShapeDtypeStruct((n_ctx, d), F32), jax.ShapeDtypeStruct((n - n_ctx, d), F32)]
    else:
        out_specs = pl.BlockSpec((tm, d), lambda i: (i, 0))
        out_shape = jax.ShapeDtypeStruct((n, d), F32)
    rows = rows.reshape(n // tm, 1, D_HEAD)
    last = n // tm - 1
    return pl.pallas_call(
        functools.partial(_combine_kernel, ctx_tiles=ctx_tiles),
        grid=(n // tm,),
        in_specs=[
            pl.BlockSpec((1, 1, D_HEAD), lambda i: (i, 0, 0), memory_space=pltpu.SMEM),
            pl.BlockSpec((1, 1, D_HEAD), lambda i: (jnp.minimum(i + 1, last), 0, 0), memory_space=pltpu.SMEM),
            pl.BlockSpec((tm, d), lambda i: (i, 0)),
            pl.BlockSpec((2, tm), lambda i: (0, i)),
            pl.BlockSpec((2, tm), lambda i: (0, i)),
            pl.BlockSpec((1, 1, d), lambda i: (mod_base + mrow(i) * 6 + 5, 0, 0)),
            pl.BlockSpec((1, d), lambda i: (0, 0)),
            pl.BlockSpec(memory_space=pl.ANY),
        ],
        out_specs=out_specs,
        out_shape=out_shape,
        scratch_shapes=[pltpu.VMEM((2, TILE_GRANULES * GRANULE, d), BF16), pltpu.SemaphoreType.DMA((2,))],
        compiler_params=pltpu.CompilerParams(dimension_semantics=("arbitrary",), vmem_limit_bytes=VMEM_LIMIT),
        name="moe_combine",
    )(rows, rows, x, slot, wts, mods, final_g.reshape(1, d), ys)


def _moe(x, hn, slot, wts, gran, counts, w_gate, w_up, w_down, layer, mods, mod_base, final_g, final_norm, n_ctx,
         t_dec):
    n = x.shape[0]
    n_rows = 2 * n + (n // TM_PROJ) * N_EXPERTS * GRANULE + N_EXPERTS * TM_MOE
    n_tiles = -(-n_rows // TM_MOE)
    rows, tails, tile_expert, tile_valid = _route_tables(gran, counts, n_tiles)
    xs = _dispatch(hn, slot, rows, tails, tile_valid, n_tiles * TM_MOE)
    ys = _experts(xs, tile_expert, tile_valid, w_gate, w_up, w_down, layer)
    return _combine(x, ys, slot, wts, rows, mods, mod_base, final_g, final_norm, n_ctx, t_dec)


def _grid_pos_embed(n_tokens):
    rows = n_tokens // GRID_W
    r = jnp.repeat(jnp.arange(rows, dtype=F32), GRID_W)
    col = jnp.tile(jnp.arange(GRID_W, dtype=F32), rows)
    quarter = D_MODEL // 4
    freq = jnp.exp(jnp.arange(quarter, dtype=F32) * (-math.log(POS_BASE) / quarter))

    def axis_embed(pos):
        a = pos[:, None] * freq[None, :]
        return jnp.concatenate([jnp.sin(a), jnp.cos(a)], axis=-1)

    return jnp.concatenate([axis_embed(r), axis_embed(col)], axis=-1)


def _split_in_weights(w_in):
    pm = 4 * N_HEADS * D_HEAD
    wg = w_in[:, pm:].reshape(-1, 2, 2, N_HEADS)
    return w_in[:, :pm], wg.transpose(0, 3, 1, 2).reshape(-1, 4 * N_HEADS)


def _head_params(first, second):
    return jnp.stack([first, second], axis=0).transpose(2, 0, 1).reshape(-1).astype(F32)


def _heads_per_step(t_len, chains_per_head_chunk=2):
    return max(2, min(N_HEADS, SCAN_CHAINS // (chains_per_head_chunk * (t_len // CHUNK))))


def kernel(x_prompt, x_sample, state_gdn_S, state_mlstm_C, state_mlstm_n, state_mlstm_m, c, c_ctx, ada_w, ada_b,
           norm1_g, norm2_g, gdn_w_in, gdn_conv_w, gdn_a_log, gdn_dt_bias, gdn_norm_g, gdn_w_out, mlstm_w_in,
           mlstm_gate_b, mlstm_norm_g, mlstm_w_out, router_w, router_bias, exp_w_gate, exp_w_up, exp_w_down,
           final_norm_g):
    bp, tp, d = x_prompt.shape
    bs, ts, _ = x_sample.shape
    n_ctx = bp * tp
    depth = ada_w.shape[0]
    assert n_ctx % ts == 0 and ts % max(TM_IN, TM_OUT) == 0 and tp % TM_PROJ == 0 and bs + 1 <= N_MOD_ROWS
    assert TILE_GRANULES <= D_HEAD

    pos = _grid_pos_embed(ts).astype(F32)
    n = n_ctx + bs * ts
    x = (x_prompt.reshape(n_ctx, d), x_sample.reshape(bs * ts, d), pos)

    conds = jnp.concatenate([c_ctx[None, :], c, jnp.zeros((N_MOD_ROWS - 1 - bs, d), F32)], axis=0)
    mods = _ada_mods(conds, ada_w, ada_b).reshape(depth * N_MOD_ROWS * 6, 1, d)

    zeros_dh = jnp.zeros_like(gdn_a_log[0])
    ctx = dict(batch=bp, t_len=tp, row0=0, emit_state=True)
    dec = dict(batch=bs, t_len=ts, row0=n_ctx, emit_state=False)
    gdn_hb = dict(ctx=_heads_per_step(tp), dec=_heads_per_step(ts))
    mlstm_hb = dict(ctx=_heads_per_step(tp, 4), dec=_heads_per_step(ts, 4))
    outs = {}
    for layer in range(depth):
        j = layer // 2
        mod_base = layer * N_MOD_ROWS * 6
        if layer % 2 == 0:
            w_main, w_gate = _split_in_weights(gdn_w_in[j])
            bias = _head_params(gdn_dt_bias[j], zeros_dh)
            mul = _head_params(-jnp.exp(gdn_a_log[j].astype(F32)), zeros_dh)
            main, grow = _inproj(x, n, mods, mod_base, norm1_g[layer], w_main, w_gate, bias, mul, "gdn", n_ctx, ts)
            o_ctx, s_new = _gdn_scan(main, grow, gdn_conv_w[j], gdn_norm_g[j], None, hb=gdn_hb["ctx"], **ctx)
            (o_dec,) = _gdn_scan(main, grow, gdn_conv_w[j], gdn_norm_g[j], state_gdn_S[:, j].astype(F32),
                                 hb=gdn_hb["dec"], **dec)
            outs.setdefault("gdn", []).append(s_new)
            w_out = gdn_w_out[j]
        else:
            w_main, w_gate = _split_in_weights(mlstm_w_in[j])
            bias = _head_params(mlstm_gate_b[j, 0], mlstm_gate_b[j, 1])
            main, grow = _inproj(x, n, mods, mod_base, norm1_g[layer], w_main, w_gate, bias, jnp.zeros_like(bias),
                                 "mlstm", n_ctx, ts)
            o_ctx, c_new, n_new, m_new = _mlstm_scan(main, grow, mlstm_norm_g[j], None, hb=mlstm_hb["ctx"], **ctx)
            init = (state_mlstm_C[:, j].astype(F32),
                    state_mlstm_n[:, j].astype(F32).transpose(0, 2, 1, 3),
                    jnp.broadcast_to(state_mlstm_m[:, j].astype(F32).transpose(0, 2, 1)[..., None],
                                     (bs, N_HEADS, 2, D_HEAD)))
            (o_dec,) = _mlstm_scan(main, grow, mlstm_norm_g[j], init, hb=mlstm_hb["dec"], **dec)
            outs.setdefault("mC", []).append(c_new)
            outs.setdefault("mn", []).append(n_new.transpose(0, 2, 1, 3))
            outs.setdefault("mm", []).append(m_new[..., 0].transpose(0, 2, 1))
            w_out = mlstm_w_out[j]
        x, hn, slot, wts, gran, counts = _outproj_route(x, n, o_ctx, o_dec, w_out, mods, mod_base, norm2_g[layer],
                                                        router_w, router_bias, n_ctx, ts)
        x = _moe(x, hn, slot, wts, gran, counts, exp_w_gate, exp_w_up, exp_w_down, layer, mods, mod_base,
                 final_norm_g, layer == depth - 1, n_ctx, ts)

    y_prompt = x[0].reshape(bp, tp, d)
    y_sample = x[1].reshape(bs, ts, d)
    return (y_prompt, y_sample, jnp.stack(outs["gdn"], axis=1), jnp.stack(outs["mC"], axis=1),
            jnp.stack(outs["mn"], axis=1), jnp.stack(outs["mm"], axis=1))
```

```python
import functools
import math

import jax
import jax.numpy as jnp
from jax import lax
from jax.experimental import pallas as pl
from jax.experimental.pallas import tpu as pltpu

F32 = jnp.float32
BF16 = jnp.bfloat16

D_MODEL = 1024
N_HEADS = 8
D_HEAD = 128
CHUNK = 64
N_EXPERTS = 16
N_GROUPS = 4
EXP_PER_GROUP = 4
D_FF = 512
EPS = 1e-6
GRID_W = 64
POS_BASE = 10000.0
N_MOD_ROWS = 8
VMEM_LIMIT = 56 * 1024 * 1024

TM_PROJ = 256
TM_IN = 512
TM_OUT = 512
TM_MOE = 512
GRANULE = 16
TILE_GRANULES = 2 * TM_PROJ // GRANULE + N_EXPERTS
SCAN_CHAINS = 64


def _silu(x):
    return x * jax.nn.sigmoid(x)


def _softplus(x):
    return jnp.maximum(x, 0.0) + jnp.log(1.0 + jnp.exp(-jnp.abs(x)))


def _dot(a, b, precision=None):
    return jnp.dot(a, b, preferred_element_type=F32, precision=precision)


def _dot_nt(a, b, precision=None):
    return lax.dot_general(a, b, (((1,), (1,)), ((), ())), preferred_element_type=F32, precision=precision)


def _dot_tn(a, b, precision=None):
    return lax.dot_general(a, b, (((0,), (0,)), ((), ())), preferred_element_type=F32, precision=precision)


def _rms(x):
    return x * lax.rsqrt(jnp.mean(x * x, axis=-1, keepdims=True) + EPS)


def _ada_kernel(c_ref, w_ref, b_ref, o_ref):
    cs = _silu(c_ref[...]).astype(BF16)
    o_ref[0] = _dot(cs, w_ref[0].astype(BF16)) + b_ref[0]


def _ada_mods(conds, ada_w, ada_b):
    depth, d, n6 = ada_w.shape
    tn = 1536
    return pl.pallas_call(
        _ada_kernel,
        grid=(depth, n6 // tn),
        in_specs=[
            pl.BlockSpec((N_MOD_ROWS, d), lambda l, j: (0, 0)),
            pl.BlockSpec((1, d, tn), lambda l, j: (l, 0, j)),
            pl.BlockSpec((1, 1, tn), lambda l, j: (l, 0, j)),
        ],
        out_specs=pl.BlockSpec((1, N_MOD_ROWS, tn), lambda l, j: (l, 0, j)),
        out_shape=jax.ShapeDtypeStruct((depth, N_MOD_ROWS, n6), F32),
        compiler_params=pltpu.CompilerParams(dimension_semantics=("arbitrary", "arbitrary"),
                                             vmem_limit_bytes=VMEM_LIMIT),
        name="ada_mods",
    )(conds, ada_w, ada_b.reshape(depth, 1, n6))


def _split3(x):
    hi = x.astype(BF16)
    r1 = x - hi.astype(F32)
    mid = r1.astype(BF16)
    lo = (r1 - mid.astype(F32)).astype(BF16)
    return hi, mid, lo


def _token_specs(x, tm, n_ctx):
    if not isinstance(x, tuple):
        return [pl.BlockSpec((tm, x.shape[1]), lambda i, *_: (i, 0))], [x], None
    xc, xd, pos = x
    d = xc.shape[1]
    ctx_tiles, pos_tiles = n_ctx // tm, pos.shape[0] // tm
    specs = [pl.BlockSpec((tm, d), lambda i, *_: (jnp.minimum(i, ctx_tiles - 1), 0)),
             pl.BlockSpec((tm, d), lambda i, *_: (jnp.maximum(i - ctx_tiles, 0), 0)),
             pl.BlockSpec((tm, d), lambda i, *_: (jnp.maximum(i - ctx_tiles, 0) % pos_tiles, 0))]
    return specs, [xc, xd, pos], ctx_tiles


def _token_tile(refs, ctx_tiles):
    if ctx_tiles is None:
        return refs[0][...], refs[1:]
    xc_ref, xd_ref, pos_ref = refs[:3]
    return jnp.where(pl.program_id(0) < ctx_tiles, xc_ref[...], xd_ref[...] + pos_ref[...]), refs[3:]


def _inproj_kernel(*refs, kind, x_ctx_tiles):
    x, (ng_ref, sc_ref, sh_ref, w_ref, wgt_ref, gb_ref, gm_ref, main_ref, gr_ref) = _token_tile(refs, x_ctx_tiles)
    hn = _rms(x) * ng_ref[...]
    hn = hn * (1.0 + sc_ref[0]) + sh_ref[0]
    hb = hn.astype(BF16)
    main_ref[...] = _dot(hb, w_ref[...])

    pre = _dot_nt(wgt_ref[...], hb) + gb_ref[...]
    row = lax.broadcasted_iota(jnp.int32, pre.shape, 0)
    first = (row % 4) < 2
    if kind == "gdn":
        act = jnp.where(first, gm_ref[...] * _softplus(pre), jax.nn.sigmoid(pre))
        scanned_kind = 0
    else:
        act = jnp.where(first, pre, -_softplus(-pre))
        scanned_kind = 1

    tm = x.shape[0]
    r = lax.broadcasted_iota(jnp.int32, (tm, tm), 0)
    c = lax.broadcasted_iota(jnp.int32, (tm, tm), 1)
    same = (r // CHUNK) == (c // CHUNK)
    before = jnp.where(same, jnp.where(r <= c, 1.0, 0.0), 0.0).astype(BF16)
    after = jnp.where(same, jnp.where(r >= c, 1.0, 0.0), 0.0).astype(BF16)
    parts = _split3(act)
    fwd = sum(_dot(p, before) for p in parts)
    bwd = sum(_dot(p, after) for p in parts)
    gt = jnp.where((row % 4) // 2 == scanned_kind, jnp.where(row % 2 == 0, fwd, bwd), act)
    for h in range(N_HEADS):
        for ch in range(tm // CHUNK):
            gr_ref[h, ch] = gt[4 * h:4 * h + 4, ch * CHUNK:(ch + 1) * CHUNK]


def _mod_row_of_tile(tm, n_ctx, t_dec):
    def f(i):
        r = i * tm
        return jnp.where(r < n_ctx, 0, 1 + (r - n_ctx) // t_dec)
    return f


def _inproj(x, n, mods, mod_base, norm_g, w_main, w_gate, gate_bias, gate_mul, kind, n_ctx, t_dec):
    d = w_main.shape[0]
    tm = TM_IN
    pm = w_main.shape[1]
    ng = w_gate.shape[1]
    mrow = _mod_row_of_tile(tm, n_ctx, t_dec)
    sc_map = lambda i: (mod_base + mrow(i) * 6 + 1, 0, 0)
    sh_map = lambda i: (mod_base + mrow(i) * 6 + 0, 0, 0)
    const2 = lambda i: (0, 0)
    x_specs, x_args, x_ctx_tiles = _token_specs(x, tm, n_ctx)
    return pl.pallas_call(
        functools.partial(_inproj_kernel, kind=kind, x_ctx_tiles=x_ctx_tiles),
        grid=(n // tm,),
        in_specs=x_specs + [
            pl.BlockSpec((1, d), const2),
            pl.BlockSpec((1, 1, d), sc_map),
            pl.BlockSpec((1, 1, d), sh_map),
            pl.BlockSpec((d, pm), const2),
            pl.BlockSpec((ng, d), const2),
            pl.BlockSpec((ng, 1), const2),
            pl.BlockSpec((ng, 1), const2),
        ],
        out_specs=[
            pl.BlockSpec((tm, pm), lambda i: (i, 0)),
            pl.BlockSpec((N_HEADS, tm // CHUNK, 4, CHUNK), lambda i: (0, i, 0, 0)),
        ],
        out_shape=[
            jax.ShapeDtypeStruct((n, pm), F32),
            jax.ShapeDtypeStruct((N_HEADS, n // CHUNK, 4, CHUNK), F32),
        ],
        compiler_params=pltpu.CompilerParams(dimension_semantics=("arbitrary",), vmem_limit_bytes=VMEM_LIMIT),
        name="inproj_" + kind,
    )(*x_args, norm_g.reshape(1, d), mods, mods, w_main.astype(BF16), w_gate.T.astype(BF16),
      gate_bias.reshape(ng, 1), gate_mul.reshape(ng, 1))


def _chunk_masks():
    r = lax.broadcasted_iota(jnp.int32, (CHUNK, CHUNK), 0)
    c = lax.broadcasted_iota(jnp.int32, (CHUNK, CHUNK), 1)
    return r >= c, r > c, r <= c, r < c


def _chunk_rows(c):
    if isinstance(c, int):
        return pl.ds(c * CHUNK, CHUNK)
    return pl.ds(pl.multiple_of(c * CHUNK, CHUNK), CHUNK)


def _head_cols(h):
    return slice(h * D_HEAD, (h + 1) * D_HEAD)


def _last_row(x, d):
    return x[CHUNK - 1:CHUNK, :] if d == 0 else x[0:1, :]


def _gate_selector():
    j = lax.broadcasted_iota(jnp.int32, (32, 4 * D_HEAD), 0) % 8
    lane = lax.broadcasted_iota(jnp.int32, (32, 4 * D_HEAD), 1)
    return jnp.where(lane // D_HEAD == j, 1.0, 0.0).astype(BF16)


def _gate_columns(gr4, selector):
    a = jnp.concatenate([gr4, jnp.zeros_like(gr4)], axis=0)
    hi = a.astype(BF16).astype(F32)
    mid = (a - hi).astype(BF16).astype(F32)
    lo = a - hi - mid
    parts = jnp.concatenate([hi, mid, lo, jnp.zeros_like(a)], axis=0).astype(BF16)
    return _dot_tn(parts, selector)


def _gate_column(cols, j):
    return cols[:, j * D_HEAD:(j + 1) * D_HEAD]


def _unit_tri_solve(a_list, rhs_list):
    pack = 4
    groups = [a_list[i:i + pack] for i in range(0, len(a_list), pack)]
    r = lax.broadcasted_iota(jnp.int32, (CHUNK, pack * CHUNK), 0)
    c = lax.broadcasted_iota(jnp.int32, (CHUNK, pack * CHUNK), 1) % CHUNK
    same = (r // 16) == (c // 16)
    eye = jnp.where(r == c, 1.0, 0.0)
    br = lax.broadcasted_iota(jnp.int32, (pack * CHUNK, pack * CHUNK), 0) // CHUNK
    bc = lax.broadcasted_iota(jnp.int32, (pack * CHUNK, pack * CHUNK), 1) // CHUNK
    zero16 = jnp.zeros((), BF16)

    def blockdiag(y):
        return jnp.where(br == bc, jnp.concatenate([y.astype(BF16)] * pack, axis=0), zero16)

    def mm(xs, ys):
        return [_dot(x.astype(BF16), blockdiag(y)) for x, y in zip(xs, ys)]

    ap = [jnp.concatenate(g, axis=1) for g in groups]
    d = [jnp.where(same, a, 0.0) for a in ap]
    t = [eye - di for di in d]
    for _ in range(3):
        d = mm(d, d)
        t = [ti + pi for ti, pi in zip(t, mm(t, d))]
    b = mm(t, [jnp.where(same, 0.0, a) for a in ap])
    b2 = mm(b, b)
    y = [ti - pi for ti, pi in zip(t, mm(b, t))]
    inv = [(yi + pi).astype(BF16) for yi, pi in zip(y, mm(b2, y))]

    def placed(rhs, j):
        z = jnp.zeros(rhs.shape, BF16)
        return jnp.concatenate([z] * j + [rhs.astype(BF16)] + [z] * (pack - 1 - j), axis=0)

    return [_dot(inv[i // pack], placed(rhs, i % pack)) for i, rhs in enumerate(rhs_list)]


def _for_chunk_groups(nc, group, fn):
    if nc == group:
        fn(list(range(nc)))
        return

    def body(g, carry):
        fn([g * group + j for j in range(group)])
        return carry
    lax.fori_loop(0, nc // group, body, 0)


def _scan_specs(t_len, row0, hb, n_proj):
    rb = row0 // t_len
    nc = t_len // CHUNK
    ngrp = N_HEADS // hb
    specs = [pl.BlockSpec((t_len, hb * D_HEAD), functools.partial(lambda b, g, j: (rb + b, j * ngrp + g), j=j))
             for j in range(n_proj)]
    specs.append(pl.BlockSpec((hb, nc, 4, CHUNK), lambda b, g: (g, rb + b, 0, 0)))
    return specs


def _scan_call(kern, name, args, in_specs, out_specs, out_shape, scratch, batch, hb):
    return pl.pallas_call(
        kern,
        grid=(batch, N_HEADS // hb),
        in_specs=in_specs,
        out_specs=out_specs,
        out_shape=out_shape,
        scratch_shapes=scratch,
        compiler_params=pltpu.CompilerParams(dimension_semantics=("arbitrary", "arbitrary"),
                                             vmem_limit_bytes=VMEM_LIMIT),
        name=name,
    )(*args)


def _gdn_kernel(*refs, t_len, hb, has_init, emit_state):
    it = iter(refs)
    q_ref, k_ref, v_ref, z_ref, gr_ref, cwq_ref, cwk_ref, cwv_ref, ng_ref = (next(it) for _ in range(9))
    s0_ref = next(it) if has_init else None
    o_ref = next(it)
    s_ref = next(it) if emit_state else None
    qs, ks, vs, oacc, qp_s, op_s, km_s, nm_s, ge_s, st_s = (next(it) for _ in range(10))
    nc = t_len // CHUNK

    rows = lax.broadcasted_iota(jnp.int32, (t_len, 1), 0)

    def conv_silu(x, cw):
        xm = jnp.where(rows == 0, 0.0, pltpu.roll(x, 1, axis=0))
        xp = jnp.where(rows == t_len - 1, 0.0, pltpu.roll(x, t_len - 1, axis=0))
        return _silu(xm * cw[0:1] + x * cw[1:2] + xp * cw[2:3])

    def l2n(x):
        return x * lax.rsqrt(jnp.sum(x * x, axis=-1, keepdims=True) + EPS)

    for h in range(hb):
        hc = _head_cols(h)
        qs[:, hc] = l2n(conv_silu(q_ref[:, hc], cwq_ref[:, hc])) * (D_HEAD ** -0.5)
        ks[:, hc] = l2n(conv_silu(k_ref[:, hc], cwk_ref[:, hc]))
        vs[:, hc] = conv_silu(v_ref[:, hc], cwv_ref[:, hc])
    oacc[...] = jnp.zeros_like(oacc)

    lo_i, lo_s, up_i, up_s = _chunk_masks()
    selector = _gate_selector()

    def slot(h, d, c):
        return (h * 2 + d) * nc + c

    def intra(chunks):
        items = [(h, c, d) for h in range(hb) for c in chunks for d in range(2)]
        qkk, cols = [], []
        for h in range(hb):
            for c in chunks:
                q16, k16 = qs[_chunk_rows(c), _head_cols(h)].astype(BF16), ks[_chunk_rows(c), _head_cols(h)].astype(BF16)
                qkk.append(_dot_nt(jnp.concatenate([q16, k16], axis=0), k16))
                cols.append(_gate_columns(gr_ref[h, c], selector))
        a_list, rhs_list, keep = [], [], []
        for n_item, (h, c, d) in enumerate(items):
            sl, hc = _chunk_rows(c), _head_cols(h)
            q, k, v = qs[sl, hc], ks[sl, hc], vs[sl, hc]
            gr4 = gr_ref[h, c]
            col4, qk, kk = cols[n_item // 2], qkk[n_item // 2][:CHUNK], qkk[n_item // 2][CHUNK:]
            g_col, beta = _gate_column(col4, d), _gate_column(col4, 2 + d)
            incl, strict = (lo_i, lo_s) if d == 0 else (up_i, up_s)
            decay = jnp.exp(jnp.where(incl, g_col[:, :CHUNK] - gr4[d:d + 1, :], -jnp.inf))
            a_list.append(jnp.where(strict, beta[:, :CHUNK] * kk * decay, 0.0))
            kb = k * beta
            e_g = jnp.exp(g_col)
            rhs_list.append(jnp.concatenate([v * beta, kb * e_g], axis=1))
            qk16 = jnp.where(incl, qk * decay, 0.0).astype(BF16)
            g_last = _last_row(g_col, d)
            ge_s[slot(h, d, c)] = jnp.broadcast_to(jnp.exp(g_last), (8, D_HEAD))
            keep.append((qk16, q * e_g, (k * jnp.exp(g_last - g_col)).astype(BF16)))
        sols = [s.astype(BF16) for s in _unit_tri_solve(a_list, rhs_list)]
        qw = [_dot(kp[0], s) for kp, s in zip(keep, sols)]
        kw = [_dot_tn(kp[2], s) for kp, s in zip(keep, sols)]
        for (h, c, d), kp, qwi, kwi in zip(items, keep, qw, kw):
            i = h * 2 + d
            sl = _chunk_rows(c)
            op_s[i, sl, :] = qwi[:, :D_HEAD]
            qp_s[i, sl, :] = (kp[1] - qwi[:, D_HEAD:]).astype(BF16)
            nm_s[slot(h, d, c)] = kwi[:, :D_HEAD]
            km_s[slot(h, d, c)] = (-kwi[:, D_HEAD:]).astype(BF16)

    _for_chunk_groups(nc, min(nc, max(1, SCAN_CHAINS // (2 * hb))), intra)

    chains = [(h, d) for h in range(hb) for d in range(2)]

    def inter_body(i, carry):
        cs = (i, nc - 1 - i)
        s16 = [st_s[h * 2 + d].astype(BF16) for h, d in chains]
        outs = [_dot(qp_s[h * 2 + d, _chunk_rows(cs[d]), :], s) for (h, d), s in zip(chains, s16)]
        upds = [_dot(km_s[slot(h, d, cs[d])], s) for (h, d), s in zip(chains, s16)]
        for (h, d), o in zip(chains, outs):
            oacc[_chunk_rows(cs[d]), _head_cols(h)] += o + op_s[h * 2 + d, _chunk_rows(cs[d]), :]
        for (h, d), u in zip(chains, upds):
            st_s[h * 2 + d] = ge_s[slot(h, d, cs[d])][0:1, :] * st_s[h * 2 + d] + u + nm_s[slot(h, d, cs[d])]
        return carry

    for h, d in chains:
        st_s[h * 2 + d] = s0_ref[0, d, h] if has_init else jnp.zeros((D_HEAD, D_HEAD), F32)
    lax.fori_loop(0, nc, inter_body, 0)
    if emit_state:
        for h, d in chains:
            s_ref[0, d, h] = st_s[h * 2 + d]
    for h in range(hb):
        hc = _head_cols(h)
        o_ref[:, hc] = (_rms(oacc[:, hc]) * ng_ref[...] * _silu(z_ref[:, hc])).astype(o_ref.dtype)


def _gdn_scan(main, grow, conv_w, norm_g, s0, *, batch, t_len, row0, hb, emit_state):
    nc = t_len // CHUNK
    ngrp = N_HEADS // hb
    has_init = s0 is not None
    in_specs = _scan_specs(t_len, row0, hb, 4)
    in_specs += [pl.BlockSpec((3, hb * D_HEAD), functools.partial(lambda b, g, j: (0, j * ngrp + g), j=j))
                 for j in range(3)]
    in_specs.append(pl.BlockSpec((1, D_HEAD), lambda b, g: (0, 0)))
    args = [main, main, main, main, grow, conv_w, conv_w, conv_w, norm_g.reshape(1, D_HEAD)]
    state_spec = pl.BlockSpec((1, 2, hb, D_HEAD, D_HEAD), lambda b, g: (b, 0, g, 0, 0))
    if has_init:
        in_specs.append(state_spec)
        args.append(s0)
    out_specs = [pl.BlockSpec((t_len, hb * D_HEAD), lambda b, g: (b, g))]
    out_shape = [jax.ShapeDtypeStruct((batch * t_len, N_HEADS * D_HEAD), BF16)]
    if emit_state:
        out_specs.append(state_spec)
        out_shape.append(jax.ShapeDtypeStruct((batch, 2, N_HEADS, D_HEAD, D_HEAD), F32))
    scratch = ([pltpu.VMEM((t_len, hb * D_HEAD), F32) for _ in range(4)]
               + [pltpu.VMEM((2 * hb, t_len, D_HEAD), BF16),
                  pltpu.VMEM((2 * hb, t_len, D_HEAD), F32),
                  pltpu.VMEM((2 * hb * nc, D_HEAD, D_HEAD), BF16),
                  pltpu.VMEM((2 * hb * nc, D_HEAD, D_HEAD), F32),
                  pltpu.VMEM((2 * hb * nc, 8, D_HEAD), F32),
                  pltpu.VMEM((2 * hb, D_HEAD, D_HEAD), F32)])
    kern = functools.partial(_gdn_kernel, t_len=t_len, hb=hb, has_init=has_init, emit_state=emit_state)
    return _scan_call(kern, "gdn_scan_t%d" % t_len, args, in_specs, out_specs, out_shape, scratch, batch, hb)


def _mlstm_kernel(*refs, t_len, hb, has_init, emit_state):
    it = iter(refs)
    q_ref, k_ref, v_ref, og_ref, gr_ref, ng_ref = (next(it) for _ in range(6))
    if has_init:
        c0_ref, n0_ref, m0_ref = (next(it) for _ in range(3))
    o_ref = next(it)
    if emit_state:
        c_ref, n_ref, m_ref = (next(it) for _ in range(3))
    hacc, q16_s, v1_s, qk_s, ld_s, bb_s, lm_s, lw_s, sc_s, st_s = (next(it) for _ in range(10))
    nc = t_len // CHUNK

    hacc[...] = jnp.zeros_like(hacc)
    ones = jnp.ones((t_len, D_HEAD), BF16)
    for h in range(hb):
        hc = _head_cols(h)
        q16_s[:, hc] = (q_ref[:, hc] * (D_HEAD ** -0.5)).astype(BF16)
        v1_s[:, h * 2 * D_HEAD:(h * 2 + 1) * D_HEAD] = v_ref[:, hc].astype(BF16)
        v1_s[:, (h * 2 + 1) * D_HEAD:(h * 2 + 2) * D_HEAD] = ones
    lo_i, _, up_i, _ = _chunk_masks()
    selector = _gate_selector()

    def slot(h, d, c):
        return (h * 2 + d) * nc + c

    def intra(chunks):
        for h in range(hb):
            for c in chunks:
                sl, hc = _chunk_rows(c), _head_cols(h)
                qk_s[h, sl, :] = _dot_nt(q16_s[sl, hc], k_ref[sl, hc].astype(BF16))
                gr4 = gr_ref[h, c]
                col4 = _gate_columns(gr4, selector)
                for d in range(2):
                    i = h * 2 + d
                    b_col, i_col = _gate_column(col4, 2 + d), _gate_column(col4, d)
                    b_last = _last_row(b_col, d)
                    lwe = b_last - b_col + i_col
                    log_d = jnp.where(lo_i if d == 0 else up_i,
                                      b_col[:, :CHUNK] - gr4[2 + d:3 + d, :] + gr4[d:d + 1, :], -jnp.inf)
                    ld_s[i, sl, :] = log_d
                    lm_s[i, sl, :] = jnp.broadcast_to(jnp.max(log_d, axis=-1, keepdims=True), (CHUNK, D_HEAD))
                    bb_s[i, sl, :] = b_col
                    lw_s[i, sl, :] = lwe
                    sc_s[slot(h, d, c), 0:1, :] = b_last
                    sc_s[slot(h, d, c), 1:2, :] = jnp.max(lwe, axis=0, keepdims=True)

    _for_chunk_groups(nc, min(nc, 4), intra)

    chains = [(h, d) for h in range(hb) for d in range(2)]

    def body(step, carry):
        cs = (step, nc - 1 - step)
        nrow, m = zip(*carry)
        sls = [_chunk_rows(cs[d]) for h, d in chains]
        idx = [h * 2 + d for h, d in chains]
        v1 = [v1_s[sl, h * 2 * D_HEAD:(h * 2 + 2) * D_HEAD] for (h, d), sl in zip(chains, sls)]
        log_last = [sc_s[slot(h, d, cs[d]), 0:1, :] + mi for (h, d), mi in zip(chains, m)]
        m_new = [jnp.maximum(ll, sc_s[slot(h, d, cs[d]), 1:2, :]) for (h, d), ll in zip(chains, log_last)]
        dec = [jnp.exp(ll - mn) for ll, mn in zip(log_last, m_new)]
        kw = [k_ref[sl, _head_cols(h)] * jnp.exp(lw_s[i, sl, :] - mn)
              for (h, d), sl, i, mn in zip(chains, sls, idx, m_new)]
        upd = [_dot_tn(a.astype(BF16), b) for a, b in zip(kw, v1)]
        qc = [_dot(q16_s[sl, _head_cols(h)], st_s[i].astype(BF16)) for (h, d), sl, i in zip(chains, sls, idx)]
        log_inter = [bb_s[i, sl, :] + mi for i, sl, mi in zip(idx, sls, m)]
        mt = [jnp.maximum(li, lm_s[i, sl, :]) for li, i, sl in zip(log_inter, idx, sls)]
        s_inter = [jnp.exp(li - t) for li, t in zip(log_inter, mt)]
        p = [(jnp.exp(ld_s[i, sl, :] - t[:, :CHUNK]) * qk_s[h, sl, :]).astype(BF16)
             for (h, d), i, sl, t in zip(chains, idx, sls, mt)]
        pv = [_dot(a, b) for a, b in zip(p, v1)]
        for (h, d), sl, si, qci, pvi, t in zip(chains, sls, s_inter, qc, pv, mt):
            num = si * qci[:, :D_HEAD] + pvi[:, :D_HEAD]
            den = si * qci[:, D_HEAD:] + pvi[:, D_HEAD:]
            hacc[sl, _head_cols(h)] += num / jnp.maximum(jnp.abs(den), jnp.exp(-t))
        for i, dc, u in zip(idx, dec, upd):
            st_s[i] = jnp.concatenate([dc, dc], axis=1) * st_s[i] + u
        return tuple((dc * nr + jnp.sum(kwi, axis=0, keepdims=True), mn)
                     for dc, nr, kwi, mn in zip(dec, nrow, kw, m_new))

    for h, d in chains:
        if has_init:
            st_s[h * 2 + d, :, :D_HEAD] = c0_ref[0, d, h]
            st_s[h * 2 + d, :, D_HEAD:] = jnp.transpose(jnp.broadcast_to(n0_ref[0, h, d:d + 1, :], (D_HEAD, D_HEAD)))
        else:
            st_s[h * 2 + d] = jnp.zeros((D_HEAD, 2 * D_HEAD), F32)
    if has_init:
        init = tuple((n0_ref[0, h, d:d + 1, :], m0_ref[0, h, d:d + 1, :]) for h, d in chains)
    else:
        init = tuple((jnp.zeros((1, D_HEAD), F32), jnp.zeros((1, D_HEAD), F32)) for _ in chains)
    fin = lax.fori_loop(0, nc, body, init)
    if emit_state:
        for (h, d), (nr, m) in zip(chains, fin):
            c_ref[0, d, h] = st_s[h * 2 + d, :, :D_HEAD]
            n_ref[0, h, d:d + 1, :] = nr
            m_ref[0, h, d:d + 1, :] = m
    for h in range(hb):
        hc = _head_cols(h)
        o_ref[:, hc] = (_rms(hacc[:, hc]) * ng_ref[...] * jax.nn.sigmoid(og_ref[:, hc])).astype(o_ref.dtype)


def _mlstm_scan(main, grow, norm_g, init, *, batch, t_len, row0, hb, emit_state):
    nc = t_len // CHUNK
    has_init = init is not None
    in_specs = _scan_specs(t_len, row0, hb, 4)
    in_specs.append(pl.BlockSpec((1, D_HEAD), lambda b, g: (0, 0)))
    args = [main, main, main, main, grow, norm_g.reshape(1, D_HEAD)]
    c_spec = pl.BlockSpec((1, 2, hb, D_HEAD, D_HEAD), lambda b, g: (b, 0, g, 0, 0))
    v_spec = pl.BlockSpec((1, hb, 2, D_HEAD), lambda b, g: (b, g, 0, 0))
    if has_init:
        in_specs += [c_spec, v_spec, v_spec]
        args += list(init)
    out_specs = [pl.BlockSpec((t_len, hb * D_HEAD), lambda b, g: (b, g))]
    out_shape = [jax.ShapeDtypeStruct((batch * t_len, N_HEADS * D_HEAD), BF16)]
    if emit_state:
        out_specs += [c_spec, v_spec, v_spec]
        out_shape += [jax.ShapeDtypeStruct((batch, 2, N_HEADS, D_HEAD, D_HEAD), F32),
                      jax.ShapeDtypeStruct((batch, N_HEADS, 2, D_HEAD), F32),
                      jax.ShapeDtypeStruct((batch, N_HEADS, 2, D_HEAD), F32)]
    scratch = [pltpu.VMEM((t_len, hb * D_HEAD), F32),
               pltpu.VMEM((t_len, hb * D_HEAD), BF16),
               pltpu.VMEM((t_len, hb * 2 * D_HEAD), BF16),
               pltpu.VMEM((hb, t_len, CHUNK), F32),
               pltpu.VMEM((2 * hb, t_len, CHUNK), F32),
               pltpu.VMEM((2 * hb, t_len, D_HEAD), F32),
               pltpu.VMEM((2 * hb, t_len, D_HEAD), F32),
               pltpu.VMEM((2 * hb, t_len, D_HEAD), F32),
               pltpu.VMEM((2 * hb * nc, 8, D_HEAD), F32),
               pltpu.VMEM((2 * hb, D_HEAD, 2 * D_HEAD), F32)]
    kern = functools.partial(_mlstm_kernel, t_len=t_len, hb=hb, has_init=has_init, emit_state=emit_state)
    return _scan_call(kern, "mlstm_scan_t%d" % t_len, args, in_specs, out_specs, out_shape, scratch, batch, hb)


def _top2_of4(v):
    best, i1 = v[0], jnp.zeros(v[0].shape, jnp.int32)
    for j in range(1, 4):
        take = v[j] > best
        i1 = jnp.where(take, j, i1)
        best = jnp.where(take, v[j], best)
    best2, i2 = None, None
    for j in range(4):
        vj = jnp.where(i1 == j, -jnp.inf, v[j])
        if best2 is None:
            best2, i2 = vj, jnp.zeros(v[0].shape, jnp.int32)
        else:
            take = vj > best2
            i2 = jnp.where(take, j, i2)
            best2 = jnp.where(take, vj, best2)
    return i1, i2


def _pick(rows, idx):
    out = rows[0]
    for j in range(1, len(rows)):
        out = jnp.where(idx == j, rows[j], out)
    return out


def _outproj_kernel(*refs, ctx_tiles, x_ctx_tiles):
    x, (oc_ref, od_ref, w_ref, g1_ref, ng_ref, sc_ref, sh_ref, rw_ref, rb_ref, xo_ref, hn_ref, slot_ref, wt_ref,
        gran_ref, cnt_ref, run_ref) = _token_tile(refs, x_ctx_tiles)
    _outproj_body(x, oc_ref, od_ref, w_ref, g1_ref, ng_ref, sc_ref, sh_ref, rw_ref, rb_ref, xo_ref, hn_ref, slot_ref,
                  wt_ref, gran_ref, cnt_ref, run_ref, ctx_tiles)


def _outproj_body(x, oc_ref, od_ref, w_ref, g1_ref, ng_ref, sc_ref, sh_ref, rw_ref, rb_ref, xo_ref, hn_ref,
                  slot_ref, wt_ref, gran_ref, cnt_ref, run_ref, ctx_tiles):
    @pl.when(pl.program_id(0) == 0)
    def _():
        run_ref[...] = jnp.zeros_like(run_ref)

    o = jnp.where(pl.program_id(0) < ctx_tiles, oc_ref[...], od_ref[...])
    xn = x + g1_ref[0] * _dot(o, w_ref[...])
    xo_ref[...] = xn
    hn = _rms(xn) * ng_ref[...]
    hn = hn * (1.0 + sc_ref[0]) + sh_ref[0]
    hn_ref[...] = hn.astype(BF16)

    tm = TM_PROJ
    tiles = range(hn.shape[0] // tm)
    lanes = [slice(j * tm, (j + 1) * tm) for j in tiles]
    rw = rw_ref[...]
    rw_hi = rw.astype(BF16)
    rw_lo = (rw - rw_hi.astype(F32)).astype(BF16)
    hn_hi = hn.astype(BF16)
    hn_lo = (hn - hn_hi.astype(F32)).astype(BF16)
    logits = [_dot_nt(rw_hi, hn_hi[ln, :]) + _dot_nt(rw_hi, hn_lo[ln, :]) + _dot_nt(rw_lo, hn_hi[ln, :])
              for ln in lanes]

    def top2(lg):
        ex = jnp.exp(lg - jnp.max(lg, axis=0, keepdims=True))
        probs = ex / jnp.sum(ex, axis=0, keepdims=True)
        sel = probs + rb_ref[...]
        sel_rows = [sel[e:e + 1, :] for e in range(N_EXPERTS)]
        prob_rows = [probs[e:e + 1, :] for e in range(N_EXPERTS)]
        scores = []
        for g in range(N_GROUPS):
            r = sel_rows[4 * g:4 * g + 4]
            a, b = jnp.maximum(r[0], r[1]), jnp.minimum(r[0], r[1])
            c, d = jnp.maximum(r[2], r[3]), jnp.minimum(r[2], r[3])
            scores.append(jnp.maximum(a, c) + jnp.maximum(jnp.minimum(a, c), jnp.maximum(b, d)))
        best, grp = scores[0], jnp.zeros(scores[0].shape, jnp.int32)
        for g in range(1, N_GROUPS):
            take = scores[g] > best
            grp = jnp.where(take, g, grp)
            best = jnp.where(take, scores[g], best)
        sel_in = [_pick([sel_rows[4 * g + j] for g in range(N_GROUPS)], grp) for j in range(4)]
        prob_in = [_pick([prob_rows[4 * g + j] for g in range(N_GROUPS)], grp) for j in range(4)]
        i1, i2 = _top2_of4(sel_in)
        w1, w2 = _pick(prob_in, i1), _pick(prob_in, i2)
        tot = w1 + w2
        return grp * 4 + i1, grp * 4 + i2, w1 / tot, w2 / tot

    picks = [top2(lg) for lg in logits]
    eidx = lax.broadcasted_iota(jnp.int32, (N_EXPERTS, tm), 0)
    onehot = [jnp.where(eidx == e1, 1.0, 0.0) + jnp.where(eidx == e2, 1.0, 0.0) for e1, e2, _, _ in picks]
    r = lax.broadcasted_iota(jnp.int32, (tm, tm), 0)
    c = lax.broadcasted_iota(jnp.int32, (tm, tm), 1)
    earlier = jnp.where(r < c, 1.0, 0.0).astype(BF16)
    ahead = [_dot(oh.astype(BF16), earlier) for oh in onehot]
    n_gran = [jnp.broadcast_to(jnp.floor((jnp.sum(oh, axis=1, keepdims=True) + (GRANULE - 1)) * (1.0 / GRANULE)),
                               (N_EXPERTS, D_HEAD)) for oh in onehot]
    er = lax.broadcasted_iota(jnp.int32, (N_EXPERTS, N_EXPERTS), 0)
    ec = lax.broadcasted_iota(jnp.int32, (N_EXPERTS, N_EXPERTS), 1)
    before = jnp.where(ec < er, 1.0, 0.0).astype(BF16)
    first = [_dot(before, ng.astype(BF16)) for ng in n_gran]
    g = lax.broadcasted_iota(jnp.int32, (N_EXPERTS, D_HEAD), 1).astype(F32)
    ge = lax.broadcasted_iota(jnp.int32, (N_EXPERTS, D_HEAD), 0).astype(F32)
    run = run_ref[...]
    for j in tiles:
        e1, e2, w1, w2 = picks[j]
        wt_ref[:, lanes[j]] = jnp.concatenate([w1, w2], axis=0)
        row0 = GRANULE * first[j][:, 0:1] + ahead[j]
        slot = [jnp.sum(jnp.where(eidx == e, row0, 0.0), axis=0, keepdims=True) for e in (e1, e2)]
        slot_ref[:, lanes[j]] = jnp.concatenate(slot, axis=0).astype(jnp.int32)
        owner = jnp.sum(jnp.where(g >= first[j] + n_gran[j], 1.0, 0.0), axis=0, keepdims=True)
        index = g[0:1, :] + jnp.sum(jnp.where(ge == owner, run - first[j], 0.0), axis=0, keepdims=True)
        gran_ref[j] = jnp.concatenate([owner, index], axis=0).astype(jnp.int32)
        run = run + n_gran[j]
    run_ref[...] = run
    cnt_ref[...] = run


def _outproj_route(x, n, o_ctx, o_dec, w_out, mods, mod_base, norm_g, router_w, router_bias, n_ctx, t_dec):
    d = w_out.shape[1]
    tm = TM_OUT
    ctx_tiles = n_ctx // tm
    mrow = _mod_row_of_tile(tm, n_ctx, t_dec)
    mod_map = lambda k: (lambda i: (mod_base + mrow(i) * 6 + k, 0, 0))
    const2 = lambda i: (0, 0)
    x_specs, x_args, x_ctx_tiles = _token_specs(x, tm, n_ctx)
    return pl.pallas_call(
        functools.partial(_outproj_kernel, ctx_tiles=ctx_tiles, x_ctx_tiles=x_ctx_tiles),
        grid=(n // tm,),
        in_specs=x_specs + [
            pl.BlockSpec((tm, d), lambda i: (jnp.minimum(i, ctx_tiles - 1), 0)),
            pl.BlockSpec((tm, d), lambda i: (jnp.maximum(i - ctx_tiles, 0), 0)),
            pl.BlockSpec((d, d), const2),
            pl.BlockSpec((1, 1, d), mod_map(2)),
            pl.BlockSpec((1, d), const2),
            pl.BlockSpec((1, 1, d), mod_map(4)),
            pl.BlockSpec((1, 1, d), mod_map(3)),
            pl.BlockSpec((N_EXPERTS, d), const2),
            pl.BlockSpec((N_EXPERTS, 1), const2),
        ],
        out_specs=[
            pl.BlockSpec((tm, d), lambda i: (i, 0)),
            pl.BlockSpec((tm, d), lambda i: (i, 0)),
            pl.BlockSpec((2, tm), lambda i: (0, i)),
            pl.BlockSpec((2, tm), lambda i: (0, i)),
            pl.BlockSpec((tm // TM_PROJ, 2, D_HEAD), lambda i: (i, 0, 0)),
            pl.BlockSpec((N_EXPERTS, D_HEAD), const2),
        ],
        out_shape=[
            jax.ShapeDtypeStruct((n, d), F32),
            jax.ShapeDtypeStruct((n, d), BF16),
            jax.ShapeDtypeStruct((2, n), jnp.int32),
            jax.ShapeDtypeStruct((2, n), F32),
            jax.ShapeDtypeStruct((n // TM_PROJ, 2, D_HEAD), jnp.int32),
            jax.ShapeDtypeStruct((N_EXPERTS, D_HEAD), F32),
        ],
        scratch_shapes=[pltpu.VMEM((N_EXPERTS, D_HEAD), F32)],
        compiler_params=pltpu.CompilerParams(dimension_semantics=("arbitrary",), vmem_limit_bytes=VMEM_LIMIT),
        name="outproj_route",
    )(*x_args, o_ctx, o_dec, w_out.astype(BF16), mods, norm_g.reshape(1, d), mods, mods, router_w.T, router_bias.reshape(N_EXPERTS, 1))


def _route_tables(gran, counts, n_tiles):
    tile_gran = TM_MOE // GRANULE
    cnt = counts[:, 0].astype(jnp.int32)
    padded = (cnt + tile_gran - 1) // tile_gran * tile_gran
    ends = jnp.cumsum(padded)
    offs = ends - padded
    owner, index = gran[:, 0, :], gran[:, 1, :]
    first_gran = index + sum(jnp.where(owner == e, offs[e], 0) for e in range(N_EXPERTS))
    rows = jnp.where(owner < N_EXPERTS, first_gran * GRANULE, -1).astype(jnp.int32)
    tile_g0 = jnp.arange(n_tiles, dtype=jnp.int32) * tile_gran
    tile_expert = jnp.minimum(jnp.sum(tile_g0[:, None] >= ends[None, :], axis=1), N_EXPERTS - 1)
    tile_valid = jnp.sum((tile_g0[:, None] >= offs[None, :]) & (tile_g0[:, None] < (offs + cnt)[None, :]), axis=1)
    j = jnp.arange(tile_gran, dtype=jnp.int32)[None, :]
    tails = jnp.where(j < (padded - cnt)[:, None], (offs + cnt)[:, None] + j, -1)
    tails = jnp.where(tails >= 0, tails * GRANULE, -1).astype(jnp.int32).reshape(1, 1, N_EXPERTS * tile_gran)
    return rows, tails, tile_expert.astype(jnp.int32), tile_valid.astype(jnp.int32)


def _granule_copy(src, src_row, dst, dst_row, sem):
    return pltpu.make_async_copy(src.at[pl.ds(pl.multiple_of(src_row, GRANULE), GRANULE)],
                                 dst.at[pl.ds(pl.multiple_of(dst_row, GRANULE), GRANULE)], sem)


def _for_granules(rows_ref, fn, count=TILE_GRANULES):
    def body(g, carry):
        row = rows_ref[0, 0, g]

        @pl.when(row >= 0)
        def _():
            fn(g, row)
        return carry
    lax.fori_loop(0, count, body, 0, unroll=8)


def _pair_matrix(slot, first_value, second_value):
    p = lax.broadcasted_iota(jnp.int32, (TILE_GRANULES * GRANULE, slot.shape[1]), 0)
    return jnp.where(p == slot[0:1, :], first_value, 0.0) + jnp.where(p == slot[1:2, :], second_value, 0.0)


def _dispatch_kernel(rows_ref, prev_ref, tail_ref, valid_ref, hn_ref, slot_ref, xs_ref, buf, zero, sems):
    i = pl.program_id(0)
    cur = i % 2
    buf[cur] = _dot(_pair_matrix(slot_ref[...], 1.0, 1.0).astype(BF16), hn_ref[...]).astype(BF16)

    def copies(table_ref, b, act):
        _for_granules(table_ref, lambda g, row: act(_granule_copy(buf.at[b], g * GRANULE, xs_ref, row, sems.at[b])))

    copies(rows_ref, cur, lambda c: c.start())

    @pl.when(i > 0)
    def _():
        copies(prev_ref, 1 - cur, lambda c: c.wait())

    @pl.when(i == pl.num_programs(0) - 1)
    def _():
        copies(rows_ref, cur, lambda c: c.wait())
        zero[...] = jnp.zeros_like(zero)
        n_tail = tail_ref.shape[2]
        _for_granules(tail_ref, lambda g, row: _granule_copy(zero, 0, xs_ref, row, sems.at[0]).start(), n_tail)
        _for_granules(tail_ref, lambda g, row: _granule_copy(zero, 0, xs_ref, row, sems.at[0]).wait(), n_tail)

        def fill_tiles(act):
            def body(t, carry):
                @pl.when(valid_ref[0, 0, t] == 0)
                def _():
                    act(pltpu.make_async_copy(zero, xs_ref.at[pl.ds(pl.multiple_of(t * TM_MOE, TM_MOE), TM_MOE)],
                                              sems.at[1]))
                return carry
            lax.fori_loop(0, valid_ref.shape[2], body, 0)

        fill_tiles(lambda c: c.start())
        fill_tiles(lambda c: c.wait())


def _dispatch(hn, slot, rows, tails, tile_valid, n_rows):
    n, d = hn.shape
    tm = TM_PROJ
    rows = rows.reshape(n // tm, 1, D_HEAD)
    table = lambda index_map: pl.BlockSpec((1, 1, D_HEAD), index_map, memory_space=pltpu.SMEM)
    return pl.pallas_call(
        _dispatch_kernel,
        grid=(n // tm,),
        in_specs=[
            table(lambda i: (i, 0, 0)),
            table(lambda i: (jnp.maximum(i - 1, 0), 0, 0)),
            pl.BlockSpec(memory_space=pltpu.SMEM),
            pl.BlockSpec(memory_space=pltpu.SMEM),
            pl.BlockSpec((tm, d), lambda i: (i, 0)),
            pl.BlockSpec((2, tm), lambda i: (0, i)),
        ],
        out_specs=pl.BlockSpec(memory_space=pl.ANY),
        out_shape=jax.ShapeDtypeStruct((n_rows, d), BF16),
        scratch_shapes=[pltpu.VMEM((2, TILE_GRANULES * GRANULE, d), BF16), pltpu.VMEM((TM_MOE, d), BF16),
                        pltpu.SemaphoreType.DMA((2,))],
        compiler_params=pltpu.CompilerParams(dimension_semantics=("arbitrary",), vmem_limit_bytes=VMEM_LIMIT),
        name="moe_dispatch",
    )(rows, rows, tails, tile_valid.reshape(1, 1, -1), hn, slot)


def _expert_kernel(te_ref, tv_ref, xs_ref, wg_ref, wu_ref, wd_ref, ys_ref, wg16, wu16, wd16):
    i = pl.program_id(0)
    fresh = jnp.logical_or(i == 0, te_ref[i] != te_ref[jnp.maximum(i - 1, 0)])

    @pl.when(fresh)
    def _():
        wg16[...] = wg_ref[0, 0].astype(BF16)
        wu16[...] = wu_ref[0, 0].astype(BF16)
        wd16[...] = wd_ref[0, 0].astype(BF16)

    @pl.when(tv_ref[i] != 0)
    def _():
        x = xs_ref[...]
        hid = _silu(_dot(x, wg16[...])) * _dot(x, wu16[...])
        ys_ref[...] = _dot(hid.astype(BF16), wd16[...]).astype(BF16)

    @pl.when(tv_ref[i] == 0)
    def _():
        ys_ref[...] = jnp.zeros_like(ys_ref)


def _experts(xs, tile_expert, tile_valid, w_gate, w_up, w_down, layer):
    n_rows, d = xs.shape
    tm = TM_MOE
    w_in_spec = pl.BlockSpec((1, 1, d, D_FF), lambda i, te, tv: (layer, te[i], 0, 0))
    return pl.pallas_call(
        _expert_kernel,
        grid_spec=pltpu.PrefetchScalarGridSpec(
            num_scalar_prefetch=2,
            grid=(n_rows // tm,),
            in_specs=[
                pl.BlockSpec((tm, d), lambda i, te, tv: (jnp.where(tv[i] != 0, i, 0), 0)),
                w_in_spec,
                w_in_spec,
                pl.BlockSpec((1, 1, D_FF, d), lambda i, te, tv: (layer, te[i], 0, 0)),
            ],
            out_specs=pl.BlockSpec((tm, d), lambda i, te, tv: (i, 0)),
            scratch_shapes=[pltpu.VMEM((d, D_FF), BF16), pltpu.VMEM((d, D_FF), BF16), pltpu.VMEM((D_FF, d), BF16)],
        ),
        out_shape=jax.ShapeDtypeStruct((n_rows, d), BF16),
        compiler_params=pltpu.CompilerParams(dimension_semantics=("arbitrary",), vmem_limit_bytes=VMEM_LIMIT),
        name="moe_experts",
    )(tile_expert, tile_valid, xs, w_gate, w_up, w_down)


def _combine_kernel(rows_ref, next_ref, x_ref, slot_ref, wt_ref, g2_ref, fg_ref, ys_ref, *rest, ctx_tiles):
    *o_refs, buf, sems = rest
    i = pl.program_id(0)
    cur = i % 2

    def copies(table_ref, b, act):
        _for_granules(table_ref, lambda g, row: act(_granule_copy(ys_ref, row, buf.at[b], g * GRANULE, sems.at[b])))

    @pl.when(i == 0)
    def _():
        buf[...] = jnp.zeros_like(buf)
        copies(rows_ref, 0, lambda c: c.start())

    @pl.when(i + 1 < pl.num_programs(0))
    def _():
        copies(next_ref, 1 - cur, lambda c: c.start())

    copies(rows_ref, cur, lambda c: c.wait())
    w = wt_ref[...]
    mix = _pair_matrix(slot_ref[...], w[0:1, :], w[1:2, :]).astype(BF16)
    xn = x_ref[...] + g2_ref[0] * _dot_tn(mix, buf[cur])
    if ctx_tiles is None:
        o_refs[0][...] = xn
        return
    xn = _rms(xn) * fg_ref[...]

    @pl.when(pl.program_id(0) < ctx_tiles)
    def _():
        o_refs[0][...] = xn

    @pl.when(pl.program_id(0) >= ctx_tiles)
    def _():
        o_refs[1][...] = xn


def _combine(x, ys, slot, wts, rows, mods, mod_base, final_g, final_norm, n_ctx, t_dec):
    n, d = x.shape
    tm = TM_PROJ
    mrow = _mod_row_of_tile(tm, n_ctx, t_dec)
    ctx_tiles = n_ctx // tm if final_norm else None
    if final_norm:
        out_specs = [pl.BlockSpec((tm, d), lambda i: (jnp.minimum(i, ctx_tiles - 1), 0)),
                     pl.BlockSpec((tm, d), lambda i: (jnp.maximum(i - ctx_tiles, 0), 0))]
        out_shape = [jax.ShapeDtypeStruct((n_ctx, d), F32), jax.ShapeDtypeStruct((n - n_ctx, d), F32)]
    else:
        out_specs = pl.BlockSpec((tm, d), lambda i: (i, 0))
        out_shape = jax.ShapeDtypeStruct((n, d), F32)
    rows = rows.reshape(n // tm, 1, D_HEAD)
    last = n // tm - 1
    return pl.pallas_call(
        functools.partial(_combine_kernel, ctx_tiles=ctx_tiles),
        grid=(n // tm,),
        in_specs=[
            pl.BlockSpec((1, 1, D_HEAD), lambda i: (i, 0, 0), memory_space=pltpu.SMEM),
            pl.BlockSpec((1, 1, D_HEAD), lambda i: (jnp.minimum(i + 1, last), 0, 0), memory_space=pltpu.SMEM),
            pl.BlockSpec((tm, d), lambda i: (i, 0)),
            pl.BlockSpec((2, tm), lambda i: (0, i)),
            pl.BlockSpec((2, tm), lambda i: (0, i)),
            pl.BlockSpec((1, 1, d), lambda i: (mod_base + mrow(i) * 6 + 5, 0, 0)),
            pl.BlockSpec((1, d), lambda i: (0, 0)),
            pl.BlockSpec(memory_space=pl.ANY),
        ],
        out_specs=out_specs,
        out_shape=out_shape,
        scratch_shapes=[pltpu.VMEM((2, TILE_GRANULES * GRANULE, d), BF16), pltpu.SemaphoreType.DMA((2,))],
        compiler_params=pltpu.CompilerParams(dimension_semantics=("arbitrary",), vmem_limit_bytes=VMEM_LIMIT),
        name="moe_combine",
    )(rows, rows, x, slot, wts, mods, final_g.reshape(1, d), ys)


def _moe(x, hn, slot, wts, gran, counts, w_gate, w_up, w_down, layer, mods, mod_base, final_g, final_norm, n_ctx,
         t_dec):
    n = x.shape[0]
    n_rows = 2 * n + (n // TM_PROJ) * N_EXPERTS * GRANULE + N_EXPERTS * TM_MOE
    n_tiles = -(-n_rows // TM_MOE)
    rows, tails, tile_expert, tile_valid = _route_tables(gran, counts, n_tiles)
    xs = _dispatch(hn, slot, rows, tails, tile_valid, n_tiles * TM_MOE)
    ys = _experts(xs, tile_expert, tile_valid, w_gate, w_up, w_down, layer)
    return _combine(x, ys, slot, wts, rows, mods, mod_base, final_g, final_norm, n_ctx, t_dec)


def _grid_pos_embed(n_tokens):
    rows = n_tokens // GRID_W
    r = jnp.repeat(jnp.arange(rows, dtype=F32), GRID_W)
    col = jnp.tile(jnp.arange(GRID_W, dtype=F32), rows)
    quarter = D_MODEL // 4
    freq = jnp.exp(jnp.arange(quarter, dtype=F32) * (-math.log(POS_BASE) / quarter))

    def axis_embed(pos):
        a = pos[:, None] * freq[None, :]
        return jnp.concatenate([jnp.sin(a), jnp.cos(a)], axis=-1)

    return jnp.concatenate([axis_embed(r), axis_embed(col)], axis=-1)


def _split_in_weights(w_in):
    pm = 4 * N_HEADS * D_HEAD
    wg = w_in[:, pm:].reshape(-1, 2, 2, N_HEADS)
    return w_in[:, :pm], wg.transpose(0, 3, 1, 2).reshape(-1, 4 * N_HEADS)


def _head_params(first, second):
    return jnp.stack([first, second], axis=0).transpose(2, 0, 1).reshape(-1).astype(F32)


def _heads_per_step(t_len, chains_per_head_chunk=2):
    return max(2, min(N_HEADS, SCAN_CHAINS // (chains_per_head_chunk * (t_len // CHUNK))))


def kernel(x_prompt, x_sample, state_gdn_S, state_mlstm_C, state_mlstm_n, state_mlstm_m, c, c_ctx, ada_w, ada_b,
           norm1_g, norm2_g, gdn_w_in, gdn_conv_w, gdn_a_log, gdn_dt_bias, gdn_norm_g, gdn_w_out, mlstm_w_in,
           mlstm_gate_b, mlstm_norm_g, mlstm_w_out, router_w, router_bias, exp_w_gate, exp_w_up, exp_w_down,
           final_norm_g):
    bp, tp, d = x_prompt.shape
    bs, ts, _ = x_sample.shape
    n_ctx = bp * tp
    depth = ada_w.shape[0]
    assert n_ctx % ts == 0 and ts % max(TM_IN, TM_OUT) == 0 and tp % TM_PROJ == 0 and bs + 1 <= N_MOD_ROWS
    assert TILE_GRANULES <= D_HEAD

    pos = _grid_pos_embed(ts).astype(F32)
    n = n_ctx + bs * ts
    x = (x_prompt.reshape(n_ctx, d), x_sample.reshape(bs * ts, d), pos)

    conds = jnp.concatenate([c_ctx[None, :], c, jnp.zeros((N_MOD_ROWS - 1 - bs, d), F32)], axis=0)
    mods = _ada_mods(conds, ada_w, ada_b).reshape(depth * N_MOD_ROWS * 6, 1, d)

    zeros_dh = jnp.zeros_like(gdn_a_log[0])
    ctx = dict(batch=bp, t_len=tp, row0=0, emit_state=True)
    dec = dict(batch=bs, t_len=ts, row0=n_ctx, emit_state=False)
    gdn_hb = dict(ctx=_heads_per_step(tp), dec=_heads_per_step(ts))
    mlstm_hb = dict(ctx=_heads_per_step(tp, 4), dec=_heads_per_step(ts, 4))
    outs = {}
    for layer in range(depth):
        j = layer // 2
        mod_base = layer * N_MOD_ROWS * 6
        if layer % 2 == 0:
            w_main, w_gate = _split_in_weights(gdn_w_in[j])
            bias = _head_params(gdn_dt_bias[j], zeros_dh)
            mul = _head_params(-jnp.exp(gdn_a_log[j].astype(F32)), zeros_dh)
            main, grow = _inproj(x, n, mods, mod_base, norm1_g[layer], w_main, w_gate, bias, mul, "gdn", n_ctx, ts)
            o_ctx, s_new = _gdn_scan(main, grow, gdn_conv_w[j], gdn_norm_g[j], None, hb=gdn_hb["ctx"], **ctx)
            (o_dec,) = _gdn_scan(main, grow, gdn_conv_w[j], gdn_norm_g[j], state_gdn_S[:, j].astype(F32),
                                 hb=gdn_hb["dec"], **dec)
            outs.setdefault("gdn", []).append(s_new)
            w_out = gdn_w_out[j]
        else:
            w_main, w_gate = _split_in_weights(mlstm_w_in[j])
            bias = _head_params(mlstm_gate_b[j, 0], mlstm_gate_b[j, 1])
            main, grow = _inproj(x, n, mods, mod_base, norm1_g[layer], w_main, w_gate, bias, jnp.zeros_like(bias),
                                 "mlstm", n_ctx, ts)
            o_ctx, c_new, n_new, m_new = _mlstm_scan(main, grow, mlstm_norm_g[j], None, hb=mlstm_hb["ctx"], **ctx)
            init = (state_mlstm_C[:, j].astype(F32),
                    state_mlstm_n[:, j].astype(F32).transpose(0, 2, 1, 3),
                    jnp.broadcast_to(state_mlstm_m[:, j].astype(F32).transpose(0, 2, 1)[..., None],
                                     (bs, N_HEADS, 2, D_HEAD)))
            (o_dec,) = _mlstm_scan(main, grow, mlstm_norm_g[j], init, hb=mlstm_hb["dec"], **dec)
            outs.setdefault("mC", []).append(c_new)
            outs.setdefault("mn", []).append(n_new.transpose(0, 2, 1, 3))
            outs.setdefault("mm", []).append(m_new[..., 0].transpose(0, 2, 1))
            w_out = mlstm_w_out[j]
        x, hn, slot, wts, gran, counts = _outproj_route(x, n, o_ctx, o_dec, w_out, mods, mod_base, norm2_g[layer],
                                                        router_w, router_bias, n_ctx, ts)
        x = _moe(x, hn, slot, wts, gran, counts, exp_w_gate, exp_w_up, exp_w_down, layer, mods, mod_base,
                 final_norm_g, layer == depth - 1, n_ctx, ts)

    y_prompt = x[0].reshape(bp, tp, d)
    y_sample = x[1].reshape(bs, ts, d)
    return (y_prompt, y_sample, jnp.stack(outs["gdn"], axis=1), jnp.stack(outs["mC"], axis=1),
            jnp.stack(outs["mn"], axis=1), jnp.stack(outs["mm"], axis=1))
```

```python
import functools
import math

import jax
import jax.numpy as jnp
from jax import lax
from jax.experimental import pallas as pl
from jax.experimental.pallas import tpu as pltpu

F32 = jnp.float32
BF16 = jnp.bfloat16

D_MODEL = 1024
N_HEADS = 8
D_HEAD = 128
CHUNK = 64
N_EXPERTS = 16
N_GROUPS = 4
EXP_PER_GROUP = 4
D_FF = 512
EPS = 1e-6
GRID_W = 64
POS_BASE = 10000.0
N_MOD_ROWS = 8
VMEM_LIMIT = 56 * 1024 * 1024

TM_PROJ = 256
TM_IN = 512
TM_OUT = 512
TM_MOE = 512
GRANULE = 16
TILE_GRANULES = 2 * TM_PROJ // GRANULE + N_EXPERTS
SCAN_CHAINS = 64


def _silu(x):
    return x * jax.nn.sigmoid(x)


def _softplus(x):
    return jnp.maximum(x, 0.0) + jnp.log(1.0 + jnp.exp(-jnp.abs(x)))


def _dot(a, b, precision=None):
    return jnp.dot(a, b, preferred_element_type=F32, precision=precision)


def _dot_nt(a, b, precision=None):
    return lax.dot_general(a, b, (((1,), (1,)), ((), ())), preferred_element_type=F32, precision=precision)


def _dot_tn(a, b, precision=None):
    return lax.dot_general(a, b, (((0,), (0,)), ((), ())), preferred_element_type=F32, precision=precision)


def _rms(x):
    return x * lax.rsqrt(jnp.mean(x * x, axis=-1, keepdims=True) + EPS)


def _ada_kernel(c_ref, w_ref, b_ref, o_ref):
    cs = _silu(c_ref[...]).astype(BF16)
    o_ref[0] = _dot(cs, w_ref[0].astype(BF16)) + b_ref[0]


def _ada_mods(conds, ada_w, ada_b):
    depth, d, n6 = ada_w.shape
    tn = 1536
    return pl.pallas_call(
        _ada_kernel,
        grid=(depth, n6 // tn),
        in_specs=[
            pl.BlockSpec((N_MOD_ROWS, d), lambda l, j: (0, 0)),
            pl.BlockSpec((1, d, tn), lambda l, j: (l, 0, j)),
            pl.BlockSpec((1, 1, tn), lambda l, j: (l, 0, j)),
        ],
        out_specs=pl.BlockSpec((1, N_MOD_ROWS, tn), lambda l, j: (l, 0, j)),
        out_shape=jax.ShapeDtypeStruct((depth, N_MOD_ROWS, n6), F32),
        compiler_params=pltpu.CompilerParams(dimension_semantics=("arbitrary", "arbitrary"),
                                             vmem_limit_bytes=VMEM_LIMIT),
        name="ada_mods",
    )(conds, ada_w, ada_b.reshape(depth, 1, n6))


def _split3(x):
    hi = x.astype(BF16)
    r1 = x - hi.astype(F32)
    mid = r1.astype(BF16)
    lo = (r1 - mid.astype(F32)).astype(BF16)
    return hi, mid, lo


def _token_specs(x, tm, n_ctx):
    if not isinstance(x, tuple):
        return [pl.BlockSpec((tm, x.shape[1]), lambda i, *_: (i, 0))], [x], None
    xc, xd, pos = x
    d = xc.shape[1]
    ctx_tiles, pos_tiles = n_ctx // tm, pos.shape[0] // tm
    specs = [pl.BlockSpec((tm, d), lambda i, *_: (jnp.minimum(i, ctx_tiles - 1), 0)),
             pl.BlockSpec((tm, d), lambda i, *_: (jnp.maximum(i - ctx_tiles, 0), 0)),
             pl.BlockSpec((tm, d), lambda i, *_: (jnp.maximum(i - ctx_tiles, 0) % pos_tiles, 0))]
    return specs, [xc, xd, pos], ctx_tiles


def _token_tile(refs, ctx_tiles):
    if ctx_tiles is None:
        return refs[0][...], refs[1:]
    xc_ref, xd_ref, pos_ref = refs[:3]
    return jnp.where(pl.program_id(0) < ctx_tiles, xc_ref[...], xd_ref[...] + pos_ref[...]), refs[3:]


def _inproj_kernel(*refs, kind, x_ctx_tiles):
    x, (ng_ref, sc_ref, sh_ref, w_ref, wgt_ref, gb_ref, gm_ref, main_ref, gr_ref) = _token_tile(refs, x_ctx_tiles)
    hn = _rms(x) * ng_ref[...]
    hn = hn * (1.0 + sc_ref[0]) + sh_ref[0]
    hb = hn.astype(BF16)
    main_ref[...] = _dot(hb, w_ref[...])

    pre = _dot_nt(wgt_ref[...], hb) + gb_ref[...]
    row = lax.broadcasted_iota(jnp.int32, pre.shape, 0)
    first = (row % 4) < 2
    if kind == "gdn":
        act = jnp.where(first, gm_ref[...] * _softplus(pre), jax.nn.sigmoid(pre))
        scanned_kind = 0
    else:
        act = jnp.where(first, pre, -_softplus(-pre))
        scanned_kind = 1

    tm = x.shape[0]
    r = lax.broadcasted_iota(jnp.int32, (tm, tm), 0)
    c = lax.broadcasted_iota(jnp.int32, (tm, tm), 1)
    same = (r // CHUNK) == (c // CHUNK)
    before = jnp.where(same, jnp.where(r <= c, 1.0, 0.0), 0.0).astype(BF16)
    after = jnp.where(same, jnp.where(r >= c, 1.0, 0.0), 0.0).astype(BF16)
    parts = _split3(act)
    fwd = sum(_dot(p, before) for p in parts)
    bwd = sum(_dot(p, after) for p in parts)
    gt = jnp.where((row % 4) // 2 == scanned_kind, jnp.where(row % 2 == 0, fwd, bwd), act)
    for h in range(N_HEADS):
        for ch in range(tm // CHUNK):
            gr_ref[h, ch] = gt[4 * h:4 * h + 4, ch * CHUNK:(ch + 1) * CHUNK]


def _mod_row_of_tile(tm, n_ctx, t_dec):
    def f(i):
        r = i * tm
        return jnp.where(r < n_ctx, 0, 1 + (r - n_ctx) // t_dec)
    return f


def _inproj(x, n, mods, mod_base, norm_g, w_main, w_gate, gate_bias, gate_mul, kind, n_ctx, t_dec):
    d = w_main.shape[0]
    tm = TM_IN
    pm = w_main.shape[1]
    ng = w_gate.shape[1]
    mrow = _mod_row_of_tile(tm, n_ctx, t_dec)
    sc_map = lambda i: (mod_base + mrow(i) * 6 + 1, 0, 0)
    sh_map = lambda i: (mod_base + mrow(i) * 6 + 0, 0, 0)
    const2 = lambda i: (0, 0)
    x_specs, x_args, x_ctx_tiles = _token_specs(x, tm, n_ctx)
    return pl.pallas_call(
        functools.partial(_inproj_kernel, kind=kind, x_ctx_tiles=x_ctx_tiles),
        grid=(n // tm,),
        in_specs=x_specs + [
            pl.BlockSpec((1, d), const2),
            pl.BlockSpec((1, 1, d), sc_map),
            pl.BlockSpec((1, 1, d), sh_map),
            pl.BlockSpec((d, pm), const2),
            pl.BlockSpec((ng, d), const2),
            pl.BlockSpec((ng, 1), const2),
            pl.BlockSpec((ng, 1), const2),
        ],
        out_specs=[
            pl.BlockSpec((tm, pm), lambda i: (i, 0)),
            pl.BlockSpec((N_HEADS, tm // CHUNK, 4, CHUNK), lambda i: (0, i, 0, 0)),
        ],
        out_shape=[
            jax.ShapeDtypeStruct((n, pm), F32),
            jax.ShapeDtypeStruct((N_HEADS, n // CHUNK, 4, CHUNK), F32),
        ],
        compiler_params=pltpu.CompilerParams(dimension_semantics=("arbitrary",), vmem_limit_bytes=VMEM_LIMIT),
        name="inproj_" + kind,
    )(*x_args, norm_g.reshape(1, d), mods, mods, w_main.astype(BF16), w_gate.T.astype(BF16),
      gate_bias.reshape(ng, 1), gate_mul.reshape(ng, 1))


def _chunk_masks():
    r = lax.broadcasted_iota(jnp.int32, (CHUNK, CHUNK), 0)
    c = lax.broadcasted_iota(jnp.int32, (CHUNK, CHUNK), 1)
    return r >= c, r > c, r <= c, r < c


def _chunk_rows(c):
    if isinstance(c, int):
        return pl.ds(c * CHUNK, CHUNK)
    return pl.ds(pl.multiple_of(c * CHUNK, CHUNK), CHUNK)


def _head_cols(h):
    return slice(h * D_HEAD, (h + 1) * D_HEAD)


def _last_row(x, d):
    return x[CHUNK - 1:CHUNK, :] if d == 0 else x[0:1, :]


def _gate_selector():
    j = lax.broadcasted_iota(jnp.int32, (32, 4 * D_HEAD), 0) % 8
    lane = lax.broadcasted_iota(jnp.int32, (32, 4 * D_HEAD), 1)
    return jnp.where(lane // D_HEAD == j, 1.0, 0.0).astype(BF16)


def _gate_columns(gr4, selector):
    a = jnp.concatenate([gr4, jnp.zeros_like(gr4)], axis=0)
    hi = a.astype(BF16).astype(F32)
    mid = (a - hi).astype(BF16).astype(F32)
    lo = a - hi - mid
    parts = jnp.concatenate([hi, mid, lo, jnp.zeros_like(a)], axis=0).astype(BF16)
    return _dot_tn(parts, selector)


def _gate_column(cols, j):
    return cols[:, j * D_HEAD:(j + 1) * D_HEAD]


def _unit_tri_solve(a_list, rhs_list):
    pack = 4
    groups = [a_list[i:i + pack] for i in range(0, len(a_list), pack)]
    r = lax.broadcasted_iota(jnp.int32, (CHUNK, pack * CHUNK), 0)
    c = lax.broadcasted_iota(jnp.int32, (CHUNK, pack * CHUNK), 1) % CHUNK
    same = (r // 16) == (c // 16)
    eye = jnp.where(r == c, 1.0, 0.0)
    br = lax.broadcasted_iota(jnp.int32, (pack * CHUNK, pack * CHUNK), 0) // CHUNK
    bc = lax.broadcasted_iota(jnp.int32, (pack * CHUNK, pack * CHUNK), 1) // CHUNK
    zero16 = jnp.zeros((), BF16)

    def blockdiag(y):
        return jnp.where(br == bc, jnp.concatenate([y.astype(BF16)] * pack, axis=0), zero16)

    def mm(xs, ys):
        return [_dot(x.astype(BF16), blockdiag(y)) for x, y in zip(xs, ys)]

    ap = [jnp.concatenate(g, axis=1) for g in groups]
    d = [jnp.where(same, a, 0.0) for a in ap]
    t = [eye - di for di in d]
    for _ in range(3):
        d = mm(d, d)
        t = [ti + pi for ti, pi in zip(t, mm(t, d))]
    b = mm(t, [jnp.where(same, 0.0, a) for a in ap])
    b2 = mm(b, b)
    y = [ti - pi for ti, pi in zip(t, mm(b, t))]
    inv = [(yi + pi).astype(BF16) for yi, pi in zip(y, mm(b2, y))]

    def placed(rhs, j):
        z = jnp.zeros(rhs.shape, BF16)
        return jnp.concatenate([z] * j + [rhs.astype(BF16)] + [z] * (pack - 1 - j), axis=0)

    return [_dot(inv[i // pack], placed(rhs, i % pack)) for i, rhs in enumerate(rhs_list)]


def _for_chunk_groups(nc, group, fn):
    if nc == group:
        fn(list(range(nc)))
        return

    def body(g, carry):
        fn([g * group + j for j in range(group)])
        return carry
    lax.fori_loop(0, nc // group, body, 0)


def _scan_specs(t_len, row0, hb, n_proj):
    rb = row0 // t_len
    nc = t_len // CHUNK
    ngrp = N_HEADS // hb
    specs = [pl.BlockSpec((t_len, hb * D_HEAD), functools.partial(lambda b, g, j: (rb + b, j * ngrp + g), j=j))
             for j in range(n_proj)]
    specs.append(pl.BlockSpec((hb, nc, 4, CHUNK), lambda b, g: (g, rb + b, 0, 0)))
    return specs


def _scan_call(kern, name, args, in_specs, out_specs, out_shape, scratch, batch, hb):
    return pl.pallas_call(
        kern,
        grid=(batch, N_HEADS // hb),
        in_specs=in_specs,
        out_specs=out_specs,
        out_shape=out_shape,
        scratch_shapes=scratch,
        compiler_params=pltpu.CompilerParams(dimension_semantics=("arbitrary", "arbitrary"),
                                             vmem_limit_bytes=VMEM_LIMIT),
        name=name,
    )(*args)


def _gdn_kernel(*refs, t_len, hb, has_init, emit_state):
    it = iter(refs)
    q_ref, k_ref, v_ref, z_ref, gr_ref, cwq_ref, cwk_ref, cwv_ref, ng_ref = (next(it) for _ in range(9))
    s0_ref = next(it) if has_init else None
    o_ref = next(it)
    s_ref = next(it) if emit_state else None
    qs, ks, vs, oacc, qp_s, op_s, km_s, nm_s, ge_s, st_s = (next(it) for _ in range(10))
    nc = t_len // CHUNK

    rows = lax.broadcasted_iota(jnp.int32, (t_len, 1), 0)

    def conv_silu(x, cw):
        xm = jnp.where(rows == 0, 0.0, pltpu.roll(x, 1, axis=0))
        xp = jnp.where(rows == t_len - 1, 0.0, pltpu.roll(x, t_len - 1, axis=0))
        return _silu(xm * cw[0:1] + x * cw[1:2] + xp * cw[2:3])

    def l2n(x):
        return x * lax.rsqrt(jnp.sum(x * x, axis=-1, keepdims=True) + EPS)

    for h in range(hb):
        hc = _head_cols(h)
        qs[:, hc] = l2n(conv_silu(q_ref[:, hc], cwq_ref[:, hc])) * (D_HEAD ** -0.5)
        ks[:, hc] = l2n(conv_silu(k_ref[:, hc], cwk_ref[:, hc]))
        vs[:, hc] = conv_silu(v_ref[:, hc], cwv_ref[:, hc])
    oacc[...] = jnp.zeros_like(oacc)

    lo_i, lo_s, up_i, up_s = _chunk_masks()
    selector = _gate_selector()

    def slot(h, d, c):
        return (h * 2 + d) * nc + c

    def intra(chunks):
        items = [(h, c, d) for h in range(hb) for c in chunks for d in range(2)]
        qkk, cols = [], []
        for h in range(hb):
            for c in chunks:
                q16, k16 = qs[_chunk_rows(c), _head_cols(h)].astype(BF16), ks[_chunk_rows(c), _head_cols(h)].astype(BF16)
                qkk.append(_dot_nt(jnp.concatenate([q16, k16], axis=0), k16))
                cols.append(_gate_columns(gr_ref[h, c], selector))
        a_list, rhs_list, keep = [], [], []
        for n_item, (h, c, d) in enumerate(items):
            sl, hc = _chunk_rows(c), _head_cols(h)
            q, k, v = qs[sl, hc], ks[sl, hc], vs[sl, hc]
            gr4 = gr_ref[h, c]
            col4, qk, kk = cols[n_item // 2], qkk[n_item // 2][:CHUNK], qkk[n_item // 2][CHUNK:]
            g_col, beta = _gate_column(col4, d), _gate_column(col4, 2 + d)
            incl, strict = (lo_i, lo_s) if d == 0 else (up_i, up_s)
            decay = jnp.exp(jnp.where(incl, g_col[:, :CHUNK] - gr4[d:d + 1, :], -jnp.inf))
            a_list.append(jnp.where(strict, beta[:, :CHUNK] * kk * decay, 0.0))
            kb = k * beta
            e_g = jnp.exp(g_col)
            rhs_list.append(jnp.concatenate([v * beta, kb * e_g], axis=1))
            qk16 = jnp.where(incl, qk * decay, 0.0).astype(BF16)
            g_last = _last_row(g_col, d)
            ge_s[slot(h, d, c)] = jnp.broadcast_to(jnp.exp(g_last), (8, D_HEAD))
            keep.append((qk16, q * e_g, (k * jnp.exp(g_last - g_col)).astype(BF16)))
        sols = [s.astype(BF16) for s in _unit_tri_solve(a_list, rhs_list)]
        qw = [_dot(kp[0], s) for kp, s in zip(keep, sols)]
        kw = [_dot_tn(kp[2], s) for kp, s in zip(keep, sols)]
        for (h, c, d), kp, qwi, kwi in zip(items, keep, qw, kw):
            i = h * 2 + d
            sl = _chunk_rows(c)
            op_s[i, sl, :] = qwi[:, :D_HEAD]
            qp_s[i, sl, :] = (kp[1] - qwi[:, D_HEAD:]).astype(BF16)
            nm_s[slot(h, d, c)] = kwi[:, :D_HEAD]
            km_s[slot(h, d, c)] = (-kwi[:, D_HEAD:]).astype(BF16)

    _for_chunk_groups(nc, min(nc, max(1, SCAN_CHAINS // (2 * hb))), intra)

    chains = [(h, d) for h in range(hb) for d in range(2)]

    def inter_body(i, carry):
        cs = (i, nc - 1 - i)
        s16 = [st_s[h * 2 + d].astype(BF16) for h, d in chains]
        outs = [_dot(qp_s[h * 2 + d, _chunk_rows(cs[d]), :], s) for (h, d), s in zip(chains, s16)]
        upds = [_dot(km_s[slot(h, d, cs[d])], s) for (h, d), s in zip(chains, s16)]
        for (h, d), o in zip(chains, outs):
            oacc[_chunk_rows(cs[d]), _head_cols(h)] += o + op_s[h * 2 + d, _chunk_rows(cs[d]), :]
        for (h, d), u in zip(chains, upds):
            st_s[h * 2 + d] = ge_s[slot(h, d, cs[d])][0:1, :] * st_s[h * 2 + d] + u + nm_s[slot(h, d, cs[d])]
        return carry

    for h, d in chains:
        st_s[h * 2 + d] = s0_ref[0, d, h] if has_init else jnp.zeros((D_HEAD, D_HEAD), F32)
    lax.fori_loop(0, nc, inter_body, 0)
    if emit_state:
        for h, d in chains:
            s_ref[0, d, h] = st_s[h * 2 + d]
    for h in range(hb):
        hc = _head_cols(h)
        o_ref[:, hc] = (_rms(oacc[:, hc]) * ng_ref[...] * _silu(z_ref[:, hc])).astype(o_ref.dtype)


def _gdn_scan(main, grow, conv_w, norm_g, s0, *, batch, t_len, row0, hb, emit_state):
    nc = t_len // CHUNK
    ngrp = N_HEADS // hb
    has_init = s0 is not None
    in_specs = _scan_specs(t_len, row0, hb, 4)
    in_specs += [pl.BlockSpec((3, hb * D_HEAD), functools.partial(lambda b, g, j: (0, j * ngrp + g), j=j))
                 for j in range(3)]
    in_specs.append(pl.BlockSpec((1, D_HEAD), lambda b, g: (0, 0)))
    args = [main, main, main, main, grow, conv_w, conv_w, conv_w, norm_g.reshape(1, D_HEAD)]
    state_spec = pl.BlockSpec((1, 2, hb, D_HEAD, D_HEAD), lambda b, g: (b, 0, g, 0, 0))
    if has_init:
        in_specs.append(state_spec)
        args.append(s0)
    out_specs = [pl.BlockSpec((t_len, hb * D_HEAD), lambda b, g: (b, g))]
    out_shape = [jax.ShapeDtypeStruct((batch * t_len, N_HEADS * D_HEAD), BF16)]
    if emit_state:
        out_specs.append(state_spec)
        out_shape.append(jax.ShapeDtypeStruct((batch, 2, N_HEADS, D_HEAD, D_HEAD), F32))
    scratch = ([pltpu.VMEM((t_len, hb * D_HEAD), F32) for _ in range(4)]
               + [pltpu.VMEM((2 * hb, t_len, D_HEAD), BF16),
                  pltpu.VMEM((2 * hb, t_len, D_HEAD), F32),
                  pltpu.VMEM((2 * hb * nc, D_HEAD, D_HEAD), BF16),
                  pltpu.VMEM((2 * hb * nc, D_HEAD, D_HEAD), F32),
                  pltpu.VMEM((2 * hb * nc, 8, D_HEAD), F32),
                  pltpu.VMEM((2 * hb, D_HEAD, D_HEAD), F32)])
    kern = functools.partial(_gdn_kernel, t_len=t_len, hb=hb, has_init=has_init, emit_state=emit_state)
    return _scan_call(kern, "gdn_scan_t%d" % t_len, args, in_specs, out_specs, out_shape, scratch, batch, hb)


def _mlstm_kernel(*refs, t_len, hb, has_init, emit_state):
    it = iter(refs)
    q_ref, k_ref, v_ref, og_ref, gr_ref, ng_ref = (next(it) for _ in range(6))
    if has_init:
        c0_ref, n0_ref, m0_ref = (next(it) for _ in range(3))
    o_ref = next(it)
    if emit_state:
        c_ref, n_ref, m_ref = (next(it) for _ in range(3))
    hacc, q16_s, v1_s, qk_s, ld_s, bb_s, lm_s, lw_s, sc_s, st_s = (next(it) for _ in range(10))
    nc = t_len // CHUNK

    hacc[...] = jnp.zeros_like(hacc)
    ones = jnp.ones((t_len, D_HEAD), BF16)
    for h in range(hb):
        hc = _head_cols(h)
        q16_s[:, hc] = (q_ref[:, hc] * (D_HEAD ** -0.5)).astype(BF16)
        v1_s[:, h * 2 * D_HEAD:(h * 2 + 1) * D_HEAD] = v_ref[:, hc].astype(BF16)
        v1_s[:, (h * 2 + 1) * D_HEAD:(h * 2 + 2) * D_HEAD] = ones
    lo_i, _, up_i, _ = _chunk_masks()
    selector = _gate_selector()

    def slot(h, d, c):
        return (h * 2 + d) * nc + c

    def intra(chunks):
        for h in range(hb):
            for c in chunks:
                sl, hc = _chunk_rows(c), _head_cols(h)
                qk_s[h, sl, :] = _dot_nt(q16_s[sl, hc], k_ref[sl, hc].astype(BF16))
                gr4 = gr_ref[h, c]
                col4 = _gate_columns(gr4, selector)
                for d in range(2):
                    i = h * 2 + d
                    b_col, i_col = _gate_column(col4, 2 + d), _gate_column(col4, d)
                    b_last = _last_row(b_col, d)
                    lwe = b_last - b_col + i_col
                    log_d = jnp.where(lo_i if d == 0 else up_i,
                                      b_col[:, :CHUNK] - gr4[2 + d:3 + d, :] + gr4[d:d + 1, :], -jnp.inf)
                    ld_s[i, sl, :] = log_d
                    lm_s[i, sl, :] = jnp.broadcast_to(jnp.max(log_d, axis=-1, keepdims=True), (CHUNK, D_HEAD))
                    bb_s[i, sl, :] = b_col
                    lw_s[i, sl, :] = lwe
                    sc_s[slot(h, d, c), 0:1, :] = b_last
                    sc_s[slot(h, d, c), 1:2, :] = jnp.max(lwe, axis=0, keepdims=True)

    _for_chunk_groups(nc, min(nc, 4), intra)

    chains = [(h, d) for h in range(hb) for d in range(2)]

    def body(step, carry):
        cs = (step, nc - 1 - step)
        nrow, m = zip(*carry)
        sls = [_chunk_rows(cs[d]) for h, d in chains]
        idx = [h * 2 + d for h, d in chains]
        v1 = [v1_s[sl, h * 2 * D_HEAD:(h * 2 + 2) * D_HEAD] for (h, d), sl in zip(chains, sls)]
        log_last = [sc_s[slot(h, d, cs[d]), 0:1, :] + mi for (h, d), mi in zip(chains, m)]
        m_new = [jnp.maximum(ll, sc_s[slot(h, d, cs[d]), 1:2, :]) for (h, d), ll in zip(chains, log_last)]
        dec = [jnp.exp(ll - mn) for ll, mn in zip(log_last, m_new)]
        kw = [k_ref[sl, _head_cols(h)] * jnp.exp(lw_s[i, sl, :] - mn)
              for (h, d), sl, i, mn in zip(chains, sls, idx, m_new)]
        upd = [_dot_tn(a.astype(BF16), b) for a, b in zip(kw, v1)]
        qc = [_dot(q16_s[sl, _head_cols(h)], st_s[i].astype(BF16)) for (h, d), sl, i in zip(chains, sls, idx)]
        log_inter = [bb_s[i, sl, :] + mi for i, sl, mi in zip(idx, sls, m)]
        mt = [jnp.maximum(li, lm_s[i, sl, :]) for li, i, sl in zip(log_inter, idx, sls)]
        s_inter = [jnp.exp(li - t) for li, t in zip(log_inter, mt)]
        p = [(jnp.exp(ld_s[i, sl, :] - t[:, :CHUNK]) * qk_s[h, sl, :]).astype(BF16)
             for (h, d), i, sl, t in zip(chains, idx, sls, mt)]
        pv = [_dot(a, b) for a, b in zip(p, v1)]
        for (h, d), sl, si, qci, pvi, t in zip(chains, sls, s_inter, qc, pv, mt):
            num = si * qci[:, :D_HEAD] + pvi[:, :D_HEAD]
            den = si * qci[:, D_HEAD:] + pvi[:, D_HEAD:]
            hacc[sl, _head_cols(h)] += num / jnp.maximum(jnp.abs(den), jnp.exp(-t))
        for i, dc, u in zip(idx, dec, upd):
            st_s[i] = jnp.concatenate([dc, dc], axis=1) * st_s[i] + u
        return tuple((dc * nr + jnp.sum(kwi, axis=0, keepdims=True), mn)
                     for dc, nr, kwi, mn in zip(dec, nrow, kw, m_new))

    for h, d in chains:
        if has_init:
            st_s[h * 2 + d, :, :D_HEAD] = c0_ref[0, d, h]
            st_s[h * 2 + d, :, D_HEAD:] = jnp.transpose(jnp.broadcast_to(n0_ref[0, h, d:d + 1, :], (D_HEAD, D_HEAD)))
        else:
            st_s[h * 2 + d] = jnp.zeros((D_HEAD, 2 * D_HEAD), F32)
    if has_init:
        init = tuple((n0_ref[0, h, d:d + 1, :], m0_ref[0, h, d:d + 1, :]) for h, d in chains)
    else:
        init = tuple((jnp.zeros((1, D_HEAD), F32), jnp.zeros((1, D_HEAD), F32)) for _ in chains)
    fin = lax.fori_loop(0, nc, body, init)
    if emit_state:
        for (h, d), (nr, m) in zip(chains, fin):
            c_ref[0, d, h] = st_s[h * 2 + d, :, :D_HEAD]
            n_ref[0, h, d:d + 1, :] = nr
            m_ref[0, h, d:d + 1, :] = m
    for h in range(hb):
        hc = _head_cols(h)
        o_ref[:, hc] = (_rms(hacc[:, hc]) * ng_ref[...] * jax.nn.sigmoid(og_ref[:, hc])).astype(o_ref.dtype)


def _mlstm_scan(main, grow, norm_g, init, *, batch, t_len, row0, hb, emit_state):
    nc = t_len // CHUNK
    has_init = init is not None
    in_specs = _scan_specs(t_len, row0, hb, 4)
    in_specs.append(pl.BlockSpec((1, D_HEAD), lambda b, g: (0, 0)))
    args = [main, main, main, main, grow, norm_g.reshape(1, D_HEAD)]
    c_spec = pl.BlockSpec((1, 2, hb, D_HEAD, D_HEAD), lambda b, g: (b, 0, g, 0, 0))
    v_spec = pl.BlockSpec((1, hb, 2, D_HEAD), lambda b, g: (b, g, 0, 0))
    if has_init:
        in_specs += [c_spec, v_spec, v_spec]
        args += list(init)
    out_specs = [pl.BlockSpec((t_len, hb * D_HEAD), lambda b, g: (b, g))]
    out_shape = [jax.ShapeDtypeStruct((batch * t_len, N_HEADS * D_HEAD), BF16)]
    if emit_state:
        out_specs += [c_spec, v_spec, v_spec]
        out_shape += [jax.ShapeDtypeStruct((batch, 2, N_HEADS, D_HEAD, D_HEAD), F32),
                      jax.ShapeDtypeStruct((batch, N_HEADS, 2, D_HEAD), F32),
                      jax.ShapeDtypeStruct((batch, N_HEADS, 2, D_HEAD), F32)]
    scratch = [pltpu.VMEM((t_len, hb * D_HEAD), F32),
               pltpu.VMEM((t_len, hb * D_HEAD), BF16),
               pltpu.VMEM((t_len, hb * 2 * D_HEAD), BF16),
               pltpu.VMEM((hb, t_len, CHUNK), F32),
               pltpu.VMEM((2 * hb, t_len, CHUNK), F32),
               pltpu.VMEM((2 * hb, t_len, D_HEAD), F32),
               pltpu.VMEM((2 * hb, t_len, D_HEAD), F32),
               pltpu.VMEM((2 * hb, t_len, D_HEAD), F32),
               pltpu.VMEM((2 * hb * nc, 8, D_HEAD), F32),
               pltpu.VMEM((2 * hb, D_HEAD, 2 * D_HEAD), F32)]
    kern = functools.partial(_mlstm_kernel, t_len=t_len, hb=hb, has_init=has_init, emit_state=emit_state)
    return _scan_call(kern, "mlstm_scan_t%d" % t_len, args, in_specs, out_specs, out_shape, scratch, batch, hb)


def _top2_of4(v):
    best, i1 = v[0], jnp.zeros(v[0].shape, jnp.int32)
    for j in range(1, 4):
        take = v[j] > best
        i1 = jnp.where(take, j, i1)
        best = jnp.where(take, v[j], best)
    best2, i2 = None, None
    for j in range(4):
        vj = jnp.where(i1 == j, -jnp.inf, v[j])
        if best2 is None:
            best2, i2 = vj, jnp.zeros(v[0].shape, jnp.int32)
        else:
            take = vj > best2
            i2 = jnp.where(take, j, i2)
            best2 = jnp.where(take, vj, best2)
    return i1, i2


def _pick(rows, idx):
    out = rows[0]
    for j in range(1, len(rows)):
        out = jnp.where(idx == j, rows[j], out)
    return out


def _outproj_kernel(*refs, ctx_tiles, x_ctx_tiles):
    x, (oc_ref, od_ref, w_ref, g1_ref, ng_ref, sc_ref, sh_ref, rw_ref, rb_ref, xo_ref, hn_ref, slot_ref, wt_ref,
        gran_ref, cnt_ref, run_ref) = _token_tile(refs, x_ctx_tiles)
    _outproj_body(x, oc_ref, od_ref, w_ref, g1_ref, ng_ref, sc_ref, sh_ref, rw_ref, rb_ref, xo_ref, hn_ref, slot_ref,
                  wt_ref, gran_ref, cnt_ref, run_ref, ctx_tiles)


def _outproj_body(x, oc_ref, od_ref, w_ref, g1_ref, ng_ref, sc_ref, sh_ref, rw_ref, rb_ref, xo_ref, hn_ref,
                  slot_ref, wt_ref, gran_ref, cnt_ref, run_ref, ctx_tiles):
    @pl.when(pl.program_id(0) == 0)
    def _():
        run_ref[...] = jnp.zeros_like(run_ref)

    o = jnp.where(pl.program_id(0) < ctx_tiles, oc_ref[...], od_ref[...])
    xn = x + g1_ref[0] * _dot(o, w_ref[...])
    xo_ref[...] = xn
    hn = _rms(xn) * ng_ref[...]
    hn = hn * (1.0 + sc_ref[0]) + sh_ref[0]
    hn_ref[...] = hn.astype(BF16)

    tm = TM_PROJ
    tiles = range(hn.shape[0] // tm)
    lanes = [slice(j * tm, (j + 1) * tm) for j in tiles]
    rw = rw_ref[...]
    rw_hi = rw.astype(BF16)
    rw_lo = (rw - rw_hi.astype(F32)).astype(BF16)
    hn_hi = hn.astype(BF16)
    hn_lo = (hn - hn_hi.astype(F32)).astype(BF16)
    logits = [_dot_nt(rw_hi, hn_hi[ln, :]) + _dot_nt(rw_hi, hn_lo[ln, :]) + _dot_nt(rw_lo, hn_hi[ln, :])
              for ln in lanes]

    def top2(lg):
        ex = jnp.exp(lg - jnp.max(lg, axis=0, keepdims=True))
        probs = ex / jnp.sum(ex, axis=0, keepdims=True)
        sel = probs + rb_ref[...]
        sel_rows = [sel[e:e + 1, :] for e in range(N_EXPERTS)]
        prob_rows = [probs[e:e + 1, :] for e in range(N_EXPERTS)]
        scores = []
        for g in range(N_GROUPS):
            r = sel_rows[4 * g:4 * g + 4]
            a, b = jnp.maximum(r[0], r[1]), jnp.minimum(r[0], r[1])
            c, d = jnp.maximum(r[2], r[3]), jnp.minimum(r[2], r[3])
            scores.append(jnp.maximum(a, c) + jnp.maximum(jnp.minimum(a, c), jnp.maximum(b, d)))
        best, grp = scores[0], jnp.zeros(scores[0].shape, jnp.int32)
        for g in range(1, N_GROUPS):
            take = scores[g] > best
            grp = jnp.where(take, g, grp)
            best = jnp.where(take, scores[g], best)
        sel_in = [_pick([sel_rows[4 * g + j] for g in range(N_GROUPS)], grp) for j in range(4)]
        prob_in = [_pick([prob_rows[4 * g + j] for g in range(N_GROUPS)], grp) for j in range(4)]
        i1, i2 = _top2_of4(sel_in)
        w1, w2 = _pick(prob_in, i1), _pick(prob_in, i2)
        tot = w1 + w2
        return grp * 4 + i1, grp * 4 + i2, w1 / tot, w2 / tot

    picks = [top2(lg) for lg in logits]
    eidx = lax.broadcasted_iota(jnp.int32, (N_EXPERTS, tm), 0)
    onehot = [jnp.where(eidx == e1, 1.0, 0.0) + jnp.where(eidx == e2, 1.0, 0.0) for e1, e2, _, _ in picks]
    r = lax.broadcasted_iota(jnp.int32, (tm, tm), 0)
    c = lax.broadcasted_iota(jnp.int32, (tm, tm), 1)
    earlier = jnp.where(r < c, 1.0, 0.0).astype(BF16)
    ahead = [_dot(oh.astype(BF16), earlier) for oh in onehot]
    n_gran = [jnp.broadcast_to(jnp.floor((jnp.sum(oh, axis=1, keepdims=True) + (GRANULE - 1)) * (1.0 / GRANULE)),
                               (N_EXPERTS, D_HEAD)) for oh in onehot]
    er = lax.broadcasted_iota(jnp.int32, (N_EXPERTS, N_EXPERTS), 0)
    ec = lax.broadcasted_iota(jnp.int32, (N_EXPERTS, N_EXPERTS), 1)
    before = jnp.where(ec < er, 1.0, 0.0).astype(BF16)
    first = [_dot(before, ng.astype(BF16)) for ng in n_gran]
    g = lax.broadcasted_iota(jnp.int32, (N_EXPERTS, D_HEAD), 1).astype(F32)
    ge = lax.broadcasted_iota(jnp.int32, (N_EXPERTS, D_HEAD), 0).astype(F32)
    run = run_ref[...]
    for j in tiles:
        e1, e2, w1, w2 = picks[j]
        wt_ref[:, lanes[j]] = jnp.concatenate([w1, w2], axis=0)
        row0 = GRANULE * first[j][:, 0:1] + ahead[j]
        slot = [jnp.sum(jnp.where(eidx == e, row0, 0.0), axis=0, keepdims=True) for e in (e1, e2)]
        slot_ref[:, lanes[j]] = jnp.concatenate(slot, axis=0).astype(jnp.int32)
        owner = jnp.sum(jnp.where(g >= first[j] + n_gran[j], 1.0, 0.0), axis=0, keepdims=True)
        index = g[0:1, :] + jnp.sum(jnp.where(ge == owner, run - first[j], 0.0), axis=0, keepdims=True)
        gran_ref[j] = jnp.concatenate([owner, index], axis=0).astype(jnp.int32)
        run = run + n_gran[j]
    run_ref[...] = run
    cnt_ref[...] = run


def _outproj_route(x, n, o_ctx, o_dec, w_out, mods, mod_base, norm_g, router_w, router_bias, n_ctx, t_dec):
    d = w_out.shape[1]
    tm = TM_OUT
    ctx_tiles = n_ctx // tm
    mrow = _mod_row_of_tile(tm, n_ctx, t_dec)
    mod_map = lambda k: (lambda i: (mod_base + mrow(i) * 6 + k, 0, 0))
    const2 = lambda i: (0, 0)
    x_specs, x_args, x_ctx_tiles = _token_specs(x, tm, n_ctx)
    return pl.pallas_call(
        functools.partial(_outproj_kernel, ctx_tiles=ctx_tiles, x_ctx_tiles=x_ctx_tiles),
        grid=(n // tm,),
        in_specs=x_specs + [
            pl.BlockSpec((tm, d), lambda i: (jnp.minimum(i, ctx_tiles - 1), 0)),
            pl.BlockSpec((tm, d), lambda i: (jnp.maximum(i - ctx_tiles, 0), 0)),
            pl.BlockSpec((d, d), const2),
            pl.BlockSpec((1, 1, d), mod_map(2)),
            pl.BlockSpec((1, d), const2),
            pl.BlockSpec((1, 1, d), mod_map(4)),
            pl.BlockSpec((1, 1, d), mod_map(3)),
            pl.BlockSpec((N_EXPERTS, d), const2),
            pl.BlockSpec((N_EXPERTS, 1), const2),
        ],
        out_specs=[
            pl.BlockSpec((tm, d), lambda i: (i, 0)),
            pl.BlockSpec((tm, d), lambda i: (i, 0)),
            pl.BlockSpec((2, tm), lambda i: (0, i)),
            pl.BlockSpec((2, tm), lambda i: (0, i)),
            pl.BlockSpec((tm // TM_PROJ, 2, D_HEAD), lambda i: (i, 0, 0)),
            pl.BlockSpec((N_EXPERTS, D_HEAD), const2),
        ],
        out_shape=[
            jax.ShapeDtypeStruct((n, d), F32),
            jax.ShapeDtypeStruct((n, d), BF16),
            jax.ShapeDtypeStruct((2, n), jnp.int32),
            jax.ShapeDtypeStruct((2, n), F32),
            jax.ShapeDtypeStruct((n // TM_PROJ, 2, D_HEAD), jnp.int32),
            jax.ShapeDtypeStruct((N_EXPERTS, D_HEAD), F32),
        ],
        scratch_shapes=[pltpu.VMEM((N_EXPERTS, D_HEAD), F32)],
        compiler_params=pltpu.CompilerParams(dimension_semantics=("arbitrary",), vmem_limit_bytes=VMEM_LIMIT),
        name="outproj_route",
    )(*x_args, o_ctx, o_dec, w_out.astype(BF16), mods, norm_g.reshape(1, d), mods, mods, router_w.T, router_bias.reshape(N_EXPERTS, 1))


def _route_tables(gran, counts, n_tiles):
    tile_gran = TM_MOE // GRANULE
    cnt = counts[:, 0].astype(jnp.int32)
    padded = (cnt + tile_gran - 1) // tile_gran * tile_gran
    ends = jnp.cumsum(padded)
    offs = ends - padded
    owner, index = gran[:, 0, :], gran[:, 1, :]
    first_gran = index + sum(jnp.where(owner == e, offs[e], 0) for e in range(N_EXPERTS))
    rows = jnp.where(owner < N_EXPERTS, first_gran * GRANULE, -1).astype(jnp.int32)
    tile_g0 = jnp.arange(n_tiles, dtype=jnp.int32) * tile_gran
    tile_expert = jnp.minimum(jnp.sum(tile_g0[:, None] >= ends[None, :], axis=1), N_EXPERTS - 1)
    tile_valid = jnp.sum((tile_g0[:, None] >= offs[None, :]) & (tile_g0[:, None] < (offs + cnt)[None, :]), axis=1)
    j = jnp.arange(tile_gran, dtype=jnp.int32)[None, :]
    tails = jnp.where(j < (padded - cnt)[:, None], (offs + cnt)[:, None] + j, -1)
    tails = jnp.where(tails >= 0, tails * GRANULE, -1).astype(jnp.int32).reshape(1, 1, N_EXPERTS * tile_gran)
    return rows, tails, tile_expert.astype(jnp.int32), tile_valid.astype(jnp.int32)


def _granule_copy(src, src_row, dst, dst_row, sem):
    return pltpu.make_async_copy(src.at[pl.ds(pl.multiple_of(src_row, GRANULE), GRANULE)],
                                 dst.at[pl.ds(pl.multiple_of(dst_row, GRANULE), GRANULE)], sem)


def _for_granules(rows_ref, fn, count=TILE_GRANULES):
    def body(g, carry):
        row = rows_ref[0, 0, g]

        @pl.when(row >= 0)
        def _():
            fn(g, row)
        return carry
    lax.fori_loop(0, count, body, 0, unroll=8)


def _pair_matrix(slot, first_value, second_value):
    p = lax.broadcasted_iota(jnp.int32, (TILE_GRANULES * GRANULE, slot.shape[1]), 0)
    return jnp.where(p == slot[0:1, :], first_value, 0.0) + jnp.where(p == slot[1:2, :], second_value, 0.0)


def _dispatch_kernel(rows_ref, prev_ref, tail_ref, valid_ref, hn_ref, slot_ref, xs_ref, buf, zero, sems):
    i = pl.program_id(0)
    cur = i % 2
    buf[cur] = _dot(_pair_matrix(slot_ref[...], 1.0, 1.0).astype(BF16), hn_ref[...]).astype(BF16)

    def copies(table_ref, b, act):
        _for_granules(table_ref, lambda g, row: act(_granule_copy(buf.at[b], g * GRANULE, xs_ref, row, sems.at[b])))

    copies(rows_ref, cur, lambda c: c.start())

    @pl.when(i > 0)
    def _():
        copies(prev_ref, 1 - cur, lambda c: c.wait())

    @pl.when(i == pl.num_programs(0) - 1)
    def _():
        copies(rows_ref, cur, lambda c: c.wait())
        zero[...] = jnp.zeros_like(zero)
        n_tail = tail_ref.shape[2]
        _for_granules(tail_ref, lambda g, row: _granule_copy(zero, 0, xs_ref, row, sems.at[0]).start(), n_tail)
        _for_granules(tail_ref, lambda g, row: _granule_copy(zero, 0, xs_ref, row, sems.at[0]).wait(), n_tail)

        def fill_tiles(act):
            def body(t, carry):
                @pl.when(valid_ref[0, 0, t] == 0)
                def _():
                    act(pltpu.make_async_copy(zero, xs_ref.at[pl.ds(pl.multiple_of(t * TM_MOE, TM_MOE), TM_MOE)],
                                              sems.at[1]))
                return carry
            lax.fori_loop(0, valid_ref.shape[2], body, 0)

        fill_tiles(lambda c: c.start())
        fill_tiles(lambda c: c.wait())


def _dispatch(hn, slot, rows, tails, tile_valid, n_rows):
    n, d = hn.shape
    tm = TM_PROJ
    rows = rows.reshape(n // tm, 1, D_HEAD)
    table = lambda index_map: pl.BlockSpec((1, 1, D_HEAD), index_map, memory_space=pltpu.SMEM)
    return pl.pallas_call(
        _dispatch_kernel,
        grid=(n // tm,),
        in_specs=[
            table(lambda i: (i, 0, 0)),
            table(lambda i: (jnp.maximum(i - 1, 0), 0, 0)),
            pl.BlockSpec(memory_space=pltpu.SMEM),
            pl.BlockSpec(memory_space=pltpu.SMEM),
            pl.BlockSpec((tm, d), lambda i: (i, 0)),
            pl.BlockSpec((2, tm), lambda i: (0, i)),
        ],
        out_specs=pl.BlockSpec(memory_space=pl.ANY),
        out_shape=jax.ShapeDtypeStruct((n_rows, d), BF16),
        scratch_shapes=[pltpu.VMEM((2, TILE_GRANULES * GRANULE, d), BF16), pltpu.VMEM((TM_MOE, d), BF16),
                        pltpu.SemaphoreType.DMA((2,))],
        compiler_params=pltpu.CompilerParams(dimension_semantics=("arbitrary",), vmem_limit_bytes=VMEM_LIMIT),
        name="moe_dispatch",
    )(rows, rows, tails, tile_valid.reshape(1, 1, -1), hn, slot)


def _expert_kernel(te_ref, tv_ref, xs_ref, wg_ref, wu_ref, wd_ref, ys_ref, wg16, wu16, wd16):
    i = pl.program_id(0)
    fresh = jnp.logical_or(i == 0, te_ref[i] != te_ref[jnp.maximum(i - 1, 0)])

    @pl.when(fresh)
    def _():
        wg16[...] = wg_ref[0, 0].astype(BF16)
        wu16[...] = wu_ref[0, 0].astype(BF16)
        wd16[...] = wd_ref[0, 0].astype(BF16)

    @pl.when(tv_ref[i] != 0)
    def _():
        x = xs_ref[...]
        hid = _silu(_dot(x, wg16[...])) * _dot(x, wu16[...])
        ys_ref[...] = _dot(hid.astype(BF16), wd16[...]).astype(BF16)

    @pl.when(tv_ref[i] == 0)
    def _():
        ys_ref[...] = jnp.zeros_like(ys_ref)


def _experts(xs, tile_expert, tile_valid, w_gate, w_up, w_down, layer):
    n_rows, d = xs.shape
    tm = TM_MOE
    w_in_spec = pl.BlockSpec((1, 1, d, D_FF), lambda i, te, tv: (layer, te[i], 0, 0))
    return pl.pallas_call(
        _expert_kernel,
        grid_spec=pltpu.PrefetchScalarGridSpec(
            num_scalar_prefetch=2,
            grid=(n_rows // tm,),
            in_specs=[
                pl.BlockSpec((tm, d), lambda i, te, tv: (jnp.where(tv[i] != 0, i, 0), 0)),
                w_in_spec,
                w_in_spec,
                pl.BlockSpec((1, 1, D_FF, d), lambda i, te, tv: (layer, te[i], 0, 0)),
            ],
            out_specs=pl.BlockSpec((tm, d), lambda i, te, tv: (i, 0)),
            scratch_shapes=[pltpu.VMEM((d, D_FF), BF16), pltpu.VMEM((d, D_FF), BF16), pltpu.VMEM((D_FF, d), BF16)],
        ),
        out_shape=jax.ShapeDtypeStruct((n_rows, d), BF16),
        compiler_params=pltpu.CompilerParams(dimension_semantics=("arbitrary",), vmem_limit_bytes=VMEM_LIMIT),
        name="moe_experts",
    )(tile_expert, tile_valid, xs, w_gate, w_up, w_down)


def _combine_kernel(rows_ref, next_ref, x_ref, slot_ref, wt_ref, g2_ref, fg_ref, ys_ref, *rest, ctx_tiles):
    *o_refs, buf, sems = rest
    i = pl.program_id(0)
    cur = i % 2

    def copies(table_ref, b, act):
        _for_granules(table_ref, lambda g, row: act(_granule_copy(ys_ref, row, buf.at[b], g * GRANULE, sems.at[b])))

    @pl.when(i == 0)
    def _():
        buf[...] = jnp.zeros_like(buf)
        copies(rows_ref, 0, lambda c: c.start())

    @pl.when(i + 1 < pl.num_programs(0))
    def _():
        copies(next_ref, 1 - cur, lambda c: c.start())

    copies(rows_ref, cur, lambda c: c.wait())
    w = wt_ref[...]
    mix = _pair_matrix(slot_ref[...], w[0:1, :], w[1:2, :]).astype(BF16)
    xn = x_ref[...] + g2_ref[0] * _dot_tn(mix, buf[cur])
    if ctx_tiles is None:
        o_refs[0][...] = xn
        return
    xn = _rms(xn) * fg_ref[...]

    @pl.when(pl.program_id(0) < ctx_tiles)
    def _():
        o_refs[0][...] = xn

    @pl.when(pl.program_id(0) >= ctx_tiles)
    def _():
        o_refs[1][...] = xn


def _combine(x, ys, slot, wts, rows, mods, mod_base, final_g, final_norm, n_ctx, t_dec):
    n, d = x.shape
    tm = TM_PROJ
    mrow = _mod_row_of_tile(tm, n_ctx, t_dec)
    ctx_tiles = n_ctx // tm if final_norm else None
    if final_norm:
        out_specs = [pl.BlockSpec((tm, d), lambda i: (jnp.minimum(i, ctx_tiles - 1), 0)),
                     pl.BlockSpec((tm, d), lambda i: (jnp.maximum(i - ctx_tiles, 0), 0))]
        out_shape = [jax.ShapeDtypeStruct((n_ctx, d), F32), jax.ShapeDtypeStruct((n - n_ctx, d), F32)]
    else:
        out_specs = pl.BlockSpec((tm, d), lambda i: (i, 0))
        out_shape = jax.ShapeDtypeStruct((n, d), F32)
    rows = rows.reshape(n // tm, 1, D_HEAD)
    last = n // tm - 1
    return pl.pallas_call(
        functools.partial(_combine_kernel, ctx_tiles=ctx_tiles),
        grid=(n // tm,),
        in_specs=[
            pl.BlockSpec((1, 1, D_HEAD), lambda i: (i, 0, 0), memory_space=pltpu.SMEM),
            pl.BlockSpec((1, 1, D_HEAD), lambda i: (jnp.minimum(i + 1, last), 0, 0), memory_space=pltpu.SMEM),
            pl.BlockSpec((tm, d), lambda i: (i, 0)),
            pl.BlockSpec((2, tm), lambda i: (0, i)),
            pl.BlockSpec((2, tm), lambda i: (0, i)),
            pl.BlockSpec((1, 1, d), lambda i: (mod_base + mrow(i) * 6 + 5, 0, 0)),
            pl.BlockSpec((1, d), lambda i: (0, 0)),
            pl.BlockSpec(memory_space=pl.ANY),
        ],
        out_specs=out_specs,
        out_shape=out_shape,
        scratch_shapes=[pltpu.VMEM((2, TILE_GRANULES * GRANULE, d), BF16), pltpu.SemaphoreType.DMA((2,))],
        compiler_params=pltpu.CompilerParams(dimension_semantics=("arbitrary",), vmem_limit_bytes=VMEM_LIMIT),
        name="moe_combine",
    )(rows, rows, x, slot, wts, mods, final_g.reshape(1, d), ys)


def _moe(x, hn, slot, wts, gran, counts, w_gate, w_up, w_down, layer, mods, mod_base, final_g, final_norm, n_ctx,
         t_dec):
    n = x.shape[0]
    n_rows = 2 * n + (n // TM_PROJ) * N_EXPERTS * GRANULE + N_EXPERTS * TM_MOE
    n_tiles = -(-n_rows // TM_MOE)
    rows, tails, tile_expert, tile_valid = _route_tables(gran, counts, n_tiles)
    xs = _dispatch(hn, slot, rows, tails, tile_valid, n_tiles * TM_MOE)
    ys = _experts(xs, tile_expert, tile_valid, w_gate, w_up, w_down, layer)
    return _combine(x, ys, slot, wts, rows, mods, mod_base, final_g, final_norm, n_ctx, t_dec)


def _grid_pos_embed(n_tokens):
    rows = n_tokens // GRID_W
    r = jnp.repeat(jnp.arange(rows, dtype=F32), GRID_W)
    col = jnp.tile(jnp.arange(GRID_W, dtype=F32), rows)
    quarter = D_MODEL // 4
    freq = jnp.exp(jnp.arange(quarter, dtype=F32) * (-math.log(POS_BASE) / quarter))

    def axis_embed(pos):
        a = pos[:, None] * freq[None, :]
        return jnp.concatenate([jnp.sin(a), jnp.cos(a)], axis=-1)

    return jnp.concatenate([axis_embed(r), axis_embed(col)], axis=-1)


def _split_in_weights(w_in):
    pm = 4 * N_HEADS * D_HEAD
    wg = w_in[:, pm:].reshape(-1, 2, 2, N_HEADS)
    return w_in[:, :pm], wg.transpose(0, 3, 1, 2).reshape(-1, 4 * N_HEADS)


def _head_params(first, second):
    return jnp.stack([first, second], axis=0).transpose(2, 0, 1).reshape(-1).astype(F32)


def _heads_per_step(t_len, chains_per_head_chunk=2):
    return max(2, min(N_HEADS, SCAN_CHAINS // (chains_per_head_chunk * (t_len // CHUNK))))


def kernel(x_prompt, x_sample, state_gdn_S, state_mlstm_C, state_mlstm_n, state_mlstm_m, c, c_ctx, ada_w, ada_b,
           norm1_g, norm2_g, gdn_w_in, gdn_conv_w, gdn_a_log, gdn_dt_bias, gdn_norm_g, gdn_w_out, mlstm_w_in,
           mlstm_gate_b, mlstm_norm_g, mlstm_w_out, router_w, router_bias, exp_w_gate, exp_w_up, exp_w_down,
           final_norm_g):
    bp, tp, d = x_prompt.shape
    bs, ts, _ = x_sample.shape
    n_ctx = bp * tp
    depth = ada_w.shape[0]
    assert n_ctx % ts == 0 and ts % max(TM_IN, TM_OUT) == 0 and tp % TM_PROJ == 0 and bs + 1 <= N_MOD_ROWS
    assert TILE_GRANULES <= D_HEAD

    pos = _grid_pos_embed(ts).astype(F32)
    n = n_ctx + bs * ts
    x = (x_prompt.reshape(n_ctx, d), x_sample.reshape(bs * ts, d), pos)

    conds = jnp.concatenate([c_ctx[None, :], c, jnp.zeros((N_MOD_ROWS - 1 - bs, d), F32)], axis=0)
    mods = _ada_mods(conds, ada_w, ada_b).reshape(depth * N_MOD_ROWS * 6, 1, d)

    zeros_dh = jnp.zeros_like(gdn_a_log[0])
    ctx = dict(batch=bp, t_len=tp, row0=0, emit_state=True)
    dec = dict(batch=bs, t_len=ts, row0=n_ctx, emit_state=False)
    gdn_hb = dict(ctx=_heads_per_step(tp), dec=_heads_per_step(ts))
    mlstm_hb = dict(ctx=_heads_per_step(tp), dec=_heads_per_step(ts, 4))
    outs = {}
    for layer in range(depth):
        j = layer // 2
        mod_base = layer * N_MOD_ROWS * 6
        if layer % 2 == 0:
            w_main, w_gate = _split_in_weights(gdn_w_in[j])
            bias = _head_params(gdn_dt_bias[j], zeros_dh)
            mul = _head_params(-jnp.exp(gdn_a_log[j].astype(F32)), zeros_dh)
            main, grow = _inproj(x, n, mods, mod_base, norm1_g[layer], w_main, w_gate, bias, mul, "gdn", n_ctx, ts)
            o_ctx, s_new = _gdn_scan(main, grow, gdn_conv_w[j], gdn_norm_g[j], None, hb=gdn_hb["ctx"], **ctx)
            (o_dec,) = _gdn_scan(main, grow, gdn_conv_w[j], gdn_norm_g[j], state_gdn_S[:, j].astype(F32),
                                 hb=gdn_hb["dec"], **dec)
            outs.setdefault("gdn", []).append(s_new)
            w_out = gdn_w_out[j]
        else:
            w_main, w_gate = _split_in_weights(mlstm_w_in[j])
            bias = _head_params(mlstm_gate_b[j, 0], mlstm_gate_b[j, 1])
            main, grow = _inproj(x, n, mods, mod_base, norm1_g[layer], w_main, w_gate, bias, jnp.zeros_like(bias),
                                 "mlstm", n_ctx, ts)
            o_ctx, c_new, n_new, m_new = _mlstm_scan(main, grow, mlstm_norm_g[j], None, hb=mlstm_hb["ctx"], **ctx)
            init = (state_mlstm_C[:, j].astype(F32),
                    state_mlstm_n[:, j].astype(F32).transpose(0, 2, 1, 3),
                    jnp.broadcast_to(state_mlstm_m[:, j].astype(F32).transpose(0, 2, 1)[..., None],
                                     (bs, N_HEADS, 2, D_HEAD)))
            (o_dec,) = _mlstm_scan(main, grow, mlstm_norm_g[j], init, hb=mlstm_hb["dec"], **dec)
            outs.setdefault("mC", []).append(c_new)
            outs.setdefault("mn", []).append(n_new.transpose(0, 2, 1, 3))
            outs.setdefault("mm", []).append(m_new[..., 0].transpose(0, 2, 1))
            w_out = mlstm_w_out[j]
        x, hn, slot, wts, gran, counts = _outproj_route(x, n, o_ctx, o_dec, w_out, mods, mod_base, norm2_g[layer],
                                                        router_w, router_bias, n_ctx, ts)
        x = _moe(x, hn, slot, wts, gran, counts, exp_w_gate, exp_w_up, exp_w_down, layer, mods, mod_base,
                 final_norm_g, layer == depth - 1, n_ctx, ts)

    y_prompt = x[0].reshape(bp, tp, d)
    y_sample = x[1].reshape(bs, ts, d)
    return (y_prompt, y_sample, jnp.stack(outs["gdn"], axis=1), jnp.stack(outs["mC"], axis=1),
            jnp.stack(outs["mn"], axis=1), jnp.stack(outs["mm"], axis=1))
```
